```python
import math
import jax
import jax.numpy as jnp
from jax import lax
import numpy as np

D_MODEL = 2048
BATCH = 16
SEQ = 2048
DEPTH = 2
DEC_BATCH = 32
DEC_SEQ = 64
PAST_LEN = 4096

CHUNK = 64
EPS = 1e-6

W_A = 1024
HEAD_K_A = 128
N_HEADS_A = W_A // HEAD_K_A
HEAD_V_A = W_A // N_HEADS_A

N_Q_B = 16
N_KV_B = 4
HEAD_DIM_B = 64
GQA_GROUP = N_Q_B // N_KV_B
W_B = N_Q_B * HEAD_DIM_B
KV_W_B = N_KV_B * HEAD_DIM_B
WINDOW = 128
ROT_DIM = HEAD_DIM_B // 4
ROPE_THETA = 500000.0
ATTN_SCALE = HEAD_DIM_B ** -0.5

W_C = 1024
HEAD_DIM_C = 64
N_HEADS_C = W_C // HEAD_DIM_C
N_GROUPS_C = 4
D_STATE = 128
CONV_W = 4
CONV_DIM = W_C + 2 * N_GROUPS_C * D_STATE

IN_SPLITS = (W_A, W_A, W_A, W_A, W_B, KV_W_B, KV_W_B, W_B, W_C, CONV_DIM, N_HEADS_C, D_MODEL, D_MODEL, D_MODEL)
D_IN = sum(IN_SPLITS)

kernel_name = 'hybrid_stream_hgrn2_swa_ssd_step'


def rmsnorm(x, w):
    xf = x.astype(jnp.float32)
    r = lax.rsqrt(jnp.mean(xf * xf, axis=-1, keepdims=True) + EPS)
    return (xf * r).astype(x.dtype) * w


def split_columns(t):
    idx = [int(i) for i in np.cumsum(IN_SPLITS)[:-1]]
    return jnp.split(t, idx, axis=-1)


def partial_rotary(x, pos):
    half = ROT_DIM // 2
    inv = jnp.power(ROPE_THETA, -jnp.arange(half, dtype=jnp.float32) / half)
    ang = pos.astype(jnp.float32)[:, None] * inv[None, :]
    cos = jnp.cos(ang)[None, :, None, :]
    sin = jnp.sin(ang)[None, :, None, :]
    xr = x[..., :ROT_DIM].astype(jnp.float32)
    x1, x2 = xr[..., :half], xr[..., half:]
    rot = jnp.concatenate([x1 * cos - x2 * sin, x2 * cos + x1 * sin], axis=-1).astype(x.dtype)
    return jnp.concatenate([rot, x[..., ROT_DIM:]], axis=-1)


def sink_softmax(s, sink):
    sink = sink.astype(jnp.float32)
    m = jnp.maximum(jnp.max(s, axis=-1, keepdims=True), sink)
    p = jnp.exp(s - m)
    return p / (jnp.sum(p, axis=-1, keepdims=True) + jnp.exp(sink - m))


def swa_banded(q, k, v, sinks):
    bsz, L = q.shape[0], q.shape[1]
    n = L // CHUNK
    nb = WINDOW // CHUNK
    band = WINDOW + CHUNK
    qc = q.reshape(bsz, n, CHUNK, N_KV_B, GQA_GROUP, HEAD_DIM_B)
    pad = jnp.zeros((bsz, WINDOW, N_KV_B, HEAD_DIM_B), k.dtype)
    kp = jnp.concatenate([pad, k], axis=1)
    vp = jnp.concatenate([pad, v], axis=1)

    def band_view(t):
        return jnp.concatenate(
            [t[:, j * CHUNK: j * CHUNK + L].reshape(bsz, n, CHUNK, N_KV_B, HEAD_DIM_B) for j in range(nb + 1)],
            axis=2)

    kb, vb = band_view(kp), band_view(vp)
    key_pos = jnp.arange(n)[:, None] * CHUNK - WINDOW + jnp.arange(band)[None, :]
    valid = key_pos >= 0
    s = jnp.einsum('bnqhgd,bnkhd->bnhgqk', qc, kb).astype(jnp.float32) * ATTN_SCALE
    s = jnp.where(valid[None, :, None, None, None, :], s, -jnp.inf)
    p = sink_softmax(s, sinks.reshape(N_KV_B, GQA_GROUP)[:, :, None, None])
    o = jnp.einsum('bnhgqk,bnkhd->bnqhgd', p.astype(v.dtype), vb)
    return o.reshape(bsz, L, W_B)


def swa_cached(q, k_all, v_all, sinks):
    bsz, L = q.shape[0], q.shape[1]
    qg = q.reshape(bsz, L, N_KV_B, GQA_GROUP, HEAD_DIM_B)
    s = jnp.einsum('bqhgd,bkhd->bhgqk', qg, k_all).astype(jnp.float32) * ATTN_SCALE
    p = sink_softmax(s, sinks.reshape(N_KV_B, GQA_GROUP)[:, :, None, None])
    o = jnp.einsum('bhgqk,bkhd->bqhgd', p.astype(v_all.dtype), v_all)
    return o.reshape(bsz, L, W_B)


def hgrn2_scan(q, k, v, log_f, s0):
    bsz, L, H, DK = q.shape
    DV = v.shape[-1]
    C = min(CHUNK, L)
    n = L // C

    def to_chunks(t):
        return jnp.moveaxis(t.reshape(bsz, n, C, H, t.shape[-1]), 1, 0)

    causal = jnp.tril(jnp.ones((C, C), dtype=bool))

    def step(S, inp):
        qc, kc, vc, gc = inp
        b = jnp.cumsum(gc, axis=1)
        pair = jnp.where(causal[None, :, :, None, None], b[:, :, None] - b[:, None, :], -jnp.inf)
        A = jnp.einsum('bthk,bshk,btshk->bhts', qc, kc, jnp.exp(pair))
        o = jnp.einsum('bhts,bshv->bthv', A, vc) + jnp.einsum('bthk,bhkv->bthv', qc * jnp.exp(b), S)
        b_last = b[:, -1]
        S = jnp.exp(b_last)[..., None] * S + jnp.einsum('bshk,bshv->bhkv', kc * jnp.exp(b_last[:, None] - b), vc)
        return S, o

    S, o = lax.scan(step, s0, (to_chunks(q), to_chunks(k), to_chunks(v), to_chunks(log_f)))
    return jnp.moveaxis(o, 0, 1).reshape(bsz, L, H, DV), S


def ssd_scan(xdt, log_a, Bm, Cm, h0):
    bsz, L, H, P = xdt.shape
    G, N = Bm.shape[2], Bm.shape[3]
    hg = H // G
    C = min(CHUNK, L)
    n = L // C
    xs = jnp.moveaxis(xdt.reshape(bsz, n, C, G, hg, P), 1, 0)
    gs = jnp.moveaxis(log_a.reshape(bsz, n, C, G, hg), 1, 0)
    bs = jnp.moveaxis(Bm.reshape(bsz, n, C, G, N), 1, 0)
    cs = jnp.moveaxis(Cm.reshape(bsz, n, C, G, N), 1, 0)
    causal = jnp.tril(jnp.ones((C, C), dtype=bool))

    def step(h, inp):
        xc, gc, bc, cc = inp
        b = jnp.cumsum(gc, axis=1)
        seg = jnp.where(causal[None, :, :, None, None], b[:, :, None] - b[:, None, :], -jnp.inf)
        cb = jnp.einsum('btgn,bsgn->btsg', cc, bc)
        y = jnp.einsum('btsg,btsgh,bsghp->btghp', cb, jnp.exp(seg), xc)
        y = y + jnp.einsum('btgn,bghpn->btghp', cc, h) * jnp.exp(b)[..., None]
        bl = b[:, -1]
        h = jnp.exp(bl)[..., None, None] * h + jnp.einsum('bsgn,bsgh,bsghp->bghpn', bc, jnp.exp(bl[:, None] - b), xc)
        return h, y

    h, y = lax.scan(step, h0.reshape(bsz, G, hg, P, N), (xs, gs, bs, cs))
    return jnp.moveaxis(y, 0, 1).reshape(bsz, L, H, P), h.reshape(bsz, H, P, N)


def causal_conv(u, prev, w, b):
    L = u.shape[1]
    up = jnp.concatenate([prev.astype(u.dtype), u], axis=1)
    acc = b
    for j in range(CONV_W):
        acc = acc + up[:, j:j + L] * w[j]
    return jax.nn.silu(acc), up[:, -(CONV_W - 1):]


def trunk_layer(x, pos, k_cache, v_cache, s_hgrn, s_ssm, s_conv, lb,
                norm_pre, norm_post, w_in, hgrn_norm, swa_sinks, conv_w, conv_b,
                dt_bias, a_log, d_skip, ssm_norm, w_branch_a, w_branch_b, w_branch_c, w_out):
    f32 = jnp.float32
    bsz, L, _ = x.shape
    prompt = k_cache is None
    if prompt:
        s_hgrn = jnp.zeros((bsz, N_HEADS_A, HEAD_K_A, HEAD_V_A), f32)
        s_ssm = jnp.zeros((bsz, N_HEADS_C, HEAD_DIM_C, D_STATE), f32)
        s_conv = jnp.zeros((bsz, CONV_W - 1, CONV_DIM), x.dtype)

    h = rmsnorm(x, norm_pre)
    proj = jnp.einsum('bld,de->ble', h, w_in)
    (a_q, a_f, a_i, a_g, b_q, b_k, b_v, b_g, c_z, c_xbc, c_dt, g_a, g_b, g_c) = split_columns(proj)

    za = a_f.astype(f32).reshape(bsz, L, N_HEADS_A, HEAD_K_A)
    lbh = lb.reshape(N_HEADS_A, HEAD_K_A)
    log_f = jnp.logaddexp(jnp.log(lbh), jnp.log1p(-lbh) + jax.nn.log_sigmoid(za))
    k_a = (1.0 - lbh) * jax.nn.sigmoid(-za)
    q_a = jax.nn.silu(a_q.astype(f32)).reshape(bsz, L, N_HEADS_A, HEAD_K_A)
    v_a = a_i.astype(f32).reshape(bsz, L, N_HEADS_A, HEAD_V_A)
    o_a, hgrn_new = hgrn2_scan(q_a, k_a, v_a, log_f, s_hgrn.astype(f32))
    y_a = (rmsnorm(o_a, hgrn_norm).reshape(bsz, L, W_A) * jax.nn.silu(a_g.astype(f32))).astype(x.dtype)

    q_b = partial_rotary(b_q.reshape(bsz, L, N_Q_B, HEAD_DIM_B), pos)
    k_b = partial_rotary(b_k.reshape(bsz, L, N_KV_B, HEAD_DIM_B), pos)
    v_b = b_v.reshape(bsz, L, N_KV_B, HEAD_DIM_B)
    if prompt:
        o_b = swa_banded(q_b, k_b, v_b, swa_sinks)
        k_new, v_new = k_b[:, -WINDOW:], v_b[:, -WINDOW:]
    else:
        k_all = jnp.concatenate([k_cache.astype(k_b.dtype), k_b], axis=1)
        v_all = jnp.concatenate([v_cache.astype(v_b.dtype), v_b], axis=1)
        o_b = swa_cached(q_b, k_all, v_all, swa_sinks)
        k_new, v_new = k_all[:, -WINDOW:], v_all[:, -WINDOW:]
    y_b = o_b * jax.nn.silu(b_g)

    xbc, conv_new = causal_conv(c_xbc, s_conv, conv_w, conv_b)
    x_c, b_c, c_c = jnp.split(xbc, [W_C, W_C + N_GROUPS_C * D_STATE], axis=-1)
    dt = jax.nn.softplus(c_dt.astype(f32) + dt_bias.astype(f32))
    log_a = -dt * jnp.exp(a_log.astype(f32))
    x_c = x_c.astype(f32).reshape(bsz, L, N_HEADS_C, HEAD_DIM_C)
    o_c, ssm_new = ssd_scan(x_c * dt[..., None], log_a,
                            b_c.astype(f32).reshape(bsz, L, N_GROUPS_C, D_STATE),
                            c_c.astype(f32).reshape(bsz, L, N_GROUPS_C, D_STATE),
                            s_ssm.astype(f32))
    o_c = (o_c + d_skip.astype(f32)[:, None] * x_c).reshape(bsz, L, W_C) * jax.nn.silu(c_z.astype(f32))
    gsz = W_C // N_GROUPS_C
    y_c = rmsnorm(o_c.reshape(bsz, L, N_GROUPS_C, gsz), ssm_norm.reshape(N_GROUPS_C, gsz)).reshape(bsz, L, W_C).astype(x.dtype)

    merged = (jax.nn.sigmoid(g_a) * (y_a @ w_branch_a)
              + jax.nn.sigmoid(g_b) * (y_b @ w_branch_b)
              + jax.nn.sigmoid(g_c) * (y_c @ w_branch_c))
    out = merged @ w_out
    x = x + rmsnorm(out, norm_post)
    return x, (k_new, v_new, hgrn_new.astype(x.dtype), ssm_new.astype(x.dtype), conv_new)


def setup_inputs(seed: int = 0) -> dict:
    key = jax.random.key(seed)
    ks = jax.random.split(key, 26)
    f32 = jnp.float32

    def nrm(k, shape, s):
        return jax.random.normal(k, shape, f32) * s

    dt0 = jnp.exp(jax.random.uniform(ks[17], (DEPTH, N_HEADS_C), f32, math.log(1e-3), math.log(1e-1)))
    return {
        'x_prompt': nrm(ks[0], (BATCH, SEQ, D_MODEL), 1.0),
        'x_sample': nrm(ks[1], (DEC_BATCH, DEC_SEQ, D_MODEL), 1.0),
        'cache_swa_k': nrm(ks[2], (DEPTH, DEC_BATCH, WINDOW, N_KV_B, HEAD_DIM_B), 1.0),
        'cache_swa_v': nrm(ks[3], (DEPTH, DEC_BATCH, WINDOW, N_KV_B, HEAD_DIM_B), 1.0),
        'state_hgrn': nrm(ks[4], (DEPTH, DEC_BATCH, N_HEADS_A, HEAD_K_A, HEAD_V_A), 0.5),
        'state_ssm': nrm(ks[5], (DEPTH, DEC_BATCH, N_HEADS_C, HEAD_DIM_C, D_STATE), 0.1),
        'state_conv': nrm(ks[6], (DEPTH, DEC_BATCH, CONV_W - 1, CONV_DIM), 1.0),
        'norm_pre': 1.0 + nrm(ks[7], (DEPTH, D_MODEL), 0.02),
        'norm_post': 1.0 + nrm(ks[8], (DEPTH, D_MODEL), 0.02),
        'w_in': nrm(ks[9], (DEPTH, D_MODEL, D_IN), D_MODEL ** -0.5),
        'hgrn_lb_logits': nrm(ks[10], (DEPTH, W_A), 0.5),
        'hgrn_norm': 1.0 + nrm(ks[11], (DEPTH, HEAD_V_A), 0.02),
        'swa_sinks': nrm(ks[12], (DEPTH, N_Q_B), 0.5),
        'conv_w': nrm(ks[13], (DEPTH, CONV_W, CONV_DIM), CONV_W ** -0.5),
        'conv_b': nrm(ks[14], (DEPTH, CONV_DIM), 0.02),
        'dt_bias': dt0 + jnp.log(-jnp.expm1(-dt0)),
        'a_log': jnp.log(jax.random.uniform(ks[15], (DEPTH, N_HEADS_C), f32, 1.0, 16.0)),
        'd_skip': 1.0 + nrm(ks[16], (DEPTH, N_HEADS_C), 0.1),
        'ssm_norm': 1.0 + nrm(ks[18], (DEPTH, W_C), 0.02),
        'w_branch_a': nrm(ks[19], (DEPTH, W_A, D_MODEL), W_A ** -0.5),
        'w_branch_b': nrm(ks[20], (DEPTH, W_B, D_MODEL), W_B ** -0.5),
        'w_branch_c': nrm(ks[21], (DEPTH, W_C, D_MODEL), W_C ** -0.5),
        'w_out': nrm(ks[22], (DEPTH, D_MODEL, D_MODEL), D_MODEL ** -0.5),
    }


def reference(x_prompt, x_sample, cache_swa_k, cache_swa_v, state_hgrn, state_ssm, state_conv,
              norm_pre, norm_post, w_in, hgrn_lb_logits, hgrn_norm, swa_sinks, conv_w, conv_b,
              dt_bias, a_log, d_skip, ssm_norm, w_branch_a, w_branch_b, w_branch_c, w_out):
    lbp = jax.nn.softmax(hgrn_lb_logits.astype(jnp.float32), axis=0)
    lbc = jnp.cumsum(lbp, axis=0)
    lb_all = lbc - lbc[0:1]
    pos_p = jnp.arange(x_prompt.shape[1], dtype=jnp.int32)
    pos_s = PAST_LEN + jnp.arange(x_sample.shape[1], dtype=jnp.int32)
    xp, xs = x_prompt, x_sample
    pst, sst = [], []
    for l in range(DEPTH):
        w = (norm_pre[l], norm_post[l], w_in[l], hgrn_norm[l], swa_sinks[l], conv_w[l], conv_b[l],
             dt_bias[l], a_log[l], d_skip[l], ssm_norm[l], w_branch_a[l], w_branch_b[l], w_branch_c[l], w_out[l])
        xp, sp = trunk_layer(xp, pos_p, None, None, None, None, None, lb_all[l], *w)
        xs, ss = trunk_layer(xs, pos_s, cache_swa_k[l], cache_swa_v[l], state_hgrn[l], state_ssm[l],
                             state_conv[l], lb_all[l], *w)
        pst.append(sp)
        sst.append(ss)
    new_k_prompt = jnp.stack([s[0] for s in pst])
    new_v_prompt = jnp.stack([s[1] for s in pst])
    new_hgrn_prompt = jnp.stack([s[2] for s in pst])
    new_ssm_prompt = jnp.stack([s[3] for s in pst])
    new_conv_prompt = jnp.stack([s[4] for s in pst])
    new_k_sample = jnp.stack([s[0] for s in sst])
    new_v_sample = jnp.stack([s[1] for s in sst])
    new_hgrn_sample = jnp.stack([s[2] for s in sst])
    new_ssm_sample = jnp.stack([s[3] for s in sst])
    new_conv_sample = jnp.stack([s[4] for s in sst])
    return (xp, xs, new_k_prompt, new_v_prompt, new_hgrn_prompt, new_ssm_prompt, new_conv_prompt,
            new_k_sample, new_v_sample, new_hgrn_sample, new_ssm_sample, new_conv_sample)
```

```python
import functools
import math

import jax
import jax.numpy as jnp
import numpy as np
from jax import lax
from jax.experimental import pallas as pl
from jax.experimental.pallas import tpu as pltpu

F32 = jnp.float32
BF16 = jnp.bfloat16

D_MODEL = 2048
CHUNK = 64
EPS = 1e-6
PAST_LEN = 4096

W_A = 1024
HEAD_K_A = 128
N_HEADS_A = 8

N_Q_B = 16
N_KV_B = 4
HEAD_DIM_B = 64
W_B = 1024
KV_W_B = 256
WINDOW = 128
ROT_DIM = 16
ROPE_THETA = 500000.0
ATTN_SCALE = HEAD_DIM_B ** -0.5

W_C = 1024
HEAD_DIM_C = 64
N_HEADS_C = 16
N_GROUPS_C = 4
D_STATE = 128
CONV_W = 4
CONV_DIM = 2048

LANES = 128
SUB_BLOCK = 16
VMEM_LIMIT = 56 * 1024 * 1024

NT_DIMS = (((1,), (1,)), ((), ()))
TN_DIMS = (((0,), (0,)), ((), ()))


def _dot(a, b):
    return jnp.dot(a, b, preferred_element_type=F32)


def _dot_nt(a, b):
    return lax.dot_general(a, b, NT_DIMS, preferred_element_type=F32)


def _dot_tn(a, b):
    return lax.dot_general(a, b, TN_DIMS, preferred_element_type=F32)


def _prenorm_bf16(x, w):
    r = lax.rsqrt(jnp.mean(x * x, axis=-1, keepdims=True) + EPS)
    return ((x * r) * w).astype(BF16)


def _sigmoid(x):
    return 1.0 / (1.0 + jnp.exp(-x))


def _silu(x):
    return x * _sigmoid(x)


def _softplus0(x):
    return jnp.maximum(x, 0.0) + jnp.log1p(jnp.exp(-jnp.abs(x)))


def _tri_bf16(n):
    r = lax.broadcasted_iota(jnp.int32, (n, n), 0)
    c = lax.broadcasted_iota(jnp.int32, (n, n), 1)
    return (r >= c).astype(BF16)


def _cumsum_rows(tri, x):
    hi = x.astype(BF16)
    r1 = x - hi.astype(F32)
    mid = r1.astype(BF16)
    lo = (r1 - mid.astype(F32)).astype(BF16)
    return _dot(tri, hi) + _dot(tri, mid) + _dot(tri, lo)


def _bcast_rows(row, n):
    return jnp.broadcast_to(row, (n, row.shape[-1]))


def _hgrn_kernel(x_ref, nw_ref, w_ref, lb_ref, hn_ref, s0_ref, y_ref, snew_ref,
                 proj_s, ybuf_s, st_s, *, tl):
    i = pl.program_id(1)
    n_chunks = tl // CHUNK

    @pl.when(i == 0)
    def _():
        for hd in range(N_HEADS_A):
            st_s[hd] = s0_ref[0, hd].T

    h = _prenorm_bf16(x_ref[0], nw_ref[...])
    for j in range(4 * W_A // 256):
        res = _dot(h, w_ref[:, 256 * j:256 * (j + 1)])
        proj_s[2 * j] = res[:, :LANES]
        proj_s[2 * j + 1] = res[:, LANES:]

    tri = _tri_bf16(CHUNK)
    ones = jnp.ones((LANES, LANES), BF16)
    row64 = lax.broadcasted_iota(jnp.int32, (CHUNK, CHUNK), 0)
    col64 = lax.broadcasted_iota(jnp.int32, (CHUNK, CHUNK), 1)
    rblk = row64 // SUB_BLOCK
    t16 = lax.broadcasted_iota(jnp.int32, (SUB_BLOCK, LANES), 0)
    lane16 = lax.broadcasted_iota(jnp.int32, (SUB_BLOCK, LANES), 1)
    n_sub = CHUNK // SUB_BLOCK

    def body(idx, carry):
        hd = idx // n_chunks
        c = idx % n_chunks
        r0 = pl.multiple_of(c * CHUNK, CHUNK)
        aq = proj_s[hd, pl.ds(r0, CHUNK), :]
        z = proj_s[N_HEADS_A + hd, pl.ds(r0, CHUNK), :]
        v = proj_s[2 * N_HEADS_A + hd, pl.ds(r0, CHUNK), :]
        ag = proj_s[3 * N_HEADS_A + hd, pl.ds(r0, CHUNK), :]
        log_lb = lb_ref[0, hd]
        log1m_lb = lb_ref[1, hd]
        one_m_lb = lb_ref[2, hd]

        log_sig = jnp.minimum(z, 0.0) - jnp.log1p(jnp.exp(-jnp.abs(z)))
        cterm = log1m_lb + log_sig
        log_f = jnp.maximum(log_lb, cterm) + jnp.log1p(jnp.exp(-jnp.abs(log_lb - cterm)))
        k = one_m_lb * (1.0 / (1.0 + jnp.exp(z)))
        q = _silu(aq)

        b = _cumsum_rows(tri, log_f)
        b_last = b[CHUNK - 1:CHUNK, :]
        st = st_s[hd]

        o = _dot_nt((q * jnp.exp(b)).astype(BF16), st.astype(BF16))

        refs = [b[SUB_BLOCK * m - 1:SUB_BLOCK * m, :] for m in range(1, n_sub)]
        rmat = jnp.concatenate([b[:SUB_BLOCK]] + [_bcast_rows(r, SUB_BLOCK) for r in refs], axis=0)
        qt = q * jnp.exp(b - rmat)
        trow = lax.broadcasted_iota(jnp.int32, (CHUNK, LANES), 0)
        q_ext = jnp.concatenate(
            [jnp.where(trow // SUB_BLOCK == m, qt, 0.0) for m in range(1, n_sub)], axis=1).astype(BF16)
        k_ext = jnp.concatenate(
            [jnp.where(trow < SUB_BLOCK * m, k * jnp.exp(jnp.minimum(refs[m - 1] - b, 0.0)), 0.0)
             for m in range(1, n_sub)], axis=1).astype(BF16)
        a_off = _dot_nt(q_ext, k_ext)

        diag_rows = []
        for m in range(n_sub):
            sl = slice(SUB_BLOCK * m, SUB_BLOCK * (m + 1))
            bb, qb, kb = b[sl], q[sl], k[sl]
            ps = []
            for s in range(SUB_BLOCK):
                e = jnp.exp(jnp.minimum(bb - bb[s:s + 1], 0.0))
                ps.append(((qb * e) * kb[s:s + 1]).astype(BF16))
            rs = _dot(jnp.concatenate(ps, axis=0), ones)
            d = jnp.zeros((SUB_BLOCK, LANES), F32)
            for s in range(SUB_BLOCK):
                d = jnp.where(lane16 == SUB_BLOCK * m + s, rs[SUB_BLOCK * s:SUB_BLOCK * (s + 1)], d)
            d = jnp.where(t16 >= lane16 - SUB_BLOCK * m, d, 0.0)
            diag_rows.append(d)
        a_diag = jnp.concatenate(diag_rows, axis=0)[:, :CHUNK]
        a_full = jnp.where(rblk > col64 // SUB_BLOCK, a_off, a_diag)
        o = o + _dot(a_full.astype(BF16), v.astype(BF16))

        r = lax.rsqrt(jnp.mean(o * o, axis=-1, keepdims=True) + EPS)
        y = ((o * r) * hn_ref[...]) * _silu(ag)
        ybuf_s[hd, pl.ds(r0, CHUNK), :] = y.astype(BF16)

        kd = (k * jnp.exp(b_last - b)).astype(BF16)
        st_s[hd] = st * jnp.exp(b_last) + _dot_tn(v.astype(BF16), kd)
        return carry

    lax.fori_loop(0, N_HEADS_A * n_chunks, body, 0)

    for hd in range(N_HEADS_A):
        y_ref[0, :, LANES * hd:LANES * (hd + 1)] = ybuf_s[hd]

    @pl.when(i == pl.num_programs(1) - 1)
    def _():
        for hd in range(N_HEADS_A):
            snew_ref[0, hd] = st_s[hd].T


def _hgrn_call(x, nw, w_a, lb_pack, hn, s0, tl):
    bsz, seq, _ = x.shape
    grid = (bsz, seq // tl)
    const2 = lambda b, i: (0, 0)
    return pl.pallas_call(
        functools.partial(_hgrn_kernel, tl=tl),
        grid=grid,
        in_specs=[
            pl.BlockSpec((1, tl, D_MODEL), lambda b, i: (b, i, 0)),
            pl.BlockSpec((1, D_MODEL), const2),
            pl.BlockSpec((D_MODEL, 4 * W_A), const2, pipeline_mode=pl.Buffered(1)),
            pl.BlockSpec((3, N_HEADS_A, 1, HEAD_K_A), lambda b, i: (0, 0, 0, 0)),
            pl.BlockSpec((1, HEAD_K_A), const2),
            pl.BlockSpec((1, N_HEADS_A, HEAD_K_A, HEAD_K_A), lambda b, i: (b, 0, 0, 0)),
        ],
        out_specs=[
            pl.BlockSpec((1, tl, W_A), lambda b, i: (b, i, 0)),
            pl.BlockSpec((1, N_HEADS_A, HEAD_K_A, HEAD_K_A), lambda b, i: (b, 0, 0, 0)),
        ],
        out_shape=[
            jax.ShapeDtypeStruct((bsz, seq, W_A), BF16),
            jax.ShapeDtypeStruct((bsz, N_HEADS_A, HEAD_K_A, HEAD_K_A), F32),
        ],
        scratch_shapes=[
            pltpu.VMEM((4 * N_HEADS_A, tl, LANES), F32),
            pltpu.VMEM((N_HEADS_A, tl, LANES), BF16),
            pltpu.VMEM((N_HEADS_A, HEAD_K_A, HEAD_K_A), F32),
        ],
        compiler_params=pltpu.CompilerParams(
            dimension_semantics=("arbitrary", "arbitrary"), vmem_limit_bytes=VMEM_LIMIT),
        name="hgrn_mixer",
    )(x, nw, w_a, lb_pack, hn, s0)


def _rotate(xs, cos, sin_a, sin_b):
    return xs * cos + pltpu.roll(xs, LANES - ROT_DIM // 2, 1) * sin_a + pltpu.roll(xs, ROT_DIM // 2, 1) * sin_b


def _swa_kernel(*refs, tl, has_cache):
    if has_cache:
        (x_ref, nw_ref, w_ref, cos_ref, sa_ref, sb_ref, sink_ref, kc_ref, vc_ref,
         y_ref, knew_ref, vnew_ref, q_s, g_s, kbuf, vbuf) = refs
    else:
        (x_ref, nw_ref, w_ref, cos_ref, sa_ref, sb_ref, sink_ref,
         y_ref, knew_ref, vnew_ref, q_s, g_s, kbuf, vbuf) = refs
    i = pl.program_id(1)
    n_chunks = tl // CHUNK
    band = WINDOW + CHUNK
    kvw = 2 * KV_W_B

    @pl.when(i == 0)
    def _():
        if has_cache:
            kbuf[0:WINDOW, :] = kc_ref[0]
            vbuf[0:WINDOW, :] = vc_ref[0]
        else:
            kbuf[0:WINDOW, :] = jnp.zeros((WINDOW, kvw), F32)
            vbuf[0:WINDOW, :] = jnp.zeros((WINDOW, kvw), F32)

    h = _prenorm_bf16(x_ref[0], nw_ref[...])
    cos, sin_a, sin_b = cos_ref[...], sa_ref[...], sb_ref[...]
    for j in range(W_B // 256):
        res = _dot(h, w_ref[:, 256 * j:256 * (j + 1)])
        for u in range(2):
            q_s[:, 256 * j + LANES * u:256 * j + LANES * (u + 1)] = _rotate(
                res[:, LANES * u:LANES * (u + 1)], cos, sin_a, sin_b)
    for j in range(kvw // 256):
        res = _dot(h, w_ref[:, W_B + 256 * j:W_B + 256 * (j + 1)])
        for u in range(2):
            kbuf[WINDOW:WINDOW + tl, 256 * j + LANES * u:256 * j + LANES * (u + 1)] = _rotate(
                res[:, LANES * u:LANES * (u + 1)], cos, sin_a, sin_b)
    for j in range(kvw // 256):
        vbuf[WINDOW:WINDOW + tl, 256 * j:256 * (j + 1)] = _dot(
            h, w_ref[:, W_B + kvw + 256 * j:W_B + kvw + 256 * (j + 1)])
    for j in range(W_B // 256):
        g_s[:, 256 * j:256 * (j + 1)] = _dot(
            h, w_ref[:, W_B + 2 * kvw + 256 * j:W_B + 2 * kvw + 256 * (j + 1)])

    lane = lax.broadcasted_iota(jnp.int32, (CHUNK, LANES), 1)
    low = lane < HEAD_DIM_B
    row2 = lax.broadcasted_iota(jnp.int32, (2 * CHUNK, 1), 0)
    col_band = lax.broadcasted_iota(jnp.int32, (2 * CHUNK, band), 1)

    def body(c, carry):
        r0 = pl.multiple_of(c * CHUNK, CHUNK)
        for j in range(N_Q_B // 2):
            g = j // 2
            qp = q_s[pl.ds(r0, CHUNK), LANES * j:LANES * (j + 1)]
            q2 = jnp.concatenate([jnp.where(low, qp, 0.0), jnp.where(low, 0.0, qp)], axis=0).astype(BF16)
            kb = kbuf[pl.ds(r0, band), LANES * g:LANES * (g + 1)].astype(BF16)
            vb = vbuf[pl.ds(r0, band), LANES * g:LANES * (g + 1)].astype(BF16)
            s = _dot_nt(q2, kb) * ATTN_SCALE
            if not has_cache:
                key_pos = i * tl + r0 - WINDOW + col_band
                s = jnp.where(key_pos >= 0, s, -jnp.inf)
            sk = jnp.where(row2 < CHUNK, sink_ref[2 * j:2 * j + 1, 0:1], sink_ref[2 * j + 1:2 * j + 2, 0:1])
            m = jnp.maximum(jnp.max(s, axis=-1, keepdims=True), sk)
            p = jnp.exp(s - m)
            den = jnp.sum(p, axis=-1, keepdims=True) + jnp.exp(sk - m)
            o2 = _dot(p.astype(BF16), vb) / den
            o = jnp.where(low, o2[:CHUNK], o2[CHUNK:])
            y = o * _silu(g_s[pl.ds(r0, CHUNK), LANES * j:LANES * (j + 1)])
            y_ref[0, pl.ds(r0, CHUNK), LANES * j:LANES * (j + 1)] = y.astype(BF16)
        return carry

    lax.fori_loop(0, n_chunks, body, 0)

    ktail = kbuf[tl:tl + WINDOW, :]
    vtail = vbuf[tl:tl + WINDOW, :]
    kbuf[0:WINDOW, :] = ktail
    vbuf[0:WINDOW, :] = vtail

    @pl.when(i == pl.num_programs(1) - 1)
    def _():
        low_w = lax.broadcasted_iota(jnp.int32, (WINDOW, LANES), 1) < HEAD_DIM_B
        for u in range(KV_W_B // LANES):
            knew_ref[0, :, LANES * u:LANES * (u + 1)] = jnp.where(
                low_w, ktail[:, 2 * LANES * u:2 * LANES * u + LANES],
                ktail[:, 2 * LANES * u + LANES:2 * LANES * (u + 1)])
            vnew_ref[0, :, LANES * u:LANES * (u + 1)] = jnp.where(
                low_w, vtail[:, 2 * LANES * u:2 * LANES * u + LANES],
                vtail[:, 2 * LANES * u + LANES:2 * LANES * (u + 1)])


def _swa_call(x, nw, w_b, cos, sin_a, sin_b, sinks, kc, vc, tl):
    bsz, seq, _ = x.shape
    has_cache = kc is not None
    grid = (bsz, seq // tl)
    const2 = lambda b, i: (0, 0)
    wcols = w_b.shape[1]
    in_specs = [
        pl.BlockSpec((1, tl, D_MODEL), lambda b, i: (b, i, 0)),
        pl.BlockSpec((1, D_MODEL), const2),
        pl.BlockSpec((D_MODEL, wcols), const2, pipeline_mode=pl.Buffered(1)),
        pl.BlockSpec((tl, LANES), lambda b, i: (i, 0)),
        pl.BlockSpec((tl, LANES), lambda b, i: (i, 0)),
        pl.BlockSpec((tl, LANES), lambda b, i: (i, 0)),
        pl.BlockSpec((N_Q_B, LANES), const2),
    ]
    args = [x, nw, w_b, cos, sin_a, sin_b, sinks]
    if has_cache:
        in_specs += [pl.BlockSpec((1, WINDOW, 2 * KV_W_B), lambda b, i: (b, 0, 0))] * 2
        args += [kc, vc]
    return pl.pallas_call(
        functools.partial(_swa_kernel, tl=tl, has_cache=has_cache),
        grid=grid,
        in_specs=in_specs,
        out_specs=[
            pl.BlockSpec((1, tl, W_B), lambda b, i: (b, i, 0)),
            pl.BlockSpec((1, WINDOW, KV_W_B), lambda b, i: (b, 0, 0)),
            pl.BlockSpec((1, WINDOW, KV_W_B), lambda b, i: (b, 0, 0)),
        ],
        out_shape=[
            jax.ShapeDtypeStruct((bsz, seq, W_B), BF16),
            jax.ShapeDtypeStruct((bsz, WINDOW, KV_W_B), F32),
            jax.ShapeDtypeStruct((bsz, WINDOW, KV_W_B), F32),
        ],
        scratch_shapes=[
            pltpu.VMEM((tl, W_B), F32),
            pltpu.VMEM((tl, W_B), F32),
            pltpu.VMEM((WINDOW + tl, 2 * KV_W_B), F32),
            pltpu.VMEM((WINDOW + tl, 2 * KV_W_B), F32),
        ],
        compiler_params=pltpu.CompilerParams(
            dimension_semantics=("arbitrary", "arbitrary"), vmem_limit_bytes=VMEM_LIMIT),
        name="swa_mixer",
    )(*args)


CONV_PAD = 8


def _ssd_kernel(x_ref, nw_ref, w_ref, cw_ref, cb_ref, dtb_ref, alog_ref, dsk_ref, sn_ref, conv0_ref, s0_ref,
                y_ref, snew_ref, convnew_ref,
                ubuf, z_s, xc_s, xdt_s, bc_s, la_s, st_s, *, tl):
    i = pl.program_id(1)
    n_chunks = tl // CHUNK
    gw = N_GROUPS_C * D_STATE
    hp = W_C // N_GROUPS_C

    @pl.when(i == 0)
    def _():
        ubuf[CONV_PAD - (CONV_W - 1):CONV_PAD, :] = conv0_ref[0]
        for g in range(N_GROUPS_C):
            st_s[g] = s0_ref[0, hp * g:hp * (g + 1), :].T

    h = _prenorm_bf16(x_ref[0], nw_ref[...])
    for j in range(W_C // 256):
        z_s[:, 256 * j:256 * (j + 1)] = _dot(h, w_ref[:, 256 * j:256 * (j + 1)])
    for j in range(CONV_DIM // 256):
        ubuf[CONV_PAD:CONV_PAD + tl, 256 * j:256 * (j + 1)] = _dot(
            h, w_ref[:, W_C + 256 * j:W_C + 256 * (j + 1)])
    for j in range(W_C // 256):
        sl = slice(256 * j, 256 * (j + 1))
        dt = _softplus0(_dot(h, w_ref[:, W_C + CONV_DIM + 256 * j:W_C + CONV_DIM + 256 * (j + 1)])
                        + dtb_ref[:, sl])
        la_s[:, sl] = -dt * jnp.exp(alog_ref[:, sl])
        xdt_s[:, sl] = dt

    for js in range(CONV_DIM // LANES):
        sl = slice(LANES * js, LANES * (js + 1))
        acc = cb_ref[:, sl]
        for t in range(CONV_W):
            lo = CONV_PAD - (CONV_W - 1) + t
            acc = acc + ubuf[lo:lo + tl, sl] * cw_ref[t:t + 1, sl]
        act = _silu(acc)
        if js < W_C // LANES:
            xc_s[:, sl] = act
            xdt_s[:, sl] = act * xdt_s[:, sl]
        else:
            bc_s[:, LANES * js - W_C:LANES * (js + 1) - W_C] = act

    tail = ubuf[tl + CONV_PAD - (CONV_W - 1):tl + CONV_PAD, :]
    ubuf[CONV_PAD - (CONV_W - 1):CONV_PAD, :] = tail

    tri = _tri_bf16(CHUNK)
    trow = lax.broadcasted_iota(jnp.int32, (CHUNK, LANES), 0)
    lane = lax.broadcasted_iota(jnp.int32, (CHUNK, LANES), 1)
    s_of_lane = lane % CHUNK
    causal = trow >= s_of_lane
    row2 = lax.broadcasted_iota(jnp.int32, (2 * CHUNK, LANES), 0)
    lane2 = lax.broadcasted_iota(jnp.int32, (2 * CHUNK, LANES), 1)
    blockdiag = (row2 < CHUNK) == (lane2 < HEAD_DIM_C)

    def body(c, carry):
        r0 = pl.multiple_of(c * CHUNK, CHUNK)
        rows = pl.ds(r0, CHUNK)
        for g in range(N_GROUPS_C):
            bg = bc_s[rows, D_STATE * g:D_STATE * (g + 1)].astype(BF16)
            cg = bc_s[rows, gw + D_STATE * g:gw + D_STATE * (g + 1)].astype(BF16)
            cb2 = _dot_nt(cg, jnp.concatenate([bg, bg], axis=0))
            st = st_s[g]
            y_state = _dot(cg, st.astype(BF16))
            o_slabs, xd_slabs, e_last = [], [], []
            for u in range(2):
                sl = slice(hp * g + LANES * u, hp * g + LANES * (u + 1))
                la = la_s[rows, sl]
                b = _cumsum_rows(tri, la)
                b_row = jnp.sum(jnp.where(trow <= s_of_lane, la, 0.0), axis=0, keepdims=True)
                decay = jnp.where(causal, jnp.exp(jnp.minimum(b - b_row, 0.0)), 0.0)
                wmat = (cb2 * decay).astype(BF16)
                xdt = xdt_s[rows, sl]
                xbd = jnp.where(blockdiag, jnp.concatenate([xdt, xdt], axis=0), 0.0).astype(BF16)
                y = _dot(wmat, xbd) + y_state[:, LANES * u:LANES * (u + 1)] * jnp.exp(b)
                o = (y + dsk_ref[:, sl] * xc_s[rows, sl]) * _silu(z_s[rows, sl])
                o_slabs.append(o)
                b_last = b[CHUNK - 1:CHUNK, :]
                xd_slabs.append((xdt * jnp.exp(b_last - b)).astype(BF16))
                e_last.append(jnp.exp(b_last))
            ms = (jnp.sum(o_slabs[0] * o_slabs[0], axis=-1, keepdims=True)
                  + jnp.sum(o_slabs[1] * o_slabs[1], axis=-1, keepdims=True)) * (1.0 / hp)
            r = lax.rsqrt(ms + EPS)
            for u in range(2):
                sl = slice(hp * g + LANES * u, hp * g + LANES * (u + 1))
                y_ref[0, rows, sl] = ((o_slabs[u] * r) * sn_ref[:, sl]).astype(BF16)
            st_s[g] = (st * jnp.concatenate(e_last, axis=1)
                       + _dot_tn(bg, jnp.concatenate(xd_slabs, axis=1)))
        return carry

    lax.fori_loop(0, n_chunks, body, 0)

    @pl.when(i == pl.num_programs(1) - 1)
    def _():
        convnew_ref[0] = tail
        for g in range(N_GROUPS_C):
            snew_ref[0, hp * g:hp * (g + 1), :] = st_s[g].T


def _ssd_call(x, nw, w_c, cw, cb, dtb, alog, dsk, sn, conv0, s0, tl):
    bsz, seq, _ = x.shape
    grid = (bsz, seq // tl)
    const2 = lambda b, i: (0, 0)
    wcols = w_c.shape[1]
    return pl.pallas_call(
        functools.partial(_ssd_kernel, tl=tl),
        grid=grid,
        in_specs=[
            pl.BlockSpec((1, tl, D_MODEL), lambda b, i: (b, i, 0)),
            pl.BlockSpec((1, D_MODEL), const2),
            pl.BlockSpec((D_MODEL, wcols), const2, pipeline_mode=pl.Buffered(1)),
            pl.BlockSpec((CONV_W, CONV_DIM), const2),
            pl.BlockSpec((1, CONV_DIM), const2),
            pl.BlockSpec((1, W_C), const2),
            pl.BlockSpec((1, W_C), const2),
            pl.BlockSpec((1, W_C), const2),
            pl.BlockSpec((1, W_C), const2),
            pl.BlockSpec((1, CONV_W - 1, CONV_DIM), lambda b, i: (b, 0, 0)),
            pl.BlockSpec((1, W_C, D_STATE), lambda b, i: (b, 0, 0)),
        ],
        out_specs=[
            pl.BlockSpec((1, tl, W_C), lambda b, i: (b, i, 0)),
            pl.BlockSpec((1, W_C, D_STATE), lambda b, i: (b, 0, 0)),
            pl.BlockSpec((1, CONV_W - 1, CONV_DIM), lambda b, i: (b, 0, 0)),
        ],
        out_shape=[
            jax.ShapeDtypeStruct((bsz, seq, W_C), BF16),
            jax.ShapeDtypeStruct((bsz, W_C, D_STATE), F32),
            jax.ShapeDtypeStruct((bsz, CONV_W - 1, CONV_DIM), F32),
        ],
        scratch_shapes=[
            pltpu.VMEM((CONV_PAD + tl, CONV_DIM), F32),
            pltpu.VMEM((tl, W_C), F32),
            pltpu.VMEM((tl, W_C), F32),
            pltpu.VMEM((tl, W_C), F32),
            pltpu.VMEM((tl, 2 * N_GROUPS_C * D_STATE), F32),
            pltpu.VMEM((tl, W_C), F32),
            pltpu.VMEM((N_GROUPS_C, D_STATE, W_C // N_GROUPS_C), F32),
        ],
        compiler_params=pltpu.CompilerParams(
            dimension_semantics=("arbitrary", "arbitrary"), vmem_limit_bytes=VMEM_LIMIT),
        name="ssd_mixer",
    )(x, nw, w_c, cw, cb, dtb, alog, dsk, sn, conv0, s0)


def _merge_kernel(x_ref, ya_ref, yb_ref, yc_ref, npre_ref, npost_ref, wg_ref, wbr_ref, wout_ref,
                  o_ref, h_s, acc_s):
    j = pl.program_id(1)

    @pl.when(j == 0)
    def _():
        h_s[...] = _prenorm_bf16(x_ref[...], npre_ref[...])
        acc_s[...] = jnp.zeros_like(acc_s)

    h = h_s[...]
    merged = None
    for bi, y_ref in enumerate((ya_ref, yb_ref, yc_ref)):
        term = _sigmoid(_dot(h, wg_ref[bi])) * _dot(y_ref[...], wbr_ref[bi])
        merged = term if merged is None else merged + term
    acc_s[...] += _dot(merged.astype(BF16), wout_ref[...])

    @pl.when(j == pl.num_programs(1) - 1)
    def _():
        out = acc_s[...]
        r = lax.rsqrt(jnp.mean(out * out, axis=-1, keepdims=True) + EPS)
        o_ref[...] = x_ref[...] + (out * r) * npost_ref[...]


def _merge_call(x2, ya, yb, yc, npre, npost, wg, wbr, wout, tm, tn):
    rows = x2.shape[0]
    grid = (rows // tm, D_MODEL // tn)
    const2 = lambda r, j: (0, 0)
    return pl.pallas_call(
        _merge_kernel,
        grid=grid,
        in_specs=[
            pl.BlockSpec((tm, D_MODEL), lambda r, j: (r, 0)),
            pl.BlockSpec((tm, W_A), lambda r, j: (r, 0)),
            pl.BlockSpec((tm, W_B), lambda r, j: (r, 0)),
            pl.BlockSpec((tm, W_C), lambda r, j: (r, 0)),
            pl.BlockSpec((1, D_MODEL), const2),
            pl.BlockSpec((1, D_MODEL), const2),
            pl.BlockSpec((3, D_MODEL, tn), lambda r, j: (0, 0, j)),
            pl.BlockSpec((3, W_A, tn), lambda r, j: (0, 0, j)),
            pl.BlockSpec((tn, D_MODEL), lambda r, j: (j, 0)),
        ],
        out_specs=pl.BlockSpec((tm, D_MODEL), lambda r, j: (r, 0)),
        out_shape=jax.ShapeDtypeStruct((rows, D_MODEL), F32),
        scratch_shapes=[
            pltpu.VMEM((tm, D_MODEL), BF16),
            pltpu.VMEM((tm, D_MODEL), F32),
        ],
        compiler_params=pltpu.CompilerParams(
            dimension_semantics=("arbitrary", "arbitrary"), vmem_limit_bytes=VMEM_LIMIT),
        name="merge_out",
    )(x2, ya, yb, yc, npre, npost, wg, wbr, wout)


def _rope_tables(pos):
    half = ROT_DIM // 2
    inv = jnp.power(ROPE_THETA, -jnp.arange(half, dtype=F32) / half)
    ang = pos.astype(F32)[:, None] * inv[None, :]
    cos, sin = jnp.cos(ang), jnp.sin(ang)
    n = pos.shape[0]
    ones = jnp.ones((n, HEAD_DIM_B - ROT_DIM), F32)
    zeros = jnp.zeros((n, HEAD_DIM_B - ROT_DIM), F32)
    zh = jnp.zeros((n, half), F32)
    cos_t = jnp.concatenate([cos, cos, ones], axis=1)
    sin_a = jnp.concatenate([-sin, zh, zeros], axis=1)
    sin_b = jnp.concatenate([zh, sin, zeros], axis=1)
    tile = lambda t: jnp.concatenate([t, t], axis=1)
    return tile(cos_t), tile(sin_a), tile(sin_b)


def _dup_heads(t):
    lead = t.shape[:-1]
    t4 = t.reshape(lead + (N_KV_B, 1, HEAD_DIM_B))
    return jnp.broadcast_to(t4, lead + (N_KV_B, 2, HEAD_DIM_B)).reshape(lead + (2 * KV_W_B,))


def _expand_heads(t):
    return jnp.repeat(t, HEAD_DIM_C, axis=-1)


def _layer(x, tl, tables, caches, lw):
    bsz, seq, _ = x.shape
    cos, sin_a, sin_b = tables
    kc, vc, s_hgrn, s_ssm, s_conv = caches
    if s_hgrn is None:
        s_hgrn = jnp.zeros((bsz, N_HEADS_A, HEAD_K_A, HEAD_K_A), F32)
        s_ssm = jnp.zeros((bsz, W_C, D_STATE), F32)
        s_conv = jnp.zeros((bsz, CONV_W - 1, CONV_DIM), F32)
    else:
        s_ssm = s_ssm.reshape(bsz, W_C, D_STATE)
        kc = _dup_heads(kc.reshape(bsz, WINDOW, KV_W_B))
        vc = _dup_heads(vc.reshape(bsz, WINDOW, KV_W_B))

    ya, hgrn_new = _hgrn_call(x, lw["norm_pre"], lw["w_a"], lw["lb_pack"], lw["hgrn_norm"], s_hgrn, tl)
    yb, k_new, v_new = _swa_call(x, lw["norm_pre"], lw["w_b"], cos, sin_a, sin_b, lw["sinks"], kc, vc, tl)
    yc, ssm_new, conv_new = _ssd_call(x, lw["norm_pre"], lw["w_c"], lw["conv_w"], lw["conv_b"], lw["dt_bias"],
                                      lw["a_log"], lw["d_skip"], lw["ssm_norm"], s_conv, s_ssm, tl)
    rows = bsz * seq
    tm = min(512, rows)
    x_new = _merge_call(x.reshape(rows, D_MODEL), ya.reshape(rows, W_A), yb.reshape(rows, W_B),
                        yc.reshape(rows, W_C), lw["norm_pre"], lw["norm_post"], lw["w_g"], lw["w_br"],
                        lw["w_out"], tm, 256)
    states = (k_new.reshape(bsz, WINDOW, N_KV_B, HEAD_DIM_B), v_new.reshape(bsz, WINDOW, N_KV_B, HEAD_DIM_B),
              hgrn_new, ssm_new.reshape(bsz, N_HEADS_C, HEAD_DIM_C, D_STATE), conv_new)
    return x_new.reshape(bsz, seq, D_MODEL), states


def kernel(x_prompt, x_sample, cache_swa_k, cache_swa_v, state_hgrn, state_ssm, state_conv, norm_pre, norm_post, w_in, hgrn_lb_logits, hgrn_norm, swa_sinks, conv_w, conv_b, dt_bias, a_log, d_skip, ssm_norm, w_branch_a, w_branch_b, w_branch_c, w_out):
    depth = w_in.shape[0]
    lbp = jax.nn.softmax(hgrn_lb_logits.astype(F32), axis=0)
    lbc = jnp.cumsum(lbp, axis=0)
    lb_all = lbc - lbc[0:1]

    offs = np.cumsum([0, W_A, W_A, W_A, W_A, W_B, KV_W_B, KV_W_B, W_B, W_C, CONV_DIM, N_HEADS_C,
                      D_MODEL, D_MODEL, D_MODEL])
    col = lambda l, a, b: w_in[l, :, int(offs[a]):int(offs[b])]

    tables_p = _rope_tables(jnp.arange(x_prompt.shape[1], dtype=jnp.int32))
    tables_s = _rope_tables(PAST_LEN + jnp.arange(x_sample.shape[1], dtype=jnp.int32))

    xp, xs = x_prompt, x_sample
    pst, sst = [], []
    for l in range(depth):
        lb = lb_all[l].reshape(N_HEADS_A, 1, HEAD_K_A)
        lw = {
            "norm_pre": norm_pre[l].reshape(1, D_MODEL),
            "norm_post": norm_post[l].reshape(1, D_MODEL),
            "w_a": col(l, 0, 4).astype(BF16),
            "w_b": jnp.concatenate([col(l, 4, 5), _dup_heads(col(l, 5, 6)), _dup_heads(col(l, 6, 7)),
                                    col(l, 7, 8)], axis=1).astype(BF16),
            "w_c": jnp.concatenate([col(l, 8, 9), col(l, 9, 10), _expand_heads(col(l, 10, 11))],
                                   axis=1).astype(BF16),
            "w_g": jnp.stack([col(l, 11, 12), col(l, 12, 13), col(l, 13, 14)]).astype(BF16),
            "w_br": jnp.stack([w_branch_a[l], w_branch_b[l], w_branch_c[l]]).astype(BF16),
            "w_out": w_out[l].astype(BF16),
            "lb_pack": jnp.stack([jnp.log(lb), jnp.log1p(-lb), 1.0 - lb]),
            "hgrn_norm": hgrn_norm[l].reshape(1, HEAD_K_A),
            "sinks": jnp.broadcast_to(swa_sinks[l].astype(F32)[:, None], (N_Q_B, LANES)),
            "conv_w": conv_w[l],
            "conv_b": conv_b[l].reshape(1, CONV_DIM),
            "dt_bias": _expand_heads(dt_bias[l].astype(F32)).reshape(1, W_C),
            "a_log": _expand_heads(a_log[l].astype(F32)).reshape(1, W_C),
            "d_skip": _expand_heads(d_skip[l].astype(F32)).reshape(1, W_C),
            "ssm_norm": ssm_norm[l].reshape(1, W_C),
        }
        xp, sp = _layer(xp, 256, tables_p, (None, None, None, None, None), lw)
        xs, ss = _layer(xs, CHUNK, tables_s,
                        (cache_swa_k[l], cache_swa_v[l], state_hgrn[l], state_ssm[l], state_conv[l]), lw)
        pst.append(sp)
        sst.append(ss)

    stack = lambda sts, k: jnp.stack([s[k] for s in sts])
    return (xp, xs,
            stack(pst, 0), stack(pst, 1), stack(pst, 2), stack(pst, 3), stack(pst, 4),
            stack(sst, 0), stack(sst, 1), stack(sst, 2), stack(sst, 3), stack(sst, 4))
```

```python
import functools
import math

import jax
import jax.numpy as jnp
import numpy as np
from jax import lax
from jax.experimental import pallas as pl
from jax.experimental.pallas import tpu as pltpu

F32 = jnp.float32
BF16 = jnp.bfloat16

D_MODEL = 2048
CHUNK = 64
EPS = 1e-6
PAST_LEN = 4096

W_A = 1024
HEAD_K_A = 128
N_HEADS_A = 8

N_Q_B = 16
N_KV_B = 4
HEAD_DIM_B = 64
W_B = 1024
KV_W_B = 256
WINDOW = 128
ROT_DIM = 16
ROPE_THETA = 500000.0
ATTN_SCALE = HEAD_DIM_B ** -0.5

W_C = 1024
HEAD_DIM_C = 64
N_HEADS_C = 16
N_GROUPS_C = 4
D_STATE = 128
CONV_W = 4
CONV_DIM = 2048

LANES = 128
SUBLANES = 8
SUB_BLOCK = 16
HEAD_GROUPS_A = 1
LOG2E = math.log2(math.e)
VMEM_LIMIT = 56 * 1024 * 1024

NT_DIMS = (((1,), (1,)), ((), ()))
TN_DIMS = (((0,), (0,)), ((), ()))


def _dot(a, b):
    return jnp.dot(a, b, preferred_element_type=F32)


def _dot_nt(a, b):
    return lax.dot_general(a, b, NT_DIMS, preferred_element_type=F32)


def _dot_tn(a, b):
    return lax.dot_general(a, b, TN_DIMS, preferred_element_type=F32)


def _prenorm_bf16(x, w):
    r = lax.rsqrt(jnp.mean(x * x, axis=-1, keepdims=True) + EPS)
    return ((x * r) * w).astype(BF16)


def _sigmoid(x):
    return 1.0 / (1.0 + jnp.exp(-x))


def _silu(x):
    return x * _sigmoid(x)


def _softplus0(x):
    return jnp.maximum(x, 0.0) + jnp.log1p(jnp.exp(-jnp.abs(x)))


def _tri_bf16(n):
    r = lax.broadcasted_iota(jnp.int32, (n, n), 0)
    c = lax.broadcasted_iota(jnp.int32, (n, n), 1)
    return (r >= c).astype(BF16)


def _cumsum_rows(tri, x):
    hi = x.astype(BF16)
    r1 = x - hi.astype(F32)
    mid = r1.astype(BF16)
    lo = (r1 - mid.astype(F32)).astype(BF16)
    return _dot(tri, hi) + _dot(tri, mid) + _dot(tri, lo)


def _round_robin(gens):
    results = [None] * len(gens)
    live = list(range(len(gens)))
    while live:
        for n in list(live):
            try:
                next(gens[n])
            except StopIteration as stop:
                results[n] = stop.value
                live.remove(n)
    return results


def _hgrn_kernel(x_ref, nw_ref, w_ref, lb_ref, hn_ref, s0_ref, y_ref, snew_ref,
                 proj_s, ybuf_s, st_s, *, tl):
    i = pl.program_id(1)
    n_chunks = tl // CHUNK

    @pl.when(i == 0)
    def _():
        for hd in range(N_HEADS_A):
            st_s[hd] = s0_ref[0, hd].T

    h = _prenorm_bf16(x_ref[0], nw_ref[...])
    for j in range(4 * W_A // 256):
        res = _dot(h, w_ref[:, 256 * j:256 * (j + 1)])
        proj_s[2 * j] = res[:, :LANES]
        proj_s[2 * j + 1] = res[:, LANES:]

    tri = _tri_bf16(CHUNK)
    ones = jnp.ones((LANES, LANES), BF16)
    n_sub = CHUNK // SUB_BLOCK
    lane8 = lax.broadcasted_iota(jnp.int32, (SUBLANES, LANES), 1)
    sub8 = lax.broadcasted_iota(jnp.int32, (SUBLANES, LANES), 0)
    row64 = lax.broadcasted_iota(jnp.int32, (CHUNK, CHUNK), 0)
    col64 = lax.broadcasted_iota(jnp.int32, (CHUNK, CHUNK), 1)
    below_diag_block = row64 // SUB_BLOCK > col64 // SUB_BLOCK
    zero8 = jnp.zeros((SUBLANES, LANES), F32)

    def zeros(n):
        return jnp.zeros((n, LANES), F32)

    def piece(a, m, u):
        lo = SUB_BLOCK * m + SUBLANES * u
        return a[lo:lo + SUBLANES]

    def unit(hd, r0):
        rows = pl.ds(r0, CHUNK)
        aq = proj_s[hd, rows, :]
        z = proj_s[N_HEADS_A + hd, rows, :]
        v = proj_s[2 * N_HEADS_A + hd, rows, :].astype(BF16)
        ag = proj_s[3 * N_HEADS_A + hd, rows, :]
        log_lb = lb_ref[0, hd]
        log1m_lb = lb_ref[1, hd]

        z2 = z * LOG2E
        log_sig = jnp.minimum(z2, 0.0) - jnp.log2(1.0 + jnp.exp2(-jnp.abs(z2)))
        cterm = log1m_lb + log_sig
        log_f = jnp.maximum(log_lb, cterm) + jnp.log2(1.0 + jnp.exp2(-jnp.abs(log_lb - cterm)))
        log_k = cterm - z2
        q = _silu(aq)

        b2 = _cumsum_rows(tri, log_f)
        yield
        c2 = b2 - log_k
        b2_last = b2[CHUNK - 1:CHUNK, :]
        st = st_s[hd]

        o = _dot_nt((q * jnp.exp2(b2)).astype(BF16), st.astype(BF16))
        yield

        q_slabs, k_slabs = [], []
        for m in range(1, n_sub):
            lo = SUB_BLOCK * m
            ref = b2[lo - 1:lo, :]
            qm = q[lo:lo + SUB_BLOCK] * jnp.exp2(b2[lo:lo + SUB_BLOCK] - ref)
            q_parts = [zeros(lo), qm] + ([zeros(CHUNK - lo - SUB_BLOCK)] if lo + SUB_BLOCK < CHUNK else [])
            q_slabs.append(jnp.concatenate(q_parts, axis=0))
            k_slabs.append(jnp.concatenate([jnp.exp2(ref - c2[:lo]), zeros(CHUNK - lo)], axis=0))
        a_off = _dot_nt(jnp.concatenate(q_slabs, axis=1).astype(BF16),
                        jnp.concatenate(k_slabs, axis=1).astype(BF16))
        yield

        order = [(s, m, u) for s in range(SUB_BLOCK) for m in range(n_sub) for u in range(2)
                 if not (u == 0 and s >= SUBLANES)]
        pieces = []
        for s, m, u in order:
            cs = c2[SUB_BLOCK * m + s:SUB_BLOCK * m + s + 1]
            pieces.append(piece(q, m, u) * jnp.exp2(piece(b2, m, u) - cs))
        sums = _dot(jnp.concatenate(pieces, axis=0).astype(BF16), ones)
        yield
        d = {(m, u): zero8 for m in range(n_sub) for u in range(2)}
        for n, (s, m, u) in enumerate(order):
            d[(m, u)] = jnp.where(lane8 == s, sums[SUBLANES * n:SUBLANES * (n + 1)], d[(m, u)])
        diag_rows = []
        for m in range(n_sub):
            for u in range(2):
                dm = jnp.where(sub8 + SUBLANES * u >= lane8, d[(m, u)], 0.0)
                diag_rows.append(pltpu.roll(dm, SUB_BLOCK * m, 1) if m else dm)
        a_diag = jnp.concatenate(diag_rows, axis=0)[:, :CHUNK]
        a_full = jnp.where(below_diag_block, a_off, a_diag)
        o = o + _dot(a_full.astype(BF16), v)
        yield

        r = lax.rsqrt(jnp.mean(o * o, axis=-1, keepdims=True) + EPS)
        y = ((o * r) * hn_ref[...]) * _silu(ag)

        st_new = st * jnp.exp2(b2_last) + _dot_tn(v, jnp.exp2(b2_last - c2).astype(BF16))
        return y.astype(BF16), st_new

    def body(idx, carry):
        hg = idx // n_chunks
        c = idx % n_chunks
        r0 = pl.multiple_of(c * CHUNK, CHUNK)
        heads = [hg + HEAD_GROUPS_A * n for n in range(N_HEADS_A // HEAD_GROUPS_A)]
        results = _round_robin([unit(hd, r0) for hd in heads])
        for hd, (y, st_new) in zip(heads, results):
            ybuf_s[hd, pl.ds(r0, CHUNK), :] = y
            st_s[hd] = st_new
        return carry

    lax.fori_loop(0, HEAD_GROUPS_A * n_chunks, body, 0)

    for hd in range(N_HEADS_A):
        y_ref[0, :, LANES * hd:LANES * (hd + 1)] = ybuf_s[hd]

    @pl.when(i == pl.num_programs(1) - 1)
    def _():
        for hd in range(N_HEADS_A):
            snew_ref[0, hd] = st_s[hd].T


def _hgrn_call(x, nw, w_a, lb_pack, hn, s0, tl):
    bsz, seq, _ = x.shape
    grid = (bsz, seq // tl)
    const2 = lambda b, i: (0, 0)
    return pl.pallas_call(
        functools.partial(_hgrn_kernel, tl=tl),
        grid=grid,
        in_specs=[
            pl.BlockSpec((1, tl, D_MODEL), lambda b, i: (b, i, 0)),
            pl.BlockSpec((1, D_MODEL), const2),
            pl.BlockSpec((D_MODEL, 4 * W_A), const2, pipeline_mode=pl.Buffered(1)),
            pl.BlockSpec((2, N_HEADS_A, 1, HEAD_K_A), lambda b, i: (0, 0, 0, 0)),
            pl.BlockSpec((1, HEAD_K_A), const2),
            pl.BlockSpec((1, N_HEADS_A, HEAD_K_A, HEAD_K_A), lambda b, i: (b, 0, 0, 0)),
        ],
        out_specs=[
            pl.BlockSpec((1, tl, W_A), lambda b, i: (b, i, 0)),
            pl.BlockSpec((1, N_HEADS_A, HEAD_K_A, HEAD_K_A), lambda b, i: (b, 0, 0, 0)),
        ],
        out_shape=[
            jax.ShapeDtypeStruct((bsz, seq, W_A), BF16),
            jax.ShapeDtypeStruct((bsz, N_HEADS_A, HEAD_K_A, HEAD_K_A), F32),
        ],
        scratch_shapes=[
            pltpu.VMEM((4 * N_HEADS_A, tl, LANES), F32),
            pltpu.VMEM((N_HEADS_A, tl, LANES), BF16),
            pltpu.VMEM((N_HEADS_A, HEAD_K_A, HEAD_K_A), F32),
        ],
        compiler_params=pltpu.CompilerParams(
            dimension_semantics=("arbitrary", "arbitrary"), vmem_limit_bytes=VMEM_LIMIT),
        name="hgrn_mixer",
    )(x, nw, w_a, lb_pack, hn, s0)


def _rotate(xs, cos, sin_a, sin_b):
    return xs * cos + pltpu.roll(xs, LANES - ROT_DIM // 2, 1) * sin_a + pltpu.roll(xs, ROT_DIM // 2, 1) * sin_b


def _swa_kernel(*refs, tl, has_cache):
    if has_cache:
        (x_ref, nw_ref, w_ref, cos_ref, sa_ref, sb_ref, sink_ref, kc_ref, vc_ref,
         y_ref, knew_ref, vnew_ref, q_s, g_s, kbuf, vbuf) = refs
    else:
        (x_ref, nw_ref, w_ref, cos_ref, sa_ref, sb_ref, sink_ref,
         y_ref, knew_ref, vnew_ref, q_s, g_s, kbuf, vbuf) = refs
    i = pl.program_id(1)
    n_chunks = tl // CHUNK
    band = WINDOW + CHUNK
    kvw = 2 * KV_W_B

    @pl.when(i == 0)
    def _():
        if has_cache:
            kbuf[0:WINDOW, :] = kc_ref[0]
            vbuf[0:WINDOW, :] = vc_ref[0]
        else:
            kbuf[0:WINDOW, :] = jnp.zeros((WINDOW, kvw), F32)
            vbuf[0:WINDOW, :] = jnp.zeros((WINDOW, kvw), F32)

    h = _prenorm_bf16(x_ref[0], nw_ref[...])
    cos, sin_a, sin_b = cos_ref[...], sa_ref[...], sb_ref[...]
    for j in range(W_B // 256):
        res = _dot(h, w_ref[:, 256 * j:256 * (j + 1)])
        for u in range(2):
            q_s[:, 256 * j + LANES * u:256 * j + LANES * (u + 1)] = _rotate(
                res[:, LANES * u:LANES * (u + 1)], cos, sin_a, sin_b)
    for j in range(kvw // 256):
        res = _dot(h, w_ref[:, W_B + 256 * j:W_B + 256 * (j + 1)])
        for u in range(2):
            kbuf[WINDOW:WINDOW + tl, 256 * j + LANES * u:256 * j + LANES * (u + 1)] = _rotate(
                res[:, LANES * u:LANES * (u + 1)], cos, sin_a, sin_b)
    for j in range(kvw // 256):
        vbuf[WINDOW:WINDOW + tl, 256 * j:256 * (j + 1)] = _dot(
            h, w_ref[:, W_B + kvw + 256 * j:W_B + kvw + 256 * (j + 1)])
    for j in range(W_B // 256):
        g_s[:, 256 * j:256 * (j + 1)] = _dot(
            h, w_ref[:, W_B + 2 * kvw + 256 * j:W_B + 2 * kvw + 256 * (j + 1)])

    lane = lax.broadcasted_iota(jnp.int32, (CHUNK, LANES), 1)
    low = lane < HEAD_DIM_B
    low1 = lax.broadcasted_iota(jnp.int32, (1, LANES), 1) < HEAD_DIM_B
    key_row = lax.broadcasted_iota(jnp.int32, (band, LANES), 0)

    def body(c, carry):
        r0 = pl.multiple_of(c * CHUNK, CHUNK)

        def pair(j):
            g = j // 2
            qp = q_s[pl.ds(r0, CHUNK), LANES * j:LANES * (j + 1)]
            q2 = jnp.concatenate([jnp.where(low, qp, 0.0), jnp.where(low, 0.0, qp)], axis=0).astype(BF16)
            kb = kbuf[pl.ds(r0, band), LANES * g:LANES * (g + 1)].astype(BF16)
            vb = vbuf[pl.ds(r0, band), LANES * g:LANES * (g + 1)].astype(BF16)
            s = _dot_nt(kb, q2) * ATTN_SCALE
            yield
            if not has_cache:
                s = jnp.where(i * tl + r0 - WINDOW + key_row >= 0, s, -jnp.inf)
            sk = jnp.where(low1, sink_ref[2 * j:2 * j + 1, :], sink_ref[2 * j + 1:2 * j + 2, :])
            m = jnp.maximum(jnp.max(s, axis=0, keepdims=True), sk)
            p = jnp.exp(s - m)
            den = jnp.sum(p, axis=0, keepdims=True) + jnp.exp(sk - m)
            o2 = _dot_tn((p * (1.0 / den)).astype(BF16), vb)
            yield
            o = jnp.where(low, o2[:CHUNK], o2[CHUNK:])
            y = o * _silu(g_s[pl.ds(r0, CHUNK), LANES * j:LANES * (j + 1)])
            return y.astype(BF16)

        ys = _round_robin([pair(j) for j in range(N_Q_B // 2)])
        for j, y in enumerate(ys):
            y_ref[0, pl.ds(r0, CHUNK), LANES * j:LANES * (j + 1)] = y
        return carry

    lax.fori_loop(0, n_chunks, body, 0)

    ktail = kbuf[tl:tl + WINDOW, :]
    vtail = vbuf[tl:tl + WINDOW, :]
    kbuf[0:WINDOW, :] = ktail
    vbuf[0:WINDOW, :] = vtail

    @pl.when(i == pl.num_programs(1) - 1)
    def _():
        low_w = lax.broadcasted_iota(jnp.int32, (WINDOW, LANES), 1) < HEAD_DIM_B
        for u in range(KV_W_B // LANES):
            knew_ref[0, :, LANES * u:LANES * (u + 1)] = jnp.where(
                low_w, ktail[:, 2 * LANES * u:2 * LANES * u + LANES],
                ktail[:, 2 * LANES * u + LANES:2 * LANES * (u + 1)])
            vnew_ref[0, :, LANES * u:LANES * (u + 1)] = jnp.where(
                low_w, vtail[:, 2 * LANES * u:2 * LANES * u + LANES],
                vtail[:, 2 * LANES * u + LANES:2 * LANES * (u + 1)])


def _swa_call(x, nw, w_b, cos, sin_a, sin_b, sinks, kc, vc, tl):
    bsz, seq, _ = x.shape
    has_cache = kc is not None
    grid = (bsz, seq // tl)
    const2 = lambda b, i: (0, 0)
    wcols = w_b.shape[1]
    in_specs = [
        pl.BlockSpec((1, tl, D_MODEL), lambda b, i: (b, i, 0)),
        pl.BlockSpec((1, D_MODEL), const2),
        pl.BlockSpec((D_MODEL, wcols), const2, pipeline_mode=pl.Buffered(1)),
        pl.BlockSpec((tl, LANES), lambda b, i: (i, 0)),
        pl.BlockSpec((tl, LANES), lambda b, i: (i, 0)),
        pl.BlockSpec((tl, LANES), lambda b, i: (i, 0)),
        pl.BlockSpec((N_Q_B, LANES), const2),
    ]
    args = [x, nw, w_b, cos, sin_a, sin_b, sinks]
    if has_cache:
        in_specs += [pl.BlockSpec((1, WINDOW, 2 * KV_W_B), lambda b, i: (b, 0, 0))] * 2
        args += [kc, vc]
    return pl.pallas_call(
        functools.partial(_swa_kernel, tl=tl, has_cache=has_cache),
        grid=grid,
        in_specs=in_specs,
        out_specs=[
            pl.BlockSpec((1, tl, W_B), lambda b, i: (b, i, 0)),
            pl.BlockSpec((1, WINDOW, KV_W_B), lambda b, i: (b, 0, 0)),
            pl.BlockSpec((1, WINDOW, KV_W_B), lambda b, i: (b, 0, 0)),
        ],
        out_shape=[
            jax.ShapeDtypeStruct((bsz, seq, W_B), BF16),
            jax.ShapeDtypeStruct((bsz, WINDOW, KV_W_B), F32),
            jax.ShapeDtypeStruct((bsz, WINDOW, KV_W_B), F32),
        ],
        scratch_shapes=[
            pltpu.VMEM((tl, W_B), F32),
            pltpu.VMEM((tl, W_B), F32),
            pltpu.VMEM((WINDOW + tl, 2 * KV_W_B), F32),
            pltpu.VMEM((WINDOW + tl, 2 * KV_W_B), F32),
        ],
        compiler_params=pltpu.CompilerParams(
            dimension_semantics=("arbitrary", "arbitrary"), vmem_limit_bytes=VMEM_LIMIT),
        name="swa_mixer",
    )(*args)


CONV_PAD = 8


def _ssd_kernel(x_ref, nw_ref, w_ref, cw_ref, cb_ref, dtb_ref, alog_ref, dsk_ref, sn_ref, conv0_ref, s0_ref,
                y_ref, snew_ref, convnew_ref,
                ubuf, z_s, xc_s, xdt_s, bc_s, la_s, st_s, *, tl):
    i = pl.program_id(1)
    n_chunks = tl // CHUNK
    gw = N_GROUPS_C * D_STATE
    hp = W_C // N_GROUPS_C

    @pl.when(i == 0)
    def _():
        ubuf[CONV_PAD - (CONV_W - 1):CONV_PAD, :] = conv0_ref[0]
        for g in range(N_GROUPS_C):
            st_s[g] = s0_ref[0, hp * g:hp * (g + 1), :].T

    h = _prenorm_bf16(x_ref[0], nw_ref[...])
    for j in range(W_C // 256):
        z_s[:, 256 * j:256 * (j + 1)] = _dot(h, w_ref[:, 256 * j:256 * (j + 1)])
    for j in range(CONV_DIM // 256):
        ubuf[CONV_PAD:CONV_PAD + tl, 256 * j:256 * (j + 1)] = _dot(
            h, w_ref[:, W_C + 256 * j:W_C + 256 * (j + 1)])
    for j in range(W_C // 256):
        sl = slice(256 * j, 256 * (j + 1))
        dt = _softplus0(_dot(h, w_ref[:, W_C + CONV_DIM + 256 * j:W_C + CONV_DIM + 256 * (j + 1)])
                        + dtb_ref[:, sl])
        la_s[:, sl] = -dt * jnp.exp(alog_ref[:, sl])
        xdt_s[:, sl] = dt

    for js in range(CONV_DIM // LANES):
        sl = slice(LANES * js, LANES * (js + 1))
        acc = cb_ref[:, sl]
        for t in range(CONV_W):
            lo = CONV_PAD - (CONV_W - 1) + t
            acc = acc + ubuf[lo:lo + tl, sl] * cw_ref[t:t + 1, sl]
        act = _silu(acc)
        if js < W_C // LANES:
            xc_s[:, sl] = act
            xdt_s[:, sl] = act * xdt_s[:, sl]
        else:
            bc_s[:, LANES * js - W_C:LANES * (js + 1) - W_C] = act

    tail = ubuf[tl + CONV_PAD - (CONV_W - 1):tl + CONV_PAD, :]
    ubuf[CONV_PAD - (CONV_W - 1):CONV_PAD, :] = tail

    tri = _tri_bf16(CHUNK)
    trow = lax.broadcasted_iota(jnp.int32, (CHUNK, LANES), 0)
    lane = lax.broadcasted_iota(jnp.int32, (CHUNK, LANES), 1)
    s_of_lane = lane % CHUNK
    causal = trow >= s_of_lane
    row2 = lax.broadcasted_iota(jnp.int32, (2 * CHUNK, LANES), 0)
    lane2 = lax.broadcasted_iota(jnp.int32, (2 * CHUNK, LANES), 1)
    blockdiag = (row2 < CHUNK) == (lane2 < HEAD_DIM_C)

    def body(c, carry):
        r0 = pl.multiple_of(c * CHUNK, CHUNK)
        rows = pl.ds(r0, CHUNK)

        def group(g):
            gsl = slice(hp * g, hp * (g + 1))
            bg = bc_s[rows, D_STATE * g:D_STATE * (g + 1)].astype(BF16)
            cg = bc_s[rows, gw + D_STATE * g:gw + D_STATE * (g + 1)].astype(BF16)
            cb2 = _dot_nt(cg, jnp.concatenate([bg, bg], axis=0))
            yield
            st = st_s[g]
            y_state = _dot(cg, st.astype(BF16))
            yield
            la = la_s[rows, gsl]
            b = _cumsum_rows(tri, la)
            yield
            xdt = xdt_s[rows, gsl]
            ys = []
            for u in range(2):
                usl = slice(LANES * u, LANES * (u + 1))
                b_row = jnp.sum(jnp.where(trow <= s_of_lane, la[:, usl], 0.0), axis=0, keepdims=True)
                decay = jnp.where(causal, jnp.exp(b[:, usl] - b_row), 0.0)
                wmat = (cb2 * decay).astype(BF16)
                xbd = jnp.where(blockdiag, jnp.concatenate([xdt[:, usl], xdt[:, usl]], axis=0), 0.0).astype(BF16)
                ys.append(_dot(wmat, xbd))
                yield
            y = jnp.concatenate(ys, axis=1) + y_state * jnp.exp(b)
            o = (y + dsk_ref[:, gsl] * xc_s[rows, gsl]) * _silu(z_s[rows, gsl])
            r = lax.rsqrt(jnp.mean(o * o, axis=-1, keepdims=True) + EPS)
            out = ((o * r) * sn_ref[:, gsl]).astype(BF16)
            b_last = b[CHUNK - 1:CHUNK, :]
            st_new = st * jnp.exp(b_last) + _dot_tn(bg, (xdt * jnp.exp(b_last - b)).astype(BF16))
            return out, st_new

        results = _round_robin([group(g) for g in range(N_GROUPS_C)])
        for g, (out, st_new) in enumerate(results):
            y_ref[0, rows, hp * g:hp * (g + 1)] = out
            st_s[g] = st_new
        return carry

    lax.fori_loop(0, n_chunks, body, 0)

    @pl.when(i == pl.num_programs(1) - 1)
    def _():
        convnew_ref[0] = tail
        for g in range(N_GROUPS_C):
            snew_ref[0, hp * g:hp * (g + 1), :] = st_s[g].T


def _ssd_call(x, nw, w_c, cw, cb, dtb, alog, dsk, sn, conv0, s0, tl):
    bsz, seq, _ = x.shape
    grid = (bsz, seq // tl)
    const2 = lambda b, i: (0, 0)
    wcols = w_c.shape[1]
    return pl.pallas_call(
        functools.partial(_ssd_kernel, tl=tl),
        grid=grid,
        in_specs=[
            pl.BlockSpec((1, tl, D_MODEL), lambda b, i: (b, i, 0)),
            pl.BlockSpec((1, D_MODEL), const2),
            pl.BlockSpec((D_MODEL, wcols), const2, pipeline_mode=pl.Buffered(1)),
            pl.BlockSpec((CONV_W, CONV_DIM), const2),
            pl.BlockSpec((1, CONV_DIM), const2),
            pl.BlockSpec((1, W_C), const2),
            pl.BlockSpec((1, W_C), const2),
            pl.BlockSpec((1, W_C), const2),
            pl.BlockSpec((1, W_C), const2),
            pl.BlockSpec((1, CONV_W - 1, CONV_DIM), lambda b, i: (b, 0, 0)),
            pl.BlockSpec((1, W_C, D_STATE), lambda b, i: (b, 0, 0)),
        ],
        out_specs=[
            pl.BlockSpec((1, tl, W_C), lambda b, i: (b, i, 0)),
            pl.BlockSpec((1, W_C, D_STATE), lambda b, i: (b, 0, 0)),
            pl.BlockSpec((1, CONV_W - 1, CONV_DIM), lambda b, i: (b, 0, 0)),
        ],
        out_shape=[
            jax.ShapeDtypeStruct((bsz, seq, W_C), BF16),
            jax.ShapeDtypeStruct((bsz, W_C, D_STATE), F32),
            jax.ShapeDtypeStruct((bsz, CONV_W - 1, CONV_DIM), F32),
        ],
        scratch_shapes=[
            pltpu.VMEM((CONV_PAD + tl, CONV_DIM), F32),
            pltpu.VMEM((tl, W_C), F32),
            pltpu.VMEM((tl, W_C), F32),
            pltpu.VMEM((tl, W_C), F32),
            pltpu.VMEM((tl, 2 * N_GROUPS_C * D_STATE), F32),
            pltpu.VMEM((tl, W_C), F32),
            pltpu.VMEM((N_GROUPS_C, D_STATE, W_C // N_GROUPS_C), F32),
        ],
        compiler_params=pltpu.CompilerParams(
            dimension_semantics=("arbitrary", "arbitrary"), vmem_limit_bytes=VMEM_LIMIT),
        name="ssd_mixer",
    )(x, nw, w_c, cw, cb, dtb, alog, dsk, sn, conv0, s0)


def _merge_kernel(x_ref, ya_ref, yb_ref, yc_ref, npre_ref, npost_ref, wg_ref, wbr_ref, wout_ref,
                  o_ref, h_s, m_s):
    j = pl.program_id(1)
    n_j = m_s.shape[0]

    @pl.when(j == 0)
    def _():
        h_s[...] = _prenorm_bf16(x_ref[...], npre_ref[...])

    h = h_s[...]
    merged = None
    for bi, y_ref in enumerate((ya_ref, yb_ref, yc_ref)):
        term = _sigmoid(_dot(h, wg_ref[bi])) * _dot(y_ref[...], wbr_ref[bi])
        merged = term if merged is None else merged + term
    m_s[j] = merged.astype(BF16)

    @pl.when(j == n_j - 1)
    def _():
        out = _dot(jnp.concatenate([m_s[n] for n in range(n_j)], axis=1), wout_ref[...])
        r = lax.rsqrt(jnp.mean(out * out, axis=-1, keepdims=True) + EPS)
        o_ref[...] = x_ref[...] + (out * r) * npost_ref[...]


def _merge_call(x2, ya, yb, yc, npre, npost, wg, wbr, wout, tm, tn):
    rows = x2.shape[0]
    grid = (rows // tm, D_MODEL // tn)
    const2 = lambda r, j: (0, 0)
    return pl.pallas_call(
        _merge_kernel,
        grid=grid,
        in_specs=[
            pl.BlockSpec((tm, D_MODEL), lambda r, j: (r, 0)),
            pl.BlockSpec((tm, W_A), lambda r, j: (r, 0)),
            pl.BlockSpec((tm, W_B), lambda r, j: (r, 0)),
            pl.BlockSpec((tm, W_C), lambda r, j: (r, 0)),
            pl.BlockSpec((1, D_MODEL), const2),
            pl.BlockSpec((1, D_MODEL), const2),
            pl.BlockSpec((3, D_MODEL, tn), lambda r, j: (0, 0, j)),
            pl.BlockSpec((3, W_A, tn), lambda r, j: (0, 0, j)),
            pl.BlockSpec((D_MODEL, D_MODEL), const2, pipeline_mode=pl.Buffered(1)),
        ],
        out_specs=pl.BlockSpec((tm, D_MODEL), lambda r, j: (r, 0)),
        out_shape=jax.ShapeDtypeStruct((rows, D_MODEL), F32),
        scratch_shapes=[
            pltpu.VMEM((tm, D_MODEL), BF16),
            pltpu.VMEM((D_MODEL // tn, tm, tn), BF16),
        ],
        compiler_params=pltpu.CompilerParams(
            dimension_semantics=("arbitrary", "arbitrary"), vmem_limit_bytes=VMEM_LIMIT),
        name="merge_out",
    )(x2, ya, yb, yc, npre, npost, wg, wbr, wout)


def _rope_tables(pos):
    half = ROT_DIM // 2
    inv = jnp.power(ROPE_THETA, -jnp.arange(half, dtype=F32) / half)
    ang = pos.astype(F32)[:, None] * inv[None, :]
    cos, sin = jnp.cos(ang), jnp.sin(ang)
    n = pos.shape[0]
    ones = jnp.ones((n, HEAD_DIM_B - ROT_DIM), F32)
    zeros = jnp.zeros((n, HEAD_DIM_B - ROT_DIM), F32)
    zh = jnp.zeros((n, half), F32)
    cos_t = jnp.concatenate([cos, cos, ones], axis=1)
    sin_a = jnp.concatenate([-sin, zh, zeros], axis=1)
    sin_b = jnp.concatenate([zh, sin, zeros], axis=1)
    tile = lambda t: jnp.concatenate([t, t], axis=1)
    return tile(cos_t), tile(sin_a), tile(sin_b)


def _dup_heads(t):
    lead = t.shape[:-1]
    t4 = t.reshape(lead + (N_KV_B, 1, HEAD_DIM_B))
    return jnp.broadcast_to(t4, lead + (N_KV_B, 2, HEAD_DIM_B)).reshape(lead + (2 * KV_W_B,))


def _expand_heads(t):
    return jnp.repeat(t, HEAD_DIM_C, axis=-1)


def _layer(x, tl, tables, caches, lw):
    bsz, seq, _ = x.shape
    cos, sin_a, sin_b = tables
    kc, vc, s_hgrn, s_ssm, s_conv = caches
    if s_hgrn is None:
        s_hgrn = jnp.zeros((bsz, N_HEADS_A, HEAD_K_A, HEAD_K_A), F32)
        s_ssm = jnp.zeros((bsz, W_C, D_STATE), F32)
        s_conv = jnp.zeros((bsz, CONV_W - 1, CONV_DIM), F32)
    else:
        s_ssm = s_ssm.reshape(bsz, W_C, D_STATE)
        kc = _dup_heads(kc.reshape(bsz, WINDOW, KV_W_B))
        vc = _dup_heads(vc.reshape(bsz, WINDOW, KV_W_B))

    ya, hgrn_new = _hgrn_call(x, lw["norm_pre"], lw["w_a"], lw["lb_pack"], lw["hgrn_norm"], s_hgrn, tl)
    yb, k_new, v_new = _swa_call(x, lw["norm_pre"], lw["w_b"], cos, sin_a, sin_b, lw["sinks"], kc, vc, tl)
    yc, ssm_new, conv_new = _ssd_call(x, lw["norm_pre"], lw["w_c"], lw["conv_w"], lw["conv_b"], lw["dt_bias"],
                                      lw["a_log"], lw["d_skip"], lw["ssm_norm"], s_conv, s_ssm, tl)
    rows = bsz * seq
    tm = min(512, rows)
    x_new = _merge_call(x.reshape(rows, D_MODEL), ya.reshape(rows, W_A), yb.reshape(rows, W_B),
                        yc.reshape(rows, W_C), lw["norm_pre"], lw["norm_post"], lw["w_g"], lw["w_br"],
                        lw["w_out"], tm, 256)
    states = (k_new.reshape(bsz, WINDOW, N_KV_B, HEAD_DIM_B), v_new.reshape(bsz, WINDOW, N_KV_B, HEAD_DIM_B),
              hgrn_new, ssm_new.reshape(bsz, N_HEADS_C, HEAD_DIM_C, D_STATE), conv_new)
    return x_new.reshape(bsz, seq, D_MODEL), states


def kernel(x_prompt, x_sample, cache_swa_k, cache_swa_v, state_hgrn, state_ssm, state_conv, norm_pre, norm_post, w_in, hgrn_lb_logits, hgrn_norm, swa_sinks, conv_w, conv_b, dt_bias, a_log, d_skip, ssm_norm, w_branch_a, w_branch_b, w_branch_c, w_out):
    depth = w_in.shape[0]
    lbp = jax.nn.softmax(hgrn_lb_logits.astype(F32), axis=0)
    lbc = jnp.cumsum(lbp, axis=0)
    lb_all = lbc - lbc[0:1]

    offs = np.cumsum([0, W_A, W_A, W_A, W_A, W_B, KV_W_B, KV_W_B, W_B, W_C, CONV_DIM, N_HEADS_C,
                      D_MODEL, D_MODEL, D_MODEL])
    col = lambda l, a, b: w_in[l, :, int(offs[a]):int(offs[b])]

    tables_p = _rope_tables(jnp.arange(x_prompt.shape[1], dtype=jnp.int32))
    tables_s = _rope_tables(PAST_LEN + jnp.arange(x_sample.shape[1], dtype=jnp.int32))

    xp, xs = x_prompt, x_sample
    pst, sst = [], []
    for l in range(depth):
        lb = lb_all[l].reshape(N_HEADS_A, 1, HEAD_K_A)
        lw = {
            "norm_pre": norm_pre[l].reshape(1, D_MODEL),
            "norm_post": norm_post[l].reshape(1, D_MODEL),
            "w_a": col(l, 0, 4).astype(BF16),
            "w_b": jnp.concatenate([col(l, 4, 5), _dup_heads(col(l, 5, 6)), _dup_heads(col(l, 6, 7)),
                                    col(l, 7, 8)], axis=1).astype(BF16),
            "w_c": jnp.concatenate([col(l, 8, 9), col(l, 9, 10), _expand_heads(col(l, 10, 11))],
                                   axis=1).astype(BF16),
            "w_g": jnp.stack([col(l, 11, 12), col(l, 12, 13), col(l, 13, 14)]).astype(BF16),
            "w_br": jnp.stack([w_branch_a[l], w_branch_b[l], w_branch_c[l]]).astype(BF16),
            "w_out": w_out[l].astype(BF16),
            "lb_pack": jnp.stack([jnp.log(lb), jnp.log1p(-lb)]) * LOG2E,
            "hgrn_norm": hgrn_norm[l].reshape(1, HEAD_K_A),
            "sinks": jnp.broadcast_to(swa_sinks[l].astype(F32)[:, None], (N_Q_B, LANES)),
            "conv_w": conv_w[l],
            "conv_b": conv_b[l].reshape(1, CONV_DIM),
            "dt_bias": _expand_heads(dt_bias[l].astype(F32)).reshape(1, W_C),
            "a_log": _expand_heads(a_log[l].astype(F32)).reshape(1, W_C),
            "d_skip": _expand_heads(d_skip[l].astype(F32)).reshape(1, W_C),
            "ssm_norm": ssm_norm[l].reshape(1, W_C),
        }
        xp, sp = _layer(xp, 256, tables_p, (None, None, None, None, None), lw)
        xs, ss = _layer(xs, CHUNK, tables_s,
                        (cache_swa_k[l], cache_swa_v[l], state_hgrn[l], state_ssm[l], state_conv[l]), lw)
        pst.append(sp)
        sst.append(ss)

    stack = lambda sts, k: jnp.stack([s[k] for s in sts])
    return (xp, xs,
            stack(pst, 0), stack(pst, 1), stack(pst, 2), stack(pst, 3), stack(pst, 4),
            stack(sst, 0), stack(sst, 1), stack(sst, 2), stack(sst, 3), stack(sst, 4))
```

```python
import functools
import math

import jax
import jax.numpy as jnp
import numpy as np
from jax import lax
from jax.experimental import pallas as pl
from jax.experimental.pallas import tpu as pltpu

F32 = jnp.float32
BF16 = jnp.bfloat16

D_MODEL = 2048
CHUNK = 64
EPS = 1e-6
PAST_LEN = 4096

W_A = 1024
HEAD_K_A = 128
N_HEADS_A = 8

N_Q_B = 16
N_KV_B = 4
HEAD_DIM_B = 64
W_B = 1024
KV_W_B = 256
WINDOW = 128
ROT_DIM = 16
ROPE_THETA = 500000.0
ATTN_SCALE = HEAD_DIM_B ** -0.5

W_C = 1024
HEAD_DIM_C = 64
N_HEADS_C = 16
N_GROUPS_C = 4
D_STATE = 128
CONV_W = 4
CONV_DIM = 2048

LANES = 128
SUBLANES = 8
SUB_BLOCK = 16
MXU_COLS = 256
LOG2E = math.log2(math.e)
VMEM_LIMIT = 56 * 1024 * 1024

NT_DIMS = (((1,), (1,)), ((), ()))
TN_DIMS = (((0,), (0,)), ((), ()))


def _dot(a, b):
    return jnp.dot(a, b, preferred_element_type=F32)


def _dot_nt(a, b):
    return lax.dot_general(a, b, NT_DIMS, preferred_element_type=F32)


def _dot_tn(a, b):
    return lax.dot_general(a, b, TN_DIMS, preferred_element_type=F32)


def _prenorm_bf16(x, w):
    r = lax.rsqrt(jnp.mean(x * x, axis=-1, keepdims=True) + EPS)
    return ((x * r) * w).astype(BF16)


def _sigmoid(x):
    return 0.5 * jnp.tanh(0.5 * x) + 0.5


def _silu(x):
    return x * _sigmoid(x)


def _softplus0(x):
    return jnp.maximum(x, 0.0) + jnp.log(1.0 + jnp.exp(-jnp.abs(x)))


def _tri_bf16(n):
    r = lax.broadcasted_iota(jnp.int32, (n, n), 0)
    c = lax.broadcasted_iota(jnp.int32, (n, n), 1)
    return (r >= c).astype(BF16)


def _cumsum_rows(tri, x):
    hi = x.astype(BF16)
    r1 = x - hi.astype(F32)
    mid = r1.astype(BF16)
    lo = (r1 - mid.astype(F32)).astype(BF16)
    return _dot(tri, hi) + _dot(tri, mid) + _dot(tri, lo)


def _round_robin(gens):
    results = [None] * len(gens)
    live = list(range(len(gens)))
    while live:
        for n in list(live):
            try:
                next(gens[n])
            except StopIteration as stop:
                results[n] = stop.value
                live.remove(n)
    return results


def _hgrn_kernel(x_ref, nw_ref, w_ref, lb_ref, hn_ref, s0_ref, y_ref, snew_ref,
                 proj_s, h_s, ybuf_s, st_s, *, tl, ni):
    n = pl.program_id(0)
    n_chunks = tl // CHUNK
    i_lag = jnp.maximum(n - 1, 0) % ni
    nxt = n % 2
    cur = 1 - nxt

    @pl.when(n == 0)
    def _():
        proj_s[...] = jnp.zeros_like(proj_s)

    @pl.when(i_lag == 0)
    def _():
        for hd in range(N_HEADS_A):
            st_s[hd] = s0_ref[0, hd].T

    h_s[...] = _prenorm_bf16(x_ref[0], nw_ref[...])
    n_col_blocks = w_ref.shape[0]
    cols_per_chunk = n_col_blocks // n_chunks

    def project(c):
        for j in range(cols_per_chunk):
            jb = c * cols_per_chunk + j
            res = _dot(h_s[...], w_ref[jb])
            proj_s[nxt, 2 * jb] = res[:, :LANES]
            proj_s[nxt, 2 * jb + 1] = res[:, LANES:]
            yield

    tri = _tri_bf16(CHUNK)
    ones = jnp.ones((LANES, LANES), BF16)
    n_sub = CHUNK // SUB_BLOCK
    lane8 = lax.broadcasted_iota(jnp.int32, (SUBLANES, LANES), 1)
    sub8 = lax.broadcasted_iota(jnp.int32, (SUBLANES, LANES), 0)
    row64 = lax.broadcasted_iota(jnp.int32, (CHUNK, CHUNK), 0)
    col64 = lax.broadcasted_iota(jnp.int32, (CHUNK, CHUNK), 1)
    below_diag_block = row64 // SUB_BLOCK > col64 // SUB_BLOCK
    zero8 = jnp.zeros((SUBLANES, LANES), F32)

    def zeros(n):
        return jnp.zeros((n, LANES), F32)

    def piece(a, m, u):
        lo = SUB_BLOCK * m + SUBLANES * u
        return a[lo:lo + SUBLANES]

    def unit(hd, r0):
        rows = pl.ds(r0, CHUNK)
        aq = proj_s[cur, hd, rows, :]
        z = proj_s[cur, N_HEADS_A + hd, rows, :]
        v = proj_s[cur, 2 * N_HEADS_A + hd, rows, :].astype(BF16)
        ag = proj_s[cur, 3 * N_HEADS_A + hd, rows, :]
        log_lb = lb_ref[0, hd]
        log1m_lb = lb_ref[1, hd]

        z2 = z * LOG2E
        log_sig = jnp.minimum(z2, 0.0) - jnp.log2(1.0 + jnp.exp2(-jnp.abs(z2)))
        cterm = log1m_lb + log_sig
        log_f = jnp.maximum(log_lb, cterm) + jnp.log2(1.0 + jnp.exp2(-jnp.abs(log_lb - cterm)))
        log_k = cterm - z2
        q = _silu(aq)

        b2 = _cumsum_rows(tri, log_f)
        yield
        c2 = b2 - log_k
        b2_last = b2[CHUNK - 1:CHUNK, :]
        st = st_s[hd]

        o = _dot_nt((q * jnp.exp2(b2)).astype(BF16), st.astype(BF16))
        yield

        q_slabs, k_slabs = [], []
        for m in range(1, n_sub):
            lo = SUB_BLOCK * m
            ref = b2[lo - 1:lo, :]
            qm = q[lo:lo + SUB_BLOCK] * jnp.exp2(b2[lo:lo + SUB_BLOCK] - ref)
            q_parts = [zeros(lo), qm] + ([zeros(CHUNK - lo - SUB_BLOCK)] if lo + SUB_BLOCK < CHUNK else [])
            q_slabs.append(jnp.concatenate(q_parts, axis=0))
            k_slabs.append(jnp.concatenate([jnp.exp2(ref - c2[:lo]), zeros(CHUNK - lo)], axis=0))
        a_off = _dot_nt(jnp.concatenate(q_slabs, axis=1).astype(BF16),
                        jnp.concatenate(k_slabs, axis=1).astype(BF16))
        yield

        order = [(s, m, u) for s in range(SUB_BLOCK) for m in range(n_sub) for u in range(2)
                 if not (u == 0 and s >= SUBLANES)]
        pieces = []
        for s, m, u in order:
            cs = c2[SUB_BLOCK * m + s:SUB_BLOCK * m + s + 1]
            pieces.append(piece(q, m, u) * jnp.exp2(piece(b2, m, u) - cs))
        sums = _dot(jnp.concatenate(pieces, axis=0).astype(BF16), ones)
        yield
        d = {(m, u): zero8 for m in range(n_sub) for u in range(2)}
        for n, (s, m, u) in enumerate(order):
            d[(m, u)] = jnp.where(lane8 == s, sums[SUBLANES * n:SUBLANES * (n + 1)], d[(m, u)])
        diag_rows = []
        for m in range(n_sub):
            for u in range(2):
                dm = jnp.where(sub8 + SUBLANES * u >= lane8, d[(m, u)], 0.0)
                diag_rows.append(pltpu.roll(dm, SUB_BLOCK * m, 1) if m else dm)
        a_diag = jnp.concatenate(diag_rows, axis=0)[:, :CHUNK]
        a_full = jnp.where(below_diag_block, a_off, a_diag)
        o = o + _dot(a_full.astype(BF16), v)
        yield

        r = lax.rsqrt(jnp.mean(o * o, axis=-1, keepdims=True) + EPS)
        y = ((o * r) * hn_ref[...]) * _silu(ag)

        st_new = st * jnp.exp2(b2_last) + _dot_tn(v, jnp.exp2(b2_last - c2).astype(BF16))
        return y.astype(BF16), st_new

    def body(c, carry):
        r0 = pl.multiple_of(c * CHUNK, CHUNK)
        gens = [unit(hd, r0) for hd in range(N_HEADS_A)] + [project(c)]
        results = _round_robin(gens)[:N_HEADS_A]
        for hd, (y, st_new) in enumerate(results):
            ybuf_s[hd, pl.ds(r0, CHUNK), :] = y
            st_s[hd] = st_new
        return carry

    lax.fori_loop(0, n_chunks, body, 0)

    for hd in range(N_HEADS_A):
        y_ref[0, :, LANES * hd:LANES * (hd + 1)] = ybuf_s[hd]

    @pl.when(i_lag == ni - 1)
    def _():
        for hd in range(N_HEADS_A):
            snew_ref[0, hd] = st_s[hd].T


def _lagged_maps(n_blocks, ni):
    def split(m):
        return m // ni, m % ni
    x_map = lambda n: split(jnp.minimum(n, n_blocks - 1)) + (0,)
    y_map = lambda n: split(jnp.maximum(n - 1, 0)) + (0,)
    pos_map = lambda n: (jnp.minimum(n, n_blocks - 1) % ni, 0)
    batch_map = lambda nd: (lambda n: (jnp.maximum(n - 1, 0) // ni,) + (0,) * (nd - 1))
    return x_map, y_map, pos_map, batch_map


def _hgrn_call(x, nw, w_a, lb_pack, hn, s0, tl):
    bsz, seq, _ = x.shape
    ni = seq // tl
    n_blocks = bsz * ni
    x_map, y_map, _, batch_map = _lagged_maps(n_blocks, ni)
    const2 = lambda n: (0, 0)
    return pl.pallas_call(
        functools.partial(_hgrn_kernel, tl=tl, ni=ni),
        grid=(n_blocks + 1,),
        in_specs=[
            pl.BlockSpec((1, tl, D_MODEL), x_map),
            pl.BlockSpec((1, D_MODEL), const2),
            pl.BlockSpec(w_a.shape, lambda n: (0, 0, 0), pipeline_mode=pl.Buffered(1)),
            pl.BlockSpec((2, N_HEADS_A, 1, HEAD_K_A), lambda n: (0, 0, 0, 0)),
            pl.BlockSpec((1, HEAD_K_A), const2),
            pl.BlockSpec((1, N_HEADS_A, HEAD_K_A, HEAD_K_A), batch_map(4)),
        ],
        out_specs=[
            pl.BlockSpec((1, tl, W_A), y_map),
            pl.BlockSpec((1, N_HEADS_A, HEAD_K_A, HEAD_K_A), batch_map(4)),
        ],
        out_shape=[
            jax.ShapeDtypeStruct((bsz, seq, W_A), BF16),
            jax.ShapeDtypeStruct((bsz, N_HEADS_A, HEAD_K_A, HEAD_K_A), F32),
        ],
        scratch_shapes=[
            pltpu.VMEM((2, 4 * N_HEADS_A, tl, LANES), F32),
            pltpu.VMEM((tl, D_MODEL), BF16),
            pltpu.VMEM((N_HEADS_A, tl, LANES), BF16),
            pltpu.VMEM((N_HEADS_A, HEAD_K_A, HEAD_K_A), F32),
        ],
        compiler_params=pltpu.CompilerParams(
            dimension_semantics=("arbitrary",), vmem_limit_bytes=VMEM_LIMIT),
        name="hgrn_mixer",
    )(x, nw, w_a, lb_pack, hn, s0)


def _rotate(xs, cos, sin_a, sin_b):
    return xs * cos + pltpu.roll(xs, LANES - ROT_DIM // 2, 1) * sin_a + pltpu.roll(xs, ROT_DIM // 2, 1) * sin_b


def _swa_kernel(*refs, tl, has_cache):
    if has_cache:
        (x_ref, nw_ref, w_ref, cos_ref, sa_ref, sb_ref, sink_ref, kc_ref, vc_ref,
         y_ref, knew_ref, vnew_ref, q_s, g_s, kbuf, vbuf) = refs
    else:
        (x_ref, nw_ref, w_ref, cos_ref, sa_ref, sb_ref, sink_ref,
         y_ref, knew_ref, vnew_ref, q_s, g_s, kbuf, vbuf) = refs
    i = pl.program_id(1)
    n_chunks = tl // CHUNK
    band = WINDOW + CHUNK
    kvw = 2 * KV_W_B

    @pl.when(i == 0)
    def _():
        if has_cache:
            kbuf[0:WINDOW, :] = kc_ref[0]
            vbuf[0:WINDOW, :] = vc_ref[0]
        else:
            kbuf[0:WINDOW, :] = jnp.zeros((WINDOW, kvw), F32)
            vbuf[0:WINDOW, :] = jnp.zeros((WINDOW, kvw), F32)

    h = _prenorm_bf16(x_ref[0], nw_ref[...])
    cos, sin_a, sin_b = cos_ref[...], sa_ref[...], sb_ref[...]
    for j in range(W_B // 256):
        res = _dot(h, w_ref[:, 256 * j:256 * (j + 1)])
        for u in range(2):
            q_s[:, 256 * j + LANES * u:256 * j + LANES * (u + 1)] = _rotate(
                res[:, LANES * u:LANES * (u + 1)], cos, sin_a, sin_b)
    for j in range(kvw // 256):
        res = _dot(h, w_ref[:, W_B + 256 * j:W_B + 256 * (j + 1)])
        for u in range(2):
            kbuf[WINDOW:WINDOW + tl, 256 * j + LANES * u:256 * j + LANES * (u + 1)] = _rotate(
                res[:, LANES * u:LANES * (u + 1)], cos, sin_a, sin_b)
    for j in range(kvw // 256):
        vbuf[WINDOW:WINDOW + tl, 256 * j:256 * (j + 1)] = _dot(
            h, w_ref[:, W_B + kvw + 256 * j:W_B + kvw + 256 * (j + 1)])
    for j in range(W_B // 256):
        g_s[:, 256 * j:256 * (j + 1)] = _dot(
            h, w_ref[:, W_B + 2 * kvw + 256 * j:W_B + 2 * kvw + 256 * (j + 1)])

    lane = lax.broadcasted_iota(jnp.int32, (CHUNK, LANES), 1)
    low = lane < HEAD_DIM_B
    low1 = lax.broadcasted_iota(jnp.int32, (1, LANES), 1) < HEAD_DIM_B
    key_row = lax.broadcasted_iota(jnp.int32, (band, LANES), 0)

    def body(c, carry):
        r0 = pl.multiple_of(c * CHUNK, CHUNK)

        def pair(j):
            g = j // 2
            qp = q_s[pl.ds(r0, CHUNK), LANES * j:LANES * (j + 1)]
            q2 = jnp.concatenate([jnp.where(low, qp, 0.0), jnp.where(low, 0.0, qp)], axis=0).astype(BF16)
            kb = kbuf[pl.ds(r0, band), LANES * g:LANES * (g + 1)].astype(BF16)
            vb = vbuf[pl.ds(r0, band), LANES * g:LANES * (g + 1)].astype(BF16)
            s = _dot_nt(kb, q2) * ATTN_SCALE
            yield
            if not has_cache:
                s = jnp.where(i * tl + r0 - WINDOW + key_row >= 0, s, -jnp.inf)
            sk = jnp.where(low1, sink_ref[2 * j:2 * j + 1, :], sink_ref[2 * j + 1:2 * j + 2, :])
            m = jnp.maximum(jnp.max(s, axis=0, keepdims=True), sk)
            p = jnp.exp(s - m)
            den = jnp.sum(p, axis=0, keepdims=True) + jnp.exp(sk - m)
            o2 = _dot_tn((p * (1.0 / den)).astype(BF16), vb)
            yield
            o = jnp.where(low, o2[:CHUNK], o2[CHUNK:])
            y = o * _silu(g_s[pl.ds(r0, CHUNK), LANES * j:LANES * (j + 1)])
            return y.astype(BF16)

        ys = _round_robin([pair(j) for j in range(N_Q_B // 2)])
        for j, y in enumerate(ys):
            y_ref[0, pl.ds(r0, CHUNK), LANES * j:LANES * (j + 1)] = y
        return carry

    lax.fori_loop(0, n_chunks, body, 0)

    ktail = kbuf[tl:tl + WINDOW, :]
    vtail = vbuf[tl:tl + WINDOW, :]
    kbuf[0:WINDOW, :] = ktail
    vbuf[0:WINDOW, :] = vtail

    @pl.when(i == pl.num_programs(1) - 1)
    def _():
        low_w = lax.broadcasted_iota(jnp.int32, (WINDOW, LANES), 1) < HEAD_DIM_B
        for u in range(KV_W_B // LANES):
            knew_ref[0, :, LANES * u:LANES * (u + 1)] = jnp.where(
                low_w, ktail[:, 2 * LANES * u:2 * LANES * u + LANES],
                ktail[:, 2 * LANES * u + LANES:2 * LANES * (u + 1)])
            vnew_ref[0, :, LANES * u:LANES * (u + 1)] = jnp.where(
                low_w, vtail[:, 2 * LANES * u:2 * LANES * u + LANES],
                vtail[:, 2 * LANES * u + LANES:2 * LANES * (u + 1)])


def _swa_call(x, nw, w_b, cos, sin_a, sin_b, sinks, kc, vc, tl):
    bsz, seq, _ = x.shape
    has_cache = kc is not None
    grid = (bsz, seq // tl)
    const2 = lambda b, i: (0, 0)
    wcols = w_b.shape[1]
    in_specs = [
        pl.BlockSpec((1, tl, D_MODEL), lambda b, i: (b, i, 0)),
        pl.BlockSpec((1, D_MODEL), const2),
        pl.BlockSpec((D_MODEL, wcols), const2, pipeline_mode=pl.Buffered(1)),
        pl.BlockSpec((tl, LANES), lambda b, i: (i, 0)),
        pl.BlockSpec((tl, LANES), lambda b, i: (i, 0)),
        pl.BlockSpec((tl, LANES), lambda b, i: (i, 0)),
        pl.BlockSpec((N_Q_B, LANES), const2),
    ]
    args = [x, nw, w_b, cos, sin_a, sin_b, sinks]
    if has_cache:
        in_specs += [pl.BlockSpec((1, WINDOW, 2 * KV_W_B), lambda b, i: (b, 0, 0))] * 2
        args += [kc, vc]
    return pl.pallas_call(
        functools.partial(_swa_kernel, tl=tl, has_cache=has_cache),
        grid=grid,
        in_specs=in_specs,
        out_specs=[
            pl.BlockSpec((1, tl, W_B), lambda b, i: (b, i, 0)),
            pl.BlockSpec((1, WINDOW, KV_W_B), lambda b, i: (b, 0, 0)),
            pl.BlockSpec((1, WINDOW, KV_W_B), lambda b, i: (b, 0, 0)),
        ],
        out_shape=[
            jax.ShapeDtypeStruct((bsz, seq, W_B), BF16),
            jax.ShapeDtypeStruct((bsz, WINDOW, KV_W_B), F32),
            jax.ShapeDtypeStruct((bsz, WINDOW, KV_W_B), F32),
        ],
        scratch_shapes=[
            pltpu.VMEM((tl, W_B), F32),
            pltpu.VMEM((tl, W_B), F32),
            pltpu.VMEM((WINDOW + tl, 2 * KV_W_B), F32),
            pltpu.VMEM((WINDOW + tl, 2 * KV_W_B), F32),
        ],
        compiler_params=pltpu.CompilerParams(
            dimension_semantics=("arbitrary", "arbitrary"), vmem_limit_bytes=VMEM_LIMIT),
        name="swa_mixer",
    )(*args)


CONV_PAD = 8


def _ssd_kernel(x_ref, nw_ref, w_ref, cw_ref, cb_ref, dtb_ref, alog_ref, dsk_ref, sn_ref, conv0_ref, s0_ref,
                y_ref, snew_ref, convnew_ref,
                ubuf, h_s, z_s, xc_s, xdt_s, bc_s, la_s, st_s, *, tl, ni, n_blocks):
    n = pl.program_id(0)
    n_chunks = tl // CHUNK
    gw = N_GROUPS_C * D_STATE
    hp = W_C // N_GROUPS_C
    i_proj = jnp.minimum(n, n_blocks - 1) % ni
    i_lag = jnp.maximum(n - 1, 0) % ni
    nxt = n % 2
    cur = 1 - nxt
    tail_rows = slice(CONV_PAD - (CONV_W - 1), CONV_PAD)

    @pl.when(n == 0)
    def _():
        for buf in (z_s, xc_s, xdt_s, bc_s, la_s):
            buf[...] = jnp.zeros_like(buf)

    @pl.when(i_proj == 0)
    def _():
        ubuf[tail_rows, :] = conv0_ref[0]

    @pl.when(i_lag == 0)
    def _():
        for g in range(N_GROUPS_C):
            st_s[g] = s0_ref[0, hp * g:hp * (g + 1), :].T

    h_s[...] = _prenorm_bf16(x_ref[0], nw_ref[...])

    def conv_act(sl):
        acc = cb_ref[:, sl]
        for t in range(CONV_W):
            lo = CONV_PAD - (CONV_W - 1) + t
            acc = acc + ubuf[lo:lo + tl, sl] * cw_ref[t:t + 1, sl]
        ubuf[tail_rows, sl] = ubuf[tl + CONV_PAD - (CONV_W - 1):tl + CONV_PAD, sl]
        return _silu(acc)

    def project(c):
        per = (W_C // MXU_COLS) // n_chunks
        for j in range(c * per, (c + 1) * per):
            sl = slice(MXU_COLS * j, MXU_COLS * (j + 1))
            z_s[nxt, :, sl] = _dot(h_s[...], w_ref[:, sl])
            yield
            dt_raw = _dot(h_s[...], w_ref[:, W_C + CONV_DIM + MXU_COLS * j:W_C + CONV_DIM + MXU_COLS * (j + 1)])
            yield
            dt = _softplus0(dt_raw + dtb_ref[:, sl])
            la_s[nxt, :, sl] = -dt * jnp.exp(alog_ref[:, sl])
            ubuf[CONV_PAD:CONV_PAD + tl, sl] = _dot(h_s[...], w_ref[:, W_C + MXU_COLS * j:W_C + MXU_COLS * (j + 1)])
            yield
            for u in range(MXU_COLS // LANES):
                usl = slice(MXU_COLS * j + LANES * u, MXU_COLS * j + LANES * (u + 1))
                act = conv_act(usl)
                xc_s[nxt, :, usl] = act
                xdt_s[nxt, :, usl] = act * dt[:, LANES * u:LANES * (u + 1)]
            bsl = slice(W_C + MXU_COLS * j, W_C + MXU_COLS * (j + 1))
            ubuf[CONV_PAD:CONV_PAD + tl, bsl] = _dot(h_s[...], w_ref[:, W_C + W_C + MXU_COLS * j:W_C + W_C + MXU_COLS * (j + 1)])
            yield
            for u in range(MXU_COLS // LANES):
                usl = slice(W_C + MXU_COLS * j + LANES * u, W_C + MXU_COLS * j + LANES * (u + 1))
                bc_s[nxt, :, MXU_COLS * j + LANES * u:MXU_COLS * j + LANES * (u + 1)] = conv_act(usl)

    tri = _tri_bf16(CHUNK)
    trow = lax.broadcasted_iota(jnp.int32, (CHUNK, LANES), 0)
    lane = lax.broadcasted_iota(jnp.int32, (CHUNK, LANES), 1)
    s_of_lane = lane % CHUNK
    causal = trow >= s_of_lane
    row2 = lax.broadcasted_iota(jnp.int32, (2 * CHUNK, LANES), 0)
    lane2 = lax.broadcasted_iota(jnp.int32, (2 * CHUNK, LANES), 1)
    blockdiag = (row2 < CHUNK) == (lane2 < HEAD_DIM_C)

    for c in range(n_chunks):
        rows = slice(CHUNK * c, CHUNK * (c + 1))

        def group(g, rows=rows):
            gsl = slice(hp * g, hp * (g + 1))
            bg = bc_s[cur, rows, D_STATE * g:D_STATE * (g + 1)].astype(BF16)
            cg = bc_s[cur, rows, gw + D_STATE * g:gw + D_STATE * (g + 1)].astype(BF16)
            cb2 = _dot_nt(cg, jnp.concatenate([bg, bg], axis=0))
            yield
            st = st_s[g]
            y_state = _dot(cg, st.astype(BF16))
            yield
            la = la_s[cur, rows, gsl]
            b = _cumsum_rows(tri, la)
            yield
            xdt = xdt_s[cur, rows, gsl]
            ys = []
            for u in range(2):
                usl = slice(LANES * u, LANES * (u + 1))
                b_row = jnp.sum(jnp.where(trow <= s_of_lane, la[:, usl], 0.0), axis=0, keepdims=True)
                decay = jnp.where(causal, jnp.exp(b[:, usl] - b_row), 0.0)
                wmat = (cb2 * decay).astype(BF16)
                xbd = jnp.where(blockdiag, jnp.concatenate([xdt[:, usl], xdt[:, usl]], axis=0), 0.0).astype(BF16)
                ys.append(_dot(wmat, xbd))
                yield
            y = jnp.concatenate(ys, axis=1) + y_state * jnp.exp(b)
            o = (y + dsk_ref[:, gsl] * xc_s[cur, rows, gsl]) * _silu(z_s[cur, rows, gsl])
            r = lax.rsqrt(jnp.mean(o * o, axis=-1, keepdims=True) + EPS)
            out = ((o * r) * sn_ref[:, gsl]).astype(BF16)
            b_last = b[CHUNK - 1:CHUNK, :]
            st_new = st * jnp.exp(b_last) + _dot_tn(bg, (xdt * jnp.exp(b_last - b)).astype(BF16))
            return out, st_new

        results = _round_robin([group(g) for g in range(N_GROUPS_C)] + [project(c)])[:N_GROUPS_C]
        for g, (out, st_new) in enumerate(results):
            y_ref[0, rows, hp * g:hp * (g + 1)] = out
            st_s[g] = st_new

    @pl.when(i_proj == ni - 1)
    def _():
        convnew_ref[0] = ubuf[tail_rows, :]

    @pl.when(i_lag == ni - 1)
    def _():
        for g in range(N_GROUPS_C):
            snew_ref[0, hp * g:hp * (g + 1), :] = st_s[g].T


def _ssd_call(x, nw, w_c, cw, cb, dtb, alog, dsk, sn, conv0, s0, tl):
    bsz, seq, _ = x.shape
    ni = seq // tl
    n_blocks = bsz * ni
    x_map, y_map, _, batch_map = _lagged_maps(n_blocks, ni)
    proj_batch_map = lambda n: (jnp.minimum(n, n_blocks - 1) // ni, 0, 0)
    const2 = lambda n: (0, 0)
    wcols = w_c.shape[1]
    return pl.pallas_call(
        functools.partial(_ssd_kernel, tl=tl, ni=ni, n_blocks=n_blocks),
        grid=(n_blocks + 1,),
        in_specs=[
            pl.BlockSpec((1, tl, D_MODEL), x_map),
            pl.BlockSpec((1, D_MODEL), const2),
            pl.BlockSpec((D_MODEL, wcols), const2, pipeline_mode=pl.Buffered(1)),
            pl.BlockSpec((CONV_W, CONV_DIM), const2),
            pl.BlockSpec((1, CONV_DIM), const2),
            pl.BlockSpec((1, W_C), const2),
            pl.BlockSpec((1, W_C), const2),
            pl.BlockSpec((1, W_C), const2),
            pl.BlockSpec((1, W_C), const2),
            pl.BlockSpec((1, CONV_W - 1, CONV_DIM), proj_batch_map),
            pl.BlockSpec((1, W_C, D_STATE), batch_map(3)),
        ],
        out_specs=[
            pl.BlockSpec((1, tl, W_C), y_map),
            pl.BlockSpec((1, W_C, D_STATE), batch_map(3)),
            pl.BlockSpec((1, CONV_W - 1, CONV_DIM), proj_batch_map),
        ],
        out_shape=[
            jax.ShapeDtypeStruct((bsz, seq, W_C), BF16),
            jax.ShapeDtypeStruct((bsz, W_C, D_STATE), F32),
            jax.ShapeDtypeStruct((bsz, CONV_W - 1, CONV_DIM), F32),
        ],
        scratch_shapes=[
            pltpu.VMEM((CONV_PAD + tl, CONV_DIM), F32),
            pltpu.VMEM((tl, D_MODEL), BF16),
            pltpu.VMEM((2, tl, W_C), F32),
            pltpu.VMEM((2, tl, W_C), F32),
            pltpu.VMEM((2, tl, W_C), F32),
            pltpu.VMEM((2, tl, 2 * N_GROUPS_C * D_STATE), F32),
            pltpu.VMEM((2, tl, W_C), F32),
            pltpu.VMEM((N_GROUPS_C, D_STATE, W_C // N_GROUPS_C), F32),
        ],
        compiler_params=pltpu.CompilerParams(
            dimension_semantics=("arbitrary",), vmem_limit_bytes=VMEM_LIMIT),
        name="ssd_mixer",
    )(x, nw, w_c, cw, cb, dtb, alog, dsk, sn, conv0, s0)


def _merge_kernel(x_ref, ya_ref, yb_ref, yc_ref, npre_ref, npost_ref, wg_ref, wbr_ref, wout_ref,
                  o_ref, h_s, m_s):
    j = pl.program_id(1)
    n_j = m_s.shape[0]

    @pl.when(j == 0)
    def _():
        h_s[...] = _prenorm_bf16(x_ref[...], npre_ref[...])

    h = h_s[...]
    merged = None
    for bi, y_ref in enumerate((ya_ref, yb_ref, yc_ref)):
        term = _sigmoid(_dot(h, wg_ref[bi])) * _dot(y_ref[...], wbr_ref[bi])
        merged = term if merged is None else merged + term
    m_s[j] = merged.astype(BF16)

    @pl.when(j == n_j - 1)
    def _():
        out = _dot(jnp.concatenate([m_s[n] for n in range(n_j)], axis=1), wout_ref[...])
        r = lax.rsqrt(jnp.mean(out * out, axis=-1, keepdims=True) + EPS)
        o_ref[...] = x_ref[...] + (out * r) * npost_ref[...]


def _merge_call(x2, ya, yb, yc, npre, npost, wg, wbr, wout, tm, tn):
    rows = x2.shape[0]
    grid = (rows // tm, D_MODEL // tn)
    const2 = lambda r, j: (0, 0)
    return pl.pallas_call(
        _merge_kernel,
        grid=grid,
        in_specs=[
            pl.BlockSpec((tm, D_MODEL), lambda r, j: (r, 0)),
            pl.BlockSpec((tm, W_A), lambda r, j: (r, 0)),
            pl.BlockSpec((tm, W_B), lambda r, j: (r, 0)),
            pl.BlockSpec((tm, W_C), lambda r, j: (r, 0)),
            pl.BlockSpec((1, D_MODEL), const2),
            pl.BlockSpec((1, D_MODEL), const2),
            pl.BlockSpec((3, D_MODEL, tn), lambda r, j: (0, 0, j)),
            pl.BlockSpec((3, W_A, tn), lambda r, j: (0, 0, j)),
            pl.BlockSpec((D_MODEL, D_MODEL), const2, pipeline_mode=pl.Buffered(1)),
        ],
        out_specs=pl.BlockSpec((tm, D_MODEL), lambda r, j: (r, 0)),
        out_shape=jax.ShapeDtypeStruct((rows, D_MODEL), F32),
        scratch_shapes=[
            pltpu.VMEM((tm, D_MODEL), BF16),
            pltpu.VMEM((D_MODEL // tn, tm, tn), BF16),
        ],
        compiler_params=pltpu.CompilerParams(
            dimension_semantics=("arbitrary", "arbitrary"), vmem_limit_bytes=VMEM_LIMIT),
        name="merge_out",
    )(x2, ya, yb, yc, npre, npost, wg, wbr, wout)


def _rope_tables(pos):
    half = ROT_DIM // 2
    inv = jnp.power(ROPE_THETA, -jnp.arange(half, dtype=F32) / half)
    ang = pos.astype(F32)[:, None] * inv[None, :]
    cos, sin = jnp.cos(ang), jnp.sin(ang)
    n = pos.shape[0]
    ones = jnp.ones((n, HEAD_DIM_B - ROT_DIM), F32)
    zeros = jnp.zeros((n, HEAD_DIM_B - ROT_DIM), F32)
    zh = jnp.zeros((n, half), F32)
    cos_t = jnp.concatenate([cos, cos, ones], axis=1)
    sin_a = jnp.concatenate([-sin, zh, zeros], axis=1)
    sin_b = jnp.concatenate([zh, sin, zeros], axis=1)
    tile = lambda t: jnp.concatenate([t, t], axis=1)
    return tile(cos_t), tile(sin_a), tile(sin_b)


def _dup_heads(t):
    lead = t.shape[:-1]
    t4 = t.reshape(lead + (N_KV_B, 1, HEAD_DIM_B))
    return jnp.broadcast_to(t4, lead + (N_KV_B, 2, HEAD_DIM_B)).reshape(lead + (2 * KV_W_B,))


def _col_blocks(w):
    return w.reshape(w.shape[0], -1, MXU_COLS).transpose(1, 0, 2)


def _expand_heads(t):
    return jnp.repeat(t, HEAD_DIM_C, axis=-1)


def _layer(x, tl, tables, caches, lw):
    bsz, seq, _ = x.shape
    cos, sin_a, sin_b = tables
    kc, vc, s_hgrn, s_ssm, s_conv = caches
    if s_hgrn is None:
        s_hgrn = jnp.zeros((bsz, N_HEADS_A, HEAD_K_A, HEAD_K_A), F32)
        s_ssm = jnp.zeros((bsz, W_C, D_STATE), F32)
        s_conv = jnp.zeros((bsz, CONV_W - 1, CONV_DIM), F32)
    else:
        s_ssm = s_ssm.reshape(bsz, W_C, D_STATE)
        kc = _dup_heads(kc.reshape(bsz, WINDOW, KV_W_B))
        vc = _dup_heads(vc.reshape(bsz, WINDOW, KV_W_B))

    ya, hgrn_new = _hgrn_call(x, lw["norm_pre"], lw["w_a"], lw["lb_pack"], lw["hgrn_norm"], s_hgrn, tl)
    yb, k_new, v_new = _swa_call(x, lw["norm_pre"], lw["w_b"], cos, sin_a, sin_b, lw["sinks"], kc, vc, tl)
    yc, ssm_new, conv_new = _ssd_call(x, lw["norm_pre"], lw["w_c"], lw["conv_w"], lw["conv_b"], lw["dt_bias"],
                                      lw["a_log"], lw["d_skip"], lw["ssm_norm"], s_conv, s_ssm, tl)
    rows = bsz * seq
    tm = min(512, rows)
    x_new = _merge_call(x.reshape(rows, D_MODEL), ya.reshape(rows, W_A), yb.reshape(rows, W_B),
                        yc.reshape(rows, W_C), lw["norm_pre"], lw["norm_post"], lw["w_g"], lw["w_br"],
                        lw["w_out"], tm, 256)
    states = (k_new.reshape(bsz, WINDOW, N_KV_B, HEAD_DIM_B), v_new.reshape(bsz, WINDOW, N_KV_B, HEAD_DIM_B),
              hgrn_new, ssm_new.reshape(bsz, N_HEADS_C, HEAD_DIM_C, D_STATE), conv_new)
    return x_new.reshape(bsz, seq, D_MODEL), states


def kernel(x_prompt, x_sample, cache_swa_k, cache_swa_v, state_hgrn, state_ssm, state_conv, norm_pre, norm_post, w_in, hgrn_lb_logits, hgrn_norm, swa_sinks, conv_w, conv_b, dt_bias, a_log, d_skip, ssm_norm, w_branch_a, w_branch_b, w_branch_c, w_out):
    depth = w_in.shape[0]
    lbp = jax.nn.softmax(hgrn_lb_logits.astype(F32), axis=0)
    lbc = jnp.cumsum(lbp, axis=0)
    lb_all = lbc - lbc[0:1]

    offs = np.cumsum([0, W_A, W_A, W_A, W_A, W_B, KV_W_B, KV_W_B, W_B, W_C, CONV_DIM, N_HEADS_C,
                      D_MODEL, D_MODEL, D_MODEL])
    col = lambda l, a, b: w_in[l, :, int(offs[a]):int(offs[b])]

    tables_p = _rope_tables(jnp.arange(x_prompt.shape[1], dtype=jnp.int32))
    tables_s = _rope_tables(PAST_LEN + jnp.arange(x_sample.shape[1], dtype=jnp.int32))

    xp, xs = x_prompt, x_sample
    pst, sst = [], []
    for l in range(depth):
        lb = lb_all[l].reshape(N_HEADS_A, 1, HEAD_K_A)
        lw = {
            "norm_pre": norm_pre[l].reshape(1, D_MODEL),
            "norm_post": norm_post[l].reshape(1, D_MODEL),
            "w_a": _col_blocks(col(l, 0, 4).astype(BF16)),
            "w_b": jnp.concatenate([col(l, 4, 5), _dup_heads(col(l, 5, 6)), _dup_heads(col(l, 6, 7)),
                                    col(l, 7, 8)], axis=1).astype(BF16),
            "w_c": jnp.concatenate([col(l, 8, 9), col(l, 9, 10), _expand_heads(col(l, 10, 11))],
                                   axis=1).astype(BF16),
            "w_g": jnp.stack([col(l, 11, 12), col(l, 12, 13), col(l, 13, 14)]).astype(BF16),
            "w_br": jnp.stack([w_branch_a[l], w_branch_b[l], w_branch_c[l]]).astype(BF16),
            "w_out": w_out[l].astype(BF16),
            "lb_pack": jnp.stack([jnp.log(lb), jnp.log1p(-lb)]) * LOG2E,
            "hgrn_norm": hgrn_norm[l].reshape(1, HEAD_K_A),
            "sinks": jnp.broadcast_to(swa_sinks[l].astype(F32)[:, None], (N_Q_B, LANES)),
            "conv_w": conv_w[l],
            "conv_b": conv_b[l].reshape(1, CONV_DIM),
            "dt_bias": _expand_heads(dt_bias[l].astype(F32)).reshape(1, W_C),
            "a_log": _expand_heads(a_log[l].astype(F32)).reshape(1, W_C),
            "d_skip": _expand_heads(d_skip[l].astype(F32)).reshape(1, W_C),
            "ssm_norm": ssm_norm[l].reshape(1, W_C),
        }
        xp, sp = _layer(xp, 256, tables_p, (None, None, None, None, None), lw)
        xs, ss = _layer(xs, CHUNK, tables_s,
                        (cache_swa_k[l], cache_swa_v[l], state_hgrn[l], state_ssm[l], state_conv[l]), lw)
        pst.append(sp)
        sst.append(ss)

    stack = lambda sts, k: jnp.stack([s[k] for s in sts])
    return (xp, xs,
            stack(pst, 0), stack(pst, 1), stack(pst, 2), stack(pst, 3), stack(pst, 4),
            stack(sst, 0), stack(sst, 1), stack(sst, 2), stack(sst, 3), stack(sst, 4))
```

```python
import functools
import math

import jax
import jax.numpy as jnp
import numpy as np
from jax import lax
from jax.experimental import pallas as pl
from jax.experimental.pallas import tpu as pltpu

F32 = jnp.float32
BF16 = jnp.bfloat16

D_MODEL = 2048
CHUNK = 64
EPS = 1e-6
PAST_LEN = 4096

W_A = 1024
HEAD_K_A = 128
N_HEADS_A = 8

N_Q_B = 16
N_KV_B = 4
HEAD_DIM_B = 64
W_B = 1024
KV_W_B = 256
WINDOW = 128
ROT_DIM = 16
ROPE_THETA = 500000.0
ATTN_SCALE = HEAD_DIM_B ** -0.5

W_C = 1024
HEAD_DIM_C = 64
N_HEADS_C = 16
N_GROUPS_C = 4
D_STATE = 128
CONV_W = 4
CONV_DIM = 2048

LANES = 128
SUBLANES = 8
SUB_BLOCK = 16
MXU_COLS = 256
LOG2E = math.log2(math.e)
VMEM_LIMIT = 56 * 1024 * 1024

NT_DIMS = (((1,), (1,)), ((), ()))
TN_DIMS = (((0,), (0,)), ((), ()))


def _dot(a, b):
    return jnp.dot(a, b, preferred_element_type=F32)


def _dot_nt(a, b):
    return lax.dot_general(a, b, NT_DIMS, preferred_element_type=F32)


def _dot_tn(a, b):
    return lax.dot_general(a, b, TN_DIMS, preferred_element_type=F32)


def _prenorm_bf16(x, w):
    r = lax.rsqrt(jnp.mean(x * x, axis=-1, keepdims=True) + EPS)
    return ((x * r) * w).astype(BF16)


def _sigmoid(x):
    return 0.5 * jnp.tanh(0.5 * x) + 0.5


def _silu(x):
    return x * _sigmoid(x)


def _softplus0(x):
    return jnp.maximum(x, 0.0) + jnp.log(1.0 + jnp.exp(-jnp.abs(x)))


def _tri_bf16(n):
    r = lax.broadcasted_iota(jnp.int32, (n, n), 0)
    c = lax.broadcasted_iota(jnp.int32, (n, n), 1)
    return (r >= c).astype(BF16)


def _cumsum_rows(tri, x):
    hi = x.astype(BF16)
    r1 = x - hi.astype(F32)
    mid = r1.astype(BF16)
    lo = (r1 - mid.astype(F32)).astype(BF16)
    return _dot(tri, hi) + _dot(tri, mid) + _dot(tri, lo)


def _round_robin(gens, background=None, heavy_round=0):
    results = [None] * len(gens)
    live = list(range(len(gens)))
    rnd = 0
    while live:
        if background is not None and rnd == heavy_round:
            for _ in background:
                pass
        for n in list(live):
            try:
                next(gens[n])
            except StopIteration as stop:
                results[n] = stop.value
                live.remove(n)
        rnd += 1
    return results


def _hgrn_kernel(x_ref, nw_ref, w_ref, lb_ref, hn_ref, s0_ref, y_ref, snew_ref,
                 proj_s, h_s, ybuf_s, st_s, *, tl, ni):
    n = pl.program_id(0)
    n_chunks = tl // CHUNK
    i_lag = jnp.maximum(n - 1, 0) % ni
    nxt = n % 2
    cur = 1 - nxt

    @pl.when(n == 0)
    def _():
        proj_s[...] = jnp.zeros_like(proj_s)

    @pl.when(i_lag == 0)
    def _():
        for hd in range(N_HEADS_A):
            st_s[hd] = s0_ref[0, hd].T

    h_s[...] = _prenorm_bf16(x_ref[0], nw_ref[...])
    n_col_blocks = w_ref.shape[0]
    cols_per_chunk = n_col_blocks // n_chunks

    def project(c):
        for j in range(cols_per_chunk):
            jb = c * cols_per_chunk + j
            res = _dot(h_s[...], w_ref[jb])
            proj_s[nxt, 2 * jb] = res[:, :LANES]
            proj_s[nxt, 2 * jb + 1] = res[:, LANES:]
            yield

    tri = _tri_bf16(CHUNK)
    ones = jnp.ones((LANES, LANES), BF16)
    n_sub = CHUNK // SUB_BLOCK
    lane8 = lax.broadcasted_iota(jnp.int32, (SUBLANES, LANES), 1)
    sub8 = lax.broadcasted_iota(jnp.int32, (SUBLANES, LANES), 0)
    row64 = lax.broadcasted_iota(jnp.int32, (CHUNK, CHUNK), 0)
    col64 = lax.broadcasted_iota(jnp.int32, (CHUNK, CHUNK), 1)
    below_diag_block = row64 // SUB_BLOCK > col64 // SUB_BLOCK
    zero8 = jnp.zeros((SUBLANES, LANES), F32)

    def zeros(n):
        return jnp.zeros((n, LANES), F32)

    def piece(a, m, u):
        lo = SUB_BLOCK * m + SUBLANES * u
        return a[lo:lo + SUBLANES]

    def unit(hd, r0):
        rows = pl.ds(r0, CHUNK)
        aq = proj_s[cur, hd, rows, :]
        z = proj_s[cur, N_HEADS_A + hd, rows, :]
        v = proj_s[cur, 2 * N_HEADS_A + hd, rows, :].astype(BF16)
        ag = proj_s[cur, 3 * N_HEADS_A + hd, rows, :]
        log_lb = lb_ref[0, hd]
        log1m_lb = lb_ref[1, hd]

        z2 = z * LOG2E
        log_sig = jnp.minimum(z2, 0.0) - jnp.log2(1.0 + jnp.exp2(-jnp.abs(z2)))
        cterm = log1m_lb + log_sig
        log_f = jnp.maximum(log_lb, cterm) + jnp.log2(1.0 + jnp.exp2(-jnp.abs(log_lb - cterm)))
        log_k = cterm - z2
        q = _silu(aq)

        b2 = _cumsum_rows(tri, log_f)
        yield
        c2 = b2 - log_k
        b2_last = b2[CHUNK - 1:CHUNK, :]
        st = st_s[hd]

        o = _dot_nt((q * jnp.exp2(b2)).astype(BF16), st.astype(BF16))
        yield

        q_slabs, k_slabs = [], []
        for m in range(1, n_sub):
            lo = SUB_BLOCK * m
            ref = b2[lo - 1:lo, :]
            qm = q[lo:lo + SUB_BLOCK] * jnp.exp2(b2[lo:lo + SUB_BLOCK] - ref)
            q_parts = [zeros(lo), qm] + ([zeros(CHUNK - lo - SUB_BLOCK)] if lo + SUB_BLOCK < CHUNK else [])
            q_slabs.append(jnp.concatenate(q_parts, axis=0))
            k_slabs.append(jnp.concatenate([jnp.exp2(ref - c2[:lo]), zeros(CHUNK - lo)], axis=0))
        a_off = _dot_nt(jnp.concatenate(q_slabs, axis=1).astype(BF16),
                        jnp.concatenate(k_slabs, axis=1).astype(BF16))
        yield

        order = [(s, m, u) for s in range(SUB_BLOCK) for m in range(n_sub) for u in range(2)
                 if not (u == 0 and s >= SUBLANES)]
        pieces = []
        for s, m, u in order:
            cs = c2[SUB_BLOCK * m + s:SUB_BLOCK * m + s + 1]
            pieces.append(piece(q, m, u) * jnp.exp2(piece(b2, m, u) - cs))
        sums = _dot(jnp.concatenate(pieces, axis=0).astype(BF16), ones)
        yield
        d = {(m, u): zero8 for m in range(n_sub) for u in range(2)}
        for n, (s, m, u) in enumerate(order):
            d[(m, u)] = jnp.where(lane8 == s, sums[SUBLANES * n:SUBLANES * (n + 1)], d[(m, u)])
        diag_rows = []
        for m in range(n_sub):
            for u in range(2):
                dm = jnp.where(sub8 + SUBLANES * u >= lane8, d[(m, u)], 0.0)
                diag_rows.append(pltpu.roll(dm, SUB_BLOCK * m, 1) if m else dm)
        a_diag = jnp.concatenate(diag_rows, axis=0)[:, :CHUNK]
        a_full = jnp.where(below_diag_block, a_off, a_diag)
        o = o + _dot(a_full.astype(BF16), v)
        yield

        r = lax.rsqrt(jnp.mean(o * o, axis=-1, keepdims=True) + EPS)
        y = ((o * r) * hn_ref[...]) * _silu(ag)

        st_new = st * jnp.exp2(b2_last) + _dot_tn(v, jnp.exp2(b2_last - c2).astype(BF16))
        return y.astype(BF16), st_new

    def body(c, carry):
        r0 = pl.multiple_of(c * CHUNK, CHUNK)
        gens = [unit(hd, r0) for hd in range(N_HEADS_A)] + [project(c)]
        results = _round_robin(gens)[:N_HEADS_A]
        for hd, (y, st_new) in enumerate(results):
            ybuf_s[hd, pl.ds(r0, CHUNK), :] = y
            st_s[hd] = st_new
        return carry

    lax.fori_loop(0, n_chunks, body, 0)

    for hd in range(N_HEADS_A):
        y_ref[0, :, LANES * hd:LANES * (hd + 1)] = ybuf_s[hd]

    @pl.when(i_lag == ni - 1)
    def _():
        for hd in range(N_HEADS_A):
            snew_ref[0, hd] = st_s[hd].T


def _lagged_maps(n_blocks, ni):
    def split(m):
        return m // ni, m % ni
    x_map = lambda n: split(jnp.minimum(n, n_blocks - 1)) + (0,)
    y_map = lambda n: split(jnp.maximum(n - 1, 0)) + (0,)
    pos_map = lambda n: (jnp.minimum(n, n_blocks - 1) % ni, 0)
    batch_map = lambda nd: (lambda n: (jnp.maximum(n - 1, 0) // ni,) + (0,) * (nd - 1))
    return x_map, y_map, pos_map, batch_map


def _hgrn_call(x, nw, w_a, lb_pack, hn, s0, tl):
    bsz, seq, _ = x.shape
    ni = seq // tl
    n_blocks = bsz * ni
    x_map, y_map, _, batch_map = _lagged_maps(n_blocks, ni)
    const2 = lambda n: (0, 0)
    return pl.pallas_call(
        functools.partial(_hgrn_kernel, tl=tl, ni=ni),
        grid=(n_blocks + 1,),
        in_specs=[
            pl.BlockSpec((1, tl, D_MODEL), x_map),
            pl.BlockSpec((1, D_MODEL), const2),
            pl.BlockSpec(w_a.shape, lambda n: (0, 0, 0), pipeline_mode=pl.Buffered(1)),
            pl.BlockSpec((2, N_HEADS_A, 1, HEAD_K_A), lambda n: (0, 0, 0, 0)),
            pl.BlockSpec((1, HEAD_K_A), const2),
            pl.BlockSpec((1, N_HEADS_A, HEAD_K_A, HEAD_K_A), batch_map(4)),
        ],
        out_specs=[
            pl.BlockSpec((1, tl, W_A), y_map),
            pl.BlockSpec((1, N_HEADS_A, HEAD_K_A, HEAD_K_A), batch_map(4)),
        ],
        out_shape=[
            jax.ShapeDtypeStruct((bsz, seq, W_A), BF16),
            jax.ShapeDtypeStruct((bsz, N_HEADS_A, HEAD_K_A, HEAD_K_A), F32),
        ],
        scratch_shapes=[
            pltpu.VMEM((2, 4 * N_HEADS_A, tl, LANES), F32),
            pltpu.VMEM((tl, D_MODEL), BF16),
            pltpu.VMEM((N_HEADS_A, tl, LANES), BF16),
            pltpu.VMEM((N_HEADS_A, HEAD_K_A, HEAD_K_A), F32),
        ],
        compiler_params=pltpu.CompilerParams(
            dimension_semantics=("arbitrary",), vmem_limit_bytes=VMEM_LIMIT),
        name="hgrn_mixer",
    )(x, nw, w_a, lb_pack, hn, s0)


def _rotate(xs, cos, sin_a, sin_b):
    return xs * cos + pltpu.roll(xs, LANES - ROT_DIM // 2, 1) * sin_a + pltpu.roll(xs, ROT_DIM // 2, 1) * sin_b


def _swa_kernel(*refs, tl, ni, n_blocks, has_cache):
    if has_cache:
        (x_ref, nw_ref, w_ref, cos_ref, sa_ref, sb_ref, sink_ref, kc_ref, vc_ref,
         y_ref, knew_ref, vnew_ref, h_s, q_s, g_s, kbuf, vbuf) = refs
    else:
        (x_ref, nw_ref, w_ref, cos_ref, sa_ref, sb_ref, sink_ref,
         y_ref, knew_ref, vnew_ref, h_s, q_s, g_s, kbuf, vbuf) = refs
    n = pl.program_id(0)
    n_chunks = tl // CHUNK
    band = WINDOW + CHUNK
    kvw = 2 * KV_W_B
    i_proj = jnp.minimum(n, n_blocks - 1) % ni
    i_lag = jnp.maximum(n - 1, 0) % ni
    nxt = n % 2
    cur = 1 - nxt

    @pl.when(n == 0)
    def _():
        for buf in (q_s, g_s, kbuf, vbuf):
            buf[...] = jnp.zeros_like(buf)

    @pl.when(i_proj == 0)
    def _():
        if has_cache:
            kbuf[nxt, 0:WINDOW, :] = kc_ref[0]
            vbuf[nxt, 0:WINDOW, :] = vc_ref[0]
        else:
            kbuf[nxt, 0:WINDOW, :] = jnp.zeros((WINDOW, kvw), F32)
            vbuf[nxt, 0:WINDOW, :] = jnp.zeros((WINDOW, kvw), F32)

    @pl.when(i_proj != 0)
    def _():
        kbuf[nxt, 0:WINDOW, :] = kbuf[cur, tl:tl + WINDOW, :]
        vbuf[nxt, 0:WINDOW, :] = vbuf[cur, tl:tl + WINDOW, :]

    h_s[...] = _prenorm_bf16(x_ref[0], nw_ref[...])

    def project(c):
        per = (W_B // MXU_COLS) // n_chunks
        new_rows = slice(WINDOW, WINDOW + tl)
        for j in range(c * per, (c + 1) * per):
            sl = slice(MXU_COLS * j, MXU_COLS * (j + 1))
            res = _dot(h_s[...], w_ref[:, sl])
            yield
            for u in range(MXU_COLS // LANES):
                q_s[nxt, :, MXU_COLS * j + LANES * u:MXU_COLS * j + LANES * (u + 1)] = _rotate(
                    res[:, LANES * u:LANES * (u + 1)], cos_ref[...], sa_ref[...], sb_ref[...])
            g_s[nxt, :, sl] = _dot(h_s[...], w_ref[:, W_B + 2 * kvw + MXU_COLS * j:W_B + 2 * kvw + MXU_COLS * (j + 1)])
            yield
            jk = j % (kvw // MXU_COLS)
            ksl = slice(MXU_COLS * jk, MXU_COLS * (jk + 1))
            if j < kvw // MXU_COLS:
                res = _dot(h_s[...], w_ref[:, W_B + MXU_COLS * jk:W_B + MXU_COLS * (jk + 1)])
                yield
                for u in range(MXU_COLS // LANES):
                    kbuf[nxt, new_rows, MXU_COLS * jk + LANES * u:MXU_COLS * jk + LANES * (u + 1)] = _rotate(
                        res[:, LANES * u:LANES * (u + 1)], cos_ref[...], sa_ref[...], sb_ref[...])
            else:
                vbuf[nxt, new_rows, ksl] = _dot(
                    h_s[...], w_ref[:, W_B + kvw + MXU_COLS * jk:W_B + kvw + MXU_COLS * (jk + 1)])
                yield

    lane = lax.broadcasted_iota(jnp.int32, (CHUNK, LANES), 1)
    low = lane < HEAD_DIM_B
    low1 = lax.broadcasted_iota(jnp.int32, (1, LANES), 1) < HEAD_DIM_B
    key_row = lax.broadcasted_iota(jnp.int32, (band, LANES), 0)

    def pair(c, j):
            r0 = CHUNK * c
            g = j // 2
            qp = q_s[cur, r0:r0 + CHUNK, LANES * j:LANES * (j + 1)]
            q2 = jnp.concatenate([jnp.where(low, qp, 0.0), jnp.where(low, 0.0, qp)], axis=0).astype(BF16)
            kb = kbuf[cur, r0:r0 + band, LANES * g:LANES * (g + 1)].astype(BF16)
            vb = vbuf[cur, r0:r0 + band, LANES * g:LANES * (g + 1)].astype(BF16)
            s = _dot_nt(kb, q2) * ATTN_SCALE
            yield
            if not has_cache:
                s = jnp.where(i_lag * tl + r0 - WINDOW + key_row >= 0, s, -jnp.inf)
            sk = jnp.where(low1, sink_ref[2 * j:2 * j + 1, :], sink_ref[2 * j + 1:2 * j + 2, :])
            m = jnp.maximum(jnp.max(s, axis=0, keepdims=True), sk)
            p = jnp.exp(s - m)
            den = jnp.sum(p, axis=0, keepdims=True) + jnp.exp(sk - m)
            o2 = _dot_tn((p * (1.0 / den)).astype(BF16), vb)
            yield
            o = jnp.where(low, o2[:CHUNK], o2[CHUNK:])
            y = o * _silu(g_s[cur, r0:r0 + CHUNK, LANES * j:LANES * (j + 1)])
            y_ref[0, r0:r0 + CHUNK, LANES * j:LANES * (j + 1)] = y.astype(BF16)

    for c in range(n_chunks):
        _round_robin([pair(c, j) for j in range(N_Q_B // 2)], project(c), heavy_round=1)

    @pl.when(i_proj == ni - 1)
    def _():
        ktail = kbuf[nxt, tl:tl + WINDOW, :]
        vtail = vbuf[nxt, tl:tl + WINDOW, :]
        low_w = lax.broadcasted_iota(jnp.int32, (WINDOW, LANES), 1) < HEAD_DIM_B
        for u in range(KV_W_B // LANES):
            knew_ref[0, :, LANES * u:LANES * (u + 1)] = jnp.where(
                low_w, ktail[:, 2 * LANES * u:2 * LANES * u + LANES],
                ktail[:, 2 * LANES * u + LANES:2 * LANES * (u + 1)])
            vnew_ref[0, :, LANES * u:LANES * (u + 1)] = jnp.where(
                low_w, vtail[:, 2 * LANES * u:2 * LANES * u + LANES],
                vtail[:, 2 * LANES * u + LANES:2 * LANES * (u + 1)])


def _swa_call(x, nw, w_b, cos, sin_a, sin_b, sinks, kc, vc, tl):
    bsz, seq, _ = x.shape
    has_cache = kc is not None
    ni = seq // tl
    n_blocks = bsz * ni
    x_map, y_map, pos_map, _ = _lagged_maps(n_blocks, ni)
    proj_batch_map = lambda n: (jnp.minimum(n, n_blocks - 1) // ni, 0, 0)
    const2 = lambda n: (0, 0)
    wcols = w_b.shape[1]
    in_specs = [
        pl.BlockSpec((1, tl, D_MODEL), x_map),
        pl.BlockSpec((1, D_MODEL), const2),
        pl.BlockSpec((D_MODEL, wcols), const2, pipeline_mode=pl.Buffered(1)),
        pl.BlockSpec((tl, LANES), pos_map),
        pl.BlockSpec((tl, LANES), pos_map),
        pl.BlockSpec((tl, LANES), pos_map),
        pl.BlockSpec((N_Q_B, LANES), const2),
    ]
    args = [x, nw, w_b, cos, sin_a, sin_b, sinks]
    if has_cache:
        in_specs += [pl.BlockSpec((1, WINDOW, 2 * KV_W_B), proj_batch_map)] * 2
        args += [kc, vc]
    return pl.pallas_call(
        functools.partial(_swa_kernel, tl=tl, ni=ni, n_blocks=n_blocks, has_cache=has_cache),
        grid=(n_blocks + 1,),
        in_specs=in_specs,
        out_specs=[
            pl.BlockSpec((1, tl, W_B), y_map),
            pl.BlockSpec((1, WINDOW, KV_W_B), proj_batch_map),
            pl.BlockSpec((1, WINDOW, KV_W_B), proj_batch_map),
        ],
        out_shape=[
            jax.ShapeDtypeStruct((bsz, seq, W_B), BF16),
            jax.ShapeDtypeStruct((bsz, WINDOW, KV_W_B), F32),
            jax.ShapeDtypeStruct((bsz, WINDOW, KV_W_B), F32),
        ],
        scratch_shapes=[
            pltpu.VMEM((tl, D_MODEL), BF16),
            pltpu.VMEM((2, tl, W_B), F32),
            pltpu.VMEM((2, tl, W_B), F32),
            pltpu.VMEM((2, WINDOW + tl, 2 * KV_W_B), F32),
            pltpu.VMEM((2, WINDOW + tl, 2 * KV_W_B), F32),
        ],
        compiler_params=pltpu.CompilerParams(
            dimension_semantics=("arbitrary",), vmem_limit_bytes=VMEM_LIMIT),
        name="swa_mixer",
    )(*args)


CONV_PAD = 8


def _ssd_kernel(x_ref, nw_ref, w_ref, cw_ref, cb_ref, dtb_ref, alog_ref, dsk_ref, sn_ref, conv0_ref, s0_ref,
                y_ref, snew_ref, convnew_ref,
                ubuf, h_s, z_s, xc_s, xdt_s, bc_s, la_s, st_s, *, tl, ni, n_blocks):
    n = pl.program_id(0)
    n_chunks = tl // CHUNK
    gw = N_GROUPS_C * D_STATE
    hp = W_C // N_GROUPS_C
    i_proj = jnp.minimum(n, n_blocks - 1) % ni
    i_lag = jnp.maximum(n - 1, 0) % ni
    nxt = n % 2
    cur = 1 - nxt
    tail_rows = slice(CONV_PAD - (CONV_W - 1), CONV_PAD)

    @pl.when(n == 0)
    def _():
        for buf in (z_s, xc_s, xdt_s, bc_s, la_s):
            buf[...] = jnp.zeros_like(buf)

    @pl.when(i_proj == 0)
    def _():
        ubuf[tail_rows, :] = conv0_ref[0]

    @pl.when(i_lag == 0)
    def _():
        for g in range(N_GROUPS_C):
            st_s[g] = s0_ref[0, hp * g:hp * (g + 1), :].T

    h_s[...] = _prenorm_bf16(x_ref[0], nw_ref[...])

    def conv_act(sl):
        acc = cb_ref[:, sl]
        for t in range(CONV_W):
            lo = CONV_PAD - (CONV_W - 1) + t
            acc = acc + ubuf[lo:lo + tl, sl] * cw_ref[t:t + 1, sl]
        ubuf[tail_rows, sl] = ubuf[tl + CONV_PAD - (CONV_W - 1):tl + CONV_PAD, sl]
        return _silu(acc)

    def project(c):
        per = (W_C // MXU_COLS) // n_chunks
        for j in range(c * per, (c + 1) * per):
            sl = slice(MXU_COLS * j, MXU_COLS * (j + 1))
            z_s[nxt, :, sl] = _dot(h_s[...], w_ref[:, sl])
            yield
            dt_raw = _dot(h_s[...], w_ref[:, W_C + CONV_DIM + MXU_COLS * j:W_C + CONV_DIM + MXU_COLS * (j + 1)])
            yield
            dt = _softplus0(dt_raw + dtb_ref[:, sl])
            la_s[nxt, :, sl] = -dt * jnp.exp(alog_ref[:, sl])
            ubuf[CONV_PAD:CONV_PAD + tl, sl] = _dot(h_s[...], w_ref[:, W_C + MXU_COLS * j:W_C + MXU_COLS * (j + 1)])
            yield
            for u in range(MXU_COLS // LANES):
                usl = slice(MXU_COLS * j + LANES * u, MXU_COLS * j + LANES * (u + 1))
                act = conv_act(usl)
                xc_s[nxt, :, usl] = act
                xdt_s[nxt, :, usl] = act * dt[:, LANES * u:LANES * (u + 1)]
            bsl = slice(W_C + MXU_COLS * j, W_C + MXU_COLS * (j + 1))
            ubuf[CONV_PAD:CONV_PAD + tl, bsl] = _dot(h_s[...], w_ref[:, W_C + W_C + MXU_COLS * j:W_C + W_C + MXU_COLS * (j + 1)])
            yield
            for u in range(MXU_COLS // LANES):
                usl = slice(W_C + MXU_COLS * j + LANES * u, W_C + MXU_COLS * j + LANES * (u + 1))
                bc_s[nxt, :, MXU_COLS * j + LANES * u:MXU_COLS * j + LANES * (u + 1)] = conv_act(usl)

    tri = _tri_bf16(CHUNK)
    trow = lax.broadcasted_iota(jnp.int32, (CHUNK, LANES), 0)
    lane = lax.broadcasted_iota(jnp.int32, (CHUNK, LANES), 1)
    s_of_lane = lane % CHUNK
    causal = trow >= s_of_lane
    row2 = lax.broadcasted_iota(jnp.int32, (2 * CHUNK, LANES), 0)
    lane2 = lax.broadcasted_iota(jnp.int32, (2 * CHUNK, LANES), 1)
    blockdiag = (row2 < CHUNK) == (lane2 < HEAD_DIM_C)

    for c in range(n_chunks):
        rows = slice(CHUNK * c, CHUNK * (c + 1))

        def group(g, rows=rows):
            gsl = slice(hp * g, hp * (g + 1))
            bg = bc_s[cur, rows, D_STATE * g:D_STATE * (g + 1)].astype(BF16)
            cg = bc_s[cur, rows, gw + D_STATE * g:gw + D_STATE * (g + 1)].astype(BF16)
            cb2 = _dot_nt(cg, jnp.concatenate([bg, bg], axis=0))
            yield
            st = st_s[g]
            y_state = _dot(cg, st.astype(BF16))
            yield
            la = la_s[cur, rows, gsl]
            b = _cumsum_rows(tri, la)
            yield
            xdt = xdt_s[cur, rows, gsl]
            ys = []
            for u in range(2):
                usl = slice(LANES * u, LANES * (u + 1))
                b_row = jnp.sum(jnp.where(trow <= s_of_lane, la[:, usl], 0.0), axis=0, keepdims=True)
                decay = jnp.where(causal, jnp.exp(b[:, usl] - b_row), 0.0)
                wmat = (cb2 * decay).astype(BF16)
                xbd = jnp.where(blockdiag, jnp.concatenate([xdt[:, usl], xdt[:, usl]], axis=0), 0.0).astype(BF16)
                ys.append(_dot(wmat, xbd))
                yield
            y = jnp.concatenate(ys, axis=1) + y_state * jnp.exp(b)
            o = (y + dsk_ref[:, gsl] * xc_s[cur, rows, gsl]) * _silu(z_s[cur, rows, gsl])
            r = lax.rsqrt(jnp.mean(o * o, axis=-1, keepdims=True) + EPS)
            out = ((o * r) * sn_ref[:, gsl]).astype(BF16)
            b_last = b[CHUNK - 1:CHUNK, :]
            st_new = st * jnp.exp(b_last) + _dot_tn(bg, (xdt * jnp.exp(b_last - b)).astype(BF16))
            return out, st_new

        results = _round_robin([group(g) for g in range(N_GROUPS_C)] + [project(c)])[:N_GROUPS_C]
        for g, (out, st_new) in enumerate(results):
            y_ref[0, rows, hp * g:hp * (g + 1)] = out
            st_s[g] = st_new

    @pl.when(i_proj == ni - 1)
    def _():
        convnew_ref[0] = ubuf[tail_rows, :]

    @pl.when(i_lag == ni - 1)
    def _():
        for g in range(N_GROUPS_C):
            snew_ref[0, hp * g:hp * (g + 1), :] = st_s[g].T


def _ssd_call(x, nw, w_c, cw, cb, dtb, alog, dsk, sn, conv0, s0, tl):
    bsz, seq, _ = x.shape
    ni = seq // tl
    n_blocks = bsz * ni
    x_map, y_map, _, batch_map = _lagged_maps(n_blocks, ni)
    proj_batch_map = lambda n: (jnp.minimum(n, n_blocks - 1) // ni, 0, 0)
    const2 = lambda n: (0, 0)
    wcols = w_c.shape[1]
    return pl.pallas_call(
        functools.partial(_ssd_kernel, tl=tl, ni=ni, n_blocks=n_blocks),
        grid=(n_blocks + 1,),
        in_specs=[
            pl.BlockSpec((1, tl, D_MODEL), x_map),
            pl.BlockSpec((1, D_MODEL), const2),
            pl.BlockSpec((D_MODEL, wcols), const2, pipeline_mode=pl.Buffered(1)),
            pl.BlockSpec((CONV_W, CONV_DIM), const2),
            pl.BlockSpec((1, CONV_DIM), const2),
            pl.BlockSpec((1, W_C), const2),
            pl.BlockSpec((1, W_C), const2),
            pl.BlockSpec((1, W_C), const2),
            pl.BlockSpec((1, W_C), const2),
            pl.BlockSpec((1, CONV_W - 1, CONV_DIM), proj_batch_map),
            pl.BlockSpec((1, W_C, D_STATE), batch_map(3)),
        ],
        out_specs=[
            pl.BlockSpec((1, tl, W_C), y_map),
            pl.BlockSpec((1, W_C, D_STATE), batch_map(3)),
            pl.BlockSpec((1, CONV_W - 1, CONV_DIM), proj_batch_map),
        ],
        out_shape=[
            jax.ShapeDtypeStruct((bsz, seq, W_C), BF16),
            jax.ShapeDtypeStruct((bsz, W_C, D_STATE), F32),
            jax.ShapeDtypeStruct((bsz, CONV_W - 1, CONV_DIM), F32),
        ],
        scratch_shapes=[
            pltpu.VMEM((CONV_PAD + tl, CONV_DIM), F32),
            pltpu.VMEM((tl, D_MODEL), BF16),
            pltpu.VMEM((2, tl, W_C), F32),
            pltpu.VMEM((2, tl, W_C), F32),
            pltpu.VMEM((2, tl, W_C), F32),
            pltpu.VMEM((2, tl, 2 * N_GROUPS_C * D_STATE), F32),
            pltpu.VMEM((2, tl, W_C), F32),
            pltpu.VMEM((N_GROUPS_C, D_STATE, W_C // N_GROUPS_C), F32),
        ],
        compiler_params=pltpu.CompilerParams(
            dimension_semantics=("arbitrary",), vmem_limit_bytes=VMEM_LIMIT),
        name="ssd_mixer",
    )(x, nw, w_c, cw, cb, dtb, alog, dsk, sn, conv0, s0)


def _merge_kernel(x_ref, ya_ref, yb_ref, yc_ref, npre_ref, npost_ref, wg_ref, wbr_ref, wout_ref,
                  o_ref, h_s, m_s):
    j = pl.program_id(1)
    n_j = m_s.shape[0]

    @pl.when(j == 0)
    def _():
        h_s[...] = _prenorm_bf16(x_ref[...], npre_ref[...])

    h = h_s[...]
    merged = None
    for bi, y_ref in enumerate((ya_ref, yb_ref, yc_ref)):
        term = _sigmoid(_dot(h, wg_ref[bi])) * _dot(y_ref[...], wbr_ref[bi])
        merged = term if merged is None else merged + term
    m_s[j] = merged.astype(BF16)

    @pl.when(j == n_j - 1)
    def _():
        out = _dot(jnp.concatenate([m_s[n] for n in range(n_j)], axis=1), wout_ref[...])
        r = lax.rsqrt(jnp.mean(out * out, axis=-1, keepdims=True) + EPS)
        o_ref[...] = x_ref[...] + (out * r) * npost_ref[...]


def _merge_call(x2, ya, yb, yc, npre, npost, wg, wbr, wout, tm, tn):
    rows = x2.shape[0]
    grid = (rows // tm, D_MODEL // tn)
    const2 = lambda r, j: (0, 0)
    return pl.pallas_call(
        _merge_kernel,
        grid=grid,
        in_specs=[
            pl.BlockSpec((tm, D_MODEL), lambda r, j: (r, 0)),
            pl.BlockSpec((tm, W_A), lambda r, j: (r, 0)),
            pl.BlockSpec((tm, W_B), lambda r, j: (r, 0)),
            pl.BlockSpec((tm, W_C), lambda r, j: (r, 0)),
            pl.BlockSpec((1, D_MODEL), const2),
            pl.BlockSpec((1, D_MODEL), const2),
            pl.BlockSpec((3, D_MODEL, tn), lambda r, j: (0, 0, j)),
            pl.BlockSpec((3, W_A, tn), lambda r, j: (0, 0, j)),
            pl.BlockSpec((D_MODEL, D_MODEL), const2, pipeline_mode=pl.Buffered(1)),
        ],
        out_specs=pl.BlockSpec((tm, D_MODEL), lambda r, j: (r, 0)),
        out_shape=jax.ShapeDtypeStruct((rows, D_MODEL), F32),
        scratch_shapes=[
            pltpu.VMEM((tm, D_MODEL), BF16),
            pltpu.VMEM((D_MODEL // tn, tm, tn), BF16),
        ],
        compiler_params=pltpu.CompilerParams(
            dimension_semantics=("arbitrary", "arbitrary"), vmem_limit_bytes=VMEM_LIMIT),
        name="merge_out",
    )(x2, ya, yb, yc, npre, npost, wg, wbr, wout)


def _rope_tables(pos):
    half = ROT_DIM // 2
    inv = jnp.power(ROPE_THETA, -jnp.arange(half, dtype=F32) / half)
    ang = pos.astype(F32)[:, None] * inv[None, :]
    cos, sin = jnp.cos(ang), jnp.sin(ang)
    n = pos.shape[0]
    ones = jnp.ones((n, HEAD_DIM_B - ROT_DIM), F32)
    zeros = jnp.zeros((n, HEAD_DIM_B - ROT_DIM), F32)
    zh = jnp.zeros((n, half), F32)
    cos_t = jnp.concatenate([cos, cos, ones], axis=1)
    sin_a = jnp.concatenate([-sin, zh, zeros], axis=1)
    sin_b = jnp.concatenate([zh, sin, zeros], axis=1)
    tile = lambda t: jnp.concatenate([t, t], axis=1)
    return tile(cos_t), tile(sin_a), tile(sin_b)


def _dup_heads(t):
    lead = t.shape[:-1]
    t4 = t.reshape(lead + (N_KV_B, 1, HEAD_DIM_B))
    return jnp.broadcast_to(t4, lead + (N_KV_B, 2, HEAD_DIM_B)).reshape(lead + (2 * KV_W_B,))


def _col_blocks(w):
    return w.reshape(w.shape[0], -1, MXU_COLS).transpose(1, 0, 2)


def _expand_heads(t):
    return jnp.repeat(t, HEAD_DIM_C, axis=-1)


def _layer(x, tl, tables, caches, lw):
    bsz, seq, _ = x.shape
    cos, sin_a, sin_b = tables
    kc, vc, s_hgrn, s_ssm, s_conv = caches
    if s_hgrn is None:
        s_hgrn = jnp.zeros((bsz, N_HEADS_A, HEAD_K_A, HEAD_K_A), F32)
        s_ssm = jnp.zeros((bsz, W_C, D_STATE), F32)
        s_conv = jnp.zeros((bsz, CONV_W - 1, CONV_DIM), F32)
    else:
        s_ssm = s_ssm.reshape(bsz, W_C, D_STATE)
        kc = _dup_heads(kc.reshape(bsz, WINDOW, KV_W_B))
        vc = _dup_heads(vc.reshape(bsz, WINDOW, KV_W_B))

    ya, hgrn_new = _hgrn_call(x, lw["norm_pre"], lw["w_a"], lw["lb_pack"], lw["hgrn_norm"], s_hgrn, tl)
    yb, k_new, v_new = _swa_call(x, lw["norm_pre"], lw["w_b"], cos, sin_a, sin_b, lw["sinks"], kc, vc, tl)
    yc, ssm_new, conv_new = _ssd_call(x, lw["norm_pre"], lw["w_c"], lw["conv_w"], lw["conv_b"], lw["dt_bias"],
                                      lw["a_log"], lw["d_skip"], lw["ssm_norm"], s_conv, s_ssm, tl)
    rows = bsz * seq
    tm = min(512, rows)
    x_new = _merge_call(x.reshape(rows, D_MODEL), ya.reshape(rows, W_A), yb.reshape(rows, W_B),
                        yc.reshape(rows, W_C), lw["norm_pre"], lw["norm_post"], lw["w_g"], lw["w_br"],
                        lw["w_out"], tm, 256)
    states = (k_new.reshape(bsz, WINDOW, N_KV_B, HEAD_DIM_B), v_new.reshape(bsz, WINDOW, N_KV_B, HEAD_DIM_B),
              hgrn_new, ssm_new.reshape(bsz, N_HEADS_C, HEAD_DIM_C, D_STATE), conv_new)
    return x_new.reshape(bsz, seq, D_MODEL), states


def kernel(x_prompt, x_sample, cache_swa_k, cache_swa_v, state_hgrn, state_ssm, state_conv, norm_pre, norm_post, w_in, hgrn_lb_logits, hgrn_norm, swa_sinks, conv_w, conv_b, dt_bias, a_log, d_skip, ssm_norm, w_branch_a, w_branch_b, w_branch_c, w_out):
    depth = w_in.shape[0]
    lbp = jax.nn.softmax(hgrn_lb_logits.astype(F32), axis=0)
    lbc = jnp.cumsum(lbp, axis=0)
    lb_all = lbc - lbc[0:1]

    offs = np.cumsum([0, W_A, W_A, W_A, W_A, W_B, KV_W_B, KV_W_B, W_B, W_C, CONV_DIM, N_HEADS_C,
                      D_MODEL, D_MODEL, D_MODEL])
    col = lambda l, a, b: w_in[l, :, int(offs[a]):int(offs[b])]

    tables_p = _rope_tables(jnp.arange(x_prompt.shape[1], dtype=jnp.int32))
    tables_s = _rope_tables(PAST_LEN + jnp.arange(x_sample.shape[1], dtype=jnp.int32))

    xp, xs = x_prompt, x_sample
    pst, sst = [], []
    for l in range(depth):
        lb = lb_all[l].reshape(N_HEADS_A, 1, HEAD_K_A)
        lw = {
            "norm_pre": norm_pre[l].reshape(1, D_MODEL),
            "norm_post": norm_post[l].reshape(1, D_MODEL),
            "w_a": _col_blocks(col(l, 0, 4).astype(BF16)),
            "w_b": jnp.concatenate([col(l, 4, 5), _dup_heads(col(l, 5, 6)), _dup_heads(col(l, 6, 7)),
                                    col(l, 7, 8)], axis=1).astype(BF16),
            "w_c": jnp.concatenate([col(l, 8, 9), col(l, 9, 10), _expand_heads(col(l, 10, 11))],
                                   axis=1).astype(BF16),
            "w_g": jnp.stack([col(l, 11, 12), col(l, 12, 13), col(l, 13, 14)]).astype(BF16),
            "w_br": jnp.stack([w_branch_a[l], w_branch_b[l], w_branch_c[l]]).astype(BF16),
            "w_out": w_out[l].astype(BF16),
            "lb_pack": jnp.stack([jnp.log(lb), jnp.log1p(-lb)]) * LOG2E,
            "hgrn_norm": hgrn_norm[l].reshape(1, HEAD_K_A),
            "sinks": jnp.broadcast_to(swa_sinks[l].astype(F32)[:, None], (N_Q_B, LANES)),
            "conv_w": conv_w[l],
            "conv_b": conv_b[l].reshape(1, CONV_DIM),
            "dt_bias": _expand_heads(dt_bias[l].astype(F32)).reshape(1, W_C),
            "a_log": _expand_heads(a_log[l].astype(F32)).reshape(1, W_C),
            "d_skip": _expand_heads(d_skip[l].astype(F32)).reshape(1, W_C),
            "ssm_norm": ssm_norm[l].reshape(1, W_C),
        }
        xp, sp = _layer(xp, 256, tables_p, (None, None, None, None, None), lw)
        xs, ss = _layer(xs, CHUNK, tables_s,
                        (cache_swa_k[l], cache_swa_v[l], state_hgrn[l], state_ssm[l], state_conv[l]), lw)
        pst.append(sp)
        sst.append(ss)

    stack = lambda sts, k: jnp.stack([s[k] for s in sts])
    return (xp, xs,
            stack(pst, 0), stack(pst, 1), stack(pst, 2), stack(pst, 3), stack(pst, 4),
            stack(sst, 0), stack(sst, 1), stack(sst, 2), stack(sst, 3), stack(sst, 4))
```

```python
import functools
import math

import jax
import jax.numpy as jnp
import numpy as np
from jax import lax
from jax.experimental import pallas as pl
from jax.experimental.pallas import tpu as pltpu

F32 = jnp.float32
BF16 = jnp.bfloat16

D_MODEL = 2048
CHUNK = 64
EPS = 1e-6
PAST_LEN = 4096

W_A = 1024
HEAD_K_A = 128
N_HEADS_A = 8

N_Q_B = 16
N_KV_B = 4
HEAD_DIM_B = 64
W_B = 1024
KV_W_B = 256
WINDOW = 128
ROT_DIM = 16
ROPE_THETA = 500000.0
ATTN_SCALE = HEAD_DIM_B ** -0.5

W_C = 1024
HEAD_DIM_C = 64
N_HEADS_C = 16
N_GROUPS_C = 4
D_STATE = 128
CONV_W = 4
CONV_DIM = 2048

LANES = 128
SUBLANES = 8
SUB_BLOCK = 16
MXU_COLS = 256
PROMPT_ROWS = 256
MERGE_ROWS = 512
LOG2E = math.log2(math.e)
VMEM_LIMIT = 56 * 1024 * 1024

NT_DIMS = (((1,), (1,)), ((), ()))
TN_DIMS = (((0,), (0,)), ((), ()))


def _dot(a, b):
    return jnp.dot(a, b, preferred_element_type=F32)


def _dot_nt(a, b):
    return lax.dot_general(a, b, NT_DIMS, preferred_element_type=F32)


def _dot_tn(a, b):
    return lax.dot_general(a, b, TN_DIMS, preferred_element_type=F32)


def _prenorm_bf16(x, w):
    r = lax.rsqrt(jnp.mean(x * x, axis=-1, keepdims=True) + EPS)
    return ((x * r) * w).astype(BF16)


def _sigmoid(x):
    return 0.5 * jnp.tanh(0.5 * x) + 0.5


def _silu(x):
    return x * _sigmoid(x)


def _softplus0(x):
    return jnp.maximum(x, 0.0) + jnp.log(1.0 + jnp.exp(-jnp.abs(x)))


def _tri_bf16(n):
    r = lax.broadcasted_iota(jnp.int32, (n, n), 0)
    c = lax.broadcasted_iota(jnp.int32, (n, n), 1)
    return (r >= c).astype(BF16)


def _cumsum_rows(tri, x):
    hi = x.astype(BF16)
    r1 = x - hi.astype(F32)
    mid = r1.astype(BF16)
    lo = (r1 - mid.astype(F32)).astype(BF16)
    return _dot(tri, hi) + _dot(tri, mid) + _dot(tri, lo)


def _round_robin(gens, background=None, heavy_round=0):
    results = [None] * len(gens)
    live = list(range(len(gens)))
    rnd = 0
    while live:
        if background is not None and rnd == heavy_round:
            for _ in background:
                pass
        for n in list(live):
            try:
                next(gens[n])
            except StopIteration as stop:
                results[n] = stop.value
                live.remove(n)
        rnd += 1
    return results


def _hgrn_kernel(h_ref, w_ref, lb_ref, hn_ref, s0_ref, y_ref, snew_ref,
                 proj_s, ybuf_s, st_s, *, tl, ni):
    n = pl.program_id(0)
    n_chunks = tl // CHUNK
    i_lag = jnp.maximum(n - 1, 0) % ni
    nxt = n % 2
    cur = 1 - nxt

    @pl.when(n == 0)
    def _():
        proj_s[...] = jnp.zeros_like(proj_s)

    @pl.when(i_lag == 0)
    def _():
        for hd in range(N_HEADS_A):
            st_s[hd] = s0_ref[0, hd].T

    n_col_blocks = w_ref.shape[0]
    cols_per_chunk = n_col_blocks // n_chunks

    def project(c):
        for j in range(cols_per_chunk):
            jb = c * cols_per_chunk + j
            res = _dot(h_ref[0], w_ref[jb])
            proj_s[nxt, 2 * jb] = res[:, :LANES]
            proj_s[nxt, 2 * jb + 1] = res[:, LANES:]
            yield

    tri = _tri_bf16(CHUNK)
    ones = jnp.ones((LANES, LANES), BF16)
    n_sub = CHUNK // SUB_BLOCK
    lane8 = lax.broadcasted_iota(jnp.int32, (SUBLANES, LANES), 1)
    sub8 = lax.broadcasted_iota(jnp.int32, (SUBLANES, LANES), 0)
    row64 = lax.broadcasted_iota(jnp.int32, (CHUNK, CHUNK), 0)
    col64 = lax.broadcasted_iota(jnp.int32, (CHUNK, CHUNK), 1)
    below_diag_block = row64 // SUB_BLOCK > col64 // SUB_BLOCK
    zero8 = jnp.zeros((SUBLANES, LANES), F32)

    def zeros(n):
        return jnp.zeros((n, LANES), F32)

    def piece(a, m, u):
        lo = SUB_BLOCK * m + SUBLANES * u
        return a[lo:lo + SUBLANES]

    def unit(hd, r0):
        rows = pl.ds(r0, CHUNK)
        aq = proj_s[cur, hd, rows, :]
        z = proj_s[cur, N_HEADS_A + hd, rows, :]
        v = proj_s[cur, 2 * N_HEADS_A + hd, rows, :].astype(BF16)
        ag = proj_s[cur, 3 * N_HEADS_A + hd, rows, :]
        log_lb = lb_ref[0, hd]
        log1m_lb = lb_ref[1, hd]

        z2 = z * LOG2E
        log_sig = jnp.minimum(z2, 0.0) - jnp.log2(1.0 + jnp.exp2(-jnp.abs(z2)))
        cterm = log1m_lb + log_sig
        log_f = jnp.maximum(log_lb, cterm) + jnp.log2(1.0 + jnp.exp2(-jnp.abs(log_lb - cterm)))
        log_k = cterm - z2
        q = _silu(aq)

        b2 = _cumsum_rows(tri, log_f)
        yield
        c2 = b2 - log_k
        b2_last = b2[CHUNK - 1:CHUNK, :]
        st = st_s[hd]

        o = _dot_nt((q * jnp.exp2(b2)).astype(BF16), st.astype(BF16))
        yield

        q_slabs, k_slabs = [], []
        for m in range(1, n_sub):
            lo = SUB_BLOCK * m
            ref = b2[lo - 1:lo, :]
            qm = q[lo:lo + SUB_BLOCK] * jnp.exp2(b2[lo:lo + SUB_BLOCK] - ref)
            q_parts = [zeros(lo), qm] + ([zeros(CHUNK - lo - SUB_BLOCK)] if lo + SUB_BLOCK < CHUNK else [])
            q_slabs.append(jnp.concatenate(q_parts, axis=0))
            k_slabs.append(jnp.concatenate([jnp.exp2(ref - c2[:lo]), zeros(CHUNK - lo)], axis=0))
        a_off = _dot_nt(jnp.concatenate(q_slabs, axis=1).astype(BF16),
                        jnp.concatenate(k_slabs, axis=1).astype(BF16))
        yield

        order = [(s, m, u) for s in range(SUB_BLOCK) for m in range(n_sub) for u in range(2)
                 if not (u == 0 and s >= SUBLANES)]
        pieces = []
        for s, m, u in order:
            cs = c2[SUB_BLOCK * m + s:SUB_BLOCK * m + s + 1]
            pieces.append(piece(q, m, u) * jnp.exp2(piece(b2, m, u) - cs))
        sums = _dot(jnp.concatenate(pieces, axis=0).astype(BF16), ones)
        yield
        d = {(m, u): zero8 for m in range(n_sub) for u in range(2)}
        for n, (s, m, u) in enumerate(order):
            d[(m, u)] = jnp.where(lane8 == s, sums[SUBLANES * n:SUBLANES * (n + 1)], d[(m, u)])
        diag_rows = []
        for m in range(n_sub):
            for u in range(2):
                dm = jnp.where(sub8 + SUBLANES * u >= lane8, d[(m, u)], 0.0)
                diag_rows.append(pltpu.roll(dm, SUB_BLOCK * m, 1) if m else dm)
        a_diag = jnp.concatenate(diag_rows, axis=0)[:, :CHUNK]
        a_full = jnp.where(below_diag_block, a_off, a_diag)
        o = o + _dot(a_full.astype(BF16), v)
        yield

        r = lax.rsqrt(jnp.mean(o * o, axis=-1, keepdims=True) + EPS)
        y = ((o * r) * hn_ref[...]) * _silu(ag)

        st_new = st * jnp.exp2(b2_last) + _dot_tn(v, jnp.exp2(b2_last - c2).astype(BF16))
        return y.astype(BF16), st_new

    def body(c, carry):
        r0 = pl.multiple_of(c * CHUNK, CHUNK)
        gens = [unit(hd, r0) for hd in range(N_HEADS_A)] + [project(c)]
        results = _round_robin(gens)[:N_HEADS_A]
        for hd, (y, st_new) in enumerate(results):
            ybuf_s[hd, pl.ds(r0, CHUNK), :] = y
            st_s[hd] = st_new
        return carry

    lax.fori_loop(0, n_chunks, body, 0)

    for hd in range(N_HEADS_A):
        y_ref[0, :, LANES * hd:LANES * (hd + 1)] = ybuf_s[hd]

    @pl.when(i_lag == ni - 1)
    def _():
        for hd in range(N_HEADS_A):
            snew_ref[0, hd] = st_s[hd].T


def _lagged_maps(n_blocks, ni):
    def split(m):
        return m // ni, m % ni
    x_map = lambda n: split(jnp.minimum(n, n_blocks - 1)) + (0,)
    y_map = lambda n: split(jnp.maximum(n - 1, 0)) + (0,)
    pos_map = lambda n: (jnp.minimum(n, n_blocks - 1) % ni, 0)
    batch_map = lambda nd: (lambda n: (jnp.maximum(n - 1, 0) // ni,) + (0,) * (nd - 1))
    return x_map, y_map, pos_map, batch_map


def _hgrn_call(h, w_a, lb_pack, hn, s0, tl):
    bsz, seq, _ = h.shape
    ni = seq // tl
    n_blocks = bsz * ni
    x_map, y_map, _, batch_map = _lagged_maps(n_blocks, ni)
    const2 = lambda n: (0, 0)
    return pl.pallas_call(
        functools.partial(_hgrn_kernel, tl=tl, ni=ni),
        grid=(n_blocks + 1,),
        in_specs=[
            pl.BlockSpec((1, tl, D_MODEL), x_map),
            pl.BlockSpec(w_a.shape, lambda n: (0, 0, 0), pipeline_mode=pl.Buffered(1)),
            pl.BlockSpec((2, N_HEADS_A, 1, HEAD_K_A), lambda n: (0, 0, 0, 0)),
            pl.BlockSpec((1, HEAD_K_A), const2),
            pl.BlockSpec((1, N_HEADS_A, HEAD_K_A, HEAD_K_A), batch_map(4)),
        ],
        out_specs=[
            pl.BlockSpec((1, tl, W_A), y_map),
            pl.BlockSpec((1, N_HEADS_A, HEAD_K_A, HEAD_K_A), batch_map(4)),
        ],
        out_shape=[
            jax.ShapeDtypeStruct((bsz, seq, W_A), BF16),
            jax.ShapeDtypeStruct((bsz, N_HEADS_A, HEAD_K_A, HEAD_K_A), F32),
        ],
        scratch_shapes=[
            pltpu.VMEM((2, 4 * N_HEADS_A, tl, LANES), F32),
            pltpu.VMEM((N_HEADS_A, tl, LANES), BF16),
            pltpu.VMEM((N_HEADS_A, HEAD_K_A, HEAD_K_A), F32),
        ],
        compiler_params=pltpu.CompilerParams(
            dimension_semantics=("arbitrary",), vmem_limit_bytes=VMEM_LIMIT),
        name="hgrn_mixer",
    )(h, w_a, lb_pack, hn, s0)


def _rotate(xs, cos, sin_a, sin_b):
    return xs * cos + pltpu.roll(xs, LANES - ROT_DIM // 2, 1) * sin_a + pltpu.roll(xs, ROT_DIM // 2, 1) * sin_b


def _swa_kernel(*refs, tl, ni, n_blocks, has_cache):
    if has_cache:
        (h_ref, w_ref, cos_ref, sa_ref, sb_ref, sink_ref, kc_ref, vc_ref,
         y_ref, knew_ref, vnew_ref, q_s, g_s, kbuf, vbuf) = refs
    else:
        (h_ref, w_ref, cos_ref, sa_ref, sb_ref, sink_ref,
         y_ref, knew_ref, vnew_ref, q_s, g_s, kbuf, vbuf) = refs
    n = pl.program_id(0)
    n_chunks = tl // CHUNK
    band = WINDOW + CHUNK
    kvw = 2 * KV_W_B
    i_proj = jnp.minimum(n, n_blocks - 1) % ni
    i_lag = jnp.maximum(n - 1, 0) % ni
    nxt = n % 2
    cur = 1 - nxt

    @pl.when(n == 0)
    def _():
        for buf in (q_s, g_s, kbuf, vbuf):
            buf[...] = jnp.zeros_like(buf)

    @pl.when(i_proj == 0)
    def _():
        if has_cache:
            kbuf[nxt, 0:WINDOW, :] = kc_ref[0]
            vbuf[nxt, 0:WINDOW, :] = vc_ref[0]
        else:
            kbuf[nxt, 0:WINDOW, :] = jnp.zeros((WINDOW, kvw), F32)
            vbuf[nxt, 0:WINDOW, :] = jnp.zeros((WINDOW, kvw), F32)

    @pl.when(i_proj != 0)
    def _():
        kbuf[nxt, 0:WINDOW, :] = kbuf[cur, tl:tl + WINDOW, :]
        vbuf[nxt, 0:WINDOW, :] = vbuf[cur, tl:tl + WINDOW, :]

    def project(c):
        per = (W_B // MXU_COLS) // n_chunks
        new_rows = slice(WINDOW, WINDOW + tl)
        for j in range(c * per, (c + 1) * per):
            sl = slice(MXU_COLS * j, MXU_COLS * (j + 1))
            res = _dot(h_ref[0], w_ref[:, sl])
            yield
            for u in range(MXU_COLS // LANES):
                q_s[nxt, :, MXU_COLS * j + LANES * u:MXU_COLS * j + LANES * (u + 1)] = _rotate(
                    res[:, LANES * u:LANES * (u + 1)], cos_ref[...], sa_ref[...], sb_ref[...])
            g_s[nxt, :, sl] = _dot(h_ref[0], w_ref[:, W_B + 2 * kvw + MXU_COLS * j:W_B + 2 * kvw + MXU_COLS * (j + 1)])
            yield
            jk = j % (kvw // MXU_COLS)
            ksl = slice(MXU_COLS * jk, MXU_COLS * (jk + 1))
            if j < kvw // MXU_COLS:
                res = _dot(h_ref[0], w_ref[:, W_B + MXU_COLS * jk:W_B + MXU_COLS * (jk + 1)])
                yield
                for u in range(MXU_COLS // LANES):
                    kbuf[nxt, new_rows, MXU_COLS * jk + LANES * u:MXU_COLS * jk + LANES * (u + 1)] = _rotate(
                        res[:, LANES * u:LANES * (u + 1)], cos_ref[...], sa_ref[...], sb_ref[...])
            else:
                vbuf[nxt, new_rows, ksl] = _dot(
                    h_ref[0], w_ref[:, W_B + kvw + MXU_COLS * jk:W_B + kvw + MXU_COLS * (jk + 1)])
                yield

    lane = lax.broadcasted_iota(jnp.int32, (CHUNK, LANES), 1)
    low = lane < HEAD_DIM_B
    low1 = lax.broadcasted_iota(jnp.int32, (1, LANES), 1) < HEAD_DIM_B
    key_row = lax.broadcasted_iota(jnp.int32, (band, LANES), 0)

    def pair(c, j):
            r0 = CHUNK * c
            g = j // 2
            qp = q_s[cur, r0:r0 + CHUNK, LANES * j:LANES * (j + 1)]
            q2 = jnp.concatenate([jnp.where(low, qp, 0.0), jnp.where(low, 0.0, qp)], axis=0).astype(BF16)
            kb = kbuf[cur, r0:r0 + band, LANES * g:LANES * (g + 1)].astype(BF16)
            vb = vbuf[cur, r0:r0 + band, LANES * g:LANES * (g + 1)].astype(BF16)
            s = _dot_nt(kb, q2) * ATTN_SCALE
            yield
            if not has_cache:
                s = jnp.where(i_lag * tl + r0 - WINDOW + key_row >= 0, s, -jnp.inf)
            sk = jnp.where(low1, sink_ref[2 * j:2 * j + 1, :], sink_ref[2 * j + 1:2 * j + 2, :])
            m = jnp.maximum(jnp.max(s, axis=0, keepdims=True), sk)
            p = jnp.exp(s - m)
            den = jnp.sum(p, axis=0, keepdims=True) + jnp.exp(sk - m)
            o2 = _dot_tn((p * (1.0 / den)).astype(BF16), vb)
            yield
            o = jnp.where(low, o2[:CHUNK], o2[CHUNK:])
            y = o * _silu(g_s[cur, r0:r0 + CHUNK, LANES * j:LANES * (j + 1)])
            y_ref[0, r0:r0 + CHUNK, LANES * j:LANES * (j + 1)] = y.astype(BF16)

    for c in range(n_chunks):
        _round_robin([pair(c, j) for j in range(N_Q_B // 2)], project(c), heavy_round=1)

    @pl.when(i_proj == ni - 1)
    def _():
        ktail = kbuf[nxt, tl:tl + WINDOW, :]
        vtail = vbuf[nxt, tl:tl + WINDOW, :]
        low_w = lax.broadcasted_iota(jnp.int32, (WINDOW, LANES), 1) < HEAD_DIM_B
        for u in range(KV_W_B // LANES):
            knew_ref[0, :, LANES * u:LANES * (u + 1)] = jnp.where(
                low_w, ktail[:, 2 * LANES * u:2 * LANES * u + LANES],
                ktail[:, 2 * LANES * u + LANES:2 * LANES * (u + 1)])
            vnew_ref[0, :, LANES * u:LANES * (u + 1)] = jnp.where(
                low_w, vtail[:, 2 * LANES * u:2 * LANES * u + LANES],
                vtail[:, 2 * LANES * u + LANES:2 * LANES * (u + 1)])


def _swa_call(h, w_b, cos, sin_a, sin_b, sinks, kc, vc, tl):
    bsz, seq, _ = h.shape
    has_cache = kc is not None
    ni = seq // tl
    n_blocks = bsz * ni
    x_map, y_map, pos_map, _ = _lagged_maps(n_blocks, ni)
    proj_batch_map = lambda n: (jnp.minimum(n, n_blocks - 1) // ni, 0, 0)
    const2 = lambda n: (0, 0)
    wcols = w_b.shape[1]
    in_specs = [
        pl.BlockSpec((1, tl, D_MODEL), x_map),
        pl.BlockSpec((D_MODEL, wcols), const2, pipeline_mode=pl.Buffered(1)),
        pl.BlockSpec((tl, LANES), pos_map),
        pl.BlockSpec((tl, LANES), pos_map),
        pl.BlockSpec((tl, LANES), pos_map),
        pl.BlockSpec((N_Q_B, LANES), const2),
    ]
    args = [h, w_b, cos, sin_a, sin_b, sinks]
    if has_cache:
        in_specs += [pl.BlockSpec((1, WINDOW, 2 * KV_W_B), proj_batch_map)] * 2
        args += [kc, vc]
    return pl.pallas_call(
        functools.partial(_swa_kernel, tl=tl, ni=ni, n_blocks=n_blocks, has_cache=has_cache),
        grid=(n_blocks + 1,),
        in_specs=in_specs,
        out_specs=[
            pl.BlockSpec((1, tl, W_B), y_map),
            pl.BlockSpec((1, WINDOW, KV_W_B), proj_batch_map),
            pl.BlockSpec((1, WINDOW, KV_W_B), proj_batch_map),
        ],
        out_shape=[
            jax.ShapeDtypeStruct((bsz, seq, W_B), BF16),
            jax.ShapeDtypeStruct((bsz, WINDOW, KV_W_B), F32),
            jax.ShapeDtypeStruct((bsz, WINDOW, KV_W_B), F32),
        ],
        scratch_shapes=[
            pltpu.VMEM((2, tl, W_B), F32),
            pltpu.VMEM((2, tl, W_B), F32),
            pltpu.VMEM((2, WINDOW + tl, 2 * KV_W_B), F32),
            pltpu.VMEM((2, WINDOW + tl, 2 * KV_W_B), F32),
        ],
        compiler_params=pltpu.CompilerParams(
            dimension_semantics=("arbitrary",), vmem_limit_bytes=VMEM_LIMIT),
        name="swa_mixer",
    )(*args)


CONV_PAD = 8


def _ssd_kernel(h_ref, w_ref, cw_ref, cb_ref, dtb_ref, alog_ref, dsk_ref, sn_ref, conv0_ref, s0_ref,
                y_ref, snew_ref, convnew_ref,
                ubuf, z_s, xc_s, xdt_s, bc_s, la_s, st_s, *, tl, ni, n_blocks):
    n = pl.program_id(0)
    n_chunks = tl // CHUNK
    gw = N_GROUPS_C * D_STATE
    hp = W_C // N_GROUPS_C
    i_proj = jnp.minimum(n, n_blocks - 1) % ni
    i_lag = jnp.maximum(n - 1, 0) % ni
    nxt = n % 2
    cur = 1 - nxt
    tail_rows = slice(CONV_PAD - (CONV_W - 1), CONV_PAD)

    @pl.when(n == 0)
    def _():
        for buf in (z_s, xc_s, xdt_s, bc_s, la_s):
            buf[...] = jnp.zeros_like(buf)

    @pl.when(i_proj == 0)
    def _():
        ubuf[tail_rows, :] = conv0_ref[0]

    @pl.when(i_lag == 0)
    def _():
        for g in range(N_GROUPS_C):
            st_s[g] = s0_ref[0, hp * g:hp * (g + 1), :].T

    def conv_act(sl):
        acc = cb_ref[:, sl]
        for t in range(CONV_W):
            lo = CONV_PAD - (CONV_W - 1) + t
            acc = acc + ubuf[lo:lo + tl, sl] * cw_ref[t:t + 1, sl]
        ubuf[tail_rows, sl] = ubuf[tl + CONV_PAD - (CONV_W - 1):tl + CONV_PAD, sl]
        return _silu(acc)

    def project(c):
        per = (W_C // MXU_COLS) // n_chunks
        for j in range(c * per, (c + 1) * per):
            sl = slice(MXU_COLS * j, MXU_COLS * (j + 1))
            z_s[nxt, :, sl] = _dot(h_ref[0], w_ref[:, sl])
            yield
            dt_raw = _dot(h_ref[0], w_ref[:, W_C + CONV_DIM + MXU_COLS * j:W_C + CONV_DIM + MXU_COLS * (j + 1)])
            yield
            dt = _softplus0(dt_raw + dtb_ref[:, sl])
            la_s[nxt, :, sl] = -dt * jnp.exp(alog_ref[:, sl])
            ubuf[CONV_PAD:CONV_PAD + tl, sl] = _dot(h_ref[0], w_ref[:, W_C + MXU_COLS * j:W_C + MXU_COLS * (j + 1)])
            yield
            for u in range(MXU_COLS // LANES):
                usl = slice(MXU_COLS * j + LANES * u, MXU_COLS * j + LANES * (u + 1))
                act = conv_act(usl)
                xc_s[nxt, :, usl] = act
                xdt_s[nxt, :, usl] = act * dt[:, LANES * u:LANES * (u + 1)]
            bsl = slice(W_C + MXU_COLS * j, W_C + MXU_COLS * (j + 1))
            ubuf[CONV_PAD:CONV_PAD + tl, bsl] = _dot(h_ref[0], w_ref[:, W_C + W_C + MXU_COLS * j:W_C + W_C + MXU_COLS * (j + 1)])
            yield
            for u in range(MXU_COLS // LANES):
                usl = slice(W_C + MXU_COLS * j + LANES * u, W_C + MXU_COLS * j + LANES * (u + 1))
                bc_s[nxt, :, MXU_COLS * j + LANES * u:MXU_COLS * j + LANES * (u + 1)] = conv_act(usl)

    tri = _tri_bf16(CHUNK)
    trow = lax.broadcasted_iota(jnp.int32, (CHUNK, LANES), 0)
    lane = lax.broadcasted_iota(jnp.int32, (CHUNK, LANES), 1)
    s_of_lane = lane % CHUNK
    causal = trow >= s_of_lane
    row2 = lax.broadcasted_iota(jnp.int32, (2 * CHUNK, LANES), 0)
    lane2 = lax.broadcasted_iota(jnp.int32, (2 * CHUNK, LANES), 1)
    blockdiag = (row2 < CHUNK) == (lane2 < HEAD_DIM_C)

    for c in range(n_chunks):
        rows = slice(CHUNK * c, CHUNK * (c + 1))

        def group(g, rows=rows):
            gsl = slice(hp * g, hp * (g + 1))
            bg = bc_s[cur, rows, D_STATE * g:D_STATE * (g + 1)].astype(BF16)
            cg = bc_s[cur, rows, gw + D_STATE * g:gw + D_STATE * (g + 1)].astype(BF16)
            cb2 = _dot_nt(cg, jnp.concatenate([bg, bg], axis=0))
            yield
            st = st_s[g]
            y_state = _dot(cg, st.astype(BF16))
            yield
            la = la_s[cur, rows, gsl]
            b = _cumsum_rows(tri, la)
            yield
            xdt = xdt_s[cur, rows, gsl]
            ys = []
            for u in range(2):
                usl = slice(LANES * u, LANES * (u + 1))
                b_row = jnp.sum(jnp.where(trow <= s_of_lane, la[:, usl], 0.0), axis=0, keepdims=True)
                decay = jnp.where(causal, jnp.exp(b[:, usl] - b_row), 0.0)
                wmat = (cb2 * decay).astype(BF16)
                xbd = jnp.where(blockdiag, jnp.concatenate([xdt[:, usl], xdt[:, usl]], axis=0), 0.0).astype(BF16)
                ys.append(_dot(wmat, xbd))
                yield
            y = jnp.concatenate(ys, axis=1) + y_state * jnp.exp(b)
            o = (y + dsk_ref[:, gsl] * xc_s[cur, rows, gsl]) * _silu(z_s[cur, rows, gsl])
            r = lax.rsqrt(jnp.mean(o * o, axis=-1, keepdims=True) + EPS)
            out = ((o * r) * sn_ref[:, gsl]).astype(BF16)
            b_last = b[CHUNK - 1:CHUNK, :]
            st_new = st * jnp.exp(b_last) + _dot_tn(bg, (xdt * jnp.exp(b_last - b)).astype(BF16))
            return out, st_new

        results = _round_robin([group(g) for g in range(N_GROUPS_C)] + [project(c)])[:N_GROUPS_C]
        for g, (out, st_new) in enumerate(results):
            y_ref[0, rows, hp * g:hp * (g + 1)] = out
            st_s[g] = st_new

    @pl.when(i_proj == ni - 1)
    def _():
        convnew_ref[0] = ubuf[tail_rows, :]

    @pl.when(i_lag == ni - 1)
    def _():
        for g in range(N_GROUPS_C):
            snew_ref[0, hp * g:hp * (g + 1), :] = st_s[g].T


def _ssd_call(h, w_c, cw, cb, dtb, alog, dsk, sn, conv0, s0, tl):
    bsz, seq, _ = h.shape
    ni = seq // tl
    n_blocks = bsz * ni
    x_map, y_map, _, batch_map = _lagged_maps(n_blocks, ni)
    proj_batch_map = lambda n: (jnp.minimum(n, n_blocks - 1) // ni, 0, 0)
    const2 = lambda n: (0, 0)
    wcols = w_c.shape[1]
    return pl.pallas_call(
        functools.partial(_ssd_kernel, tl=tl, ni=ni, n_blocks=n_blocks),
        grid=(n_blocks + 1,),
        in_specs=[
            pl.BlockSpec((1, tl, D_MODEL), x_map),
            pl.BlockSpec((D_MODEL, wcols), const2, pipeline_mode=pl.Buffered(1)),
            pl.BlockSpec((CONV_W, CONV_DIM), const2),
            pl.BlockSpec((1, CONV_DIM), const2),
            pl.BlockSpec((1, W_C), const2),
            pl.BlockSpec((1, W_C), const2),
            pl.BlockSpec((1, W_C), const2),
            pl.BlockSpec((1, W_C), const2),
            pl.BlockSpec((1, CONV_W - 1, CONV_DIM), proj_batch_map),
            pl.BlockSpec((1, W_C, D_STATE), batch_map(3)),
        ],
        out_specs=[
            pl.BlockSpec((1, tl, W_C), y_map),
            pl.BlockSpec((1, W_C, D_STATE), batch_map(3)),
            pl.BlockSpec((1, CONV_W - 1, CONV_DIM), proj_batch_map),
        ],
        out_shape=[
            jax.ShapeDtypeStruct((bsz, seq, W_C), BF16),
            jax.ShapeDtypeStruct((bsz, W_C, D_STATE), F32),
            jax.ShapeDtypeStruct((bsz, CONV_W - 1, CONV_DIM), F32),
        ],
        scratch_shapes=[
            pltpu.VMEM((CONV_PAD + tl, CONV_DIM), F32),
            pltpu.VMEM((2, tl, W_C), F32),
            pltpu.VMEM((2, tl, W_C), F32),
            pltpu.VMEM((2, tl, W_C), F32),
            pltpu.VMEM((2, tl, 2 * N_GROUPS_C * D_STATE), F32),
            pltpu.VMEM((2, tl, W_C), F32),
            pltpu.VMEM((N_GROUPS_C, D_STATE, W_C // N_GROUPS_C), F32),
        ],
        compiler_params=pltpu.CompilerParams(
            dimension_semantics=("arbitrary",), vmem_limit_bytes=VMEM_LIMIT),
        name="ssd_mixer",
    )(h, w_c, cw, cb, dtb, alog, dsk, sn, conv0, s0)


def _merge_kernel(*refs, emit_next):
    if emit_next:
        (x_ref, h_ref, ya_ref, yb_ref, yc_ref, npost_ref, nnext_ref, wg_ref, wbr_ref, wout_ref,
         o_ref, hnext_ref, m_s) = refs
    else:
        (x_ref, h_ref, ya_ref, yb_ref, yc_ref, npost_ref, wg_ref, wbr_ref, wout_ref, o_ref, m_s) = refs
    j = pl.program_id(1)
    n_j = m_s.shape[0]

    h = h_ref[...]
    merged = None
    for bi, y_ref in enumerate((ya_ref, yb_ref, yc_ref)):
        term = _sigmoid(_dot(h, wg_ref[bi])) * _dot(y_ref[...], wbr_ref[bi])
        merged = term if merged is None else merged + term
    m_s[j] = merged.astype(BF16)

    @pl.when(j == n_j - 1)
    def _():
        out = _dot(jnp.concatenate([m_s[n] for n in range(n_j)], axis=1), wout_ref[...])
        r = lax.rsqrt(jnp.mean(out * out, axis=-1, keepdims=True) + EPS)
        x_new = x_ref[...] + (out * r) * npost_ref[...]
        o_ref[...] = x_new
        if emit_next:
            hnext_ref[...] = _prenorm_bf16(x_new, nnext_ref[...])


def _merge_call(x2, h2, ya, yb, yc, npost, nnext, wg, wbr, wout, tm, tn):
    rows = x2.shape[0]
    emit_next = nnext is not None
    grid = (rows // tm, D_MODEL // tn)
    const2 = lambda r, j: (0, 0)
    row_block = lambda width: pl.BlockSpec((tm, width), lambda r, j: (r, 0))
    vec = pl.BlockSpec((1, D_MODEL), const2)
    in_specs = [row_block(D_MODEL), row_block(D_MODEL), row_block(W_A), row_block(W_B), row_block(W_C), vec]
    args = [x2, h2, ya, yb, yc, npost]
    if emit_next:
        in_specs.append(vec)
        args.append(nnext)
    in_specs += [
        pl.BlockSpec((3, D_MODEL, tn), lambda r, j: (0, 0, j)),
        pl.BlockSpec((3, W_A, tn), lambda r, j: (0, 0, j)),
        pl.BlockSpec((D_MODEL, D_MODEL), const2, pipeline_mode=pl.Buffered(1)),
    ]
    args += [wg, wbr, wout]
    out_specs = [row_block(D_MODEL)]
    out_shape = [jax.ShapeDtypeStruct((rows, D_MODEL), F32)]
    if emit_next:
        out_specs.append(row_block(D_MODEL))
        out_shape.append(jax.ShapeDtypeStruct((rows, D_MODEL), BF16))
    outs = pl.pallas_call(
        functools.partial(_merge_kernel, emit_next=emit_next),
        grid=grid,
        in_specs=in_specs,
        out_specs=out_specs,
        out_shape=out_shape,
        scratch_shapes=[pltpu.VMEM((D_MODEL // tn, tm, tn), BF16)],
        compiler_params=pltpu.CompilerParams(
            dimension_semantics=("arbitrary", "arbitrary"), vmem_limit_bytes=VMEM_LIMIT),
        name="merge_out",
    )(*args)
    return (outs[0], outs[1]) if emit_next else (outs[0], None)


def _prenorm_kernel(x_ref, w_ref, h_ref):
    h_ref[...] = _prenorm_bf16(x_ref[...], w_ref[...])


def _prenorm_call(x2, nw, tm):
    rows = x2.shape[0]
    return pl.pallas_call(
        _prenorm_kernel,
        grid=(rows // tm,),
        in_specs=[pl.BlockSpec((tm, D_MODEL), lambda r: (r, 0)), pl.BlockSpec((1, D_MODEL), lambda r: (0, 0))],
        out_specs=pl.BlockSpec((tm, D_MODEL), lambda r: (r, 0)),
        out_shape=jax.ShapeDtypeStruct((rows, D_MODEL), BF16),
        compiler_params=pltpu.CompilerParams(dimension_semantics=("arbitrary",), vmem_limit_bytes=VMEM_LIMIT),
        name="prenorm",
    )(x2, nw)


def _rope_tables(pos):
    half = ROT_DIM // 2
    inv = jnp.power(ROPE_THETA, -jnp.arange(half, dtype=F32) / half)
    ang = pos.astype(F32)[:, None] * inv[None, :]
    cos, sin = jnp.cos(ang), jnp.sin(ang)
    n = pos.shape[0]
    ones = jnp.ones((n, HEAD_DIM_B - ROT_DIM), F32)
    zeros = jnp.zeros((n, HEAD_DIM_B - ROT_DIM), F32)
    zh = jnp.zeros((n, half), F32)
    cos_t = jnp.concatenate([cos, cos, ones], axis=1)
    sin_a = jnp.concatenate([-sin, zh, zeros], axis=1)
    sin_b = jnp.concatenate([zh, sin, zeros], axis=1)
    tile = lambda t: jnp.concatenate([t, t], axis=1)
    return tile(cos_t), tile(sin_a), tile(sin_b)


def _dup_heads(t):
    lead = t.shape[:-1]
    t4 = t.reshape(lead + (N_KV_B, 1, HEAD_DIM_B))
    return jnp.broadcast_to(t4, lead + (N_KV_B, 2, HEAD_DIM_B)).reshape(lead + (2 * KV_W_B,))


def _col_blocks(w):
    return w.reshape(w.shape[0], -1, MXU_COLS).transpose(1, 0, 2)


def _expand_heads(t):
    return jnp.repeat(t, HEAD_DIM_C, axis=-1)


def _layer(x, h, tl, tables, caches, lw, norm_next):
    bsz, seq, _ = x.shape
    cos, sin_a, sin_b = tables
    kc, vc, s_hgrn, s_ssm, s_conv = caches
    if s_hgrn is None:
        s_hgrn = jnp.zeros((bsz, N_HEADS_A, HEAD_K_A, HEAD_K_A), F32)
        s_ssm = jnp.zeros((bsz, W_C, D_STATE), F32)
        s_conv = jnp.zeros((bsz, CONV_W - 1, CONV_DIM), F32)
    else:
        s_ssm = s_ssm.reshape(bsz, W_C, D_STATE)
        kc = _dup_heads(kc.reshape(bsz, WINDOW, KV_W_B))
        vc = _dup_heads(vc.reshape(bsz, WINDOW, KV_W_B))

    ya, hgrn_new = _hgrn_call(h, lw["w_a"], lw["lb_pack"], lw["hgrn_norm"], s_hgrn, tl)
    yb, k_new, v_new = _swa_call(h, lw["w_b"], cos, sin_a, sin_b, lw["sinks"], kc, vc, tl)
    yc, ssm_new, conv_new = _ssd_call(h, lw["w_c"], lw["conv_w"], lw["conv_b"], lw["dt_bias"],
                                      lw["a_log"], lw["d_skip"], lw["ssm_norm"], s_conv, s_ssm, tl)
    rows = bsz * seq
    x_new, h_next = _merge_call(x.reshape(rows, D_MODEL), h.reshape(rows, D_MODEL), ya.reshape(rows, W_A),
                                yb.reshape(rows, W_B), yc.reshape(rows, W_C), lw["norm_post"], norm_next,
                                lw["w_g"], lw["w_br"], lw["w_out"], min(MERGE_ROWS, rows), MXU_COLS)
    states = (k_new.reshape(bsz, WINDOW, N_KV_B, HEAD_DIM_B), v_new.reshape(bsz, WINDOW, N_KV_B, HEAD_DIM_B),
              hgrn_new, ssm_new.reshape(bsz, N_HEADS_C, HEAD_DIM_C, D_STATE), conv_new)
    if h_next is not None:
        h_next = h_next.reshape(bsz, seq, D_MODEL)
    return x_new.reshape(bsz, seq, D_MODEL), h_next, states


def kernel(x_prompt, x_sample, cache_swa_k, cache_swa_v, state_hgrn, state_ssm, state_conv, norm_pre, norm_post, w_in, hgrn_lb_logits, hgrn_norm, swa_sinks, conv_w, conv_b, dt_bias, a_log, d_skip, ssm_norm, w_branch_a, w_branch_b, w_branch_c, w_out):
    depth = w_in.shape[0]
    lbp = jax.nn.softmax(hgrn_lb_logits.astype(F32), axis=0)
    lbc = jnp.cumsum(lbp, axis=0)
    lb_all = lbc - lbc[0:1]

    offs = np.cumsum([0, W_A, W_A, W_A, W_A, W_B, KV_W_B, KV_W_B, W_B, W_C, CONV_DIM, N_HEADS_C,
                      D_MODEL, D_MODEL, D_MODEL])
    col = lambda l, a, b: w_in[l, :, int(offs[a]):int(offs[b])]

    tables_p = _rope_tables(jnp.arange(x_prompt.shape[1], dtype=jnp.int32))
    tables_s = _rope_tables(PAST_LEN + jnp.arange(x_sample.shape[1], dtype=jnp.int32))

    xp, xs = x_prompt, x_sample
    pre = lambda l: norm_pre[l].reshape(1, D_MODEL)
    hp = _prenorm_call(xp.reshape(-1, D_MODEL), pre(0), MERGE_ROWS).reshape(xp.shape)
    hs = _prenorm_call(xs.reshape(-1, D_MODEL), pre(0), MERGE_ROWS).reshape(xs.shape)
    pst, sst = [], []
    for l in range(depth):
        lb = lb_all[l].reshape(N_HEADS_A, 1, HEAD_K_A)
        norm_next = pre(l + 1) if l + 1 < depth else None
        lw = {
            "norm_post": norm_post[l].reshape(1, D_MODEL),
            "w_a": _col_blocks(col(l, 0, 4).astype(BF16)),
            "w_b": jnp.concatenate([col(l, 4, 5), _dup_heads(col(l, 5, 6)), _dup_heads(col(l, 6, 7)),
                                    col(l, 7, 8)], axis=1).astype(BF16),
            "w_c": jnp.concatenate([col(l, 8, 9), col(l, 9, 10), _expand_heads(col(l, 10, 11))],
                                   axis=1).astype(BF16),
            "w_g": jnp.stack([col(l, 11, 12), col(l, 12, 13), col(l, 13, 14)]).astype(BF16),
            "w_br": jnp.stack([w_branch_a[l], w_branch_b[l], w_branch_c[l]]).astype(BF16),
            "w_out": w_out[l].astype(BF16),
            "lb_pack": jnp.stack([jnp.log(lb), jnp.log1p(-lb)]) * LOG2E,
            "hgrn_norm": hgrn_norm[l].reshape(1, HEAD_K_A),
            "sinks": jnp.broadcast_to(swa_sinks[l].astype(F32)[:, None], (N_Q_B, LANES)),
            "conv_w": conv_w[l],
            "conv_b": conv_b[l].reshape(1, CONV_DIM),
            "dt_bias": _expand_heads(dt_bias[l].astype(F32)).reshape(1, W_C),
            "a_log": _expand_heads(a_log[l].astype(F32)).reshape(1, W_C),
            "d_skip": _expand_heads(d_skip[l].astype(F32)).reshape(1, W_C),
            "ssm_norm": ssm_norm[l].reshape(1, W_C),
        }
        xp, hp, sp = _layer(xp, hp, PROMPT_ROWS, tables_p, (None, None, None, None, None), lw, norm_next)
        xs, hs, ss = _layer(xs, hs, CHUNK, tables_s,
                            (cache_swa_k[l], cache_swa_v[l], state_hgrn[l], state_ssm[l], state_conv[l]),
                            lw, norm_next)
        pst.append(sp)
        sst.append(ss)

    stack = lambda sts, k: jnp.stack([s[k] for s in sts])
    return (xp, xs,
            stack(pst, 0), stack(pst, 1), stack(pst, 2), stack(pst, 3), stack(pst, 4),
            stack(sst, 0), stack(sst, 1), stack(sst, 2), stack(sst, 3), stack(sst, 4))
```

```python
import functools
import math

import jax
import jax.numpy as jnp
import numpy as np
from jax import lax
from jax.experimental import pallas as pl
from jax.experimental.pallas import tpu as pltpu

F32 = jnp.float32
BF16 = jnp.bfloat16

D_MODEL = 2048
CHUNK = 64
EPS = 1e-6
PAST_LEN = 4096

W_A = 1024
HEAD_K_A = 128
N_HEADS_A = 8

N_Q_B = 16
N_KV_B = 4
HEAD_DIM_B = 64
W_B = 1024
KV_W_B = 256
WINDOW = 128
ROT_DIM = 16
ROPE_THETA = 500000.0
ATTN_SCALE = HEAD_DIM_B ** -0.5

W_C = 1024
HEAD_DIM_C = 64
N_HEADS_C = 16
N_GROUPS_C = 4
D_STATE = 128
CONV_W = 4
CONV_DIM = 2048

LANES = 128
SUBLANES = 8
SUB_BLOCK = 16
MXU_COLS = 256
PROMPT_ROWS = 256
MERGE_ROWS = 512
LOG2E = math.log2(math.e)
VMEM_LIMIT = 56 * 1024 * 1024

NT_DIMS = (((1,), (1,)), ((), ()))
TN_DIMS = (((0,), (0,)), ((), ()))


def _dot(a, b):
    return jnp.dot(a, b, preferred_element_type=F32)


def _dot_nt(a, b):
    return lax.dot_general(a, b, NT_DIMS, preferred_element_type=F32)


def _dot_tn(a, b):
    return lax.dot_general(a, b, TN_DIMS, preferred_element_type=F32)


def _prenorm_bf16(x, w):
    r = lax.rsqrt(jnp.mean(x * x, axis=-1, keepdims=True) + EPS)
    return ((x * r) * w).astype(BF16)


def _sigmoid(x):
    return 0.5 * jnp.tanh(0.5 * x) + 0.5


def _silu(x):
    return x * _sigmoid(x)


def _softplus0(x):
    return jnp.maximum(x, 0.0) + jnp.log(1.0 + jnp.exp(-jnp.abs(x)))


def _tri_bf16(n):
    r = lax.broadcasted_iota(jnp.int32, (n, n), 0)
    c = lax.broadcasted_iota(jnp.int32, (n, n), 1)
    return (r >= c).astype(BF16)


def _cumsum_rows(tri, x):
    hi = x.astype(BF16)
    r1 = x - hi.astype(F32)
    mid = r1.astype(BF16)
    lo = (r1 - mid.astype(F32)).astype(BF16)
    return _dot(tri, hi) + _dot(tri, mid) + _dot(tri, lo)


def _round_robin(gens, background=None, heavy_round=0):
    results = [None] * len(gens)
    live = list(range(len(gens)))
    rnd = 0
    while live:
        if background is not None and rnd == heavy_round:
            for _ in background:
                pass
        for n in list(live):
            try:
                next(gens[n])
            except StopIteration as stop:
                results[n] = stop.value
                live.remove(n)
        rnd += 1
    return results


def _hgrn_kernel(h_ref, w_ref, lb_ref, hn_ref, s0_ref, y_ref, snew_ref,
                 proj_s, st_s, *, tl, ni):
    n = pl.program_id(0)
    n_chunks = tl // CHUNK
    i_lag = jnp.maximum(n - 1, 0) % ni
    nxt = n % 2
    cur = 1 - nxt

    @pl.when(n == 0)
    def _():
        proj_s[...] = jnp.zeros_like(proj_s)

    @pl.when(i_lag == 0)
    def _():
        for hd in range(N_HEADS_A):
            st_s[hd] = s0_ref[0, hd].T

    n_col_blocks = w_ref.shape[0]
    cols_per_chunk = n_col_blocks // n_chunks

    def project(c):
        for j in range(cols_per_chunk):
            jb = c * cols_per_chunk + j
            res = _dot(h_ref[0], w_ref[jb])
            proj_s[nxt, 2 * jb] = res[:, :LANES]
            proj_s[nxt, 2 * jb + 1] = res[:, LANES:]
            yield

    tri = _tri_bf16(CHUNK)
    n_sub = CHUNK // SUB_BLOCK
    lane8 = lax.broadcasted_iota(jnp.int32, (SUBLANES, LANES), 1)
    sub8 = lax.broadcasted_iota(jnp.int32, (SUBLANES, LANES), 0)
    row64 = lax.broadcasted_iota(jnp.int32, (CHUNK, CHUNK), 0)
    col64 = lax.broadcasted_iota(jnp.int32, (CHUNK, CHUNK), 1)
    below_diag_block = row64 // SUB_BLOCK > col64 // SUB_BLOCK
    zero8 = jnp.zeros((SUBLANES, LANES), F32)

    def zeros(n):
        return jnp.zeros((n, LANES), F32)

    def piece(a, m, u):
        lo = SUB_BLOCK * m + SUBLANES * u
        return a[lo:lo + SUBLANES]

    def unit(hd, r0):
        rows = pl.ds(r0, CHUNK)
        aq = proj_s[cur, hd, rows, :]
        z = proj_s[cur, N_HEADS_A + hd, rows, :]
        v = proj_s[cur, 2 * N_HEADS_A + hd, rows, :].astype(BF16)
        ag = proj_s[cur, 3 * N_HEADS_A + hd, rows, :]
        log_lb = lb_ref[0, hd]
        log1m_lb = lb_ref[1, hd]

        z2 = z * LOG2E
        log_sig = jnp.minimum(z2, 0.0) - jnp.log2(1.0 + jnp.exp2(-jnp.abs(z2)))
        cterm = log1m_lb + log_sig
        log_f = jnp.maximum(log_lb, cterm) + jnp.log2(1.0 + jnp.exp2(-jnp.abs(log_lb - cterm)))
        log_k = cterm - z2
        q = _silu(aq)

        b2 = _cumsum_rows(tri, log_f)
        yield
        c2 = b2 - log_k
        b2_last = b2[CHUNK - 1:CHUNK, :]
        st = st_s[hd]

        o = _dot_nt((q * jnp.exp2(b2)).astype(BF16), st.astype(BF16))
        yield

        q_slabs, k_slabs = [], []
        for m in range(1, n_sub):
            lo = SUB_BLOCK * m
            ref = b2[lo - 1:lo, :]
            qm = q[lo:lo + SUB_BLOCK] * jnp.exp2(b2[lo:lo + SUB_BLOCK] - ref)
            q_parts = [zeros(lo), qm] + ([zeros(CHUNK - lo - SUB_BLOCK)] if lo + SUB_BLOCK < CHUNK else [])
            q_slabs.append(jnp.concatenate(q_parts, axis=0))
            k_slabs.append(jnp.concatenate([jnp.exp2(ref - c2[:lo]), zeros(CHUNK - lo)], axis=0))
        a_off = _dot_nt(jnp.concatenate(q_slabs, axis=1).astype(BF16),
                        jnp.concatenate(k_slabs, axis=1).astype(BF16))
        yield

        order = [(s, m, u) for s in range(SUB_BLOCK) for m in range(n_sub) for u in range(2)
                 if not (u == 0 and s >= SUBLANES)]
        pieces = []
        for s, m, u in order:
            cs = c2[SUB_BLOCK * m + s:SUB_BLOCK * m + s + 1]
            pieces.append(piece(q, m, u) * jnp.exp2(piece(b2, m, u) - cs))
        sums = [jnp.sum(p, axis=-1, keepdims=True) for p in pieces]
        yield
        d = {(m, u): zero8 for m in range(n_sub) for u in range(2)}
        for n, (s, m, u) in enumerate(order):
            d[(m, u)] = jnp.where(lane8 == s, sums[n], d[(m, u)])
        diag_rows = []
        for m in range(n_sub):
            for u in range(2):
                dm = jnp.where(sub8 + SUBLANES * u >= lane8, d[(m, u)], 0.0)
                diag_rows.append(pltpu.roll(dm, SUB_BLOCK * m, 1) if m else dm)
        a_diag = jnp.concatenate(diag_rows, axis=0)[:, :CHUNK]
        a_full = jnp.where(below_diag_block, a_off, a_diag)
        o = o + _dot(a_full.astype(BF16), v)
        yield

        r = lax.rsqrt(jnp.mean(o * o, axis=-1, keepdims=True) + EPS)
        y = ((o * r) * hn_ref[...]) * _silu(ag)

        st_new = st * jnp.exp2(b2_last) + _dot_tn(v, jnp.exp2(b2_last - c2).astype(BF16))
        return y.astype(BF16), st_new

    def body(c, carry):
        r0 = pl.multiple_of(c * CHUNK, CHUNK)
        gens = [unit(hd, r0) for hd in range(N_HEADS_A)] + [project(c)]
        results = _round_robin(gens)[:N_HEADS_A]
        for hd, (y, st_new) in enumerate(results):
            y_ref[0, pl.ds(r0, CHUNK), LANES * hd:LANES * (hd + 1)] = y
            st_s[hd] = st_new
        return carry

    lax.fori_loop(0, n_chunks, body, 0)

    @pl.when(i_lag == ni - 1)
    def _():
        for hd in range(N_HEADS_A):
            snew_ref[0, hd] = st_s[hd].T


def _lagged_maps(n_blocks, ni):
    def split(m):
        return m // ni, m % ni
    x_map = lambda n: split(jnp.minimum(n, n_blocks - 1)) + (0,)
    y_map = lambda n: split(jnp.maximum(n - 1, 0)) + (0,)
    pos_map = lambda n: (jnp.minimum(n, n_blocks - 1) % ni, 0)
    batch_map = lambda nd: (lambda n: (jnp.maximum(n - 1, 0) // ni,) + (0,) * (nd - 1))
    return x_map, y_map, pos_map, batch_map


def _hgrn_call(h, w_a, lb_pack, hn, s0, tl):
    bsz, seq, _ = h.shape
    ni = seq // tl
    n_blocks = bsz * ni
    x_map, y_map, _, batch_map = _lagged_maps(n_blocks, ni)
    const2 = lambda n: (0, 0)
    return pl.pallas_call(
        functools.partial(_hgrn_kernel, tl=tl, ni=ni),
        grid=(n_blocks + 1,),
        in_specs=[
            pl.BlockSpec((1, tl, D_MODEL), x_map),
            pl.BlockSpec(w_a.shape, lambda n: (0, 0, 0), pipeline_mode=pl.Buffered(1)),
            pl.BlockSpec((2, N_HEADS_A, 1, HEAD_K_A), lambda n: (0, 0, 0, 0)),
            pl.BlockSpec((1, HEAD_K_A), const2),
            pl.BlockSpec((1, N_HEADS_A, HEAD_K_A, HEAD_K_A), batch_map(4)),
        ],
        out_specs=[
            pl.BlockSpec((1, tl, W_A), y_map),
            pl.BlockSpec((1, N_HEADS_A, HEAD_K_A, HEAD_K_A), batch_map(4)),
        ],
        out_shape=[
            jax.ShapeDtypeStruct((bsz, seq, W_A), BF16),
            jax.ShapeDtypeStruct((bsz, N_HEADS_A, HEAD_K_A, HEAD_K_A), F32),
        ],
        scratch_shapes=[
            pltpu.VMEM((2, 4 * N_HEADS_A, tl, LANES), F32),
            pltpu.VMEM((N_HEADS_A, HEAD_K_A, HEAD_K_A), F32),
        ],
        compiler_params=pltpu.CompilerParams(
            dimension_semantics=("arbitrary",), vmem_limit_bytes=VMEM_LIMIT),
        name="hgrn_mixer",
    )(h, w_a, lb_pack, hn, s0)


def _rotate(xs, cos, sin_a, sin_b):
    return xs * cos + pltpu.roll(xs, LANES - ROT_DIM // 2, 1) * sin_a + pltpu.roll(xs, ROT_DIM // 2, 1) * sin_b


def _swa_kernel(*refs, tl, ni, n_blocks, has_cache):
    if has_cache:
        (h_ref, w_ref, cos_ref, sa_ref, sb_ref, sink_ref, kc_ref, vc_ref,
         y_ref, knew_ref, vnew_ref, q_s, g_s, kbuf, vbuf) = refs
    else:
        (h_ref, w_ref, cos_ref, sa_ref, sb_ref, sink_ref,
         y_ref, knew_ref, vnew_ref, q_s, g_s, kbuf, vbuf) = refs
    n = pl.program_id(0)
    n_chunks = tl // CHUNK
    band = WINDOW + CHUNK
    kvw = 2 * KV_W_B
    i_proj = jnp.minimum(n, n_blocks - 1) % ni
    i_lag = jnp.maximum(n - 1, 0) % ni
    nxt = n % 2
    cur = 1 - nxt

    @pl.when(n == 0)
    def _():
        for buf in (q_s, g_s, kbuf, vbuf):
            buf[...] = jnp.zeros_like(buf)

    @pl.when(i_proj == 0)
    def _():
        if has_cache:
            kbuf[nxt, 0:WINDOW, :] = kc_ref[0]
            vbuf[nxt, 0:WINDOW, :] = vc_ref[0]
        else:
            kbuf[nxt, 0:WINDOW, :] = jnp.zeros((WINDOW, kvw), F32)
            vbuf[nxt, 0:WINDOW, :] = jnp.zeros((WINDOW, kvw), F32)

    @pl.when(i_proj != 0)
    def _():
        kbuf[nxt, 0:WINDOW, :] = kbuf[cur, tl:tl + WINDOW, :]
        vbuf[nxt, 0:WINDOW, :] = vbuf[cur, tl:tl + WINDOW, :]

    def project(c):
        per = (W_B // MXU_COLS) // n_chunks
        new_rows = slice(WINDOW, WINDOW + tl)
        for j in range(c * per, (c + 1) * per):
            sl = slice(MXU_COLS * j, MXU_COLS * (j + 1))
            res = _dot(h_ref[0], w_ref[:, sl])
            yield
            for u in range(MXU_COLS // LANES):
                q_s[nxt, :, MXU_COLS * j + LANES * u:MXU_COLS * j + LANES * (u + 1)] = _rotate(
                    res[:, LANES * u:LANES * (u + 1)], cos_ref[...], sa_ref[...], sb_ref[...])
            g_s[nxt, :, sl] = _dot(h_ref[0], w_ref[:, W_B + 2 * kvw + MXU_COLS * j:W_B + 2 * kvw + MXU_COLS * (j + 1)])
            yield
            jk = j % (kvw // MXU_COLS)
            ksl = slice(MXU_COLS * jk, MXU_COLS * (jk + 1))
            if j < kvw // MXU_COLS:
                res = _dot(h_ref[0], w_ref[:, W_B + MXU_COLS * jk:W_B + MXU_COLS * (jk + 1)])
                yield
                for u in range(MXU_COLS // LANES):
                    kbuf[nxt, new_rows, MXU_COLS * jk + LANES * u:MXU_COLS * jk + LANES * (u + 1)] = _rotate(
                        res[:, LANES * u:LANES * (u + 1)], cos_ref[...], sa_ref[...], sb_ref[...])
            else:
                vbuf[nxt, new_rows, ksl] = _dot(
                    h_ref[0], w_ref[:, W_B + kvw + MXU_COLS * jk:W_B + kvw + MXU_COLS * (jk + 1)])
                yield

    lane = lax.broadcasted_iota(jnp.int32, (CHUNK, LANES), 1)
    low = lane < HEAD_DIM_B
    low1 = lax.broadcasted_iota(jnp.int32, (1, LANES), 1) < HEAD_DIM_B
    key_row = lax.broadcasted_iota(jnp.int32, (band, LANES), 0)

    def pair(c, j):
            r0 = CHUNK * c
            g = j // 2
            qp = q_s[cur, r0:r0 + CHUNK, LANES * j:LANES * (j + 1)]
            q2 = jnp.concatenate([jnp.where(low, qp, 0.0), jnp.where(low, 0.0, qp)], axis=0).astype(BF16)
            kb = kbuf[cur, r0:r0 + band, LANES * g:LANES * (g + 1)].astype(BF16)
            vb = vbuf[cur, r0:r0 + band, LANES * g:LANES * (g + 1)].astype(BF16)
            s = _dot_nt(kb, q2) * ATTN_SCALE
            yield
            if not has_cache:
                s = jnp.where(i_lag * tl + r0 - WINDOW + key_row >= 0, s, -jnp.inf)
            sk = jnp.where(low1, sink_ref[2 * j:2 * j + 1, :], sink_ref[2 * j + 1:2 * j + 2, :])
            m = jnp.maximum(jnp.max(s, axis=0, keepdims=True), sk)
            p = jnp.exp(s - m)
            den = jnp.sum(p, axis=0, keepdims=True) + jnp.exp(sk - m)
            o2 = _dot_tn((p * (1.0 / den)).astype(BF16), vb)
            yield
            o = jnp.where(low, o2[:CHUNK], o2[CHUNK:])
            y = o * _silu(g_s[cur, r0:r0 + CHUNK, LANES * j:LANES * (j + 1)])
            y_ref[0, r0:r0 + CHUNK, LANES * j:LANES * (j + 1)] = y.astype(BF16)

    for c in range(n_chunks):
        _round_robin([pair(c, j) for j in range(N_Q_B // 2)], project(c), heavy_round=1)

    @pl.when(i_proj == ni - 1)
    def _():
        ktail = kbuf[nxt, tl:tl + WINDOW, :]
        vtail = vbuf[nxt, tl:tl + WINDOW, :]
        low_w = lax.broadcasted_iota(jnp.int32, (WINDOW, LANES), 1) < HEAD_DIM_B
        for u in range(KV_W_B // LANES):
            knew_ref[0, :, LANES * u:LANES * (u + 1)] = jnp.where(
                low_w, ktail[:, 2 * LANES * u:2 * LANES * u + LANES],
                ktail[:, 2 * LANES * u + LANES:2 * LANES * (u + 1)])
            vnew_ref[0, :, LANES * u:LANES * (u + 1)] = jnp.where(
                low_w, vtail[:, 2 * LANES * u:2 * LANES * u + LANES],
                vtail[:, 2 * LANES * u + LANES:2 * LANES * (u + 1)])


def _swa_call(h, w_b, cos, sin_a, sin_b, sinks, kc, vc, tl):
    bsz, seq, _ = h.shape
    has_cache = kc is not None
    ni = seq // tl
    n_blocks = bsz * ni
    x_map, y_map, pos_map, _ = _lagged_maps(n_blocks, ni)
    proj_batch_map = lambda n: (jnp.minimum(n, n_blocks - 1) // ni, 0, 0)
    const2 = lambda n: (0, 0)
    wcols = w_b.shape[1]
    in_specs = [
        pl.BlockSpec((1, tl, D_MODEL), x_map),
        pl.BlockSpec((D_MODEL, wcols), const2, pipeline_mode=pl.Buffered(1)),
        pl.BlockSpec((tl, LANES), pos_map),
        pl.BlockSpec((tl, LANES), pos_map),
        pl.BlockSpec((tl, LANES), pos_map),
        pl.BlockSpec((N_Q_B, LANES), const2),
    ]
    args = [h, w_b, cos, sin_a, sin_b, sinks]
    if has_cache:
        in_specs += [pl.BlockSpec((1, WINDOW, 2 * KV_W_B), proj_batch_map)] * 2
        args += [kc, vc]
    return pl.pallas_call(
        functools.partial(_swa_kernel, tl=tl, ni=ni, n_blocks=n_blocks, has_cache=has_cache),
        grid=(n_blocks + 1,),
        in_specs=in_specs,
        out_specs=[
            pl.BlockSpec((1, tl, W_B), y_map),
            pl.BlockSpec((1, WINDOW, KV_W_B), proj_batch_map),
            pl.BlockSpec((1, WINDOW, KV_W_B), proj_batch_map),
        ],
        out_shape=[
            jax.ShapeDtypeStruct((bsz, seq, W_B), BF16),
            jax.ShapeDtypeStruct((bsz, WINDOW, KV_W_B), F32),
            jax.ShapeDtypeStruct((bsz, WINDOW, KV_W_B), F32),
        ],
        scratch_shapes=[
            pltpu.VMEM((2, tl, W_B), F32),
            pltpu.VMEM((2, tl, W_B), F32),
            pltpu.VMEM((2, WINDOW + tl, 2 * KV_W_B), F32),
            pltpu.VMEM((2, WINDOW + tl, 2 * KV_W_B), F32),
        ],
        compiler_params=pltpu.CompilerParams(
            dimension_semantics=("arbitrary",), vmem_limit_bytes=VMEM_LIMIT),
        name="swa_mixer",
    )(*args)


CONV_PAD = 8


def _ssd_kernel(h_ref, w_ref, cw_ref, cb_ref, dtb_ref, alog_ref, dsk_ref, sn_ref, conv0_ref, s0_ref,
                y_ref, snew_ref, convnew_ref,
                ubuf, z_s, xc_s, xdt_s, bc_s, la_s, st_s, *, tl, ni, n_blocks):
    n = pl.program_id(0)
    n_chunks = tl // CHUNK
    gw = N_GROUPS_C * D_STATE
    hp = W_C // N_GROUPS_C
    i_proj = jnp.minimum(n, n_blocks - 1) % ni
    i_lag = jnp.maximum(n - 1, 0) % ni
    nxt = n % 2
    cur = 1 - nxt
    tail_rows = slice(CONV_PAD - (CONV_W - 1), CONV_PAD)

    @pl.when(n == 0)
    def _():
        for buf in (z_s, xc_s, xdt_s, bc_s, la_s):
            buf[...] = jnp.zeros_like(buf)

    @pl.when(i_proj == 0)
    def _():
        ubuf[tail_rows, :] = conv0_ref[0]

    @pl.when(i_lag == 0)
    def _():
        for g in range(N_GROUPS_C):
            st_s[g] = s0_ref[0, hp * g:hp * (g + 1), :].T

    def conv_act(sl):
        acc = cb_ref[:, sl]
        for t in range(CONV_W):
            lo = CONV_PAD - (CONV_W - 1) + t
            acc = acc + ubuf[lo:lo + tl, sl] * cw_ref[t:t + 1, sl]
        ubuf[tail_rows, sl] = ubuf[tl + CONV_PAD - (CONV_W - 1):tl + CONV_PAD, sl]
        return _silu(acc)

    def project(c):
        per = (W_C // MXU_COLS) // n_chunks
        for j in range(c * per, (c + 1) * per):
            sl = slice(MXU_COLS * j, MXU_COLS * (j + 1))
            z_s[nxt, :, sl] = _dot(h_ref[0], w_ref[:, sl])
            yield
            dt_raw = _dot(h_ref[0], w_ref[:, W_C + CONV_DIM + MXU_COLS * j:W_C + CONV_DIM + MXU_COLS * (j + 1)])
            yield
            dt = _softplus0(dt_raw + dtb_ref[:, sl])
            la_s[nxt, :, sl] = -dt * jnp.exp(alog_ref[:, sl])
            ubuf[CONV_PAD:CONV_PAD + tl, sl] = _dot(h_ref[0], w_ref[:, W_C + MXU_COLS * j:W_C + MXU_COLS * (j + 1)])
            yield
            for u in range(MXU_COLS // LANES):
                usl = slice(MXU_COLS * j + LANES * u, MXU_COLS * j + LANES * (u + 1))
                act = conv_act(usl)
                xc_s[nxt, :, usl] = act
                xdt_s[nxt, :, usl] = act * dt[:, LANES * u:LANES * (u + 1)]
            bsl = slice(W_C + MXU_COLS * j, W_C + MXU_COLS * (j + 1))
            ubuf[CONV_PAD:CONV_PAD + tl, bsl] = _dot(h_ref[0], w_ref[:, W_C + W_C + MXU_COLS * j:W_C + W_C + MXU_COLS * (j + 1)])
            yield
            for u in range(MXU_COLS // LANES):
                usl = slice(W_C + MXU_COLS * j + LANES * u, W_C + MXU_COLS * j + LANES * (u + 1))
                bc_s[nxt, :, MXU_COLS * j + LANES * u:MXU_COLS * j + LANES * (u + 1)] = conv_act(usl)

    tri = _tri_bf16(CHUNK)
    trow = lax.broadcasted_iota(jnp.int32, (CHUNK, LANES), 0)
    lane = lax.broadcasted_iota(jnp.int32, (CHUNK, LANES), 1)
    s_of_lane = lane % CHUNK
    causal = trow >= s_of_lane
    row2 = lax.broadcasted_iota(jnp.int32, (2 * CHUNK, LANES), 0)
    lane2 = lax.broadcasted_iota(jnp.int32, (2 * CHUNK, LANES), 1)
    blockdiag = (row2 < CHUNK) == (lane2 < HEAD_DIM_C)

    for c in range(n_chunks):
        rows = slice(CHUNK * c, CHUNK * (c + 1))

        def group(g, rows=rows):
            gsl = slice(hp * g, hp * (g + 1))
            bg = bc_s[cur, rows, D_STATE * g:D_STATE * (g + 1)].astype(BF16)
            cg = bc_s[cur, rows, gw + D_STATE * g:gw + D_STATE * (g + 1)].astype(BF16)
            cb2 = _dot_nt(cg, jnp.concatenate([bg, bg], axis=0))
            yield
            st = st_s[g]
            y_state = _dot(cg, st.astype(BF16))
            yield
            la = la_s[cur, rows, gsl]
            b = _cumsum_rows(tri, la)
            yield
            xdt = xdt_s[cur, rows, gsl]
            ys = []
            for u in range(2):
                usl = slice(LANES * u, LANES * (u + 1))
                b_row = jnp.sum(jnp.where(trow <= s_of_lane, la[:, usl], 0.0), axis=0, keepdims=True)
                decay = jnp.where(causal, jnp.exp(b[:, usl] - b_row), 0.0)
                wmat = (cb2 * decay).astype(BF16)
                xbd = jnp.where(blockdiag, jnp.concatenate([xdt[:, usl], xdt[:, usl]], axis=0), 0.0).astype(BF16)
                ys.append(_dot(wmat, xbd))
                yield
            y = jnp.concatenate(ys, axis=1) + y_state * jnp.exp(b)
            o = (y + dsk_ref[:, gsl] * xc_s[cur, rows, gsl]) * _silu(z_s[cur, rows, gsl])
            r = lax.rsqrt(jnp.mean(o * o, axis=-1, keepdims=True) + EPS)
            out = ((o * r) * sn_ref[:, gsl]).astype(BF16)
            b_last = b[CHUNK - 1:CHUNK, :]
            st_new = st * jnp.exp(b_last) + _dot_tn(bg, (xdt * jnp.exp(b_last - b)).astype(BF16))
            return out, st_new

        results = _round_robin([group(g) for g in range(N_GROUPS_C)] + [project(c)])[:N_GROUPS_C]
        for g, (out, st_new) in enumerate(results):
            y_ref[0, rows, hp * g:hp * (g + 1)] = out
            st_s[g] = st_new

    @pl.when(i_proj == ni - 1)
    def _():
        convnew_ref[0] = ubuf[tail_rows, :]

    @pl.when(i_lag == ni - 1)
    def _():
        for g in range(N_GROUPS_C):
            snew_ref[0, hp * g:hp * (g + 1), :] = st_s[g].T


def _ssd_call(h, w_c, cw, cb, dtb, alog, dsk, sn, conv0, s0, tl):
    bsz, seq, _ = h.shape
    ni = seq // tl
    n_blocks = bsz * ni
    x_map, y_map, _, batch_map = _lagged_maps(n_blocks, ni)
    proj_batch_map = lambda n: (jnp.minimum(n, n_blocks - 1) // ni, 0, 0)
    const2 = lambda n: (0, 0)
    wcols = w_c.shape[1]
    return pl.pallas_call(
        functools.partial(_ssd_kernel, tl=tl, ni=ni, n_blocks=n_blocks),
        grid=(n_blocks + 1,),
        in_specs=[
            pl.BlockSpec((1, tl, D_MODEL), x_map),
            pl.BlockSpec((D_MODEL, wcols), const2, pipeline_mode=pl.Buffered(1)),
            pl.BlockSpec((CONV_W, CONV_DIM), const2),
            pl.BlockSpec((1, CONV_DIM), const2),
            pl.BlockSpec((1, W_C), const2),
            pl.BlockSpec((1, W_C), const2),
            pl.BlockSpec((1, W_C), const2),
            pl.BlockSpec((1, W_C), const2),
            pl.BlockSpec((1, CONV_W - 1, CONV_DIM), proj_batch_map),
            pl.BlockSpec((1, W_C, D_STATE), batch_map(3)),
        ],
        out_specs=[
            pl.BlockSpec((1, tl, W_C), y_map),
            pl.BlockSpec((1, W_C, D_STATE), batch_map(3)),
            pl.BlockSpec((1, CONV_W - 1, CONV_DIM), proj_batch_map),
        ],
        out_shape=[
            jax.ShapeDtypeStruct((bsz, seq, W_C), BF16),
            jax.ShapeDtypeStruct((bsz, W_C, D_STATE), F32),
            jax.ShapeDtypeStruct((bsz, CONV_W - 1, CONV_DIM), F32),
        ],
        scratch_shapes=[
            pltpu.VMEM((CONV_PAD + tl, CONV_DIM), F32),
            pltpu.VMEM((2, tl, W_C), F32),
            pltpu.VMEM((2, tl, W_C), F32),
            pltpu.VMEM((2, tl, W_C), F32),
            pltpu.VMEM((2, tl, 2 * N_GROUPS_C * D_STATE), F32),
            pltpu.VMEM((2, tl, W_C), F32),
            pltpu.VMEM((N_GROUPS_C, D_STATE, W_C // N_GROUPS_C), F32),
        ],
        compiler_params=pltpu.CompilerParams(
            dimension_semantics=("arbitrary",), vmem_limit_bytes=VMEM_LIMIT),
        name="ssd_mixer",
    )(h, w_c, cw, cb, dtb, alog, dsk, sn, conv0, s0)


def _merge_kernel(*refs, emit_next):
    if emit_next:
        (x_ref, h_ref, ya_ref, yb_ref, yc_ref, npost_ref, nnext_ref, wg_ref, wbr_ref, wout_ref,
         o_ref, hnext_ref, m_s) = refs
    else:
        (x_ref, h_ref, ya_ref, yb_ref, yc_ref, npost_ref, wg_ref, wbr_ref, wout_ref, o_ref, m_s) = refs
    j = pl.program_id(1)
    n_j = m_s.shape[0]

    h = h_ref[...]
    merged = None
    for bi, y_ref in enumerate((ya_ref, yb_ref, yc_ref)):
        term = _sigmoid(_dot(h, wg_ref[bi])) * _dot(y_ref[...], wbr_ref[bi])
        merged = term if merged is None else merged + term
    m_s[j] = merged.astype(BF16)

    @pl.when(j == n_j - 1)
    def _():
        out = _dot(jnp.concatenate([m_s[n] for n in range(n_j)], axis=1), wout_ref[...])
        r = lax.rsqrt(jnp.mean(out * out, axis=-1, keepdims=True) + EPS)
        x_new = x_ref[...] + (out * r) * npost_ref[...]
        o_ref[...] = x_new
        if emit_next:
            hnext_ref[...] = _prenorm_bf16(x_new, nnext_ref[...])


def _merge_call(x2, h2, ya, yb, yc, npost, nnext, wg, wbr, wout, tm, tn):
    rows = x2.shape[0]
    emit_next = nnext is not None
    grid = (rows // tm, D_MODEL // tn)
    const2 = lambda r, j: (0, 0)
    row_block = lambda width: pl.BlockSpec((tm, width), lambda r, j: (r, 0))
    vec = pl.BlockSpec((1, D_MODEL), const2)
    in_specs = [row_block(D_MODEL), row_block(D_MODEL), row_block(W_A), row_block(W_B), row_block(W_C), vec]
    args = [x2, h2, ya, yb, yc, npost]
    if emit_next:
        in_specs.append(vec)
        args.append(nnext)
    in_specs += [
        pl.BlockSpec((3, D_MODEL, tn), lambda r, j: (0, 0, j)),
        pl.BlockSpec((3, W_A, tn), lambda r, j: (0, 0, j)),
        pl.BlockSpec((D_MODEL, D_MODEL), const2, pipeline_mode=pl.Buffered(1)),
    ]
    args += [wg, wbr, wout]
    out_specs = [row_block(D_MODEL)]
    out_shape = [jax.ShapeDtypeStruct((rows, D_MODEL), F32)]
    if emit_next:
        out_specs.append(row_block(D_MODEL))
        out_shape.append(jax.ShapeDtypeStruct((rows, D_MODEL), BF16))
    outs = pl.pallas_call(
        functools.partial(_merge_kernel, emit_next=emit_next),
        grid=grid,
        in_specs=in_specs,
        out_specs=out_specs,
        out_shape=out_shape,
        scratch_shapes=[pltpu.VMEM((D_MODEL // tn, tm, tn), BF16)],
        compiler_params=pltpu.CompilerParams(
            dimension_semantics=("arbitrary", "arbitrary"), vmem_limit_bytes=VMEM_LIMIT),
        name="merge_out",
    )(*args)
    return (outs[0], outs[1]) if emit_next else (outs[0], None)


def _prenorm_kernel(x_ref, w_ref, h_ref):
    h_ref[...] = _prenorm_bf16(x_ref[...], w_ref[...])


def _prenorm_call(x2, nw, tm):
    rows = x2.shape[0]
    return pl.pallas_call(
        _prenorm_kernel,
        grid=(rows // tm,),
        in_specs=[pl.BlockSpec((tm, D_MODEL), lambda r: (r, 0)), pl.BlockSpec((1, D_MODEL), lambda r: (0, 0))],
        out_specs=pl.BlockSpec((tm, D_MODEL), lambda r: (r, 0)),
        out_shape=jax.ShapeDtypeStruct((rows, D_MODEL), BF16),
        compiler_params=pltpu.CompilerParams(dimension_semantics=("arbitrary",), vmem_limit_bytes=VMEM_LIMIT),
        name="prenorm",
    )(x2, nw)


def _rope_tables(pos):
    half = ROT_DIM // 2
    inv = jnp.power(ROPE_THETA, -jnp.arange(half, dtype=F32) / half)
    ang = pos.astype(F32)[:, None] * inv[None, :]
    cos, sin = jnp.cos(ang), jnp.sin(ang)
    n = pos.shape[0]
    ones = jnp.ones((n, HEAD_DIM_B - ROT_DIM), F32)
    zeros = jnp.zeros((n, HEAD_DIM_B - ROT_DIM), F32)
    zh = jnp.zeros((n, half), F32)
    cos_t = jnp.concatenate([cos, cos, ones], axis=1)
    sin_a = jnp.concatenate([-sin, zh, zeros], axis=1)
    sin_b = jnp.concatenate([zh, sin, zeros], axis=1)
    tile = lambda t: jnp.concatenate([t, t], axis=1)
    return tile(cos_t), tile(sin_a), tile(sin_b)


def _dup_heads(t):
    lead = t.shape[:-1]
    t4 = t.reshape(lead + (N_KV_B, 1, HEAD_DIM_B))
    return jnp.broadcast_to(t4, lead + (N_KV_B, 2, HEAD_DIM_B)).reshape(lead + (2 * KV_W_B,))


def _col_blocks(w):
    return w.reshape(w.shape[0], -1, MXU_COLS).transpose(1, 0, 2)


def _expand_heads(t):
    return jnp.repeat(t, HEAD_DIM_C, axis=-1)


def _layer(x, h, tl, tables, caches, lw, norm_next):
    bsz, seq, _ = x.shape
    cos, sin_a, sin_b = tables
    kc, vc, s_hgrn, s_ssm, s_conv = caches
    if s_hgrn is None:
        s_hgrn = jnp.zeros((bsz, N_HEADS_A, HEAD_K_A, HEAD_K_A), F32)
        s_ssm = jnp.zeros((bsz, W_C, D_STATE), F32)
        s_conv = jnp.zeros((bsz, CONV_W - 1, CONV_DIM), F32)
    else:
        s_ssm = s_ssm.reshape(bsz, W_C, D_STATE)
        kc = _dup_heads(kc.reshape(bsz, WINDOW, KV_W_B))
        vc = _dup_heads(vc.reshape(bsz, WINDOW, KV_W_B))

    ya, hgrn_new = _hgrn_call(h, lw["w_a"], lw["lb_pack"], lw["hgrn_norm"], s_hgrn, tl)
    yb, k_new, v_new = _swa_call(h, lw["w_b"], cos, sin_a, sin_b, lw["sinks"], kc, vc, tl)
    yc, ssm_new, conv_new = _ssd_call(h, lw["w_c"], lw["conv_w"], lw["conv_b"], lw["dt_bias"],
                                      lw["a_log"], lw["d_skip"], lw["ssm_norm"], s_conv, s_ssm, tl)
    rows = bsz * seq
    x_new, h_next = _merge_call(x.reshape(rows, D_MODEL), h.reshape(rows, D_MODEL), ya.reshape(rows, W_A),
                                yb.reshape(rows, W_B), yc.reshape(rows, W_C), lw["norm_post"], norm_next,
                                lw["w_g"], lw["w_br"], lw["w_out"], min(MERGE_ROWS, rows), MXU_COLS)
    states = (k_new.reshape(bsz, WINDOW, N_KV_B, HEAD_DIM_B), v_new.reshape(bsz, WINDOW, N_KV_B, HEAD_DIM_B),
              hgrn_new, ssm_new.reshape(bsz, N_HEADS_C, HEAD_DIM_C, D_STATE), conv_new)
    if h_next is not None:
        h_next = h_next.reshape(bsz, seq, D_MODEL)
    return x_new.reshape(bsz, seq, D_MODEL), h_next, states


def kernel(x_prompt, x_sample, cache_swa_k, cache_swa_v, state_hgrn, state_ssm, state_conv, norm_pre, norm_post, w_in, hgrn_lb_logits, hgrn_norm, swa_sinks, conv_w, conv_b, dt_bias, a_log, d_skip, ssm_norm, w_branch_a, w_branch_b, w_branch_c, w_out):
    depth = w_in.shape[0]
    lbp = jax.nn.softmax(hgrn_lb_logits.astype(F32), axis=0)
    lbc = jnp.cumsum(lbp, axis=0)
    lb_all = lbc - lbc[0:1]

    offs = np.cumsum([0, W_A, W_A, W_A, W_A, W_B, KV_W_B, KV_W_B, W_B, W_C, CONV_DIM, N_HEADS_C,
                      D_MODEL, D_MODEL, D_MODEL])
    col = lambda l, a, b: w_in[l, :, int(offs[a]):int(offs[b])]

    tables_p = _rope_tables(jnp.arange(x_prompt.shape[1], dtype=jnp.int32))
    tables_s = _rope_tables(PAST_LEN + jnp.arange(x_sample.shape[1], dtype=jnp.int32))

    xp, xs = x_prompt, x_sample
    pre = lambda l: norm_pre[l].reshape(1, D_MODEL)
    hp = _prenorm_call(xp.reshape(-1, D_MODEL), pre(0), MERGE_ROWS).reshape(xp.shape)
    hs = _prenorm_call(xs.reshape(-1, D_MODEL), pre(0), MERGE_ROWS).reshape(xs.shape)
    pst, sst = [], []
    for l in range(depth):
        lb = lb_all[l].reshape(N_HEADS_A, 1, HEAD_K_A)
        norm_next = pre(l + 1) if l + 1 < depth else None
        lw = {
            "norm_post": norm_post[l].reshape(1, D_MODEL),
            "w_a": _col_blocks(col(l, 0, 4).astype(BF16)),
            "w_b": jnp.concatenate([col(l, 4, 5), _dup_heads(col(l, 5, 6)), _dup_heads(col(l, 6, 7)),
                                    col(l, 7, 8)], axis=1).astype(BF16),
            "w_c": jnp.concatenate([col(l, 8, 9), col(l, 9, 10), _expand_heads(col(l, 10, 11))],
                                   axis=1).astype(BF16),
            "w_g": jnp.stack([col(l, 11, 12), col(l, 12, 13), col(l, 13, 14)]).astype(BF16),
            "w_br": jnp.stack([w_branch_a[l], w_branch_b[l], w_branch_c[l]]).astype(BF16),
            "w_out": w_out[l].astype(BF16),
            "lb_pack": jnp.stack([jnp.log(lb), jnp.log1p(-lb)]) * LOG2E,
            "hgrn_norm": hgrn_norm[l].reshape(1, HEAD_K_A),
            "sinks": jnp.broadcast_to(swa_sinks[l].astype(F32)[:, None], (N_Q_B, LANES)),
            "conv_w": conv_w[l],
            "conv_b": conv_b[l].reshape(1, CONV_DIM),
            "dt_bias": _expand_heads(dt_bias[l].astype(F32)).reshape(1, W_C),
            "a_log": _expand_heads(a_log[l].astype(F32)).reshape(1, W_C),
            "d_skip": _expand_heads(d_skip[l].astype(F32)).reshape(1, W_C),
            "ssm_norm": ssm_norm[l].reshape(1, W_C),
        }
        xp, hp, sp = _layer(xp, hp, PROMPT_ROWS, tables_p, (None, None, None, None, None), lw, norm_next)
        xs, hs, ss = _layer(xs, hs, CHUNK, tables_s,
                            (cache_swa_k[l], cache_swa_v[l], state_hgrn[l], state_ssm[l], state_conv[l]),
                            lw, norm_next)
        pst.append(sp)
        sst.append(ss)

    stack = lambda sts, k: jnp.stack([s[k] for s in sts])
    return (xp, xs,
            stack(pst, 0), stack(pst, 1), stack(pst, 2), stack(pst, 3), stack(pst, 4),
            stack(sst, 0), stack(sst, 1), stack(sst, 2), stack(sst, 3), stack(sst, 4))
```

```python
import functools
import math

import jax
import jax.numpy as jnp
import numpy as np
from jax import lax
from jax.experimental import pallas as pl
from jax.experimental.pallas import tpu as pltpu

F32 = jnp.float32
BF16 = jnp.bfloat16

D_MODEL = 2048
CHUNK = 64
EPS = 1e-6
PAST_LEN = 4096

W_A = 1024
HEAD_K_A = 128
N_HEADS_A = 8

N_Q_B = 16
N_KV_B = 4
HEAD_DIM_B = 64
W_B = 1024
KV_W_B = 256
WINDOW = 128
ROT_DIM = 16
ROPE_THETA = 500000.0
ATTN_SCALE = HEAD_DIM_B ** -0.5

W_C = 1024
HEAD_DIM_C = 64
N_HEADS_C = 16
N_GROUPS_C = 4
D_STATE = 128
CONV_W = 4
CONV_DIM = 2048

LANES = 128
SUBLANES = 8
SUB_BLOCK = 16
MXU_COLS = 256
PROMPT_ROWS = 256
MERGE_ROWS = 512
LOG2E = math.log2(math.e)
VMEM_LIMIT = 56 * 1024 * 1024

NT_DIMS = (((1,), (1,)), ((), ()))
TN_DIMS = (((0,), (0,)), ((), ()))


def _dot(a, b):
    return jnp.dot(a, b, preferred_element_type=F32)


def _dot_nt(a, b):
    return lax.dot_general(a, b, NT_DIMS, preferred_element_type=F32)


def _dot_tn(a, b):
    return lax.dot_general(a, b, TN_DIMS, preferred_element_type=F32)


def _prenorm_bf16(x, w):
    r = lax.rsqrt(jnp.mean(x * x, axis=-1, keepdims=True) + EPS)
    return ((x * r) * w).astype(BF16)


def _sigmoid(x):
    return 0.5 * jnp.tanh(0.5 * x) + 0.5


def _silu(x):
    return x * _sigmoid(x)


def _softplus0(x):
    return jnp.maximum(x, 0.0) + jnp.log(1.0 + jnp.exp(-jnp.abs(x)))


def _tri_bf16(n):
    r = lax.broadcasted_iota(jnp.int32, (n, n), 0)
    c = lax.broadcasted_iota(jnp.int32, (n, n), 1)
    return (r >= c).astype(BF16)


def _cumsum_rows(tri, x):
    hi = x.astype(BF16)
    r1 = x - hi.astype(F32)
    mid = r1.astype(BF16)
    lo = (r1 - mid.astype(F32)).astype(BF16)
    return _dot(tri, hi) + _dot(tri, mid) + _dot(tri, lo)


def _round_robin(gens, background=None, heavy_round=0):
    results = [None] * len(gens)
    live = list(range(len(gens)))
    rnd = 0
    while live:
        if background is not None and rnd == heavy_round:
            for _ in background:
                pass
        for n in list(live):
            try:
                next(gens[n])
            except StopIteration as stop:
                results[n] = stop.value
                live.remove(n)
        rnd += 1
    return results


def _hgrn_kernel(h_ref, w_ref, lb_ref, hn_ref, s0_ref, y_ref, snew_ref,
                 proj_s, st_s, *, tl, ni):
    n = pl.program_id(0)
    n_chunks = tl // CHUNK
    i_lag = jnp.maximum(n - 1, 0) % ni
    nxt = n % 2
    cur = 1 - nxt

    @pl.when(n == 0)
    def _():
        proj_s[...] = jnp.zeros_like(proj_s)

    @pl.when(i_lag == 0)
    def _():
        for hd in range(N_HEADS_A):
            st_s[hd] = s0_ref[0, hd].T

    n_col_blocks = w_ref.shape[0]
    cols_per_chunk = n_col_blocks // n_chunks

    def project(c):
        for j in range(cols_per_chunk):
            jb = c * cols_per_chunk + j
            res = _dot(h_ref[0], w_ref[jb])
            proj_s[nxt, 2 * jb] = res[:, :LANES]
            proj_s[nxt, 2 * jb + 1] = res[:, LANES:]
            yield

    tri = _tri_bf16(CHUNK)
    n_sub = CHUNK // SUB_BLOCK
    lane8 = lax.broadcasted_iota(jnp.int32, (SUBLANES, LANES), 1)
    sub8 = lax.broadcasted_iota(jnp.int32, (SUBLANES, LANES), 0)
    row64 = lax.broadcasted_iota(jnp.int32, (CHUNK, CHUNK), 0)
    col64 = lax.broadcasted_iota(jnp.int32, (CHUNK, CHUNK), 1)
    below_diag_block = row64 // SUB_BLOCK > col64 // SUB_BLOCK
    zero8 = jnp.zeros((SUBLANES, LANES), F32)

    def zeros(n):
        return jnp.zeros((n, LANES), F32)

    def piece(a, m, u):
        lo = SUB_BLOCK * m + SUBLANES * u
        return a[lo:lo + SUBLANES]

    def unit(hd, r0):
        rows = pl.ds(r0, CHUNK)
        aq = proj_s[cur, hd, rows, :]
        z = proj_s[cur, N_HEADS_A + hd, rows, :]
        v = proj_s[cur, 2 * N_HEADS_A + hd, rows, :].astype(BF16)
        ag = proj_s[cur, 3 * N_HEADS_A + hd, rows, :]
        log_lb = lb_ref[0, hd]
        log1m_lb = lb_ref[1, hd]

        z2 = z * LOG2E
        log_sig = jnp.minimum(z2, 0.0) - jnp.log2(1.0 + jnp.exp2(-jnp.abs(z2)))
        cterm = log1m_lb + log_sig
        log_f = jnp.maximum(log_lb, cterm) + jnp.log2(1.0 + jnp.exp2(-jnp.abs(log_lb - cterm)))
        log_k = cterm - z2
        q = _silu(aq)

        b2 = _cumsum_rows(tri, log_f)
        yield
        c2 = b2 - log_k
        b2_last = b2[CHUNK - 1:CHUNK, :]
        st = st_s[hd]

        o = _dot_nt((q * jnp.exp2(b2)).astype(BF16), st.astype(BF16))
        yield

        q_slabs, k_slabs = [], []
        for m in range(1, n_sub):
            lo = SUB_BLOCK * m
            ref = b2[lo - 1:lo, :]
            qm = q[lo:lo + SUB_BLOCK] * jnp.exp2(b2[lo:lo + SUB_BLOCK] - ref)
            q_parts = [zeros(lo), qm] + ([zeros(CHUNK - lo - SUB_BLOCK)] if lo + SUB_BLOCK < CHUNK else [])
            q_slabs.append(jnp.concatenate(q_parts, axis=0))
            k_slabs.append(jnp.concatenate([jnp.exp2(ref - c2[:lo]), zeros(CHUNK - lo)], axis=0))
        a_off = _dot_nt(jnp.concatenate(q_slabs, axis=1).astype(BF16),
                        jnp.concatenate(k_slabs, axis=1).astype(BF16))
        yield

        order = [(s, m, u) for s in range(SUB_BLOCK) for m in range(n_sub) for u in range(2)
                 if not (u == 0 and s >= SUBLANES)]
        pieces = []
        for s, m, u in order:
            cs = c2[SUB_BLOCK * m + s:SUB_BLOCK * m + s + 1]
            pieces.append(piece(q, m, u) * jnp.exp2(piece(b2, m, u) - cs))
        sums = [jnp.sum(p, axis=-1, keepdims=True) for p in pieces]
        yield
        d = {(m, u): zero8 for m in range(n_sub) for u in range(2)}
        for n, (s, m, u) in enumerate(order):
            d[(m, u)] = jnp.where(lane8 == s, sums[n], d[(m, u)])
        diag_rows = []
        for m in range(n_sub):
            for u in range(2):
                dm = jnp.where(sub8 + SUBLANES * u >= lane8, d[(m, u)], 0.0)
                diag_rows.append(pltpu.roll(dm, SUB_BLOCK * m, 1) if m else dm)
        a_diag = jnp.concatenate(diag_rows, axis=0)[:, :CHUNK]
        a_full = jnp.where(below_diag_block, a_off, a_diag)
        o = o + _dot(a_full.astype(BF16), v)
        yield

        r = lax.rsqrt(jnp.mean(o * o, axis=-1, keepdims=True) + EPS)
        y = ((o * r) * hn_ref[...]) * _silu(ag)

        st_new = st * jnp.exp2(b2_last) + _dot_tn(v, jnp.exp2(b2_last - c2).astype(BF16))
        return y.astype(BF16), st_new

    def body(c, carry):
        r0 = pl.multiple_of(c * CHUNK, CHUNK)
        gens = [unit(hd, r0) for hd in range(N_HEADS_A)] + [project(c)]
        results = _round_robin(gens)[:N_HEADS_A]
        for hd, (y, st_new) in enumerate(results):
            y_ref[0, pl.ds(r0, CHUNK), LANES * hd:LANES * (hd + 1)] = y
            st_s[hd] = st_new
        return carry

    lax.fori_loop(0, n_chunks, body, 0)

    @pl.when(i_lag == ni - 1)
    def _():
        for hd in range(N_HEADS_A):
            snew_ref[0, hd] = st_s[hd].T


def _lagged_maps(n_blocks, ni):
    def split(m):
        return m // ni, m % ni
    x_map = lambda n: split(jnp.minimum(n, n_blocks - 1)) + (0,)
    y_map = lambda n: split(jnp.maximum(n - 1, 0)) + (0,)
    pos_map = lambda n: (jnp.minimum(n, n_blocks - 1) % ni, 0)
    batch_map = lambda nd: (lambda n: (jnp.maximum(n - 1, 0) // ni,) + (0,) * (nd - 1))
    return x_map, y_map, pos_map, batch_map


def _hgrn_call(h, w_a, lb_pack, hn, s0, tl):
    bsz, seq, _ = h.shape
    ni = seq // tl
    n_blocks = bsz * ni
    x_map, y_map, _, batch_map = _lagged_maps(n_blocks, ni)
    const2 = lambda n: (0, 0)
    return pl.pallas_call(
        functools.partial(_hgrn_kernel, tl=tl, ni=ni),
        grid=(n_blocks + 1,),
        in_specs=[
            pl.BlockSpec((1, tl, D_MODEL), x_map),
            pl.BlockSpec(w_a.shape, lambda n: (0, 0, 0), pipeline_mode=pl.Buffered(1)),
            pl.BlockSpec((2, N_HEADS_A, 1, HEAD_K_A), lambda n: (0, 0, 0, 0)),
            pl.BlockSpec((1, HEAD_K_A), const2),
            pl.BlockSpec((1, N_HEADS_A, HEAD_K_A, HEAD_K_A), batch_map(4)),
        ],
        out_specs=[
            pl.BlockSpec((1, tl, W_A), y_map),
            pl.BlockSpec((1, N_HEADS_A, HEAD_K_A, HEAD_K_A), batch_map(4)),
        ],
        out_shape=[
            jax.ShapeDtypeStruct((bsz, seq, W_A), BF16),
            jax.ShapeDtypeStruct((bsz, N_HEADS_A, HEAD_K_A, HEAD_K_A), F32),
        ],
        scratch_shapes=[
            pltpu.VMEM((2, 4 * N_HEADS_A, tl, LANES), F32),
            pltpu.VMEM((N_HEADS_A, HEAD_K_A, HEAD_K_A), F32),
        ],
        compiler_params=pltpu.CompilerParams(
            dimension_semantics=("arbitrary",), vmem_limit_bytes=VMEM_LIMIT),
        name="hgrn_mixer",
    )(h, w_a, lb_pack, hn, s0)


def _rotate(xs, cos, sin_a, sin_b):
    return xs * cos + pltpu.roll(xs, LANES - ROT_DIM // 2, 1) * sin_a + pltpu.roll(xs, ROT_DIM // 2, 1) * sin_b


def _swa_kernel(*refs, tl, ni, n_blocks, has_cache):
    if has_cache:
        (h_ref, w_ref, cos_ref, sa_ref, sb_ref, sink_ref, kc_ref, vc_ref,
         y_ref, knew_ref, vnew_ref, q_s, g_s, kbuf, vbuf) = refs
    else:
        (h_ref, w_ref, cos_ref, sa_ref, sb_ref, sink_ref,
         y_ref, knew_ref, vnew_ref, q_s, g_s, kbuf, vbuf) = refs
    n = pl.program_id(0)
    n_chunks = tl // CHUNK
    band = WINDOW + CHUNK
    kvw = 2 * KV_W_B
    i_proj = jnp.minimum(n, n_blocks - 1) % ni
    i_lag = jnp.maximum(n - 1, 0) % ni
    nxt = n % 2
    cur = 1 - nxt

    @pl.when(n == 0)
    def _():
        for buf in (q_s, g_s, kbuf, vbuf):
            buf[...] = jnp.zeros_like(buf)

    @pl.when(i_proj == 0)
    def _():
        if has_cache:
            kbuf[nxt, 0:WINDOW, :] = kc_ref[0]
            vbuf[nxt, 0:WINDOW, :] = vc_ref[0]
        else:
            kbuf[nxt, 0:WINDOW, :] = jnp.zeros((WINDOW, kvw), F32)
            vbuf[nxt, 0:WINDOW, :] = jnp.zeros((WINDOW, kvw), F32)

    @pl.when(i_proj != 0)
    def _():
        kbuf[nxt, 0:WINDOW, :] = kbuf[cur, tl:tl + WINDOW, :]
        vbuf[nxt, 0:WINDOW, :] = vbuf[cur, tl:tl + WINDOW, :]

    def project(c):
        per = (W_B // MXU_COLS) // n_chunks
        new_rows = slice(WINDOW, WINDOW + tl)
        for j in range(c * per, (c + 1) * per):
            sl = slice(MXU_COLS * j, MXU_COLS * (j + 1))
            res = _dot(h_ref[0], w_ref[:, sl])
            yield
            for u in range(MXU_COLS // LANES):
                q_s[nxt, :, MXU_COLS * j + LANES * u:MXU_COLS * j + LANES * (u + 1)] = _rotate(
                    res[:, LANES * u:LANES * (u + 1)], cos_ref[...], sa_ref[...], sb_ref[...])
            g_s[nxt, :, sl] = _dot(h_ref[0], w_ref[:, W_B + 2 * kvw + MXU_COLS * j:W_B + 2 * kvw + MXU_COLS * (j + 1)])
            yield
            jk = j % (kvw // MXU_COLS)
            ksl = slice(MXU_COLS * jk, MXU_COLS * (jk + 1))
            if j < kvw // MXU_COLS:
                res = _dot(h_ref[0], w_ref[:, W_B + MXU_COLS * jk:W_B + MXU_COLS * (jk + 1)])
                yield
                for u in range(MXU_COLS // LANES):
                    kbuf[nxt, new_rows, MXU_COLS * jk + LANES * u:MXU_COLS * jk + LANES * (u + 1)] = _rotate(
                        res[:, LANES * u:LANES * (u + 1)], cos_ref[...], sa_ref[...], sb_ref[...])
            else:
                vbuf[nxt, new_rows, ksl] = _dot(
                    h_ref[0], w_ref[:, W_B + kvw + MXU_COLS * jk:W_B + kvw + MXU_COLS * (jk + 1)])
                yield

    lane = lax.broadcasted_iota(jnp.int32, (CHUNK, LANES), 1)
    low = lane < HEAD_DIM_B
    low1 = lax.broadcasted_iota(jnp.int32, (1, LANES), 1) < HEAD_DIM_B
    key_row = lax.broadcasted_iota(jnp.int32, (band, LANES), 0)

    def pair(c, j):
        r0 = CHUNK * c
        g = j // 2
        qp = q_s[cur, r0:r0 + CHUNK, LANES * j:LANES * (j + 1)]
        q2 = jnp.concatenate([jnp.where(low, qp, 0.0), jnp.where(low, 0.0, qp)], axis=0).astype(BF16)
        kb = kbuf[cur, r0:r0 + band, LANES * g:LANES * (g + 1)].astype(BF16)
        vb = vbuf[cur, r0:r0 + band, LANES * g:LANES * (g + 1)].astype(BF16)
        s = _dot_nt(kb, q2) * ATTN_SCALE
        yield
        if not has_cache:
            s = jnp.where(i_lag * tl + r0 - WINDOW + key_row >= 0, s, -jnp.inf)
        sk = jnp.where(low1, sink_ref[2 * j:2 * j + 1, :], sink_ref[2 * j + 1:2 * j + 2, :])
        m = jnp.maximum(jnp.max(s, axis=0, keepdims=True), sk)
        p = jnp.exp(s - m)
        den = jnp.sum(p, axis=0, keepdims=True) + jnp.exp(sk - m)
        o2 = _dot_tn((p * (1.0 / den)).astype(BF16), vb)
        yield
        o = jnp.where(low, o2[:CHUNK], o2[CHUNK:])
        y = o * _silu(g_s[cur, r0:r0 + CHUNK, LANES * j:LANES * (j + 1)])
        y_ref[0, r0:r0 + CHUNK, LANES * j:LANES * (j + 1)] = y.astype(BF16)

    for c in range(n_chunks):
        _round_robin([pair(c, j) for j in range(N_Q_B // 2)], project(c), heavy_round=1)

    @pl.when(i_proj == ni - 1)
    def _():
        ktail = kbuf[nxt, tl:tl + WINDOW, :]
        vtail = vbuf[nxt, tl:tl + WINDOW, :]
        low_w = lax.broadcasted_iota(jnp.int32, (WINDOW, LANES), 1) < HEAD_DIM_B
        for u in range(KV_W_B // LANES):
            knew_ref[0, :, LANES * u:LANES * (u + 1)] = jnp.where(
                low_w, ktail[:, 2 * LANES * u:2 * LANES * u + LANES],
                ktail[:, 2 * LANES * u + LANES:2 * LANES * (u + 1)])
            vnew_ref[0, :, LANES * u:LANES * (u + 1)] = jnp.where(
                low_w, vtail[:, 2 * LANES * u:2 * LANES * u + LANES],
                vtail[:, 2 * LANES * u + LANES:2 * LANES * (u + 1)])


def _swa_call(h, w_b, cos, sin_a, sin_b, sinks, kc, vc, tl):
    bsz, seq, _ = h.shape
    has_cache = kc is not None
    ni = seq // tl
    n_blocks = bsz * ni
    x_map, y_map, pos_map, _ = _lagged_maps(n_blocks, ni)
    proj_batch_map = lambda n: (jnp.minimum(n, n_blocks - 1) // ni, 0, 0)
    const2 = lambda n: (0, 0)
    wcols = w_b.shape[1]
    in_specs = [
        pl.BlockSpec((1, tl, D_MODEL), x_map),
        pl.BlockSpec((D_MODEL, wcols), const2, pipeline_mode=pl.Buffered(1)),
        pl.BlockSpec((tl, LANES), pos_map),
        pl.BlockSpec((tl, LANES), pos_map),
        pl.BlockSpec((tl, LANES), pos_map),
        pl.BlockSpec((N_Q_B, LANES), const2),
    ]
    args = [h, w_b, cos, sin_a, sin_b, sinks]
    if has_cache:
        in_specs += [pl.BlockSpec((1, WINDOW, 2 * KV_W_B), proj_batch_map)] * 2
        args += [kc, vc]
    return pl.pallas_call(
        functools.partial(_swa_kernel, tl=tl, ni=ni, n_blocks=n_blocks, has_cache=has_cache),
        grid=(n_blocks + 1,),
        in_specs=in_specs,
        out_specs=[
            pl.BlockSpec((1, tl, W_B), y_map),
            pl.BlockSpec((1, WINDOW, KV_W_B), proj_batch_map),
            pl.BlockSpec((1, WINDOW, KV_W_B), proj_batch_map),
        ],
        out_shape=[
            jax.ShapeDtypeStruct((bsz, seq, W_B), BF16),
            jax.ShapeDtypeStruct((bsz, WINDOW, KV_W_B), F32),
            jax.ShapeDtypeStruct((bsz, WINDOW, KV_W_B), F32),
        ],
        scratch_shapes=[
            pltpu.VMEM((2, tl, W_B), F32),
            pltpu.VMEM((2, tl, W_B), F32),
            pltpu.VMEM((2, WINDOW + tl, 2 * KV_W_B), F32),
            pltpu.VMEM((2, WINDOW + tl, 2 * KV_W_B), F32),
        ],
        compiler_params=pltpu.CompilerParams(
            dimension_semantics=("arbitrary",), vmem_limit_bytes=VMEM_LIMIT),
        name="swa_mixer",
    )(*args)


CONV_PAD = 8


def _ssd_kernel(h_ref, w_ref, cw_ref, cb_ref, dtb_ref, alog_ref, dsk_ref, sn_ref, conv0_ref, s0_ref,
                y_ref, snew_ref, convnew_ref,
                ubuf, z_s, xc_s, xdt_s, bc_s, la_s, st_s, *, tl, ni, n_blocks):
    n = pl.program_id(0)
    n_chunks = tl // CHUNK
    gw = N_GROUPS_C * D_STATE
    hp = W_C // N_GROUPS_C
    i_proj = jnp.minimum(n, n_blocks - 1) % ni
    i_lag = jnp.maximum(n - 1, 0) % ni
    nxt = n % 2
    cur = 1 - nxt
    tail_rows = slice(CONV_PAD - (CONV_W - 1), CONV_PAD)

    @pl.when(n == 0)
    def _():
        for buf in (z_s, xc_s, xdt_s, bc_s, la_s):
            buf[...] = jnp.zeros_like(buf)

    @pl.when(i_proj == 0)
    def _():
        ubuf[tail_rows, :] = conv0_ref[0]

    @pl.when(i_lag == 0)
    def _():
        for g in range(N_GROUPS_C):
            st_s[g] = s0_ref[0, hp * g:hp * (g + 1), :].T

    def conv_act(sl):
        win = ubuf[0:CONV_PAD + tl, sl]
        acc = cb_ref[:, sl]
        for t in range(CONV_W):
            back = CONV_W - 1 - t
            src = pltpu.roll(win, back, 0) if back else win
            acc = acc + src[CONV_PAD:CONV_PAD + tl] * cw_ref[t:t + 1, sl]
        ubuf[tail_rows, sl] = ubuf[tl + CONV_PAD - (CONV_W - 1):tl + CONV_PAD, sl]
        return _silu(acc)

    def project(c):
        per = (W_C // MXU_COLS) // n_chunks
        for j in range(c * per, (c + 1) * per):
            sl = slice(MXU_COLS * j, MXU_COLS * (j + 1))
            z_s[nxt, :, sl] = _dot(h_ref[0], w_ref[:, sl])
            yield
            dt_raw = _dot(h_ref[0], w_ref[:, W_C + CONV_DIM + MXU_COLS * j:W_C + CONV_DIM + MXU_COLS * (j + 1)])
            yield
            dt = _softplus0(dt_raw + dtb_ref[:, sl])
            la_s[nxt, :, sl] = -dt * jnp.exp(alog_ref[:, sl])
            ubuf[CONV_PAD:CONV_PAD + tl, sl] = _dot(h_ref[0], w_ref[:, W_C + MXU_COLS * j:W_C + MXU_COLS * (j + 1)])
            yield
            for u in range(MXU_COLS // LANES):
                usl = slice(MXU_COLS * j + LANES * u, MXU_COLS * j + LANES * (u + 1))
                act = conv_act(usl)
                xc_s[nxt, :, usl] = act
                xdt_s[nxt, :, usl] = act * dt[:, LANES * u:LANES * (u + 1)]
            bsl = slice(W_C + MXU_COLS * j, W_C + MXU_COLS * (j + 1))
            ubuf[CONV_PAD:CONV_PAD + tl, bsl] = _dot(h_ref[0], w_ref[:, W_C + W_C + MXU_COLS * j:W_C + W_C + MXU_COLS * (j + 1)])
            yield
            for u in range(MXU_COLS // LANES):
                usl = slice(W_C + MXU_COLS * j + LANES * u, W_C + MXU_COLS * j + LANES * (u + 1))
                bc_s[nxt, :, MXU_COLS * j + LANES * u:MXU_COLS * j + LANES * (u + 1)] = conv_act(usl)

    tri = _tri_bf16(CHUNK)
    trow = lax.broadcasted_iota(jnp.int32, (CHUNK, LANES), 0)
    lane = lax.broadcasted_iota(jnp.int32, (CHUNK, LANES), 1)
    s_of_lane = lane % CHUNK
    causal = trow >= s_of_lane
    row2 = lax.broadcasted_iota(jnp.int32, (2 * CHUNK, LANES), 0)
    lane2 = lax.broadcasted_iota(jnp.int32, (2 * CHUNK, LANES), 1)
    blockdiag = (row2 < CHUNK) == (lane2 < HEAD_DIM_C)

    for c in range(n_chunks):
        rows = slice(CHUNK * c, CHUNK * (c + 1))

        def group(g, rows=rows):
            gsl = slice(hp * g, hp * (g + 1))
            bg = bc_s[cur, rows, D_STATE * g:D_STATE * (g + 1)].astype(BF16)
            cg = bc_s[cur, rows, gw + D_STATE * g:gw + D_STATE * (g + 1)].astype(BF16)
            cb2 = _dot_nt(cg, jnp.concatenate([bg, bg], axis=0))
            yield
            st = st_s[g]
            y_state = _dot(cg, st.astype(BF16))
            yield
            la = la_s[cur, rows, gsl]
            b = _cumsum_rows(tri, la)
            yield
            xdt = xdt_s[cur, rows, gsl]
            ys = []
            for u in range(2):
                usl = slice(LANES * u, LANES * (u + 1))
                b_row = jnp.sum(jnp.where(trow <= s_of_lane, la[:, usl], 0.0), axis=0, keepdims=True)
                decay = jnp.where(causal, jnp.exp(b[:, usl] - b_row), 0.0)
                wmat = (cb2 * decay).astype(BF16)
                xbd = jnp.where(blockdiag, jnp.concatenate([xdt[:, usl], xdt[:, usl]], axis=0), 0.0).astype(BF16)
                ys.append(_dot(wmat, xbd))
                yield
            y = jnp.concatenate(ys, axis=1) + y_state * jnp.exp(b)
            o = (y + dsk_ref[:, gsl] * xc_s[cur, rows, gsl]) * _silu(z_s[cur, rows, gsl])
            r = lax.rsqrt(jnp.mean(o * o, axis=-1, keepdims=True) + EPS)
            out = ((o * r) * sn_ref[:, gsl]).astype(BF16)
            b_last = b[CHUNK - 1:CHUNK, :]
            st_new = st * jnp.exp(b_last) + _dot_tn(bg, (xdt * jnp.exp(b_last - b)).astype(BF16))
            return out, st_new

        results = _round_robin([group(g) for g in range(N_GROUPS_C)] + [project(c)])[:N_GROUPS_C]
        for g, (out, st_new) in enumerate(results):
            y_ref[0, rows, hp * g:hp * (g + 1)] = out
            st_s[g] = st_new

    @pl.when(i_proj == ni - 1)
    def _():
        convnew_ref[0] = ubuf[tail_rows, :]

    @pl.when(i_lag == ni - 1)
    def _():
        for g in range(N_GROUPS_C):
            snew_ref[0, hp * g:hp * (g + 1), :] = st_s[g].T


def _ssd_call(h, w_c, cw, cb, dtb, alog, dsk, sn, conv0, s0, tl):
    bsz, seq, _ = h.shape
    ni = seq // tl
    n_blocks = bsz * ni
    x_map, y_map, _, batch_map = _lagged_maps(n_blocks, ni)
    proj_batch_map = lambda n: (jnp.minimum(n, n_blocks - 1) // ni, 0, 0)
    const2 = lambda n: (0, 0)
    wcols = w_c.shape[1]
    return pl.pallas_call(
        functools.partial(_ssd_kernel, tl=tl, ni=ni, n_blocks=n_blocks),
        grid=(n_blocks + 1,),
        in_specs=[
            pl.BlockSpec((1, tl, D_MODEL), x_map),
            pl.BlockSpec((D_MODEL, wcols), const2, pipeline_mode=pl.Buffered(1)),
            pl.BlockSpec((CONV_W, CONV_DIM), const2),
            pl.BlockSpec((1, CONV_DIM), const2),
            pl.BlockSpec((1, W_C), const2),
            pl.BlockSpec((1, W_C), const2),
            pl.BlockSpec((1, W_C), const2),
            pl.BlockSpec((1, W_C), const2),
            pl.BlockSpec((1, CONV_W - 1, CONV_DIM), proj_batch_map),
            pl.BlockSpec((1, W_C, D_STATE), batch_map(3)),
        ],
        out_specs=[
            pl.BlockSpec((1, tl, W_C), y_map),
            pl.BlockSpec((1, W_C, D_STATE), batch_map(3)),
            pl.BlockSpec((1, CONV_W - 1, CONV_DIM), proj_batch_map),
        ],
        out_shape=[
            jax.ShapeDtypeStruct((bsz, seq, W_C), BF16),
            jax.ShapeDtypeStruct((bsz, W_C, D_STATE), F32),
            jax.ShapeDtypeStruct((bsz, CONV_W - 1, CONV_DIM), F32),
        ],
        scratch_shapes=[
            pltpu.VMEM((CONV_PAD + tl, CONV_DIM), F32),
            pltpu.VMEM((2, tl, W_C), F32),
            pltpu.VMEM((2, tl, W_C), F32),
            pltpu.VMEM((2, tl, W_C), F32),
            pltpu.VMEM((2, tl, 2 * N_GROUPS_C * D_STATE), F32),
            pltpu.VMEM((2, tl, W_C), F32),
            pltpu.VMEM((N_GROUPS_C, D_STATE, W_C // N_GROUPS_C), F32),
        ],
        compiler_params=pltpu.CompilerParams(
            dimension_semantics=("arbitrary",), vmem_limit_bytes=VMEM_LIMIT),
        name="ssd_mixer",
    )(h, w_c, cw, cb, dtb, alog, dsk, sn, conv0, s0)


def _merge_kernel(*refs, emit_next):
    if emit_next:
        (x_ref, h_ref, ya_ref, yb_ref, yc_ref, npost_ref, nnext_ref, wg_ref, wbr_ref, wout_ref,
         o_ref, hnext_ref, m_s) = refs
    else:
        (x_ref, h_ref, ya_ref, yb_ref, yc_ref, npost_ref, wg_ref, wbr_ref, wout_ref, o_ref, m_s) = refs
    j = pl.program_id(1)
    n_j = m_s.shape[0]

    h = h_ref[...]
    merged = None
    for bi, y_ref in enumerate((ya_ref, yb_ref, yc_ref)):
        term = _sigmoid(_dot(h, wg_ref[bi])) * _dot(y_ref[...], wbr_ref[bi])
        merged = term if merged is None else merged + term
    m_s[j] = merged.astype(BF16)

    @pl.when(j == n_j - 1)
    def _():
        out = _dot(jnp.concatenate([m_s[n] for n in range(n_j)], axis=1), wout_ref[...])
        r = lax.rsqrt(jnp.mean(out * out, axis=-1, keepdims=True) + EPS)
        x_new = x_ref[...] + (out * r) * npost_ref[...]
        o_ref[...] = x_new
        if emit_next:
            hnext_ref[...] = _prenorm_bf16(x_new, nnext_ref[...])


def _merge_call(x2, h2, ya, yb, yc, npost, nnext, wg, wbr, wout, tm, tn):
    rows = x2.shape[0]
    emit_next = nnext is not None
    grid = (rows // tm, D_MODEL // tn)
    const2 = lambda r, j: (0, 0)
    row_block = lambda width: pl.BlockSpec((tm, width), lambda r, j: (r, 0))
    vec = pl.BlockSpec((1, D_MODEL), const2)
    in_specs = [row_block(D_MODEL), row_block(D_MODEL), row_block(W_A), row_block(W_B), row_block(W_C), vec]
    args = [x2, h2, ya, yb, yc, npost]
    if emit_next:
        in_specs.append(vec)
        args.append(nnext)
    in_specs += [
        pl.BlockSpec((3, D_MODEL, tn), lambda r, j: (0, 0, j)),
        pl.BlockSpec((3, W_A, tn), lambda r, j: (0, 0, j)),
        pl.BlockSpec((D_MODEL, D_MODEL), const2, pipeline_mode=pl.Buffered(1)),
    ]
    args += [wg, wbr, wout]
    out_specs = [row_block(D_MODEL)]
    out_shape = [jax.ShapeDtypeStruct((rows, D_MODEL), F32)]
    if emit_next:
        out_specs.append(row_block(D_MODEL))
        out_shape.append(jax.ShapeDtypeStruct((rows, D_MODEL), BF16))
    outs = pl.pallas_call(
        functools.partial(_merge_kernel, emit_next=emit_next),
        grid=grid,
        in_specs=in_specs,
        out_specs=out_specs,
        out_shape=out_shape,
        scratch_shapes=[pltpu.VMEM((D_MODEL // tn, tm, tn), BF16)],
        compiler_params=pltpu.CompilerParams(
            dimension_semantics=("arbitrary", "arbitrary"), vmem_limit_bytes=VMEM_LIMIT),
        name="merge_out",
    )(*args)
    return (outs[0], outs[1]) if emit_next else (outs[0], None)


def _prenorm_kernel(x_ref, w_ref, h_ref):
    h_ref[...] = _prenorm_bf16(x_ref[...], w_ref[...])


def _prenorm_call(x2, nw, tm):
    rows = x2.shape[0]
    return pl.pallas_call(
        _prenorm_kernel,
        grid=(rows // tm,),
        in_specs=[pl.BlockSpec((tm, D_MODEL), lambda r: (r, 0)), pl.BlockSpec((1, D_MODEL), lambda r: (0, 0))],
        out_specs=pl.BlockSpec((tm, D_MODEL), lambda r: (r, 0)),
        out_shape=jax.ShapeDtypeStruct((rows, D_MODEL), BF16),
        compiler_params=pltpu.CompilerParams(dimension_semantics=("arbitrary",), vmem_limit_bytes=VMEM_LIMIT),
        name="prenorm",
    )(x2, nw)


def _rope_tables(pos):
    half = ROT_DIM // 2
    inv = jnp.power(ROPE_THETA, -jnp.arange(half, dtype=F32) / half)
    ang = pos.astype(F32)[:, None] * inv[None, :]
    cos, sin = jnp.cos(ang), jnp.sin(ang)
    n = pos.shape[0]
    ones = jnp.ones((n, HEAD_DIM_B - ROT_DIM), F32)
    zeros = jnp.zeros((n, HEAD_DIM_B - ROT_DIM), F32)
    zh = jnp.zeros((n, half), F32)
    cos_t = jnp.concatenate([cos, cos, ones], axis=1)
    sin_a = jnp.concatenate([-sin, zh, zeros], axis=1)
    sin_b = jnp.concatenate([zh, sin, zeros], axis=1)
    tile = lambda t: jnp.concatenate([t, t], axis=1)
    return tile(cos_t), tile(sin_a), tile(sin_b)


def _dup_heads(t):
    lead = t.shape[:-1]
    t4 = t.reshape(lead + (N_KV_B, 1, HEAD_DIM_B))
    return jnp.broadcast_to(t4, lead + (N_KV_B, 2, HEAD_DIM_B)).reshape(lead + (2 * KV_W_B,))


def _col_blocks(w):
    return w.reshape(w.shape[0], -1, MXU_COLS).transpose(1, 0, 2)


def _expand_heads(t):
    return jnp.repeat(t, HEAD_DIM_C, axis=-1)


def _layer(x, h, tl, tables, caches, lw, norm_next):
    bsz, seq, _ = x.shape
    cos, sin_a, sin_b = tables
    kc, vc, s_hgrn, s_ssm, s_conv = caches
    if s_hgrn is None:
        s_hgrn = jnp.zeros((bsz, N_HEADS_A, HEAD_K_A, HEAD_K_A), F32)
        s_ssm = jnp.zeros((bsz, W_C, D_STATE), F32)
        s_conv = jnp.zeros((bsz, CONV_W - 1, CONV_DIM), F32)
    else:
        s_ssm = s_ssm.reshape(bsz, W_C, D_STATE)
        kc = _dup_heads(kc.reshape(bsz, WINDOW, KV_W_B))
        vc = _dup_heads(vc.reshape(bsz, WINDOW, KV_W_B))

    ya, hgrn_new = _hgrn_call(h, lw["w_a"], lw["lb_pack"], lw["hgrn_norm"], s_hgrn, tl)
    yb, k_new, v_new = _swa_call(h, lw["w_b"], cos, sin_a, sin_b, lw["sinks"], kc, vc, tl)
    yc, ssm_new, conv_new = _ssd_call(h, lw["w_c"], lw["conv_w"], lw["conv_b"], lw["dt_bias"],
                                      lw["a_log"], lw["d_skip"], lw["ssm_norm"], s_conv, s_ssm, tl)
    rows = bsz * seq
    x_new, h_next = _merge_call(x.reshape(rows, D_MODEL), h.reshape(rows, D_MODEL), ya.reshape(rows, W_A),
                                yb.reshape(rows, W_B), yc.reshape(rows, W_C), lw["norm_post"], norm_next,
                                lw["w_g"], lw["w_br"], lw["w_out"], min(MERGE_ROWS, rows), MXU_COLS)
    states = (k_new.reshape(bsz, WINDOW, N_KV_B, HEAD_DIM_B), v_new.reshape(bsz, WINDOW, N_KV_B, HEAD_DIM_B),
              hgrn_new, ssm_new.reshape(bsz, N_HEADS_C, HEAD_DIM_C, D_STATE), conv_new)
    if h_next is not None:
        h_next = h_next.reshape(bsz, seq, D_MODEL)
    return x_new.reshape(bsz, seq, D_MODEL), h_next, states


def kernel(x_prompt, x_sample, cache_swa_k, cache_swa_v, state_hgrn, state_ssm, state_conv, norm_pre, norm_post, w_in, hgrn_lb_logits, hgrn_norm, swa_sinks, conv_w, conv_b, dt_bias, a_log, d_skip, ssm_norm, w_branch_a, w_branch_b, w_branch_c, w_out):
    depth = w_in.shape[0]
    lbp = jax.nn.softmax(hgrn_lb_logits.astype(F32), axis=0)
    lbc = jnp.cumsum(lbp, axis=0)
    lb_all = lbc - lbc[0:1]

    offs = np.cumsum([0, W_A, W_A, W_A, W_A, W_B, KV_W_B, KV_W_B, W_B, W_C, CONV_DIM, N_HEADS_C,
                      D_MODEL, D_MODEL, D_MODEL])
    col = lambda l, a, b: w_in[l, :, int(offs[a]):int(offs[b])]

    tables_p = _rope_tables(jnp.arange(x_prompt.shape[1], dtype=jnp.int32))
    tables_s = _rope_tables(PAST_LEN + jnp.arange(x_sample.shape[1], dtype=jnp.int32))

    xp, xs = x_prompt, x_sample
    pre = lambda l: norm_pre[l].reshape(1, D_MODEL)
    hp = _prenorm_call(xp.reshape(-1, D_MODEL), pre(0), MERGE_ROWS).reshape(xp.shape)
    hs = _prenorm_call(xs.reshape(-1, D_MODEL), pre(0), MERGE_ROWS).reshape(xs.shape)
    pst, sst = [], []
    for l in range(depth):
        lb = lb_all[l].reshape(N_HEADS_A, 1, HEAD_K_A)
        norm_next = pre(l + 1) if l + 1 < depth else None
        lw = {
            "norm_post": norm_post[l].reshape(1, D_MODEL),
            "w_a": _col_blocks(col(l, 0, 4).astype(BF16)),
            "w_b": jnp.concatenate([col(l, 4, 5), _dup_heads(col(l, 5, 6)), _dup_heads(col(l, 6, 7)),
                                    col(l, 7, 8)], axis=1).astype(BF16),
            "w_c": jnp.concatenate([col(l, 8, 9), col(l, 9, 10), _expand_heads(col(l, 10, 11))],
                                   axis=1).astype(BF16),
            "w_g": jnp.stack([col(l, 11, 12), col(l, 12, 13), col(l, 13, 14)]).astype(BF16),
            "w_br": jnp.stack([w_branch_a[l], w_branch_b[l], w_branch_c[l]]).astype(BF16),
            "w_out": w_out[l].astype(BF16),
            "lb_pack": jnp.stack([jnp.log(lb), jnp.log1p(-lb)]) * LOG2E,
            "hgrn_norm": hgrn_norm[l].reshape(1, HEAD_K_A),
            "sinks": jnp.broadcast_to(swa_sinks[l].astype(F32)[:, None], (N_Q_B, LANES)),
            "conv_w": conv_w[l],
            "conv_b": conv_b[l].reshape(1, CONV_DIM),
            "dt_bias": _expand_heads(dt_bias[l].astype(F32)).reshape(1, W_C),
            "a_log": _expand_heads(a_log[l].astype(F32)).reshape(1, W_C),
            "d_skip": _expand_heads(d_skip[l].astype(F32)).reshape(1, W_C),
            "ssm_norm": ssm_norm[l].reshape(1, W_C),
        }
        xp, hp, sp = _layer(xp, hp, PROMPT_ROWS, tables_p, (None, None, None, None, None), lw, norm_next)
        xs, hs, ss = _layer(xs, hs, CHUNK, tables_s,
                            (cache_swa_k[l], cache_swa_v[l], state_hgrn[l], state_ssm[l], state_conv[l]),
                            lw, norm_next)
        pst.append(sp)
        sst.append(ss)

    stack = lambda sts, k: jnp.stack([s[k] for s in sts])
    return (xp, xs,
            stack(pst, 0), stack(pst, 1), stack(pst, 2), stack(pst, 3), stack(pst, 4),
            stack(sst, 0), stack(sst, 1), stack(sst, 2), stack(sst, 3), stack(sst, 4))
```

```python
import functools
import math

import jax
import jax.numpy as jnp
import numpy as np
from jax import lax
from jax.experimental import pallas as pl
from jax.experimental.pallas import tpu as pltpu

F32 = jnp.float32
BF16 = jnp.bfloat16

D_MODEL = 2048
CHUNK = 64
EPS = 1e-6
PAST_LEN = 4096

W_A = 1024
HEAD_K_A = 128
N_HEADS_A = 8

N_Q_B = 16
N_KV_B = 4
HEAD_DIM_B = 64
W_B = 1024
KV_W_B = 256
WINDOW = 128
ROT_DIM = 16
ROPE_THETA = 500000.0
ATTN_SCALE = HEAD_DIM_B ** -0.5

W_C = 1024
HEAD_DIM_C = 64
N_HEADS_C = 16
N_GROUPS_C = 4
D_STATE = 128
CONV_W = 4
CONV_DIM = 2048

LANES = 128
SUBLANES = 8
SUB_BLOCK = 16
MXU_COLS = 256
PROMPT_ROWS = 256
MERGE_ROWS = 512
LOG2E = math.log2(math.e)
VMEM_LIMIT = 56 * 1024 * 1024

NT_DIMS = (((1,), (1,)), ((), ()))
TN_DIMS = (((0,), (0,)), ((), ()))


def _dot(a, b):
    return jnp.dot(a, b, preferred_element_type=F32)


def _dot_nt(a, b):
    return lax.dot_general(a, b, NT_DIMS, preferred_element_type=F32)


def _dot_tn(a, b):
    return lax.dot_general(a, b, TN_DIMS, preferred_element_type=F32)


def _prenorm_bf16(x, w):
    r = lax.rsqrt(jnp.mean(x * x, axis=-1, keepdims=True) + EPS)
    return ((x * r) * w).astype(BF16)


def _sigmoid(x):
    return 0.5 * jnp.tanh(0.5 * x) + 0.5


def _silu(x):
    return x * _sigmoid(x)


def _softplus0(x):
    return jnp.maximum(x, 0.0) + jnp.log(1.0 + jnp.exp(-jnp.abs(x)))


def _tri_bf16(n):
    r = lax.broadcasted_iota(jnp.int32, (n, n), 0)
    c = lax.broadcasted_iota(jnp.int32, (n, n), 1)
    return (r >= c).astype(BF16)


def _cumsum_rows(tri, x):
    hi = x.astype(BF16)
    r1 = x - hi.astype(F32)
    mid = r1.astype(BF16)
    lo = (r1 - mid.astype(F32)).astype(BF16)
    return _dot(tri, hi) + _dot(tri, mid) + _dot(tri, lo)


def _round_robin(gens, background=None, heavy_round=0):
    results = [None] * len(gens)
    live = list(range(len(gens)))
    rnd = 0
    while live:
        if background is not None and rnd == heavy_round:
            for _ in background:
                pass
        for n in list(live):
            try:
                next(gens[n])
            except StopIteration as stop:
                results[n] = stop.value
                live.remove(n)
        rnd += 1
    return results


def _hgrn_kernel(h_ref, w_ref, lb_ref, hn_ref, s0_ref, y_ref, snew_ref,
                 proj_s, st_s, *, tl, ni):
    n = pl.program_id(0)
    n_chunks = tl // CHUNK
    i_lag = jnp.maximum(n - 1, 0) % ni
    nxt = n % 2
    cur = 1 - nxt

    @pl.when(n == 0)
    def _():
        proj_s[...] = jnp.zeros_like(proj_s)

    @pl.when(i_lag == 0)
    def _():
        for hd in range(N_HEADS_A):
            st_s[hd] = s0_ref[0, hd].T

    n_col_blocks = w_ref.shape[0]
    cols_per_chunk = n_col_blocks // n_chunks

    def project(c):
        for j in range(cols_per_chunk):
            jb = c * cols_per_chunk + j
            res = _dot(h_ref[0], w_ref[jb])
            proj_s[nxt, 2 * jb] = res[:, :LANES]
            proj_s[nxt, 2 * jb + 1] = res[:, LANES:]
            yield

    tri = _tri_bf16(CHUNK)
    n_sub = CHUNK // SUB_BLOCK
    lane8 = lax.broadcasted_iota(jnp.int32, (SUBLANES, LANES), 1)
    sub8 = lax.broadcasted_iota(jnp.int32, (SUBLANES, LANES), 0)
    row64 = lax.broadcasted_iota(jnp.int32, (CHUNK, CHUNK), 0)
    col64 = lax.broadcasted_iota(jnp.int32, (CHUNK, CHUNK), 1)
    below_diag_block = row64 // SUB_BLOCK > col64 // SUB_BLOCK
    zero8 = jnp.zeros((SUBLANES, LANES), F32)

    def zeros(n):
        return jnp.zeros((n, LANES), F32)

    def piece(a, m, u):
        lo = SUB_BLOCK * m + SUBLANES * u
        return a[lo:lo + SUBLANES]

    def unit(hd, r0):
        rows = pl.ds(r0, CHUNK)
        aq = proj_s[cur, hd, rows, :]
        z = proj_s[cur, N_HEADS_A + hd, rows, :]
        v = proj_s[cur, 2 * N_HEADS_A + hd, rows, :].astype(BF16)
        ag = proj_s[cur, 3 * N_HEADS_A + hd, rows, :]
        log_lb = lb_ref[0, hd]
        log1m_lb = lb_ref[1, hd]

        z2 = z * LOG2E
        log_sig = jnp.minimum(z2, 0.0) - jnp.log2(1.0 + jnp.exp2(-jnp.abs(z2)))
        cterm = log1m_lb + log_sig
        log_f = jnp.maximum(log_lb, cterm) + jnp.log2(1.0 + jnp.exp2(-jnp.abs(log_lb - cterm)))
        log_k = cterm - z2
        q = _silu(aq)

        b2 = _cumsum_rows(tri, log_f)
        yield
        c2 = b2 - log_k
        b2_last = b2[CHUNK - 1:CHUNK, :]
        st = st_s[hd]

        o = _dot_nt((q * jnp.exp2(b2)).astype(BF16), st.astype(BF16))
        yield

        q_slabs, k_slabs = [], []
        for m in range(1, n_sub):
            lo = SUB_BLOCK * m
            ref = b2[lo - 1:lo, :]
            qm = q[lo:lo + SUB_BLOCK] * jnp.exp2(b2[lo:lo + SUB_BLOCK] - ref)
            q_parts = [zeros(lo), qm] + ([zeros(CHUNK - lo - SUB_BLOCK)] if lo + SUB_BLOCK < CHUNK else [])
            q_slabs.append(jnp.concatenate(q_parts, axis=0))
            k_slabs.append(jnp.concatenate([jnp.exp2(ref - c2[:lo]), zeros(CHUNK - lo)], axis=0))
        a_off = _dot_nt(jnp.concatenate(q_slabs, axis=1).astype(BF16),
                        jnp.concatenate(k_slabs, axis=1).astype(BF16))
        yield

        order = [(s, m, u) for s in range(SUB_BLOCK) for m in range(n_sub) for u in range(2)
                 if not (u == 0 and s >= SUBLANES)]
        pieces = []
        for s, m, u in order:
            cs = c2[SUB_BLOCK * m + s:SUB_BLOCK * m + s + 1]
            pieces.append(piece(q, m, u) * jnp.exp2(piece(b2, m, u) - cs))
        sums = [jnp.sum(p, axis=-1, keepdims=True) for p in pieces]
        yield
        d = {(m, u): zero8 for m in range(n_sub) for u in range(2)}
        for n, (s, m, u) in enumerate(order):
            d[(m, u)] = jnp.where(lane8 == s, sums[n], d[(m, u)])
        diag_rows = []
        for m in range(n_sub):
            for u in range(2):
                dm = jnp.where(sub8 + SUBLANES * u >= lane8, d[(m, u)], 0.0)
                diag_rows.append(pltpu.roll(dm, SUB_BLOCK * m, 1) if m else dm)
        a_diag = jnp.concatenate(diag_rows, axis=0)[:, :CHUNK]
        a_full = jnp.where(below_diag_block, a_off, a_diag)
        o = o + _dot(a_full.astype(BF16), v)
        yield

        r = lax.rsqrt(jnp.mean(o * o, axis=-1, keepdims=True) + EPS)
        y = ((o * r) * hn_ref[...]) * _silu(ag)

        st_new = st * jnp.exp2(b2_last) + _dot_tn(v, jnp.exp2(b2_last - c2).astype(BF16))
        return y.astype(BF16), st_new

    def body(c, carry):
        r0 = pl.multiple_of(c * CHUNK, CHUNK)
        gens = [unit(hd, r0) for hd in range(N_HEADS_A)] + [project(c)]
        results = _round_robin(gens)[:N_HEADS_A]
        for hd, (y, st_new) in enumerate(results):
            y_ref[0, pl.ds(r0, CHUNK), LANES * hd:LANES * (hd + 1)] = y
            st_s[hd] = st_new
        return carry

    lax.fori_loop(0, n_chunks, body, 0)

    @pl.when(i_lag == ni - 1)
    def _():
        for hd in range(N_HEADS_A):
            snew_ref[0, hd] = st_s[hd].T


def _lagged_maps(n_blocks, ni):
    def split(m):
        return m // ni, m % ni
    x_map = lambda n: split(jnp.minimum(n, n_blocks - 1)) + (0,)
    y_map = lambda n: split(jnp.maximum(n - 1, 0)) + (0,)
    pos_map = lambda n: (jnp.minimum(n, n_blocks - 1) % ni, 0)
    batch_map = lambda nd: (lambda n: (jnp.maximum(n - 1, 0) // ni,) + (0,) * (nd - 1))
    return x_map, y_map, pos_map, batch_map


def _hgrn_call(h, w_a, lb_pack, hn, s0, tl):
    bsz, seq, _ = h.shape
    ni = seq // tl
    n_blocks = bsz * ni
    x_map, y_map, _, batch_map = _lagged_maps(n_blocks, ni)
    const2 = lambda n: (0, 0)
    return pl.pallas_call(
        functools.partial(_hgrn_kernel, tl=tl, ni=ni),
        grid=(n_blocks + 1,),
        in_specs=[
            pl.BlockSpec((1, tl, D_MODEL), x_map),
            pl.BlockSpec(w_a.shape, lambda n: (0, 0, 0), pipeline_mode=pl.Buffered(1)),
            pl.BlockSpec((2, N_HEADS_A, 1, HEAD_K_A), lambda n: (0, 0, 0, 0)),
            pl.BlockSpec((1, HEAD_K_A), const2),
            pl.BlockSpec((1, N_HEADS_A, HEAD_K_A, HEAD_K_A), batch_map(4)),
        ],
        out_specs=[
            pl.BlockSpec((1, tl, W_A), y_map),
            pl.BlockSpec((1, N_HEADS_A, HEAD_K_A, HEAD_K_A), batch_map(4)),
        ],
        out_shape=[
            jax.ShapeDtypeStruct((bsz, seq, W_A), BF16),
            jax.ShapeDtypeStruct((bsz, N_HEADS_A, HEAD_K_A, HEAD_K_A), F32),
        ],
        scratch_shapes=[
            pltpu.VMEM((2, 4 * N_HEADS_A, tl, LANES), F32),
            pltpu.VMEM((N_HEADS_A, HEAD_K_A, HEAD_K_A), F32),
        ],
        compiler_params=pltpu.CompilerParams(
            dimension_semantics=("arbitrary",), vmem_limit_bytes=VMEM_LIMIT),
        name="hgrn_mixer",
    )(h, w_a, lb_pack, hn, s0)


def _rotate(xs, cos, sin_a, sin_b):
    return xs * cos + pltpu.roll(xs, LANES - ROT_DIM // 2, 1) * sin_a + pltpu.roll(xs, ROT_DIM // 2, 1) * sin_b


def _swa_kernel(*refs, tl, ni, n_blocks, has_cache, from_x):
    it = iter(refs)
    if from_x:
        x_ref, nw_ref = next(it), next(it)
    else:
        h_ref = next(it)
    w_ref, cos_ref, sa_ref, sb_ref, sink_ref = (next(it) for _ in range(5))
    if has_cache:
        kc_ref, vc_ref = next(it), next(it)
    y_ref, knew_ref, vnew_ref = (next(it) for _ in range(3))
    if from_x:
        h_ref = next(it)
    q_s, g_s, kbuf, vbuf = (next(it) for _ in range(4))
    n = pl.program_id(0)
    n_chunks = tl // CHUNK
    band = WINDOW + CHUNK
    kvw = 2 * KV_W_B
    i_proj = jnp.minimum(n, n_blocks - 1) % ni
    i_lag = jnp.maximum(n - 1, 0) % ni
    nxt = n % 2
    cur = 1 - nxt

    @pl.when(n == 0)
    def _():
        for buf in (q_s, g_s, kbuf, vbuf):
            buf[...] = jnp.zeros_like(buf)

    @pl.when(i_proj == 0)
    def _():
        if has_cache:
            kbuf[nxt, 0:WINDOW, :] = kc_ref[0]
            vbuf[nxt, 0:WINDOW, :] = vc_ref[0]
        else:
            kbuf[nxt, 0:WINDOW, :] = jnp.zeros((WINDOW, kvw), F32)
            vbuf[nxt, 0:WINDOW, :] = jnp.zeros((WINDOW, kvw), F32)

    @pl.when(i_proj != 0)
    def _():
        kbuf[nxt, 0:WINDOW, :] = kbuf[cur, tl:tl + WINDOW, :]
        vbuf[nxt, 0:WINDOW, :] = vbuf[cur, tl:tl + WINDOW, :]

    if from_x:
        h_ref[0] = _prenorm_bf16(x_ref[0], nw_ref[...])

    def project(c):
        per = (W_B // MXU_COLS) // n_chunks
        new_rows = slice(WINDOW, WINDOW + tl)
        for j in range(c * per, (c + 1) * per):
            sl = slice(MXU_COLS * j, MXU_COLS * (j + 1))
            res = _dot(h_ref[0], w_ref[:, sl])
            yield
            for u in range(MXU_COLS // LANES):
                q_s[nxt, :, MXU_COLS * j + LANES * u:MXU_COLS * j + LANES * (u + 1)] = _rotate(
                    res[:, LANES * u:LANES * (u + 1)], cos_ref[...], sa_ref[...], sb_ref[...])
            g_s[nxt, :, sl] = _dot(h_ref[0], w_ref[:, W_B + 2 * kvw + MXU_COLS * j:W_B + 2 * kvw + MXU_COLS * (j + 1)])
            yield
            jk = j % (kvw // MXU_COLS)
            ksl = slice(MXU_COLS * jk, MXU_COLS * (jk + 1))
            if j < kvw // MXU_COLS:
                res = _dot(h_ref[0], w_ref[:, W_B + MXU_COLS * jk:W_B + MXU_COLS * (jk + 1)])
                yield
                for u in range(MXU_COLS // LANES):
                    kbuf[nxt, new_rows, MXU_COLS * jk + LANES * u:MXU_COLS * jk + LANES * (u + 1)] = _rotate(
                        res[:, LANES * u:LANES * (u + 1)], cos_ref[...], sa_ref[...], sb_ref[...])
            else:
                vbuf[nxt, new_rows, ksl] = _dot(
                    h_ref[0], w_ref[:, W_B + kvw + MXU_COLS * jk:W_B + kvw + MXU_COLS * (jk + 1)])
                yield

    lane = lax.broadcasted_iota(jnp.int32, (CHUNK, LANES), 1)
    low = lane < HEAD_DIM_B
    low1 = lax.broadcasted_iota(jnp.int32, (1, LANES), 1) < HEAD_DIM_B
    key_row = lax.broadcasted_iota(jnp.int32, (band, LANES), 0)

    def pair(c, j):
        r0 = CHUNK * c
        g = j // 2
        qp = q_s[cur, r0:r0 + CHUNK, LANES * j:LANES * (j + 1)]
        q2 = jnp.concatenate([jnp.where(low, qp, 0.0), jnp.where(low, 0.0, qp)], axis=0).astype(BF16)
        kb = kbuf[cur, r0:r0 + band, LANES * g:LANES * (g + 1)].astype(BF16)
        vb = vbuf[cur, r0:r0 + band, LANES * g:LANES * (g + 1)].astype(BF16)
        s = _dot_nt(kb, q2) * ATTN_SCALE
        yield
        if not has_cache:
            s = jnp.where(i_lag * tl + r0 - WINDOW + key_row >= 0, s, -jnp.inf)
        sk = jnp.where(low1, sink_ref[2 * j:2 * j + 1, :], sink_ref[2 * j + 1:2 * j + 2, :])
        m = jnp.maximum(jnp.max(s, axis=0, keepdims=True), sk)
        p = jnp.exp(s - m)
        den = jnp.sum(p, axis=0, keepdims=True) + jnp.exp(sk - m)
        o2 = _dot_tn((p * (1.0 / den)).astype(BF16), vb)
        yield
        o = jnp.where(low, o2[:CHUNK], o2[CHUNK:])
        y = o * _silu(g_s[cur, r0:r0 + CHUNK, LANES * j:LANES * (j + 1)])
        y_ref[0, r0:r0 + CHUNK, LANES * j:LANES * (j + 1)] = y.astype(BF16)

    for c in range(n_chunks):
        _round_robin([pair(c, j) for j in range(N_Q_B // 2)], project(c), heavy_round=1)

    @pl.when(i_proj == ni - 1)
    def _():
        ktail = kbuf[nxt, tl:tl + WINDOW, :]
        vtail = vbuf[nxt, tl:tl + WINDOW, :]
        low_w = lax.broadcasted_iota(jnp.int32, (WINDOW, LANES), 1) < HEAD_DIM_B
        for u in range(KV_W_B // LANES):
            knew_ref[0, :, LANES * u:LANES * (u + 1)] = jnp.where(
                low_w, ktail[:, 2 * LANES * u:2 * LANES * u + LANES],
                ktail[:, 2 * LANES * u + LANES:2 * LANES * (u + 1)])
            vnew_ref[0, :, LANES * u:LANES * (u + 1)] = jnp.where(
                low_w, vtail[:, 2 * LANES * u:2 * LANES * u + LANES],
                vtail[:, 2 * LANES * u + LANES:2 * LANES * (u + 1)])


def _swa_call(h, x_norm, w_b, cos, sin_a, sin_b, sinks, kc, vc, tl):
    from_x = h is None
    bsz, seq, _ = x_norm[0].shape if from_x else h.shape
    has_cache = kc is not None
    ni = seq // tl
    n_blocks = bsz * ni
    x_map, y_map, pos_map, _ = _lagged_maps(n_blocks, ni)
    proj_batch_map = lambda n: (jnp.minimum(n, n_blocks - 1) // ni, 0, 0)
    const2 = lambda n: (0, 0)
    wcols = w_b.shape[1]
    act_spec = pl.BlockSpec((1, tl, D_MODEL), x_map)
    in_specs = [act_spec, pl.BlockSpec((1, D_MODEL), const2)] if from_x else [act_spec]
    args = list(x_norm) if from_x else [h]
    in_specs += [
        pl.BlockSpec((D_MODEL, wcols), const2, pipeline_mode=pl.Buffered(1)),
        pl.BlockSpec((tl, LANES), pos_map),
        pl.BlockSpec((tl, LANES), pos_map),
        pl.BlockSpec((tl, LANES), pos_map),
        pl.BlockSpec((N_Q_B, LANES), const2),
    ]
    args += [w_b, cos, sin_a, sin_b, sinks]
    if has_cache:
        in_specs += [pl.BlockSpec((1, WINDOW, 2 * KV_W_B), proj_batch_map)] * 2
        args += [kc, vc]
    out_specs = [
        pl.BlockSpec((1, tl, W_B), y_map),
        pl.BlockSpec((1, WINDOW, KV_W_B), proj_batch_map),
        pl.BlockSpec((1, WINDOW, KV_W_B), proj_batch_map),
    ]
    out_shape = [
        jax.ShapeDtypeStruct((bsz, seq, W_B), BF16),
        jax.ShapeDtypeStruct((bsz, WINDOW, KV_W_B), F32),
        jax.ShapeDtypeStruct((bsz, WINDOW, KV_W_B), F32),
    ]
    if from_x:
        out_specs.append(act_spec)
        out_shape.append(jax.ShapeDtypeStruct((bsz, seq, D_MODEL), BF16))
    return pl.pallas_call(
        functools.partial(_swa_kernel, tl=tl, ni=ni, n_blocks=n_blocks, has_cache=has_cache, from_x=from_x),
        grid=(n_blocks + 1,),
        in_specs=in_specs,
        out_specs=out_specs,
        out_shape=out_shape,
        scratch_shapes=[
            pltpu.VMEM((2, tl, W_B), F32),
            pltpu.VMEM((2, tl, W_B), F32),
            pltpu.VMEM((2, WINDOW + tl, 2 * KV_W_B), F32),
            pltpu.VMEM((2, WINDOW + tl, 2 * KV_W_B), F32),
        ],
        compiler_params=pltpu.CompilerParams(
            dimension_semantics=("arbitrary",), vmem_limit_bytes=VMEM_LIMIT),
        name="swa_mixer",
    )(*args)


CONV_PAD = 8


def _ssd_kernel(h_ref, w_ref, cw_ref, cb_ref, dtb_ref, alog_ref, dsk_ref, sn_ref, conv0_ref, s0_ref,
                y_ref, snew_ref, convnew_ref,
                ubuf, z_s, xc_s, xdt_s, bc_s, la_s, st_s, *, tl, ni, n_blocks):
    n = pl.program_id(0)
    n_chunks = tl // CHUNK
    gw = N_GROUPS_C * D_STATE
    hp = W_C // N_GROUPS_C
    i_proj = jnp.minimum(n, n_blocks - 1) % ni
    i_lag = jnp.maximum(n - 1, 0) % ni
    nxt = n % 2
    cur = 1 - nxt
    tail_rows = slice(CONV_PAD - (CONV_W - 1), CONV_PAD)

    @pl.when(n == 0)
    def _():
        for buf in (z_s, xc_s, xdt_s, bc_s, la_s):
            buf[...] = jnp.zeros_like(buf)

    @pl.when(i_proj == 0)
    def _():
        ubuf[tail_rows, :] = conv0_ref[0]

    @pl.when(i_lag == 0)
    def _():
        for g in range(N_GROUPS_C):
            st_s[g] = s0_ref[0, hp * g:hp * (g + 1), :].T

    def conv_act(sl):
        win = ubuf[0:CONV_PAD + tl, sl]
        acc = cb_ref[:, sl]
        for t in range(CONV_W):
            back = CONV_W - 1 - t
            src = pltpu.roll(win, back, 0) if back else win
            acc = acc + src[CONV_PAD:CONV_PAD + tl] * cw_ref[t:t + 1, sl]
        ubuf[tail_rows, sl] = ubuf[tl + CONV_PAD - (CONV_W - 1):tl + CONV_PAD, sl]
        return _silu(acc)

    def project(c):
        per = (W_C // MXU_COLS) // n_chunks
        for j in range(c * per, (c + 1) * per):
            sl = slice(MXU_COLS * j, MXU_COLS * (j + 1))
            z_s[nxt, :, sl] = _dot(h_ref[0], w_ref[:, sl])
            yield
            dt_raw = _dot(h_ref[0], w_ref[:, W_C + CONV_DIM + MXU_COLS * j:W_C + CONV_DIM + MXU_COLS * (j + 1)])
            yield
            dt = _softplus0(dt_raw + dtb_ref[:, sl])
            la_s[nxt, :, sl] = -dt * jnp.exp(alog_ref[:, sl])
            ubuf[CONV_PAD:CONV_PAD + tl, sl] = _dot(h_ref[0], w_ref[:, W_C + MXU_COLS * j:W_C + MXU_COLS * (j + 1)])
            yield
            for u in range(MXU_COLS // LANES):
                usl = slice(MXU_COLS * j + LANES * u, MXU_COLS * j + LANES * (u + 1))
                act = conv_act(usl)
                xc_s[nxt, :, usl] = act
                xdt_s[nxt, :, usl] = act * dt[:, LANES * u:LANES * (u + 1)]
            bsl = slice(W_C + MXU_COLS * j, W_C + MXU_COLS * (j + 1))
            ubuf[CONV_PAD:CONV_PAD + tl, bsl] = _dot(h_ref[0], w_ref[:, W_C + W_C + MXU_COLS * j:W_C + W_C + MXU_COLS * (j + 1)])
            yield
            for u in range(MXU_COLS // LANES):
                usl = slice(W_C + MXU_COLS * j + LANES * u, W_C + MXU_COLS * j + LANES * (u + 1))
                bc_s[nxt, :, MXU_COLS * j + LANES * u:MXU_COLS * j + LANES * (u + 1)] = conv_act(usl)

    tri = _tri_bf16(CHUNK)
    trow = lax.broadcasted_iota(jnp.int32, (CHUNK, LANES), 0)
    lane = lax.broadcasted_iota(jnp.int32, (CHUNK, LANES), 1)
    s_of_lane = lane % CHUNK
    causal = trow >= s_of_lane
    row2 = lax.broadcasted_iota(jnp.int32, (2 * CHUNK, LANES), 0)
    lane2 = lax.broadcasted_iota(jnp.int32, (2 * CHUNK, LANES), 1)
    blockdiag = (row2 < CHUNK) == (lane2 < HEAD_DIM_C)

    for c in range(n_chunks):
        rows = slice(CHUNK * c, CHUNK * (c + 1))

        def group(g, rows=rows):
            gsl = slice(hp * g, hp * (g + 1))
            bg = bc_s[cur, rows, D_STATE * g:D_STATE * (g + 1)].astype(BF16)
            cg = bc_s[cur, rows, gw + D_STATE * g:gw + D_STATE * (g + 1)].astype(BF16)
            cb2 = _dot_nt(cg, jnp.concatenate([bg, bg], axis=0))
            yield
            st = st_s[g]
            y_state = _dot(cg, st.astype(BF16))
            yield
            la = la_s[cur, rows, gsl]
            b = _cumsum_rows(tri, la)
            yield
            xdt = xdt_s[cur, rows, gsl]
            ys = []
            for u in range(2):
                usl = slice(LANES * u, LANES * (u + 1))
                b_row = jnp.sum(jnp.where(trow <= s_of_lane, la[:, usl], 0.0), axis=0, keepdims=True)
                decay = jnp.where(causal, jnp.exp(b[:, usl] - b_row), 0.0)
                wmat = (cb2 * decay).astype(BF16)
                xbd = jnp.where(blockdiag, jnp.concatenate([xdt[:, usl], xdt[:, usl]], axis=0), 0.0).astype(BF16)
                ys.append(_dot(wmat, xbd))
                yield
            y = jnp.concatenate(ys, axis=1) + y_state * jnp.exp(b)
            o = (y + dsk_ref[:, gsl] * xc_s[cur, rows, gsl]) * _silu(z_s[cur, rows, gsl])
            r = lax.rsqrt(jnp.mean(o * o, axis=-1, keepdims=True) + EPS)
            out = ((o * r) * sn_ref[:, gsl]).astype(BF16)
            b_last = b[CHUNK - 1:CHUNK, :]
            st_new = st * jnp.exp(b_last) + _dot_tn(bg, (xdt * jnp.exp(b_last - b)).astype(BF16))
            return out, st_new

        results = _round_robin([group(g) for g in range(N_GROUPS_C)] + [project(c)])[:N_GROUPS_C]
        for g, (out, st_new) in enumerate(results):
            y_ref[0, rows, hp * g:hp * (g + 1)] = out
            st_s[g] = st_new

    @pl.when(i_proj == ni - 1)
    def _():
        convnew_ref[0] = ubuf[tail_rows, :]

    @pl.when(i_lag == ni - 1)
    def _():
        for g in range(N_GROUPS_C):
            snew_ref[0, hp * g:hp * (g + 1), :] = st_s[g].T


def _ssd_call(h, w_c, cw, cb, dtb, alog, dsk, sn, conv0, s0, tl):
    bsz, seq, _ = h.shape
    ni = seq // tl
    n_blocks = bsz * ni
    x_map, y_map, _, batch_map = _lagged_maps(n_blocks, ni)
    proj_batch_map = lambda n: (jnp.minimum(n, n_blocks - 1) // ni, 0, 0)
    const2 = lambda n: (0, 0)
    wcols = w_c.shape[1]
    return pl.pallas_call(
        functools.partial(_ssd_kernel, tl=tl, ni=ni, n_blocks=n_blocks),
        grid=(n_blocks + 1,),
        in_specs=[
            pl.BlockSpec((1, tl, D_MODEL), x_map),
            pl.BlockSpec((D_MODEL, wcols), const2, pipeline_mode=pl.Buffered(1)),
            pl.BlockSpec((CONV_W, CONV_DIM), const2),
            pl.BlockSpec((1, CONV_DIM), const2),
            pl.BlockSpec((1, W_C), const2),
            pl.BlockSpec((1, W_C), const2),
            pl.BlockSpec((1, W_C), const2),
            pl.BlockSpec((1, W_C), const2),
            pl.BlockSpec((1, CONV_W - 1, CONV_DIM), proj_batch_map),
            pl.BlockSpec((1, W_C, D_STATE), batch_map(3)),
        ],
        out_specs=[
            pl.BlockSpec((1, tl, W_C), y_map),
            pl.BlockSpec((1, W_C, D_STATE), batch_map(3)),
            pl.BlockSpec((1, CONV_W - 1, CONV_DIM), proj_batch_map),
        ],
        out_shape=[
            jax.ShapeDtypeStruct((bsz, seq, W_C), BF16),
            jax.ShapeDtypeStruct((bsz, W_C, D_STATE), F32),
            jax.ShapeDtypeStruct((bsz, CONV_W - 1, CONV_DIM), F32),
        ],
        scratch_shapes=[
            pltpu.VMEM((CONV_PAD + tl, CONV_DIM), F32),
            pltpu.VMEM((2, tl, W_C), F32),
            pltpu.VMEM((2, tl, W_C), F32),
            pltpu.VMEM((2, tl, W_C), F32),
            pltpu.VMEM((2, tl, 2 * N_GROUPS_C * D_STATE), F32),
            pltpu.VMEM((2, tl, W_C), F32),
            pltpu.VMEM((N_GROUPS_C, D_STATE, W_C // N_GROUPS_C), F32),
        ],
        compiler_params=pltpu.CompilerParams(
            dimension_semantics=("arbitrary",), vmem_limit_bytes=VMEM_LIMIT),
        name="ssd_mixer",
    )(h, w_c, cw, cb, dtb, alog, dsk, sn, conv0, s0)


def _merge_kernel(*refs, emit_next):
    if emit_next:
        (x_ref, h_ref, ya_ref, yb_ref, yc_ref, npost_ref, nnext_ref, wg_ref, wbr_ref, wout_ref,
         o_ref, hnext_ref, m_s) = refs
    else:
        (x_ref, h_ref, ya_ref, yb_ref, yc_ref, npost_ref, wg_ref, wbr_ref, wout_ref, o_ref, m_s) = refs
    j = pl.program_id(1)
    n_j = m_s.shape[0]

    h = h_ref[...]
    merged = None
    for bi, y_ref in enumerate((ya_ref, yb_ref, yc_ref)):
        term = _sigmoid(_dot(h, wg_ref[bi])) * _dot(y_ref[...], wbr_ref[bi])
        merged = term if merged is None else merged + term
    m_s[j] = merged.astype(BF16)

    @pl.when(j == n_j - 1)
    def _():
        out = _dot(jnp.concatenate([m_s[n] for n in range(n_j)], axis=1), wout_ref[...])
        r = lax.rsqrt(jnp.mean(out * out, axis=-1, keepdims=True) + EPS)
        x_new = x_ref[...] + (out * r) * npost_ref[...]
        o_ref[...] = x_new
        if emit_next:
            hnext_ref[...] = _prenorm_bf16(x_new, nnext_ref[...])


def _merge_call(x2, h2, ya, yb, yc, npost, nnext, wg, wbr, wout, tm, tn):
    rows = x2.shape[0]
    emit_next = nnext is not None
    grid = (rows // tm, D_MODEL // tn)
    const2 = lambda r, j: (0, 0)
    row_block = lambda width: pl.BlockSpec((tm, width), lambda r, j: (r, 0))
    vec = pl.BlockSpec((1, D_MODEL), const2)
    in_specs = [row_block(D_MODEL), row_block(D_MODEL), row_block(W_A), row_block(W_B), row_block(W_C), vec]
    args = [x2, h2, ya, yb, yc, npost]
    if emit_next:
        in_specs.append(vec)
        args.append(nnext)
    in_specs += [
        pl.BlockSpec((3, D_MODEL, tn), lambda r, j: (0, 0, j)),
        pl.BlockSpec((3, W_A, tn), lambda r, j: (0, 0, j)),
        pl.BlockSpec((D_MODEL, D_MODEL), const2, pipeline_mode=pl.Buffered(1)),
    ]
    args += [wg, wbr, wout]
    out_specs = [row_block(D_MODEL)]
    out_shape = [jax.ShapeDtypeStruct((rows, D_MODEL), F32)]
    if emit_next:
        out_specs.append(row_block(D_MODEL))
        out_shape.append(jax.ShapeDtypeStruct((rows, D_MODEL), BF16))
    outs = pl.pallas_call(
        functools.partial(_merge_kernel, emit_next=emit_next),
        grid=grid,
        in_specs=in_specs,
        out_specs=out_specs,
        out_shape=out_shape,
        scratch_shapes=[pltpu.VMEM((D_MODEL // tn, tm, tn), BF16)],
        compiler_params=pltpu.CompilerParams(
            dimension_semantics=("arbitrary", "arbitrary"), vmem_limit_bytes=VMEM_LIMIT),
        name="merge_out",
    )(*args)
    return (outs[0], outs[1]) if emit_next else (outs[0], None)


def _rope_tables(pos):
    half = ROT_DIM // 2
    inv = jnp.power(ROPE_THETA, -jnp.arange(half, dtype=F32) / half)
    ang = pos.astype(F32)[:, None] * inv[None, :]
    cos, sin = jnp.cos(ang), jnp.sin(ang)
    n = pos.shape[0]
    ones = jnp.ones((n, HEAD_DIM_B - ROT_DIM), F32)
    zeros = jnp.zeros((n, HEAD_DIM_B - ROT_DIM), F32)
    zh = jnp.zeros((n, half), F32)
    cos_t = jnp.concatenate([cos, cos, ones], axis=1)
    sin_a = jnp.concatenate([-sin, zh, zeros], axis=1)
    sin_b = jnp.concatenate([zh, sin, zeros], axis=1)
    tile = lambda t: jnp.concatenate([t, t], axis=1)
    return tile(cos_t), tile(sin_a), tile(sin_b)


def _dup_heads(t):
    lead = t.shape[:-1]
    t4 = t.reshape(lead + (N_KV_B, 1, HEAD_DIM_B))
    return jnp.broadcast_to(t4, lead + (N_KV_B, 2, HEAD_DIM_B)).reshape(lead + (2 * KV_W_B,))


def _col_blocks(w):
    return w.reshape(w.shape[0], -1, MXU_COLS).transpose(1, 0, 2)


def _expand_heads(t):
    return jnp.repeat(t, HEAD_DIM_C, axis=-1)


def _layer(x, h, tl, tables, caches, lw, norm_next):
    bsz, seq, _ = x.shape
    cos, sin_a, sin_b = tables
    kc, vc, s_hgrn, s_ssm, s_conv = caches
    if s_hgrn is None:
        s_hgrn = jnp.zeros((bsz, N_HEADS_A, HEAD_K_A, HEAD_K_A), F32)
        s_ssm = jnp.zeros((bsz, W_C, D_STATE), F32)
        s_conv = jnp.zeros((bsz, CONV_W - 1, CONV_DIM), F32)
    else:
        s_ssm = s_ssm.reshape(bsz, W_C, D_STATE)
        kc = _dup_heads(kc.reshape(bsz, WINDOW, KV_W_B))
        vc = _dup_heads(vc.reshape(bsz, WINDOW, KV_W_B))

    if h is None:
        yb, k_new, v_new, h = _swa_call(None, (x, lw["norm_pre"]), lw["w_b"], cos, sin_a, sin_b, lw["sinks"],
                                        kc, vc, tl)
    else:
        yb, k_new, v_new = _swa_call(h, None, lw["w_b"], cos, sin_a, sin_b, lw["sinks"], kc, vc, tl)
    ya, hgrn_new = _hgrn_call(h, lw["w_a"], lw["lb_pack"], lw["hgrn_norm"], s_hgrn, tl)
    yc, ssm_new, conv_new = _ssd_call(h, lw["w_c"], lw["conv_w"], lw["conv_b"], lw["dt_bias"],
                                      lw["a_log"], lw["d_skip"], lw["ssm_norm"], s_conv, s_ssm, tl)
    rows = bsz * seq
    x_new, h_next = _merge_call(x.reshape(rows, D_MODEL), h.reshape(rows, D_MODEL), ya.reshape(rows, W_A),
                                yb.reshape(rows, W_B), yc.reshape(rows, W_C), lw["norm_post"], norm_next,
                                lw["w_g"], lw["w_br"], lw["w_out"], min(MERGE_ROWS, rows), MXU_COLS)
    states = (k_new.reshape(bsz, WINDOW, N_KV_B, HEAD_DIM_B), v_new.reshape(bsz, WINDOW, N_KV_B, HEAD_DIM_B),
              hgrn_new, ssm_new.reshape(bsz, N_HEADS_C, HEAD_DIM_C, D_STATE), conv_new)
    if h_next is not None:
        h_next = h_next.reshape(bsz, seq, D_MODEL)
    return x_new.reshape(bsz, seq, D_MODEL), h_next, states


def kernel(x_prompt, x_sample, cache_swa_k, cache_swa_v, state_hgrn, state_ssm, state_conv, norm_pre, norm_post, w_in, hgrn_lb_logits, hgrn_norm, swa_sinks, conv_w, conv_b, dt_bias, a_log, d_skip, ssm_norm, w_branch_a, w_branch_b, w_branch_c, w_out):
    depth = w_in.shape[0]
    lbp = jax.nn.softmax(hgrn_lb_logits.astype(F32), axis=0)
    lbc = jnp.cumsum(lbp, axis=0)
    lb_all = lbc - lbc[0:1]

    offs = np.cumsum([0, W_A, W_A, W_A, W_A, W_B, KV_W_B, KV_W_B, W_B, W_C, CONV_DIM, N_HEADS_C,
                      D_MODEL, D_MODEL, D_MODEL])
    col = lambda l, a, b: w_in[l, :, int(offs[a]):int(offs[b])]

    tables_p = _rope_tables(jnp.arange(x_prompt.shape[1], dtype=jnp.int32))
    tables_s = _rope_tables(PAST_LEN + jnp.arange(x_sample.shape[1], dtype=jnp.int32))

    xp, xs = x_prompt, x_sample
    pre = lambda l: norm_pre[l].reshape(1, D_MODEL)
    hp = hs = None
    pst, sst = [], []
    for l in range(depth):
        lb = lb_all[l].reshape(N_HEADS_A, 1, HEAD_K_A)
        norm_next = pre(l + 1) if l + 1 < depth else None
        lw = {
            "norm_pre": pre(l),
            "norm_post": norm_post[l].reshape(1, D_MODEL),
            "w_a": _col_blocks(col(l, 0, 4).astype(BF16)),
            "w_b": jnp.concatenate([col(l, 4, 5), _dup_heads(col(l, 5, 6)), _dup_heads(col(l, 6, 7)),
                                    col(l, 7, 8)], axis=1).astype(BF16),
            "w_c": jnp.concatenate([col(l, 8, 9), col(l, 9, 10), _expand_heads(col(l, 10, 11))],
                                   axis=1).astype(BF16),
            "w_g": jnp.stack([col(l, 11, 12), col(l, 12, 13), col(l, 13, 14)]).astype(BF16),
            "w_br": jnp.stack([w_branch_a[l], w_branch_b[l], w_branch_c[l]]).astype(BF16),
            "w_out": w_out[l].astype(BF16),
            "lb_pack": jnp.stack([jnp.log(lb), jnp.log1p(-lb)]) * LOG2E,
            "hgrn_norm": hgrn_norm[l].reshape(1, HEAD_K_A),
            "sinks": jnp.broadcast_to(swa_sinks[l].astype(F32)[:, None], (N_Q_B, LANES)),
            "conv_w": conv_w[l],
            "conv_b": conv_b[l].reshape(1, CONV_DIM),
            "dt_bias": _expand_heads(dt_bias[l].astype(F32)).reshape(1, W_C),
            "a_log": _expand_heads(a_log[l].astype(F32)).reshape(1, W_C),
            "d_skip": _expand_heads(d_skip[l].astype(F32)).reshape(1, W_C),
            "ssm_norm": ssm_norm[l].reshape(1, W_C),
        }
        xp, hp, sp = _layer(xp, hp, PROMPT_ROWS, tables_p, (None, None, None, None, None), lw, norm_next)
        xs, hs, ss = _layer(xs, hs, CHUNK, tables_s,
                            (cache_swa_k[l], cache_swa_v[l], state_hgrn[l], state_ssm[l], state_conv[l]),
                            lw, norm_next)
        pst.append(sp)
        sst.append(ss)

    stack = lambda sts, k: jnp.stack([s[k] for s in sts])
    return (xp, xs,
            stack(pst, 0), stack(pst, 1), stack(pst, 2), stack(pst, 3), stack(pst, 4),
            stack(sst, 0), stack(sst, 1), stack(sst, 2), stack(sst, 3), stack(sst, 4))
```

```python
import functools
import math

import jax
import jax.numpy as jnp
import numpy as np
from jax import lax
from jax.experimental import pallas as pl
from jax.experimental.pallas import tpu as pltpu

F32 = jnp.float32
BF16 = jnp.bfloat16

D_MODEL = 2048
CHUNK = 64
EPS = 1e-6
PAST_LEN = 4096

W_A = 1024
HEAD_K_A = 128
N_HEADS_A = 8

N_Q_B = 16
N_KV_B = 4
HEAD_DIM_B = 64
W_B = 1024
KV_W_B = 256
WINDOW = 128
ROT_DIM = 16
ROPE_THETA = 500000.0
ATTN_SCALE = HEAD_DIM_B ** -0.5

W_C = 1024
HEAD_DIM_C = 64
N_HEADS_C = 16
N_GROUPS_C = 4
D_STATE = 128
CONV_W = 4
CONV_DIM = 2048

LANES = 128
SUBLANES = 8
SUB_BLOCK = 16
MXU_COLS = 256
PROMPT_ROWS = 256
MERGE_ROWS = 512
LOG2E = math.log2(math.e)
VMEM_LIMIT = 56 * 1024 * 1024

NT_DIMS = (((1,), (1,)), ((), ()))
TN_DIMS = (((0,), (0,)), ((), ()))


def _dot(a, b):
    return jnp.dot(a, b, preferred_element_type=F32)


def _dot_nt(a, b):
    return lax.dot_general(a, b, NT_DIMS, preferred_element_type=F32)


def _dot_tn(a, b):
    return lax.dot_general(a, b, TN_DIMS, preferred_element_type=F32)


def _prenorm_bf16(x, w):
    r = lax.rsqrt(jnp.mean(x * x, axis=-1, keepdims=True) + EPS)
    return ((x * r) * w).astype(BF16)


def _sigmoid(x):
    return 0.5 * jnp.tanh(0.5 * x) + 0.5


def _silu(x):
    return x * _sigmoid(x)


def _softplus0(x):
    return jnp.maximum(x, 0.0) + jnp.log(1.0 + jnp.exp(-jnp.abs(x)))


def _tri_bf16(n):
    r = lax.broadcasted_iota(jnp.int32, (n, n), 0)
    c = lax.broadcasted_iota(jnp.int32, (n, n), 1)
    return (r >= c).astype(BF16)


def _split3(x):
    hi = x.astype(BF16)
    r1 = x - hi.astype(F32)
    mid = r1.astype(BF16)
    lo = (r1 - mid.astype(F32)).astype(BF16)
    return hi, mid, lo


def _cumsum_rows(tri, x):
    hi, mid, lo = _split3(x)
    return _dot(tri, hi) + _dot(tri, mid) + _dot(tri, lo)


def _exact_dot(x, sel):
    hi, mid, lo = _split3(x)
    return _dot(hi, sel) + _dot(mid, sel) + _dot(lo, sel)


def _round_robin(gens, background=None, heavy_round=0):
    results = [None] * len(gens)
    live = list(range(len(gens)))
    rnd = 0
    while live:
        if background is not None and rnd == heavy_round:
            for _ in background:
                pass
        for n in list(live):
            try:
                next(gens[n])
            except StopIteration as stop:
                results[n] = stop.value
                live.remove(n)
        rnd += 1
    return results


def _hgrn_kernel(h_ref, w_ref, lb_ref, hn_ref, s0_ref, y_ref, snew_ref,
                 proj_s, st_s, *, tl, ni):
    n = pl.program_id(0)
    n_chunks = tl // CHUNK
    i_lag = jnp.maximum(n - 1, 0) % ni
    nxt = n % 2
    cur = 1 - nxt

    @pl.when(n == 0)
    def _():
        proj_s[...] = jnp.zeros_like(proj_s)

    @pl.when(i_lag == 0)
    def _():
        for hd in range(N_HEADS_A):
            st_s[hd] = s0_ref[0, hd].T

    n_col_blocks = w_ref.shape[0]
    cols_per_chunk = n_col_blocks // n_chunks

    def project(c):
        for j in range(cols_per_chunk):
            jb = c * cols_per_chunk + j
            res = _dot(h_ref[0], w_ref[jb])
            proj_s[nxt, 2 * jb] = res[:, :LANES]
            proj_s[nxt, 2 * jb + 1] = res[:, LANES:]
            yield

    tri = _tri_bf16(CHUNK)
    n_sub = CHUNK // SUB_BLOCK
    lane8 = lax.broadcasted_iota(jnp.int32, (SUBLANES, LANES), 1)
    sub8 = lax.broadcasted_iota(jnp.int32, (SUBLANES, LANES), 0)
    row64 = lax.broadcasted_iota(jnp.int32, (CHUNK, CHUNK), 0)
    col64 = lax.broadcasted_iota(jnp.int32, (CHUNK, CHUNK), 1)
    below_diag_block = row64 // SUB_BLOCK > col64 // SUB_BLOCK
    zero8 = jnp.zeros((SUBLANES, LANES), F32)

    def zeros(n):
        return jnp.zeros((n, LANES), F32)

    def piece(a, m, u):
        lo = SUB_BLOCK * m + SUBLANES * u
        return a[lo:lo + SUBLANES]

    def unit(hd, r0):
        rows = pl.ds(r0, CHUNK)
        aq = proj_s[cur, hd, rows, :]
        z = proj_s[cur, N_HEADS_A + hd, rows, :]
        v = proj_s[cur, 2 * N_HEADS_A + hd, rows, :].astype(BF16)
        ag = proj_s[cur, 3 * N_HEADS_A + hd, rows, :]
        log_lb = lb_ref[0, hd]
        log1m_lb = lb_ref[1, hd]

        z2 = z * LOG2E
        log_sig = jnp.minimum(z2, 0.0) - jnp.log2(1.0 + jnp.exp2(-jnp.abs(z2)))
        cterm = log1m_lb + log_sig
        log_f = jnp.maximum(log_lb, cterm) + jnp.log2(1.0 + jnp.exp2(-jnp.abs(log_lb - cterm)))
        log_k = cterm - z2
        q = _silu(aq)

        b2 = _cumsum_rows(tri, log_f)
        yield
        c2 = b2 - log_k
        b2_last = b2[CHUNK - 1:CHUNK, :]
        st = st_s[hd]

        o = _dot_nt((q * jnp.exp2(b2)).astype(BF16), st.astype(BF16))
        yield

        q_slabs, k_slabs = [], []
        for m in range(1, n_sub):
            lo = SUB_BLOCK * m
            ref = b2[lo - 1:lo, :]
            qm = q[lo:lo + SUB_BLOCK] * jnp.exp2(b2[lo:lo + SUB_BLOCK] - ref)
            q_parts = [zeros(lo), qm] + ([zeros(CHUNK - lo - SUB_BLOCK)] if lo + SUB_BLOCK < CHUNK else [])
            q_slabs.append(jnp.concatenate(q_parts, axis=0))
            k_slabs.append(jnp.concatenate([jnp.exp2(ref - c2[:lo]), zeros(CHUNK - lo)], axis=0))
        a_off = _dot_nt(jnp.concatenate(q_slabs, axis=1).astype(BF16),
                        jnp.concatenate(k_slabs, axis=1).astype(BF16))
        yield

        order = [(s, m, u) for s in range(SUB_BLOCK) for m in range(n_sub) for u in range(2)
                 if not (u == 0 and s >= SUBLANES)]
        pieces = []
        for s, m, u in order:
            cs = c2[SUB_BLOCK * m + s:SUB_BLOCK * m + s + 1]
            pieces.append(piece(q, m, u) * jnp.exp2(piece(b2, m, u) - cs))
        sums = [jnp.sum(p, axis=-1, keepdims=True) for p in pieces]
        yield
        d = {(m, u): zero8 for m in range(n_sub) for u in range(2)}
        for n, (s, m, u) in enumerate(order):
            d[(m, u)] = jnp.where(lane8 == s, sums[n], d[(m, u)])
        diag_rows = []
        for m in range(n_sub):
            for u in range(2):
                dm = jnp.where(sub8 + SUBLANES * u >= lane8, d[(m, u)], 0.0)
                diag_rows.append(pltpu.roll(dm, SUB_BLOCK * m, 1) if m else dm)
        a_diag = jnp.concatenate(diag_rows, axis=0)[:, :CHUNK]
        a_full = jnp.where(below_diag_block, a_off, a_diag)
        o = o + _dot(a_full.astype(BF16), v)
        yield

        r = lax.rsqrt(jnp.mean(o * o, axis=-1, keepdims=True) + EPS)
        y = ((o * r) * hn_ref[...]) * _silu(ag)

        st_new = st * jnp.exp2(b2_last) + _dot_tn(v, jnp.exp2(b2_last - c2).astype(BF16))
        return y.astype(BF16), st_new

    def body(c, carry):
        r0 = pl.multiple_of(c * CHUNK, CHUNK)
        gens = [unit(hd, r0) for hd in range(N_HEADS_A)] + [project(c)]
        results = _round_robin(gens)[:N_HEADS_A]
        for hd, (y, st_new) in enumerate(results):
            y_ref[0, pl.ds(r0, CHUNK), LANES * hd:LANES * (hd + 1)] = y
            st_s[hd] = st_new
        return carry

    lax.fori_loop(0, n_chunks, body, 0)

    @pl.when(i_lag == ni - 1)
    def _():
        for hd in range(N_HEADS_A):
            snew_ref[0, hd] = st_s[hd].T


def _lagged_maps(n_blocks, ni):
    def split(m):
        return m // ni, m % ni
    x_map = lambda n: split(jnp.minimum(n, n_blocks - 1)) + (0,)
    y_map = lambda n: split(jnp.maximum(n - 1, 0)) + (0,)
    pos_map = lambda n: (jnp.minimum(n, n_blocks - 1) % ni, 0)
    batch_map = lambda nd: (lambda n: (jnp.maximum(n - 1, 0) // ni,) + (0,) * (nd - 1))
    return x_map, y_map, pos_map, batch_map


def _hgrn_call(h, w_a, lb_pack, hn, s0, tl):
    bsz, seq, _ = h.shape
    ni = seq // tl
    n_blocks = bsz * ni
    x_map, y_map, _, batch_map = _lagged_maps(n_blocks, ni)
    const2 = lambda n: (0, 0)
    return pl.pallas_call(
        functools.partial(_hgrn_kernel, tl=tl, ni=ni),
        grid=(n_blocks + 1,),
        in_specs=[
            pl.BlockSpec((1, tl, D_MODEL), x_map),
            pl.BlockSpec(w_a.shape, lambda n: (0, 0, 0), pipeline_mode=pl.Buffered(1)),
            pl.BlockSpec((2, N_HEADS_A, 1, HEAD_K_A), lambda n: (0, 0, 0, 0)),
            pl.BlockSpec((1, HEAD_K_A), const2),
            pl.BlockSpec((1, N_HEADS_A, HEAD_K_A, HEAD_K_A), batch_map(4)),
        ],
        out_specs=[
            pl.BlockSpec((1, tl, W_A), y_map),
            pl.BlockSpec((1, N_HEADS_A, HEAD_K_A, HEAD_K_A), batch_map(4)),
        ],
        out_shape=[
            jax.ShapeDtypeStruct((bsz, seq, W_A), BF16),
            jax.ShapeDtypeStruct((bsz, N_HEADS_A, HEAD_K_A, HEAD_K_A), F32),
        ],
        scratch_shapes=[
            pltpu.VMEM((2, 4 * N_HEADS_A, tl, LANES), F32),
            pltpu.VMEM((N_HEADS_A, HEAD_K_A, HEAD_K_A), F32),
        ],
        compiler_params=pltpu.CompilerParams(
            dimension_semantics=("arbitrary",), vmem_limit_bytes=VMEM_LIMIT),
        name="hgrn_mixer",
    )(h, w_a, lb_pack, hn, s0)


def _rotate(xs, cos, sin_a, sin_b):
    return xs * cos + pltpu.roll(xs, LANES - ROT_DIM // 2, 1) * sin_a + pltpu.roll(xs, ROT_DIM // 2, 1) * sin_b


def _swa_kernel(*refs, tl, ni, n_blocks, has_cache, from_x):
    it = iter(refs)
    if from_x:
        x_ref, nw_ref = next(it), next(it)
    else:
        h_ref = next(it)
    w_ref, cos_ref, sa_ref, sb_ref, sink_ref = (next(it) for _ in range(5))
    if has_cache:
        kc_ref, vc_ref = next(it), next(it)
    y_ref, knew_ref, vnew_ref = (next(it) for _ in range(3))
    if from_x:
        h_ref = next(it)
    q_s, g_s, kbuf, vbuf = (next(it) for _ in range(4))
    n = pl.program_id(0)
    n_chunks = tl // CHUNK
    band = WINDOW + CHUNK
    kvw = 2 * KV_W_B
    i_proj = jnp.minimum(n, n_blocks - 1) % ni
    i_lag = jnp.maximum(n - 1, 0) % ni
    nxt = n % 2
    cur = 1 - nxt

    @pl.when(n == 0)
    def _():
        for buf in (q_s, g_s, kbuf, vbuf):
            buf[...] = jnp.zeros_like(buf)

    @pl.when(i_proj == 0)
    def _():
        if has_cache:
            kbuf[nxt, 0:WINDOW, :] = kc_ref[0]
            vbuf[nxt, 0:WINDOW, :] = vc_ref[0]
        else:
            kbuf[nxt, 0:WINDOW, :] = jnp.zeros((WINDOW, kvw), F32)
            vbuf[nxt, 0:WINDOW, :] = jnp.zeros((WINDOW, kvw), F32)

    @pl.when(i_proj != 0)
    def _():
        kbuf[nxt, 0:WINDOW, :] = kbuf[cur, tl:tl + WINDOW, :]
        vbuf[nxt, 0:WINDOW, :] = vbuf[cur, tl:tl + WINDOW, :]

    if from_x:
        h_ref[0] = _prenorm_bf16(x_ref[0], nw_ref[...])

    def project(c):
        per = (W_B // MXU_COLS) // n_chunks
        new_rows = slice(WINDOW, WINDOW + tl)
        for j in range(c * per, (c + 1) * per):
            sl = slice(MXU_COLS * j, MXU_COLS * (j + 1))
            res = _dot(h_ref[0], w_ref[:, sl])
            yield
            for u in range(MXU_COLS // LANES):
                q_s[nxt, :, MXU_COLS * j + LANES * u:MXU_COLS * j + LANES * (u + 1)] = _rotate(
                    res[:, LANES * u:LANES * (u + 1)], cos_ref[...], sa_ref[...], sb_ref[...])
            g_s[nxt, :, sl] = _dot(h_ref[0], w_ref[:, W_B + 2 * KV_W_B + MXU_COLS * j:W_B + 2 * KV_W_B + MXU_COLS * (j + 1)])
            yield
            if j < 2:
                res = _dot(h_ref[0], w_ref[:, W_B + KV_W_B * j:W_B + KV_W_B * (j + 1)])
                yield
                for u in range(KV_W_B // LANES):
                    a = res[:, LANES * u:LANES * (u + 1)]
                    if j == 0:
                        a = _rotate(a, cos_ref[...], sa_ref[...], sb_ref[...])
                    swapped = pltpu.roll(a, HEAD_DIM_B, 1)
                    buf = kbuf if j == 0 else vbuf
                    buf[nxt, new_rows, LANES * 2 * u:LANES * (2 * u + 1)] = jnp.where(low1, a, swapped)
                    buf[nxt, new_rows, LANES * (2 * u + 1):LANES * (2 * u + 2)] = jnp.where(low1, swapped, a)

    lane = lax.broadcasted_iota(jnp.int32, (CHUNK, LANES), 1)
    low = lane < HEAD_DIM_B
    low1 = lax.broadcasted_iota(jnp.int32, (1, LANES), 1) < HEAD_DIM_B
    key_row = lax.broadcasted_iota(jnp.int32, (band, LANES), 0)

    def pair(c, j):
        r0 = CHUNK * c
        g = j // 2
        qp = q_s[cur, r0:r0 + CHUNK, LANES * j:LANES * (j + 1)]
        q2 = jnp.concatenate([jnp.where(low, qp, 0.0), jnp.where(low, 0.0, qp)], axis=0).astype(BF16)
        kb = kbuf[cur, r0:r0 + band, LANES * g:LANES * (g + 1)].astype(BF16)
        vb = vbuf[cur, r0:r0 + band, LANES * g:LANES * (g + 1)].astype(BF16)
        s = _dot_nt(kb, q2) * ATTN_SCALE
        yield
        if not has_cache:
            s = jnp.where(i_lag * tl + r0 - WINDOW + key_row >= 0, s, -jnp.inf)
        sk = jnp.where(low1, sink_ref[2 * j:2 * j + 1, :], sink_ref[2 * j + 1:2 * j + 2, :])
        m = jnp.maximum(jnp.max(s, axis=0, keepdims=True), sk)
        p = jnp.exp(s - m)
        den = jnp.sum(p, axis=0, keepdims=True) + jnp.exp(sk - m)
        o2 = _dot_tn((p * (1.0 / den)).astype(BF16), vb)
        yield
        o = jnp.where(low, o2[:CHUNK], o2[CHUNK:])
        y = o * _silu(g_s[cur, r0:r0 + CHUNK, LANES * j:LANES * (j + 1)])
        y_ref[0, r0:r0 + CHUNK, LANES * j:LANES * (j + 1)] = y.astype(BF16)

    for c in range(n_chunks):
        _round_robin([pair(c, j) for j in range(N_Q_B // 2)], project(c), heavy_round=1)

    @pl.when(i_proj == ni - 1)
    def _():
        ktail = kbuf[nxt, tl:tl + WINDOW, :]
        vtail = vbuf[nxt, tl:tl + WINDOW, :]
        low_w = lax.broadcasted_iota(jnp.int32, (WINDOW, LANES), 1) < HEAD_DIM_B
        for u in range(KV_W_B // LANES):
            knew_ref[0, :, LANES * u:LANES * (u + 1)] = jnp.where(
                low_w, ktail[:, 2 * LANES * u:2 * LANES * u + LANES],
                ktail[:, 2 * LANES * u + LANES:2 * LANES * (u + 1)])
            vnew_ref[0, :, LANES * u:LANES * (u + 1)] = jnp.where(
                low_w, vtail[:, 2 * LANES * u:2 * LANES * u + LANES],
                vtail[:, 2 * LANES * u + LANES:2 * LANES * (u + 1)])


def _swa_call(h, x_norm, w_b, cos, sin_a, sin_b, sinks, kc, vc, tl):
    from_x = h is None
    bsz, seq, _ = x_norm[0].shape if from_x else h.shape
    has_cache = kc is not None
    ni = seq // tl
    n_blocks = bsz * ni
    x_map, y_map, pos_map, _ = _lagged_maps(n_blocks, ni)
    proj_batch_map = lambda n: (jnp.minimum(n, n_blocks - 1) // ni, 0, 0)
    const2 = lambda n: (0, 0)
    wcols = w_b.shape[1]
    act_spec = pl.BlockSpec((1, tl, D_MODEL), x_map)
    in_specs = [act_spec, pl.BlockSpec((1, D_MODEL), const2)] if from_x else [act_spec]
    args = list(x_norm) if from_x else [h]
    in_specs += [
        pl.BlockSpec((D_MODEL, wcols), const2, pipeline_mode=pl.Buffered(1)),
        pl.BlockSpec((tl, LANES), pos_map),
        pl.BlockSpec((tl, LANES), pos_map),
        pl.BlockSpec((tl, LANES), pos_map),
        pl.BlockSpec((N_Q_B, LANES), const2),
    ]
    args += [w_b, cos, sin_a, sin_b, sinks]
    if has_cache:
        in_specs += [pl.BlockSpec((1, WINDOW, 2 * KV_W_B), proj_batch_map)] * 2
        args += [kc, vc]
    out_specs = [
        pl.BlockSpec((1, tl, W_B), y_map),
        pl.BlockSpec((1, WINDOW, KV_W_B), proj_batch_map),
        pl.BlockSpec((1, WINDOW, KV_W_B), proj_batch_map),
    ]
    out_shape = [
        jax.ShapeDtypeStruct((bsz, seq, W_B), BF16),
        jax.ShapeDtypeStruct((bsz, WINDOW, KV_W_B), F32),
        jax.ShapeDtypeStruct((bsz, WINDOW, KV_W_B), F32),
    ]
    if from_x:
        out_specs.append(act_spec)
        out_shape.append(jax.ShapeDtypeStruct((bsz, seq, D_MODEL), BF16))
    return pl.pallas_call(
        functools.partial(_swa_kernel, tl=tl, ni=ni, n_blocks=n_blocks, has_cache=has_cache, from_x=from_x),
        grid=(n_blocks + 1,),
        in_specs=in_specs,
        out_specs=out_specs,
        out_shape=out_shape,
        scratch_shapes=[
            pltpu.VMEM((2, tl, W_B), F32),
            pltpu.VMEM((2, tl, W_B), F32),
            pltpu.VMEM((2, WINDOW + tl, 2 * KV_W_B), F32),
            pltpu.VMEM((2, WINDOW + tl, 2 * KV_W_B), F32),
        ],
        compiler_params=pltpu.CompilerParams(
            dimension_semantics=("arbitrary",), vmem_limit_bytes=VMEM_LIMIT),
        name="swa_mixer",
    )(*args)


CONV_PAD = 8


def _ssd_kernel(h_ref, w_ref, cw_ref, cb_ref, dtb_ref, alog_ref, dsk_ref, sn_ref, conv0_ref, s0_ref,
                y_ref, snew_ref, convnew_ref,
                ubuf, dt_s, z_s, xc_s, xdt_s, bc_s, la_s, st_s, *, tl, ni, n_blocks):
    n = pl.program_id(0)
    n_chunks = tl // CHUNK
    gw = N_GROUPS_C * D_STATE
    hp = W_C // N_GROUPS_C
    i_proj = jnp.minimum(n, n_blocks - 1) % ni
    i_lag = jnp.maximum(n - 1, 0) % ni
    nxt = n % 2
    cur = 1 - nxt
    tail_rows = slice(CONV_PAD - (CONV_W - 1), CONV_PAD)

    @pl.when(n == 0)
    def _():
        for buf in (z_s, xc_s, xdt_s, bc_s, la_s):
            buf[...] = jnp.zeros_like(buf)

    @pl.when(i_proj == 0)
    def _():
        ubuf[tail_rows, :] = conv0_ref[0]

    @pl.when(i_lag == 0)
    def _():
        for g in range(N_GROUPS_C):
            st_s[g] = s0_ref[0, hp * g:hp * (g + 1), :].T

    def conv_act(sl):
        win = ubuf[0:CONV_PAD + tl, sl]
        acc = cb_ref[:, sl]
        for t in range(CONV_W):
            back = CONV_W - 1 - t
            src = pltpu.roll(win, back, 0) if back else win
            acc = acc + src[CONV_PAD:CONV_PAD + tl] * cw_ref[t:t + 1, sl]
        ubuf[tail_rows, sl] = ubuf[tl + CONV_PAD - (CONV_W - 1):tl + CONV_PAD, sl]
        return _silu(acc)

    def project(c):
        per = (W_C // MXU_COLS) // n_chunks
        for j in range(c * per, (c + 1) * per):
            sl = slice(MXU_COLS * j, MXU_COLS * (j + 1))
            z_s[nxt, :, sl] = _dot(h_ref[0], w_ref[:, sl])
            yield
            if j == 0:
                dt_raw = _dot(h_ref[0], w_ref[:, W_C + CONV_DIM:W_C + CONV_DIM + LANES])
                yield
                dt_s[...] = _softplus0(dt_raw + dtb_ref[...])
            head_of_col = (MXU_COLS * j + lax.broadcasted_iota(jnp.int32, (LANES, MXU_COLS), 1)) // HEAD_DIM_C
            spread = (lax.broadcasted_iota(jnp.int32, (LANES, MXU_COLS), 0) == head_of_col).astype(BF16)
            dt = _exact_dot(dt_s[...], spread)
            yield
            la_s[nxt, :, sl] = -dt * jnp.exp(alog_ref[:, sl])
            ubuf[CONV_PAD:CONV_PAD + tl, sl] = _dot(h_ref[0], w_ref[:, W_C + MXU_COLS * j:W_C + MXU_COLS * (j + 1)])
            yield
            for u in range(MXU_COLS // LANES):
                usl = slice(MXU_COLS * j + LANES * u, MXU_COLS * j + LANES * (u + 1))
                act = conv_act(usl)
                xc_s[nxt, :, usl] = act
                xdt_s[nxt, :, usl] = act * dt[:, LANES * u:LANES * (u + 1)]
            bsl = slice(W_C + MXU_COLS * j, W_C + MXU_COLS * (j + 1))
            ubuf[CONV_PAD:CONV_PAD + tl, bsl] = _dot(h_ref[0], w_ref[:, W_C + W_C + MXU_COLS * j:W_C + W_C + MXU_COLS * (j + 1)])
            yield
            for u in range(MXU_COLS // LANES):
                usl = slice(W_C + MXU_COLS * j + LANES * u, W_C + MXU_COLS * j + LANES * (u + 1))
                bc_s[nxt, :, MXU_COLS * j + LANES * u:MXU_COLS * j + LANES * (u + 1)] = conv_act(usl)

    tri = _tri_bf16(CHUNK)
    trow = lax.broadcasted_iota(jnp.int32, (CHUNK, LANES), 0)
    lane = lax.broadcasted_iota(jnp.int32, (CHUNK, LANES), 1)
    s_of_lane = lane % CHUNK
    causal = trow >= s_of_lane
    row2 = lax.broadcasted_iota(jnp.int32, (2 * CHUNK, LANES), 0)
    lane2 = lax.broadcasted_iota(jnp.int32, (2 * CHUNK, LANES), 1)
    blockdiag = (row2 < CHUNK) == (lane2 < HEAD_DIM_C)

    for c in range(n_chunks):
        rows = slice(CHUNK * c, CHUNK * (c + 1))

        def group(g, rows=rows):
            gsl = slice(hp * g, hp * (g + 1))
            bg = bc_s[cur, rows, D_STATE * g:D_STATE * (g + 1)].astype(BF16)
            cg = bc_s[cur, rows, gw + D_STATE * g:gw + D_STATE * (g + 1)].astype(BF16)
            cb2 = _dot_nt(cg, jnp.concatenate([bg, bg], axis=0))
            yield
            st = st_s[g]
            y_state = _dot(cg, st.astype(BF16))
            yield
            la = la_s[cur, rows, gsl]
            b = _cumsum_rows(tri, la)
            yield
            xdt = xdt_s[cur, rows, gsl]
            ys = []
            for u in range(2):
                usl = slice(LANES * u, LANES * (u + 1))
                b_row = jnp.sum(jnp.where(trow <= s_of_lane, la[:, usl], 0.0), axis=0, keepdims=True)
                decay = jnp.where(causal, jnp.exp(b[:, usl] - b_row), 0.0)
                wmat = (cb2 * decay).astype(BF16)
                xbd = jnp.where(blockdiag, jnp.concatenate([xdt[:, usl], xdt[:, usl]], axis=0), 0.0).astype(BF16)
                ys.append(_dot(wmat, xbd))
                yield
            y = jnp.concatenate(ys, axis=1) + y_state * jnp.exp(b)
            o = (y + dsk_ref[:, gsl] * xc_s[cur, rows, gsl]) * _silu(z_s[cur, rows, gsl])
            r = lax.rsqrt(jnp.mean(o * o, axis=-1, keepdims=True) + EPS)
            out = ((o * r) * sn_ref[:, gsl]).astype(BF16)
            b_last = b[CHUNK - 1:CHUNK, :]
            st_new = st * jnp.exp(b_last) + _dot_tn(bg, (xdt * jnp.exp(b_last - b)).astype(BF16))
            return out, st_new

        results = _round_robin([group(g) for g in range(N_GROUPS_C)] + [project(c)])[:N_GROUPS_C]
        for g, (out, st_new) in enumerate(results):
            y_ref[0, rows, hp * g:hp * (g + 1)] = out
            st_s[g] = st_new

    @pl.when(i_proj == ni - 1)
    def _():
        convnew_ref[0] = ubuf[tail_rows, :]

    @pl.when(i_lag == ni - 1)
    def _():
        for g in range(N_GROUPS_C):
            snew_ref[0, hp * g:hp * (g + 1), :] = st_s[g].T


def _ssd_call(h, w_c, cw, cb, dtb, alog, dsk, sn, conv0, s0, tl):
    bsz, seq, _ = h.shape
    ni = seq // tl
    n_blocks = bsz * ni
    x_map, y_map, _, batch_map = _lagged_maps(n_blocks, ni)
    proj_batch_map = lambda n: (jnp.minimum(n, n_blocks - 1) // ni, 0, 0)
    const2 = lambda n: (0, 0)
    wcols = w_c.shape[1]
    return pl.pallas_call(
        functools.partial(_ssd_kernel, tl=tl, ni=ni, n_blocks=n_blocks),
        grid=(n_blocks + 1,),
        in_specs=[
            pl.BlockSpec((1, tl, D_MODEL), x_map),
            pl.BlockSpec((D_MODEL, wcols), const2, pipeline_mode=pl.Buffered(1)),
            pl.BlockSpec((CONV_W, CONV_DIM), const2),
            pl.BlockSpec((1, CONV_DIM), const2),
            pl.BlockSpec((1, LANES), const2),
            pl.BlockSpec((1, W_C), const2),
            pl.BlockSpec((1, W_C), const2),
            pl.BlockSpec((1, W_C), const2),
            pl.BlockSpec((1, CONV_W - 1, CONV_DIM), proj_batch_map),
            pl.BlockSpec((1, W_C, D_STATE), batch_map(3)),
        ],
        out_specs=[
            pl.BlockSpec((1, tl, W_C), y_map),
            pl.BlockSpec((1, W_C, D_STATE), batch_map(3)),
            pl.BlockSpec((1, CONV_W - 1, CONV_DIM), proj_batch_map),
        ],
        out_shape=[
            jax.ShapeDtypeStruct((bsz, seq, W_C), BF16),
            jax.ShapeDtypeStruct((bsz, W_C, D_STATE), F32),
            jax.ShapeDtypeStruct((bsz, CONV_W - 1, CONV_DIM), F32),
        ],
        scratch_shapes=[
            pltpu.VMEM((CONV_PAD + tl, CONV_DIM), F32),
            pltpu.VMEM((tl, LANES), F32),
            pltpu.VMEM((2, tl, W_C), F32),
            pltpu.VMEM((2, tl, W_C), F32),
            pltpu.VMEM((2, tl, W_C), F32),
            pltpu.VMEM((2, tl, 2 * N_GROUPS_C * D_STATE), F32),
            pltpu.VMEM((2, tl, W_C), F32),
            pltpu.VMEM((N_GROUPS_C, D_STATE, W_C // N_GROUPS_C), F32),
        ],
        compiler_params=pltpu.CompilerParams(
            dimension_semantics=("arbitrary",), vmem_limit_bytes=VMEM_LIMIT),
        name="ssd_mixer",
    )(h, w_c, cw, cb, dtb, alog, dsk, sn, conv0, s0)


def _merge_kernel(*refs, emit_next):
    if emit_next:
        (x_ref, h_ref, ya_ref, yb_ref, yc_ref, npost_ref, nnext_ref, wg_ref, wbr_ref, wout_ref,
         o_ref, hnext_ref, m_s) = refs
    else:
        (x_ref, h_ref, ya_ref, yb_ref, yc_ref, npost_ref, wg_ref, wbr_ref, wout_ref, o_ref, m_s) = refs
    j = pl.program_id(1)
    n_j = m_s.shape[0]

    h = h_ref[...]
    merged = None
    for bi, y_ref in enumerate((ya_ref, yb_ref, yc_ref)):
        term = _sigmoid(_dot(h, wg_ref[bi])) * _dot(y_ref[...], wbr_ref[bi])
        merged = term if merged is None else merged + term
    m_s[j] = merged.astype(BF16)

    @pl.when(j == n_j - 1)
    def _():
        out = _dot(jnp.concatenate([m_s[n] for n in range(n_j)], axis=1), wout_ref[...])
        r = lax.rsqrt(jnp.mean(out * out, axis=-1, keepdims=True) + EPS)
        x_new = x_ref[...] + (out * r) * npost_ref[...]
        o_ref[...] = x_new
        if emit_next:
            hnext_ref[...] = _prenorm_bf16(x_new, nnext_ref[...])


def _merge_call(x2, h2, ya, yb, yc, npost, nnext, wg, wbr, wout, tm, tn):
    rows = x2.shape[0]
    emit_next = nnext is not None
    grid = (rows // tm, D_MODEL // tn)
    const2 = lambda r, j: (0, 0)
    row_block = lambda width: pl.BlockSpec((tm, width), lambda r, j: (r, 0))
    vec = pl.BlockSpec((1, D_MODEL), const2)
    in_specs = [row_block(D_MODEL), row_block(D_MODEL), row_block(W_A), row_block(W_B), row_block(W_C), vec]
    args = [x2, h2, ya, yb, yc, npost]
    if emit_next:
        in_specs.append(vec)
        args.append(nnext)
    in_specs += [
        pl.BlockSpec((3, D_MODEL, tn), lambda r, j: (0, 0, j)),
        pl.BlockSpec((3, W_A, tn), lambda r, j: (0, 0, j)),
        pl.BlockSpec((D_MODEL, D_MODEL), const2, pipeline_mode=pl.Buffered(1)),
    ]
    args += [wg, wbr, wout]
    out_specs = [row_block(D_MODEL)]
    out_shape = [jax.ShapeDtypeStruct((rows, D_MODEL), F32)]
    if emit_next:
        out_specs.append(row_block(D_MODEL))
        out_shape.append(jax.ShapeDtypeStruct((rows, D_MODEL), BF16))
    outs = pl.pallas_call(
        functools.partial(_merge_kernel, emit_next=emit_next),
        grid=grid,
        in_specs=in_specs,
        out_specs=out_specs,
        out_shape=out_shape,
        scratch_shapes=[pltpu.VMEM((D_MODEL // tn, tm, tn), BF16)],
        compiler_params=pltpu.CompilerParams(
            dimension_semantics=("arbitrary", "arbitrary"), vmem_limit_bytes=VMEM_LIMIT),
        name="merge_out",
    )(*args)
    return (outs[0], outs[1]) if emit_next else (outs[0], None)


def _rope_tables(pos):
    half = ROT_DIM // 2
    inv = jnp.power(ROPE_THETA, -jnp.arange(half, dtype=F32) / half)
    ang = pos.astype(F32)[:, None] * inv[None, :]
    cos, sin = jnp.cos(ang), jnp.sin(ang)
    n = pos.shape[0]
    ones = jnp.ones((n, HEAD_DIM_B - ROT_DIM), F32)
    zeros = jnp.zeros((n, HEAD_DIM_B - ROT_DIM), F32)
    zh = jnp.zeros((n, half), F32)
    cos_t = jnp.concatenate([cos, cos, ones], axis=1)
    sin_a = jnp.concatenate([-sin, zh, zeros], axis=1)
    sin_b = jnp.concatenate([zh, sin, zeros], axis=1)
    tile = lambda t: jnp.concatenate([t, t], axis=1)
    return tile(cos_t), tile(sin_a), tile(sin_b)


def _dup_heads(t):
    lead = t.shape[:-1]
    t4 = t.reshape(lead + (N_KV_B, 1, HEAD_DIM_B))
    return jnp.broadcast_to(t4, lead + (N_KV_B, 2, HEAD_DIM_B)).reshape(lead + (2 * KV_W_B,))


def _col_blocks(w):
    return w.reshape(w.shape[0], -1, MXU_COLS).transpose(1, 0, 2)


def _pad_lanes(t):
    return jnp.pad(t, [(0, 0)] * (t.ndim - 1) + [(0, LANES - t.shape[-1])])


def _expand_heads(t):
    return jnp.repeat(t, HEAD_DIM_C, axis=-1)


def _layer(x, h, tl, tables, caches, lw, norm_next):
    bsz, seq, _ = x.shape
    cos, sin_a, sin_b = tables
    kc, vc, s_hgrn, s_ssm, s_conv = caches
    if s_hgrn is None:
        s_hgrn = jnp.zeros((bsz, N_HEADS_A, HEAD_K_A, HEAD_K_A), F32)
        s_ssm = jnp.zeros((bsz, W_C, D_STATE), F32)
        s_conv = jnp.zeros((bsz, CONV_W - 1, CONV_DIM), F32)
    else:
        s_ssm = s_ssm.reshape(bsz, W_C, D_STATE)
        kc = _dup_heads(kc.reshape(bsz, WINDOW, KV_W_B))
        vc = _dup_heads(vc.reshape(bsz, WINDOW, KV_W_B))

    if h is None:
        yb, k_new, v_new, h = _swa_call(None, (x, lw["norm_pre"]), lw["w_b"], cos, sin_a, sin_b, lw["sinks"],
                                        kc, vc, tl)
    else:
        yb, k_new, v_new = _swa_call(h, None, lw["w_b"], cos, sin_a, sin_b, lw["sinks"], kc, vc, tl)
    ya, hgrn_new = _hgrn_call(h, lw["w_a"], lw["lb_pack"], lw["hgrn_norm"], s_hgrn, tl)
    yc, ssm_new, conv_new = _ssd_call(h, lw["w_c"], lw["conv_w"], lw["conv_b"], lw["dt_bias"],
                                      lw["a_log"], lw["d_skip"], lw["ssm_norm"], s_conv, s_ssm, tl)
    rows = bsz * seq
    x_new, h_next = _merge_call(x.reshape(rows, D_MODEL), h.reshape(rows, D_MODEL), ya.reshape(rows, W_A),
                                yb.reshape(rows, W_B), yc.reshape(rows, W_C), lw["norm_post"], norm_next,
                                lw["w_g"], lw["w_br"], lw["w_out"], min(MERGE_ROWS, rows), MXU_COLS)
    states = (k_new.reshape(bsz, WINDOW, N_KV_B, HEAD_DIM_B), v_new.reshape(bsz, WINDOW, N_KV_B, HEAD_DIM_B),
              hgrn_new, ssm_new.reshape(bsz, N_HEADS_C, HEAD_DIM_C, D_STATE), conv_new)
    if h_next is not None:
        h_next = h_next.reshape(bsz, seq, D_MODEL)
    return x_new.reshape(bsz, seq, D_MODEL), h_next, states


def kernel(x_prompt, x_sample, cache_swa_k, cache_swa_v, state_hgrn, state_ssm, state_conv, norm_pre, norm_post, w_in, hgrn_lb_logits, hgrn_norm, swa_sinks, conv_w, conv_b, dt_bias, a_log, d_skip, ssm_norm, w_branch_a, w_branch_b, w_branch_c, w_out):
    depth = w_in.shape[0]
    lbp = jax.nn.softmax(hgrn_lb_logits.astype(F32), axis=0)
    lbc = jnp.cumsum(lbp, axis=0)
    lb_all = lbc - lbc[0:1]

    offs = np.cumsum([0, W_A, W_A, W_A, W_A, W_B, KV_W_B, KV_W_B, W_B, W_C, CONV_DIM, N_HEADS_C,
                      D_MODEL, D_MODEL, D_MODEL])
    col = lambda l, a, b: w_in[l, :, int(offs[a]):int(offs[b])]

    tables_p = _rope_tables(jnp.arange(x_prompt.shape[1], dtype=jnp.int32))
    tables_s = _rope_tables(PAST_LEN + jnp.arange(x_sample.shape[1], dtype=jnp.int32))

    xp, xs = x_prompt, x_sample
    pre = lambda l: norm_pre[l].reshape(1, D_MODEL)
    hp = hs = None
    pst, sst = [], []
    for l in range(depth):
        lb = lb_all[l].reshape(N_HEADS_A, 1, HEAD_K_A)
        norm_next = pre(l + 1) if l + 1 < depth else None
        lw = {
            "norm_pre": pre(l),
            "norm_post": norm_post[l].reshape(1, D_MODEL),
            "w_a": _col_blocks(col(l, 0, 4).astype(BF16)),
            "w_b": col(l, 4, 8).astype(BF16),
            "w_c": jnp.concatenate([col(l, 8, 9), col(l, 9, 10), _pad_lanes(col(l, 10, 11))],
                                   axis=1).astype(BF16),
            "w_g": jnp.stack([col(l, 11, 12), col(l, 12, 13), col(l, 13, 14)]).astype(BF16),
            "w_br": jnp.stack([w_branch_a[l], w_branch_b[l], w_branch_c[l]]).astype(BF16),
            "w_out": w_out[l].astype(BF16),
            "lb_pack": jnp.stack([jnp.log(lb), jnp.log1p(-lb)]) * LOG2E,
            "hgrn_norm": hgrn_norm[l].reshape(1, HEAD_K_A),
            "sinks": jnp.broadcast_to(swa_sinks[l].astype(F32)[:, None], (N_Q_B, LANES)),
            "conv_w": conv_w[l],
            "conv_b": conv_b[l].reshape(1, CONV_DIM),
            "dt_bias": _pad_lanes(dt_bias[l].astype(F32).reshape(1, N_HEADS_C)),
            "a_log": _expand_heads(a_log[l].astype(F32)).reshape(1, W_C),
            "d_skip": _expand_heads(d_skip[l].astype(F32)).reshape(1, W_C),
            "ssm_norm": ssm_norm[l].reshape(1, W_C),
        }
        xp, hp, sp = _layer(xp, hp, PROMPT_ROWS, tables_p, (None, None, None, None, None), lw, norm_next)
        xs, hs, ss = _layer(xs, hs, CHUNK, tables_s,
                            (cache_swa_k[l], cache_swa_v[l], state_hgrn[l], state_ssm[l], state_conv[l]),
                            lw, norm_next)
        pst.append(sp)
        sst.append(ss)

    stack = lambda sts, k: jnp.stack([s[k] for s in sts])
    return (xp, xs,
            stack(pst, 0), stack(pst, 1), stack(pst, 2), stack(pst, 3), stack(pst, 4),
            stack(sst, 0), stack(sst, 1), stack(sst, 2), stack(sst, 3), stack(sst, 4))
```

```python
import functools
import math

import jax
import jax.numpy as jnp
import numpy as np
from jax import lax
from jax.experimental import pallas as pl
from jax.experimental.pallas import tpu as pltpu

F32 = jnp.float32
BF16 = jnp.bfloat16

D_MODEL = 2048
CHUNK = 64
EPS = 1e-6
PAST_LEN = 4096

W_A = 1024
HEAD_K_A = 128
N_HEADS_A = 8

N_Q_B = 16
N_KV_B = 4
HEAD_DIM_B = 64
W_B = 1024
KV_W_B = 256
WINDOW = 128
ROT_DIM = 16
ROPE_THETA = 500000.0
ATTN_SCALE = HEAD_DIM_B ** -0.5

W_C = 1024
HEAD_DIM_C = 64
N_HEADS_C = 16
N_GROUPS_C = 4
D_STATE = 128
CONV_W = 4
CONV_DIM = 2048

LANES = 128
SUBLANES = 8
SUB_BLOCK = 16
MXU_COLS = 256
PROMPT_ROWS = 256
MERGE_ROWS = 512
LOG2E = math.log2(math.e)
VMEM_LIMIT = 56 * 1024 * 1024

NT_DIMS = (((1,), (1,)), ((), ()))
TN_DIMS = (((0,), (0,)), ((), ()))


def _dot(a, b):
    return jnp.dot(a, b, preferred_element_type=F32)


def _dot_nt(a, b):
    return lax.dot_general(a, b, NT_DIMS, preferred_element_type=F32)


def _dot_tn(a, b):
    return lax.dot_general(a, b, TN_DIMS, preferred_element_type=F32)


def _prenorm_bf16(x, w):
    r = lax.rsqrt(jnp.mean(x * x, axis=-1, keepdims=True) + EPS)
    return ((x * r) * w).astype(BF16)


def _sigmoid(x):
    return 0.5 * jnp.tanh(0.5 * x) + 0.5


def _silu(x):
    return x * _sigmoid(x)


def _softplus0(x):
    return jnp.maximum(x, 0.0) + jnp.log(1.0 + jnp.exp(-jnp.abs(x)))


def _tri_bf16(n):
    r = lax.broadcasted_iota(jnp.int32, (n, n), 0)
    c = lax.broadcasted_iota(jnp.int32, (n, n), 1)
    return (r >= c).astype(BF16)


def _split3(x):
    hi = x.astype(BF16)
    r1 = x - hi.astype(F32)
    mid = r1.astype(BF16)
    lo = (r1 - mid.astype(F32)).astype(BF16)
    return hi, mid, lo


def _cumsum_rows(tri, x):
    hi, mid, lo = _split3(x)
    return _dot(tri, hi) + _dot(tri, mid) + _dot(tri, lo)


def _exact_dot(x, sel):
    hi, mid, lo = _split3(x)
    return _dot(hi, sel) + _dot(mid, sel) + _dot(lo, sel)


def _round_robin(gens, background=None, heavy_round=0):
    results = [None] * len(gens)
    live = list(range(len(gens)))
    rnd = 0
    while live:
        if background is not None and rnd == heavy_round:
            for _ in background:
                pass
        for n in list(live):
            try:
                next(gens[n])
            except StopIteration as stop:
                results[n] = stop.value
                live.remove(n)
        rnd += 1
    return results


def _hgrn_kernel(h_ref, w_ref, lb_ref, hn_ref, s0_ref, y_ref, snew_ref,
                 proj_s, st_s, *, tl, ni, seq_per_chunk):
    n = pl.program_id(0)
    n_chunks = tl // CHUNK
    i_lag = jnp.maximum(n - 1, 0) % ni
    nxt = n % 2
    cur = 1 - nxt

    @pl.when(n == 0)
    def _():
        proj_s[...] = jnp.zeros_like(proj_s)

    if not seq_per_chunk:
        @pl.when(i_lag == 0)
        def _():
            for hd in range(N_HEADS_A):
                st_s[hd] = s0_ref[0, hd].T

    n_col_blocks = w_ref.shape[0]
    cols_per_chunk = n_col_blocks // n_chunks

    def project(c):
        for j in range(cols_per_chunk):
            jb = c * cols_per_chunk + j
            res = _dot(h_ref[0], w_ref[jb])
            proj_s[nxt, 2 * jb] = res[:, :LANES]
            proj_s[nxt, 2 * jb + 1] = res[:, LANES:]
            yield

    tri = _tri_bf16(CHUNK)
    n_sub = CHUNK // SUB_BLOCK
    lane8 = lax.broadcasted_iota(jnp.int32, (SUBLANES, LANES), 1)
    sub8 = lax.broadcasted_iota(jnp.int32, (SUBLANES, LANES), 0)
    row64 = lax.broadcasted_iota(jnp.int32, (CHUNK, CHUNK), 0)
    col64 = lax.broadcasted_iota(jnp.int32, (CHUNK, CHUNK), 1)
    below_diag_block = row64 // SUB_BLOCK > col64 // SUB_BLOCK
    zero8 = jnp.zeros((SUBLANES, LANES), F32)

    def zeros(n):
        return jnp.zeros((n, LANES), F32)

    def piece(a, m, u):
        lo = SUB_BLOCK * m + SUBLANES * u
        return a[lo:lo + SUBLANES]

    def unit(hd, r0):
        rows = pl.ds(r0, CHUNK)
        aq = proj_s[cur, hd, rows, :]
        z = proj_s[cur, N_HEADS_A + hd, rows, :]
        v = proj_s[cur, 2 * N_HEADS_A + hd, rows, :].astype(BF16)
        ag = proj_s[cur, 3 * N_HEADS_A + hd, rows, :]
        log_lb = lb_ref[0, hd]
        log1m_lb = lb_ref[1, hd]

        z2 = z * LOG2E
        log_sig = jnp.minimum(z2, 0.0) - jnp.log2(1.0 + jnp.exp2(-jnp.abs(z2)))
        cterm = log1m_lb + log_sig
        log_f = jnp.maximum(log_lb, cterm) + jnp.log2(1.0 + jnp.exp2(-jnp.abs(log_lb - cterm)))
        log_k = cterm - z2
        q = _silu(aq)

        b2 = _cumsum_rows(tri, log_f)
        yield
        c2 = b2 - log_k
        b2_last = b2[CHUNK - 1:CHUNK, :]
        st = st_s[hd]

        o = _dot_nt((q * jnp.exp2(b2)).astype(BF16), st.astype(BF16))
        yield

        q_slabs, k_slabs = [], []
        for m in range(1, n_sub):
            lo = SUB_BLOCK * m
            ref = b2[lo - 1:lo, :]
            qm = q[lo:lo + SUB_BLOCK] * jnp.exp2(b2[lo:lo + SUB_BLOCK] - ref)
            q_parts = [zeros(lo), qm] + ([zeros(CHUNK - lo - SUB_BLOCK)] if lo + SUB_BLOCK < CHUNK else [])
            q_slabs.append(jnp.concatenate(q_parts, axis=0))
            k_slabs.append(jnp.concatenate([jnp.exp2(ref - c2[:lo]), zeros(CHUNK - lo)], axis=0))
        a_off = _dot_nt(jnp.concatenate(q_slabs, axis=1).astype(BF16),
                        jnp.concatenate(k_slabs, axis=1).astype(BF16))
        yield

        order = [(s, m, u) for s in range(SUB_BLOCK) for m in range(n_sub) for u in range(2)
                 if not (u == 0 and s >= SUBLANES)]
        pieces = []
        for s, m, u in order:
            cs = c2[SUB_BLOCK * m + s:SUB_BLOCK * m + s + 1]
            pieces.append(piece(q, m, u) * jnp.exp2(piece(b2, m, u) - cs))
        sums = [jnp.sum(p, axis=-1, keepdims=True) for p in pieces]
        yield
        d = {(m, u): zero8 for m in range(n_sub) for u in range(2)}
        for n, (s, m, u) in enumerate(order):
            d[(m, u)] = jnp.where(lane8 == s, sums[n], d[(m, u)])
        diag_rows = []
        for m in range(n_sub):
            for u in range(2):
                dm = jnp.where(sub8 + SUBLANES * u >= lane8, d[(m, u)], 0.0)
                diag_rows.append(pltpu.roll(dm, SUB_BLOCK * m, 1) if m else dm)
        a_diag = jnp.concatenate(diag_rows, axis=0)[:, :CHUNK]
        a_full = jnp.where(below_diag_block, a_off, a_diag)
        o = o + _dot(a_full.astype(BF16), v)
        yield

        r = lax.rsqrt(jnp.mean(o * o, axis=-1, keepdims=True) + EPS)
        y = ((o * r) * hn_ref[...]) * _silu(ag)

        st_new = st * jnp.exp2(b2_last) + _dot_tn(v, jnp.exp2(b2_last - c2).astype(BF16))
        return y.astype(BF16), st_new

    def body(c, carry):
        r0 = pl.multiple_of(c * CHUNK, CHUNK)
        if seq_per_chunk:
            for hd in range(N_HEADS_A):
                st_s[hd] = s0_ref[c, hd].T
        gens = [unit(hd, r0) for hd in range(N_HEADS_A)] + [project(c)]
        results = _round_robin(gens)[:N_HEADS_A]
        for hd, (y, st_new) in enumerate(results):
            y_ref[0, pl.ds(r0, CHUNK), LANES * hd:LANES * (hd + 1)] = y
            if seq_per_chunk:
                snew_ref[c, hd] = st_new.T
            else:
                st_s[hd] = st_new
        return carry

    lax.fori_loop(0, n_chunks, body, 0)

    if not seq_per_chunk:
        @pl.when(i_lag == ni - 1)
        def _():
            for hd in range(N_HEADS_A):
                snew_ref[0, hd] = st_s[hd].T


def _lagged_maps(n_blocks, ni):
    def split(m):
        return m // ni, m % ni
    x_map = lambda n: split(jnp.minimum(n, n_blocks - 1)) + (0,)
    y_map = lambda n: split(jnp.maximum(n - 1, 0)) + (0,)
    pos_map = lambda n: (jnp.minimum(n, n_blocks - 1) % ni, 0)
    batch_map = lambda nd: (lambda n: (jnp.maximum(n - 1, 0) // ni,) + (0,) * (nd - 1))
    return x_map, y_map, pos_map, batch_map


def _hgrn_call(h, w_a, lb_pack, hn, s0, tl):
    n_seq, seq, _ = h.shape
    seq_per_chunk = seq == CHUNK and n_seq % (PROMPT_ROWS // CHUNK) == 0
    if seq_per_chunk:
        group = PROMPT_ROWS // CHUNK
        tl = PROMPT_ROWS
        h = h.reshape(n_seq // group, tl, D_MODEL)
    else:
        group = 1
    bsz = h.shape[0]
    ni = h.shape[1] // tl
    n_blocks = bsz * ni
    x_map, y_map, _, batch_map = _lagged_maps(n_blocks, ni)
    const2 = lambda n: (0, 0)
    state_spec = pl.BlockSpec((group, N_HEADS_A, HEAD_K_A, HEAD_K_A), batch_map(4))
    y, s_new = pl.pallas_call(
        functools.partial(_hgrn_kernel, tl=tl, ni=ni, seq_per_chunk=seq_per_chunk),
        grid=(n_blocks + 1,),
        in_specs=[
            pl.BlockSpec((1, tl, D_MODEL), x_map),
            pl.BlockSpec(w_a.shape, lambda n: (0, 0, 0), pipeline_mode=pl.Buffered(1)),
            pl.BlockSpec((2, N_HEADS_A, 1, HEAD_K_A), lambda n: (0, 0, 0, 0)),
            pl.BlockSpec((1, HEAD_K_A), const2),
            state_spec,
        ],
        out_specs=[pl.BlockSpec((1, tl, W_A), y_map), state_spec],
        out_shape=[
            jax.ShapeDtypeStruct((bsz, ni * tl, W_A), BF16),
            jax.ShapeDtypeStruct((n_seq, N_HEADS_A, HEAD_K_A, HEAD_K_A), F32),
        ],
        scratch_shapes=[
            pltpu.VMEM((2, 4 * N_HEADS_A, tl, LANES), F32),
            pltpu.VMEM((N_HEADS_A, HEAD_K_A, HEAD_K_A), F32),
        ],
        compiler_params=pltpu.CompilerParams(
            dimension_semantics=("arbitrary",), vmem_limit_bytes=VMEM_LIMIT),
        name="hgrn_mixer",
    )(h, w_a, lb_pack, hn, s0)
    return y.reshape(n_seq, seq, W_A), s_new


def _rotate(xs, cos, sin_a, sin_b):
    return xs * cos + pltpu.roll(xs, LANES - ROT_DIM // 2, 1) * sin_a + pltpu.roll(xs, ROT_DIM // 2, 1) * sin_b


def _swa_kernel(*refs, tl, ni, n_blocks, has_cache, from_x):
    it = iter(refs)
    if from_x:
        x_ref, nw_ref = next(it), next(it)
    else:
        h_ref = next(it)
    w_ref, cos_ref, sa_ref, sb_ref, sink_ref = (next(it) for _ in range(5))
    if has_cache:
        kc_ref, vc_ref = next(it), next(it)
    y_ref, knew_ref, vnew_ref = (next(it) for _ in range(3))
    if from_x:
        h_ref = next(it)
    q_s, g_s, kbuf, vbuf = (next(it) for _ in range(4))
    n = pl.program_id(0)
    n_chunks = tl // CHUNK
    band = WINDOW + CHUNK
    kvw = 2 * KV_W_B
    i_proj = jnp.minimum(n, n_blocks - 1) % ni
    i_lag = jnp.maximum(n - 1, 0) % ni
    nxt = n % 2
    cur = 1 - nxt

    @pl.when(n == 0)
    def _():
        for buf in (q_s, g_s, kbuf, vbuf):
            buf[...] = jnp.zeros_like(buf)

    @pl.when(i_proj == 0)
    def _():
        if has_cache:
            kbuf[nxt, 0:WINDOW, :] = kc_ref[0]
            vbuf[nxt, 0:WINDOW, :] = vc_ref[0]
        else:
            kbuf[nxt, 0:WINDOW, :] = jnp.zeros((WINDOW, kvw), F32)
            vbuf[nxt, 0:WINDOW, :] = jnp.zeros((WINDOW, kvw), F32)

    @pl.when(i_proj != 0)
    def _():
        kbuf[nxt, 0:WINDOW, :] = kbuf[cur, tl:tl + WINDOW, :]
        vbuf[nxt, 0:WINDOW, :] = vbuf[cur, tl:tl + WINDOW, :]

    if from_x:
        h_ref[0] = _prenorm_bf16(x_ref[0], nw_ref[...])

    def project(c):
        per = (W_B // MXU_COLS) // n_chunks
        new_rows = slice(WINDOW, WINDOW + tl)
        for j in range(c * per, (c + 1) * per):
            sl = slice(MXU_COLS * j, MXU_COLS * (j + 1))
            res = _dot(h_ref[0], w_ref[:, sl])
            yield
            for u in range(MXU_COLS // LANES):
                q_s[nxt, :, MXU_COLS * j + LANES * u:MXU_COLS * j + LANES * (u + 1)] = _rotate(
                    res[:, LANES * u:LANES * (u + 1)], cos_ref[...], sa_ref[...], sb_ref[...])
            g_s[nxt, :, sl] = _dot(h_ref[0], w_ref[:, W_B + 2 * KV_W_B + MXU_COLS * j:W_B + 2 * KV_W_B + MXU_COLS * (j + 1)])
            yield
            if j < 2:
                res = _dot(h_ref[0], w_ref[:, W_B + KV_W_B * j:W_B + KV_W_B * (j + 1)])
                yield
                for u in range(KV_W_B // LANES):
                    a = res[:, LANES * u:LANES * (u + 1)]
                    if j == 0:
                        a = _rotate(a, cos_ref[...], sa_ref[...], sb_ref[...])
                    swapped = pltpu.roll(a, HEAD_DIM_B, 1)
                    buf = kbuf if j == 0 else vbuf
                    buf[nxt, new_rows, LANES * 2 * u:LANES * (2 * u + 1)] = jnp.where(low1, a, swapped)
                    buf[nxt, new_rows, LANES * (2 * u + 1):LANES * (2 * u + 2)] = jnp.where(low1, swapped, a)

    lane = lax.broadcasted_iota(jnp.int32, (CHUNK, LANES), 1)
    low = lane < HEAD_DIM_B
    low1 = lax.broadcasted_iota(jnp.int32, (1, LANES), 1) < HEAD_DIM_B
    key_row = lax.broadcasted_iota(jnp.int32, (band, LANES), 0)

    def pair(c, j):
        r0 = CHUNK * c
        g = j // 2
        qp = q_s[cur, r0:r0 + CHUNK, LANES * j:LANES * (j + 1)]
        q2 = jnp.concatenate([jnp.where(low, qp, 0.0), jnp.where(low, 0.0, qp)], axis=0).astype(BF16)
        kb = kbuf[cur, r0:r0 + band, LANES * g:LANES * (g + 1)].astype(BF16)
        vb = vbuf[cur, r0:r0 + band, LANES * g:LANES * (g + 1)].astype(BF16)
        s = _dot_nt(kb, q2) * ATTN_SCALE
        yield
        if not has_cache:
            s = jnp.where(i_lag * tl + r0 - WINDOW + key_row >= 0, s, -jnp.inf)
        sk = jnp.where(low1, sink_ref[2 * j:2 * j + 1, :], sink_ref[2 * j + 1:2 * j + 2, :])
        m = jnp.maximum(jnp.max(s, axis=0, keepdims=True), sk)
        p = jnp.exp(s - m)
        den = jnp.sum(p, axis=0, keepdims=True) + jnp.exp(sk - m)
        o2 = _dot_tn((p * (1.0 / den)).astype(BF16), vb)
        yield
        o = jnp.where(low, o2[:CHUNK], o2[CHUNK:])
        y = o * _silu(g_s[cur, r0:r0 + CHUNK, LANES * j:LANES * (j + 1)])
        y_ref[0, r0:r0 + CHUNK, LANES * j:LANES * (j + 1)] = y.astype(BF16)

    for c in range(n_chunks):
        _round_robin([pair(c, j) for j in range(N_Q_B // 2)], project(c), heavy_round=1)

    @pl.when(i_proj == ni - 1)
    def _():
        ktail = kbuf[nxt, tl:tl + WINDOW, :]
        vtail = vbuf[nxt, tl:tl + WINDOW, :]
        low_w = lax.broadcasted_iota(jnp.int32, (WINDOW, LANES), 1) < HEAD_DIM_B
        for u in range(KV_W_B // LANES):
            knew_ref[0, :, LANES * u:LANES * (u + 1)] = jnp.where(
                low_w, ktail[:, 2 * LANES * u:2 * LANES * u + LANES],
                ktail[:, 2 * LANES * u + LANES:2 * LANES * (u + 1)])
            vnew_ref[0, :, LANES * u:LANES * (u + 1)] = jnp.where(
                low_w, vtail[:, 2 * LANES * u:2 * LANES * u + LANES],
                vtail[:, 2 * LANES * u + LANES:2 * LANES * (u + 1)])


def _swa_call(h, x_norm, w_b, cos, sin_a, sin_b, sinks, kc, vc, tl):
    from_x = h is None
    bsz, seq, _ = x_norm[0].shape if from_x else h.shape
    has_cache = kc is not None
    ni = seq // tl
    n_blocks = bsz * ni
    x_map, y_map, pos_map, _ = _lagged_maps(n_blocks, ni)
    proj_batch_map = lambda n: (jnp.minimum(n, n_blocks - 1) // ni, 0, 0)
    const2 = lambda n: (0, 0)
    wcols = w_b.shape[1]
    act_spec = pl.BlockSpec((1, tl, D_MODEL), x_map)
    in_specs = [act_spec, pl.BlockSpec((1, D_MODEL), const2)] if from_x else [act_spec]
    args = list(x_norm) if from_x else [h]
    in_specs += [
        pl.BlockSpec((D_MODEL, wcols), const2, pipeline_mode=pl.Buffered(1)),
        pl.BlockSpec((tl, LANES), pos_map),
        pl.BlockSpec((tl, LANES), pos_map),
        pl.BlockSpec((tl, LANES), pos_map),
        pl.BlockSpec((N_Q_B, LANES), const2),
    ]
    args += [w_b, cos, sin_a, sin_b, sinks]
    if has_cache:
        in_specs += [pl.BlockSpec((1, WINDOW, 2 * KV_W_B), proj_batch_map)] * 2
        args += [kc, vc]
    out_specs = [
        pl.BlockSpec((1, tl, W_B), y_map),
        pl.BlockSpec((1, WINDOW, KV_W_B), proj_batch_map),
        pl.BlockSpec((1, WINDOW, KV_W_B), proj_batch_map),
    ]
    out_shape = [
        jax.ShapeDtypeStruct((bsz, seq, W_B), BF16),
        jax.ShapeDtypeStruct((bsz, WINDOW, KV_W_B), F32),
        jax.ShapeDtypeStruct((bsz, WINDOW, KV_W_B), F32),
    ]
    if from_x:
        out_specs.append(act_spec)
        out_shape.append(jax.ShapeDtypeStruct((bsz, seq, D_MODEL), BF16))
    return pl.pallas_call(
        functools.partial(_swa_kernel, tl=tl, ni=ni, n_blocks=n_blocks, has_cache=has_cache, from_x=from_x),
        grid=(n_blocks + 1,),
        in_specs=in_specs,
        out_specs=out_specs,
        out_shape=out_shape,
        scratch_shapes=[
            pltpu.VMEM((2, tl, W_B), F32),
            pltpu.VMEM((2, tl, W_B), F32),
            pltpu.VMEM((2, WINDOW + tl, 2 * KV_W_B), F32),
            pltpu.VMEM((2, WINDOW + tl, 2 * KV_W_B), F32),
        ],
        compiler_params=pltpu.CompilerParams(
            dimension_semantics=("arbitrary",), vmem_limit_bytes=VMEM_LIMIT),
        name="swa_mixer",
    )(*args)


CONV_PAD = 8


def _ssd_kernel(h_ref, w_ref, cw_ref, cb_ref, dtb_ref, alog_ref, dsk_ref, sn_ref, conv0_ref, s0_ref,
                y_ref, snew_ref, convnew_ref,
                ubuf, dt_s, z_s, xc_s, xdt_s, bc_s, la_s, st_s, *, tl, ni, n_blocks):
    n = pl.program_id(0)
    n_chunks = tl // CHUNK
    gw = N_GROUPS_C * D_STATE
    hp = W_C // N_GROUPS_C
    i_proj = jnp.minimum(n, n_blocks - 1) % ni
    i_lag = jnp.maximum(n - 1, 0) % ni
    nxt = n % 2
    cur = 1 - nxt
    tail_rows = slice(CONV_PAD - (CONV_W - 1), CONV_PAD)

    @pl.when(n == 0)
    def _():
        for buf in (z_s, xc_s, xdt_s, bc_s, la_s):
            buf[...] = jnp.zeros_like(buf)

    @pl.when(i_proj == 0)
    def _():
        ubuf[tail_rows, :] = conv0_ref[0]

    @pl.when(i_lag == 0)
    def _():
        for g in range(N_GROUPS_C):
            st_s[g] = s0_ref[0, hp * g:hp * (g + 1), :].T

    def conv_act(sl):
        win = ubuf[0:CONV_PAD + tl, sl]
        acc = cb_ref[:, sl]
        for t in range(CONV_W):
            back = CONV_W - 1 - t
            src = pltpu.roll(win, back, 0) if back else win
            acc = acc + src[CONV_PAD:CONV_PAD + tl] * cw_ref[t:t + 1, sl]
        ubuf[tail_rows, sl] = ubuf[tl + CONV_PAD - (CONV_W - 1):tl + CONV_PAD, sl]
        return _silu(acc)

    def project(c):
        per = (W_C // MXU_COLS) // n_chunks
        for j in range(c * per, (c + 1) * per):
            sl = slice(MXU_COLS * j, MXU_COLS * (j + 1))
            z_s[nxt, :, sl] = _dot(h_ref[0], w_ref[:, sl])
            yield
            if j == 0:
                dt_raw = _dot(h_ref[0], w_ref[:, W_C + CONV_DIM:W_C + CONV_DIM + LANES])
                yield
                dt_s[...] = _softplus0(dt_raw + dtb_ref[...])
            head_of_col = (MXU_COLS * j + lax.broadcasted_iota(jnp.int32, (LANES, MXU_COLS), 1)) // HEAD_DIM_C
            spread = (lax.broadcasted_iota(jnp.int32, (LANES, MXU_COLS), 0) == head_of_col).astype(BF16)
            dt = _exact_dot(dt_s[...], spread)
            yield
            la_s[nxt, :, sl] = -dt * jnp.exp(alog_ref[:, sl])
            ubuf[CONV_PAD:CONV_PAD + tl, sl] = _dot(h_ref[0], w_ref[:, W_C + MXU_COLS * j:W_C + MXU_COLS * (j + 1)])
            yield
            for u in range(MXU_COLS // LANES):
                usl = slice(MXU_COLS * j + LANES * u, MXU_COLS * j + LANES * (u + 1))
                act = conv_act(usl)
                xc_s[nxt, :, usl] = act
                xdt_s[nxt, :, usl] = act * dt[:, LANES * u:LANES * (u + 1)]
            bsl = slice(W_C + MXU_COLS * j, W_C + MXU_COLS * (j + 1))
            ubuf[CONV_PAD:CONV_PAD + tl, bsl] = _dot(h_ref[0], w_ref[:, W_C + W_C + MXU_COLS * j:W_C + W_C + MXU_COLS * (j + 1)])
            yield
            for u in range(MXU_COLS // LANES):
                usl = slice(W_C + MXU_COLS * j + LANES * u, W_C + MXU_COLS * j + LANES * (u + 1))
                bc_s[nxt, :, MXU_COLS * j + LANES * u:MXU_COLS * j + LANES * (u + 1)] = conv_act(usl)

    tri = _tri_bf16(CHUNK)
    trow = lax.broadcasted_iota(jnp.int32, (CHUNK, LANES), 0)
    lane = lax.broadcasted_iota(jnp.int32, (CHUNK, LANES), 1)
    s_of_lane = lane % CHUNK
    causal = trow >= s_of_lane
    row2 = lax.broadcasted_iota(jnp.int32, (2 * CHUNK, LANES), 0)
    lane2 = lax.broadcasted_iota(jnp.int32, (2 * CHUNK, LANES), 1)
    blockdiag = (row2 < CHUNK) == (lane2 < HEAD_DIM_C)

    for c in range(n_chunks):
        rows = slice(CHUNK * c, CHUNK * (c + 1))

        def group(g, rows=rows):
            gsl = slice(hp * g, hp * (g + 1))
            bg = bc_s[cur, rows, D_STATE * g:D_STATE * (g + 1)].astype(BF16)
            cg = bc_s[cur, rows, gw + D_STATE * g:gw + D_STATE * (g + 1)].astype(BF16)
            cb2 = _dot_nt(cg, jnp.concatenate([bg, bg], axis=0))
            yield
            st = st_s[g]
            y_state = _dot(cg, st.astype(BF16))
            yield
            la = la_s[cur, rows, gsl]
            b = _cumsum_rows(tri, la)
            yield
            xdt = xdt_s[cur, rows, gsl]
            ys = []
            for u in range(2):
                usl = slice(LANES * u, LANES * (u + 1))
                b_row = jnp.sum(jnp.where(trow <= s_of_lane, la[:, usl], 0.0), axis=0, keepdims=True)
                decay = jnp.where(causal, jnp.exp(b[:, usl] - b_row), 0.0)
                wmat = (cb2 * decay).astype(BF16)
                xbd = jnp.where(blockdiag, jnp.concatenate([xdt[:, usl], xdt[:, usl]], axis=0), 0.0).astype(BF16)
                ys.append(_dot(wmat, xbd))
                yield
            y = jnp.concatenate(ys, axis=1) + y_state * jnp.exp(b)
            o = (y + dsk_ref[:, gsl] * xc_s[cur, rows, gsl]) * _silu(z_s[cur, rows, gsl])
            r = lax.rsqrt(jnp.mean(o * o, axis=-1, keepdims=True) + EPS)
            out = ((o * r) * sn_ref[:, gsl]).astype(BF16)
            b_last = b[CHUNK - 1:CHUNK, :]
            st_new = st * jnp.exp(b_last) + _dot_tn(bg, (xdt * jnp.exp(b_last - b)).astype(BF16))
            return out, st_new

        results = _round_robin([group(g) for g in range(N_GROUPS_C)] + [project(c)])[:N_GROUPS_C]
        for g, (out, st_new) in enumerate(results):
            y_ref[0, rows, hp * g:hp * (g + 1)] = out
            st_s[g] = st_new

    @pl.when(i_proj == ni - 1)
    def _():
        convnew_ref[0] = ubuf[tail_rows, :]

    @pl.when(i_lag == ni - 1)
    def _():
        for g in range(N_GROUPS_C):
            snew_ref[0, hp * g:hp * (g + 1), :] = st_s[g].T


def _ssd_call(h, w_c, cw, cb, dtb, alog, dsk, sn, conv0, s0, tl):
    bsz, seq, _ = h.shape
    ni = seq // tl
    n_blocks = bsz * ni
    x_map, y_map, _, batch_map = _lagged_maps(n_blocks, ni)
    proj_batch_map = lambda n: (jnp.minimum(n, n_blocks - 1) // ni, 0, 0)
    const2 = lambda n: (0, 0)
    wcols = w_c.shape[1]
    return pl.pallas_call(
        functools.partial(_ssd_kernel, tl=tl, ni=ni, n_blocks=n_blocks),
        grid=(n_blocks + 1,),
        in_specs=[
            pl.BlockSpec((1, tl, D_MODEL), x_map),
            pl.BlockSpec((D_MODEL, wcols), const2, pipeline_mode=pl.Buffered(1)),
            pl.BlockSpec((CONV_W, CONV_DIM), const2),
            pl.BlockSpec((1, CONV_DIM), const2),
            pl.BlockSpec((1, LANES), const2),
            pl.BlockSpec((1, W_C), const2),
            pl.BlockSpec((1, W_C), const2),
            pl.BlockSpec((1, W_C), const2),
            pl.BlockSpec((1, CONV_W - 1, CONV_DIM), proj_batch_map),
            pl.BlockSpec((1, W_C, D_STATE), batch_map(3)),
        ],
        out_specs=[
            pl.BlockSpec((1, tl, W_C), y_map),
            pl.BlockSpec((1, W_C, D_STATE), batch_map(3)),
            pl.BlockSpec((1, CONV_W - 1, CONV_DIM), proj_batch_map),
        ],
        out_shape=[
            jax.ShapeDtypeStruct((bsz, seq, W_C), BF16),
            jax.ShapeDtypeStruct((bsz, W_C, D_STATE), F32),
            jax.ShapeDtypeStruct((bsz, CONV_W - 1, CONV_DIM), F32),
        ],
        scratch_shapes=[
            pltpu.VMEM((CONV_PAD + tl, CONV_DIM), F32),
            pltpu.VMEM((tl, LANES), F32),
            pltpu.VMEM((2, tl, W_C), F32),
            pltpu.VMEM((2, tl, W_C), F32),
            pltpu.VMEM((2, tl, W_C), F32),
            pltpu.VMEM((2, tl, 2 * N_GROUPS_C * D_STATE), F32),
            pltpu.VMEM((2, tl, W_C), F32),
            pltpu.VMEM((N_GROUPS_C, D_STATE, W_C // N_GROUPS_C), F32),
        ],
        compiler_params=pltpu.CompilerParams(
            dimension_semantics=("arbitrary",), vmem_limit_bytes=VMEM_LIMIT),
        name="ssd_mixer",
    )(h, w_c, cw, cb, dtb, alog, dsk, sn, conv0, s0)


def _merge_kernel(*refs, emit_next):
    if emit_next:
        (x_ref, h_ref, ya_ref, yb_ref, yc_ref, npost_ref, nnext_ref, wg_ref, wbr_ref, wout_ref,
         o_ref, hnext_ref, m_s) = refs
    else:
        (x_ref, h_ref, ya_ref, yb_ref, yc_ref, npost_ref, wg_ref, wbr_ref, wout_ref, o_ref, m_s) = refs
    j = pl.program_id(1)
    n_j = m_s.shape[0]

    h = h_ref[...]
    merged = None
    for bi, y_ref in enumerate((ya_ref, yb_ref, yc_ref)):
        term = _sigmoid(_dot(h, wg_ref[bi])) * _dot(y_ref[...], wbr_ref[bi])
        merged = term if merged is None else merged + term
    m_s[j] = merged.astype(BF16)

    @pl.when(j == n_j - 1)
    def _():
        out = _dot(jnp.concatenate([m_s[n] for n in range(n_j)], axis=1), wout_ref[...])
        r = lax.rsqrt(jnp.mean(out * out, axis=-1, keepdims=True) + EPS)
        x_new = x_ref[...] + (out * r) * npost_ref[...]
        o_ref[...] = x_new
        if emit_next:
            hnext_ref[...] = _prenorm_bf16(x_new, nnext_ref[...])


def _merge_call(x2, h2, ya, yb, yc, npost, nnext, wg, wbr, wout, tm, tn):
    rows = x2.shape[0]
    emit_next = nnext is not None
    grid = (rows // tm, D_MODEL // tn)
    const2 = lambda r, j: (0, 0)
    row_block = lambda width: pl.BlockSpec((tm, width), lambda r, j: (r, 0))
    vec = pl.BlockSpec((1, D_MODEL), const2)
    in_specs = [row_block(D_MODEL), row_block(D_MODEL), row_block(W_A), row_block(W_B), row_block(W_C), vec]
    args = [x2, h2, ya, yb, yc, npost]
    if emit_next:
        in_specs.append(vec)
        args.append(nnext)
    in_specs += [
        pl.BlockSpec((3, D_MODEL, tn), lambda r, j: (0, 0, j)),
        pl.BlockSpec((3, W_A, tn), lambda r, j: (0, 0, j)),
        pl.BlockSpec((D_MODEL, D_MODEL), const2, pipeline_mode=pl.Buffered(1)),
    ]
    args += [wg, wbr, wout]
    out_specs = [row_block(D_MODEL)]
    out_shape = [jax.ShapeDtypeStruct((rows, D_MODEL), F32)]
    if emit_next:
        out_specs.append(row_block(D_MODEL))
        out_shape.append(jax.ShapeDtypeStruct((rows, D_MODEL), BF16))
    outs = pl.pallas_call(
        functools.partial(_merge_kernel, emit_next=emit_next),
        grid=grid,
        in_specs=in_specs,
        out_specs=out_specs,
        out_shape=out_shape,
        scratch_shapes=[pltpu.VMEM((D_MODEL // tn, tm, tn), BF16)],
        compiler_params=pltpu.CompilerParams(
            dimension_semantics=("arbitrary", "arbitrary"), vmem_limit_bytes=VMEM_LIMIT),
        name="merge_out",
    )(*args)
    return (outs[0], outs[1]) if emit_next else (outs[0], None)


def _rope_tables(pos):
    half = ROT_DIM // 2
    inv = jnp.power(ROPE_THETA, -jnp.arange(half, dtype=F32) / half)
    ang = pos.astype(F32)[:, None] * inv[None, :]
    cos, sin = jnp.cos(ang), jnp.sin(ang)
    n = pos.shape[0]
    ones = jnp.ones((n, HEAD_DIM_B - ROT_DIM), F32)
    zeros = jnp.zeros((n, HEAD_DIM_B - ROT_DIM), F32)
    zh = jnp.zeros((n, half), F32)
    cos_t = jnp.concatenate([cos, cos, ones], axis=1)
    sin_a = jnp.concatenate([-sin, zh, zeros], axis=1)
    sin_b = jnp.concatenate([zh, sin, zeros], axis=1)
    tile = lambda t: jnp.concatenate([t, t], axis=1)
    return tile(cos_t), tile(sin_a), tile(sin_b)


def _dup_heads(t):
    lead = t.shape[:-1]
    t4 = t.reshape(lead + (N_KV_B, 1, HEAD_DIM_B))
    return jnp.broadcast_to(t4, lead + (N_KV_B, 2, HEAD_DIM_B)).reshape(lead + (2 * KV_W_B,))


def _col_blocks(w):
    return w.reshape(w.shape[0], -1, MXU_COLS).transpose(1, 0, 2)


def _pad_lanes(t):
    return jnp.pad(t, [(0, 0)] * (t.ndim - 1) + [(0, LANES - t.shape[-1])])


def _expand_heads(t):
    return jnp.repeat(t, HEAD_DIM_C, axis=-1)


def _layer(x, h, tl, tables, caches, lw, norm_next):
    bsz, seq, _ = x.shape
    cos, sin_a, sin_b = tables
    kc, vc, s_hgrn, s_ssm, s_conv = caches
    if s_hgrn is None:
        s_hgrn = jnp.zeros((bsz, N_HEADS_A, HEAD_K_A, HEAD_K_A), F32)
        s_ssm = jnp.zeros((bsz, W_C, D_STATE), F32)
        s_conv = jnp.zeros((bsz, CONV_W - 1, CONV_DIM), F32)
    else:
        s_ssm = s_ssm.reshape(bsz, W_C, D_STATE)
        kc = _dup_heads(kc.reshape(bsz, WINDOW, KV_W_B))
        vc = _dup_heads(vc.reshape(bsz, WINDOW, KV_W_B))

    if h is None:
        yb, k_new, v_new, h = _swa_call(None, (x, lw["norm_pre"]), lw["w_b"], cos, sin_a, sin_b, lw["sinks"],
                                        kc, vc, tl)
    else:
        yb, k_new, v_new = _swa_call(h, None, lw["w_b"], cos, sin_a, sin_b, lw["sinks"], kc, vc, tl)
    ya, hgrn_new = _hgrn_call(h, lw["w_a"], lw["lb_pack"], lw["hgrn_norm"], s_hgrn, tl)
    yc, ssm_new, conv_new = _ssd_call(h, lw["w_c"], lw["conv_w"], lw["conv_b"], lw["dt_bias"],
                                      lw["a_log"], lw["d_skip"], lw["ssm_norm"], s_conv, s_ssm, tl)
    rows = bsz * seq
    x_new, h_next = _merge_call(x.reshape(rows, D_MODEL), h.reshape(rows, D_MODEL), ya.reshape(rows, W_A),
                                yb.reshape(rows, W_B), yc.reshape(rows, W_C), lw["norm_post"], norm_next,
                                lw["w_g"], lw["w_br"], lw["w_out"], min(MERGE_ROWS, rows), MXU_COLS)
    states = (k_new.reshape(bsz, WINDOW, N_KV_B, HEAD_DIM_B), v_new.reshape(bsz, WINDOW, N_KV_B, HEAD_DIM_B),
              hgrn_new, ssm_new.reshape(bsz, N_HEADS_C, HEAD_DIM_C, D_STATE), conv_new)
    if h_next is not None:
        h_next = h_next.reshape(bsz, seq, D_MODEL)
    return x_new.reshape(bsz, seq, D_MODEL), h_next, states


def kernel(x_prompt, x_sample, cache_swa_k, cache_swa_v, state_hgrn, state_ssm, state_conv, norm_pre, norm_post, w_in, hgrn_lb_logits, hgrn_norm, swa_sinks, conv_w, conv_b, dt_bias, a_log, d_skip, ssm_norm, w_branch_a, w_branch_b, w_branch_c, w_out):
    depth = w_in.shape[0]
    lbp = jax.nn.softmax(hgrn_lb_logits.astype(F32), axis=0)
    lbc = jnp.cumsum(lbp, axis=0)
    lb_all = lbc - lbc[0:1]

    offs = np.cumsum([0, W_A, W_A, W_A, W_A, W_B, KV_W_B, KV_W_B, W_B, W_C, CONV_DIM, N_HEADS_C,
                      D_MODEL, D_MODEL, D_MODEL])
    col = lambda l, a, b: w_in[l, :, int(offs[a]):int(offs[b])]

    tables_p = _rope_tables(jnp.arange(x_prompt.shape[1], dtype=jnp.int32))
    tables_s = _rope_tables(PAST_LEN + jnp.arange(x_sample.shape[1], dtype=jnp.int32))

    xp, xs = x_prompt, x_sample
    pre = lambda l: norm_pre[l].reshape(1, D_MODEL)
    hp = hs = None
    pst, sst = [], []
    for l in range(depth):
        lb = lb_all[l].reshape(N_HEADS_A, 1, HEAD_K_A)
        norm_next = pre(l + 1) if l + 1 < depth else None
        lw = {
            "norm_pre": pre(l),
            "norm_post": norm_post[l].reshape(1, D_MODEL),
            "w_a": _col_blocks(col(l, 0, 4).astype(BF16)),
            "w_b": col(l, 4, 8).astype(BF16),
            "w_c": jnp.concatenate([col(l, 8, 9), col(l, 9, 10), _pad_lanes(col(l, 10, 11))],
                                   axis=1).astype(BF16),
            "w_g": jnp.stack([col(l, 11, 12), col(l, 12, 13), col(l, 13, 14)]).astype(BF16),
            "w_br": jnp.stack([w_branch_a[l], w_branch_b[l], w_branch_c[l]]).astype(BF16),
            "w_out": w_out[l].astype(BF16),
            "lb_pack": jnp.stack([jnp.log(lb), jnp.log1p(-lb)]) * LOG2E,
            "hgrn_norm": hgrn_norm[l].reshape(1, HEAD_K_A),
            "sinks": jnp.broadcast_to(swa_sinks[l].astype(F32)[:, None], (N_Q_B, LANES)),
            "conv_w": conv_w[l],
            "conv_b": conv_b[l].reshape(1, CONV_DIM),
            "dt_bias": _pad_lanes(dt_bias[l].astype(F32).reshape(1, N_HEADS_C)),
            "a_log": _expand_heads(a_log[l].astype(F32)).reshape(1, W_C),
            "d_skip": _expand_heads(d_skip[l].astype(F32)).reshape(1, W_C),
            "ssm_norm": ssm_norm[l].reshape(1, W_C),
        }
        xp, hp, sp = _layer(xp, hp, PROMPT_ROWS, tables_p, (None, None, None, None, None), lw, norm_next)
        xs, hs, ss = _layer(xs, hs, CHUNK, tables_s,
                            (cache_swa_k[l], cache_swa_v[l], state_hgrn[l], state_ssm[l], state_conv[l]),
                            lw, norm_next)
        pst.append(sp)
        sst.append(ss)

    stack = lambda sts, k: jnp.stack([s[k] for s in sts])
    return (xp, xs,
            stack(pst, 0), stack(pst, 1), stack(pst, 2), stack(pst, 3), stack(pst, 4),
            stack(sst, 0), stack(sst, 1), stack(sst, 2), stack(sst, 3), stack(sst, 4))
```

```python
import functools
import math

import jax
import jax.numpy as jnp
import numpy as np
from jax import lax
from jax.experimental import pallas as pl
from jax.experimental.pallas import tpu as pltpu

F32 = jnp.float32
BF16 = jnp.bfloat16

D_MODEL = 2048
CHUNK = 64
EPS = 1e-6
PAST_LEN = 4096

W_A = 1024
HEAD_K_A = 128
N_HEADS_A = 8

N_Q_B = 16
N_KV_B = 4
HEAD_DIM_B = 64
W_B = 1024
KV_W_B = 256
WINDOW = 128
ROT_DIM = 16
ROPE_THETA = 500000.0
ATTN_SCALE = HEAD_DIM_B ** -0.5

W_C = 1024
HEAD_DIM_C = 64
N_HEADS_C = 16
N_GROUPS_C = 4
D_STATE = 128
CONV_W = 4
CONV_DIM = 2048

LANES = 128
SUBLANES = 8
SUB_BLOCK = 16
MXU_COLS = 256
PROMPT_ROWS = 256
MERGE_ROWS = 512
LOG2E = math.log2(math.e)
VMEM_LIMIT = 56 * 1024 * 1024

NT_DIMS = (((1,), (1,)), ((), ()))
TN_DIMS = (((0,), (0,)), ((), ()))


def _dot(a, b):
    return jnp.dot(a, b, preferred_element_type=F32)


def _dot_nt(a, b):
    return lax.dot_general(a, b, NT_DIMS, preferred_element_type=F32)


def _dot_tn(a, b):
    return lax.dot_general(a, b, TN_DIMS, preferred_element_type=F32)


def _prenorm_bf16(x, w):
    r = lax.rsqrt(jnp.mean(x * x, axis=-1, keepdims=True) + EPS)
    return ((x * r) * w).astype(BF16)


def _sigmoid(x):
    return 0.5 * jnp.tanh(0.5 * x) + 0.5


def _silu(x):
    return x * _sigmoid(x)


def _softplus0(x):
    return jnp.maximum(x, 0.0) + jnp.log(1.0 + jnp.exp(-jnp.abs(x)))


def _tri_bf16(n):
    r = lax.broadcasted_iota(jnp.int32, (n, n), 0)
    c = lax.broadcasted_iota(jnp.int32, (n, n), 1)
    return (r >= c).astype(BF16)


def _split3(x):
    hi = x.astype(BF16)
    r1 = x - hi.astype(F32)
    mid = r1.astype(BF16)
    lo = (r1 - mid.astype(F32)).astype(BF16)
    return hi, mid, lo


def _cumsum_rows(tri, x):
    hi, mid, lo = _split3(x)
    return _dot(tri, hi) + _dot(tri, mid) + _dot(tri, lo)


def _exact_dot(x, sel):
    hi, mid, lo = _split3(x)
    return _dot(hi, sel) + _dot(mid, sel) + _dot(lo, sel)


def _round_robin(gens, background=None, heavy_round=0):
    results = [None] * len(gens)
    live = list(range(len(gens)))
    rnd = 0
    while live:
        if background is not None and rnd == heavy_round:
            for _ in background:
                pass
        for n in list(live):
            try:
                next(gens[n])
            except StopIteration as stop:
                results[n] = stop.value
                live.remove(n)
        rnd += 1
    return results


def _hgrn_kernel(h_ref, w_ref, lb_ref, hn_ref, s0_ref, y_ref, snew_ref,
                 proj_s, st_s, *, tl, ni, seq_per_chunk):
    n = pl.program_id(0)
    n_chunks = tl // CHUNK
    i_lag = jnp.maximum(n - 1, 0) % ni
    nxt = n % 2
    cur = 1 - nxt

    @pl.when(n == 0)
    def _():
        proj_s[...] = jnp.zeros_like(proj_s)

    if not seq_per_chunk:
        @pl.when(i_lag == 0)
        def _():
            for hd in range(N_HEADS_A):
                st_s[hd] = s0_ref[0, hd].T

    n_col_blocks = w_ref.shape[0]
    cols_per_chunk = n_col_blocks // n_chunks

    def project(c):
        for j in range(cols_per_chunk):
            jb = c * cols_per_chunk + j
            res = _dot(h_ref[0], w_ref[jb])
            proj_s[nxt, 2 * jb] = res[:, :LANES]
            proj_s[nxt, 2 * jb + 1] = res[:, LANES:]
            yield

    tri = _tri_bf16(CHUNK)
    n_sub = CHUNK // SUB_BLOCK
    lane8 = lax.broadcasted_iota(jnp.int32, (SUBLANES, LANES), 1)
    sub8 = lax.broadcasted_iota(jnp.int32, (SUBLANES, LANES), 0)
    row64 = lax.broadcasted_iota(jnp.int32, (CHUNK, CHUNK), 0)
    col64 = lax.broadcasted_iota(jnp.int32, (CHUNK, CHUNK), 1)
    below_diag_block = row64 // SUB_BLOCK > col64 // SUB_BLOCK
    zero8 = jnp.zeros((SUBLANES, LANES), F32)

    def zeros(n):
        return jnp.zeros((n, LANES), F32)

    def piece(a, m, u):
        lo = SUB_BLOCK * m + SUBLANES * u
        return a[lo:lo + SUBLANES]

    def unit(hd, r0):
        rows = pl.ds(r0, CHUNK)
        aq = proj_s[cur, hd, rows, :]
        z = proj_s[cur, N_HEADS_A + hd, rows, :]
        v = proj_s[cur, 2 * N_HEADS_A + hd, rows, :].astype(BF16)
        ag = proj_s[cur, 3 * N_HEADS_A + hd, rows, :]
        log_lb = lb_ref[0, hd]
        log1m_lb = lb_ref[1, hd]

        z2 = z * LOG2E
        log_sig = jnp.minimum(z2, 0.0) - jnp.log2(1.0 + jnp.exp2(-jnp.abs(z2)))
        cterm = log1m_lb + log_sig
        log_f = jnp.maximum(log_lb, cterm) + jnp.log2(1.0 + jnp.exp2(-jnp.abs(log_lb - cterm)))
        log_k = cterm - z2
        q = _silu(aq)

        b2 = _cumsum_rows(tri, log_f)
        yield
        c2 = b2 - log_k
        b2_last = b2[CHUNK - 1:CHUNK, :]
        st = st_s[hd]

        o = _dot_nt((q * jnp.exp2(b2)).astype(BF16), st.astype(BF16))
        yield

        q_slabs, k_slabs = [], []
        for m in range(1, n_sub):
            lo = SUB_BLOCK * m
            ref = b2[lo - 1:lo, :]
            qm = q[lo:lo + SUB_BLOCK] * jnp.exp2(b2[lo:lo + SUB_BLOCK] - ref)
            q_parts = [zeros(lo), qm] + ([zeros(CHUNK - lo - SUB_BLOCK)] if lo + SUB_BLOCK < CHUNK else [])
            q_slabs.append(jnp.concatenate(q_parts, axis=0))
            k_slabs.append(jnp.concatenate([jnp.exp2(ref - c2[:lo]), zeros(CHUNK - lo)], axis=0))
        a_off = _dot_nt(jnp.concatenate(q_slabs, axis=1).astype(BF16),
                        jnp.concatenate(k_slabs, axis=1).astype(BF16))
        yield

        order = [(s, m, u) for s in range(SUB_BLOCK) for m in range(n_sub) for u in range(2)
                 if not (u == 0 and s >= SUBLANES)]
        pieces = []
        for s, m, u in order:
            cs = c2[SUB_BLOCK * m + s:SUB_BLOCK * m + s + 1]
            pieces.append(piece(q, m, u) * jnp.exp2(piece(b2, m, u) - cs))
        sums = [jnp.sum(p, axis=-1, keepdims=True) for p in pieces]
        yield
        d = {(m, u): zero8 for m in range(n_sub) for u in range(2)}
        for n, (s, m, u) in enumerate(order):
            d[(m, u)] = jnp.where(lane8 == s, sums[n], d[(m, u)])
        diag_rows = []
        for m in range(n_sub):
            for u in range(2):
                dm = jnp.where(sub8 + SUBLANES * u >= lane8, d[(m, u)], 0.0)
                diag_rows.append(pltpu.roll(dm, SUB_BLOCK * m, 1) if m else dm)
        a_diag = jnp.concatenate(diag_rows, axis=0)[:, :CHUNK]
        a_full = jnp.where(below_diag_block, a_off, a_diag)
        o = o + _dot(a_full.astype(BF16), v)
        yield

        r = lax.rsqrt(jnp.mean(o * o, axis=-1, keepdims=True) + EPS)
        y = ((o * r) * hn_ref[...]) * _silu(ag)

        st_new = st * jnp.exp2(b2_last) + _dot_tn(v, jnp.exp2(b2_last - c2).astype(BF16))
        return y.astype(BF16), st_new

    def body(c, carry):
        r0 = pl.multiple_of(c * CHUNK, CHUNK)
        if seq_per_chunk:
            for hd in range(N_HEADS_A):
                st_s[hd] = s0_ref[c, hd].T
        gens = [unit(hd, r0) for hd in range(N_HEADS_A)] + [project(c)]
        results = _round_robin(gens)[:N_HEADS_A]
        for hd, (y, st_new) in enumerate(results):
            y_ref[0, pl.ds(r0, CHUNK), LANES * hd:LANES * (hd + 1)] = y
            if seq_per_chunk:
                snew_ref[c, hd] = st_new.T
            else:
                st_s[hd] = st_new
        return carry

    lax.fori_loop(0, n_chunks, body, 0)

    if not seq_per_chunk:
        @pl.when(i_lag == ni - 1)
        def _():
            for hd in range(N_HEADS_A):
                snew_ref[0, hd] = st_s[hd].T


def _lagged_maps(n_blocks, ni):
    def split(m):
        return m // ni, m % ni
    x_map = lambda n: split(jnp.minimum(n, n_blocks - 1)) + (0,)
    y_map = lambda n: split(jnp.maximum(n - 1, 0)) + (0,)
    pos_map = lambda n: (jnp.minimum(n, n_blocks - 1) % ni, 0)
    batch_map = lambda nd: (lambda n: (jnp.maximum(n - 1, 0) // ni,) + (0,) * (nd - 1))
    return x_map, y_map, pos_map, batch_map


def _hgrn_call(h, w_a, lb_pack, hn, s0, tl):
    n_seq, seq, _ = h.shape
    h, tl, group, seq_per_chunk = _batch_one_chunk_sequences(h, tl)
    bsz = h.shape[0]
    ni = h.shape[1] // tl
    n_blocks = bsz * ni
    x_map, y_map, _, batch_map = _lagged_maps(n_blocks, ni)
    const2 = lambda n: (0, 0)
    state_spec = pl.BlockSpec((group, N_HEADS_A, HEAD_K_A, HEAD_K_A), batch_map(4))
    y, s_new = pl.pallas_call(
        functools.partial(_hgrn_kernel, tl=tl, ni=ni, seq_per_chunk=seq_per_chunk),
        grid=(n_blocks + 1,),
        in_specs=[
            pl.BlockSpec((1, tl, D_MODEL), x_map),
            pl.BlockSpec(w_a.shape, lambda n: (0, 0, 0), pipeline_mode=pl.Buffered(1)),
            pl.BlockSpec((2, N_HEADS_A, 1, HEAD_K_A), lambda n: (0, 0, 0, 0)),
            pl.BlockSpec((1, HEAD_K_A), const2),
            state_spec,
        ],
        out_specs=[pl.BlockSpec((1, tl, W_A), y_map), state_spec],
        out_shape=[
            jax.ShapeDtypeStruct((bsz, ni * tl, W_A), BF16),
            jax.ShapeDtypeStruct((n_seq, N_HEADS_A, HEAD_K_A, HEAD_K_A), F32),
        ],
        scratch_shapes=[
            pltpu.VMEM((2, 4 * N_HEADS_A, tl, LANES), F32),
            pltpu.VMEM((N_HEADS_A, HEAD_K_A, HEAD_K_A), F32),
        ],
        compiler_params=pltpu.CompilerParams(
            dimension_semantics=("arbitrary",), vmem_limit_bytes=VMEM_LIMIT),
        name="hgrn_mixer",
    )(h, w_a, lb_pack, hn, s0)
    return y.reshape(n_seq, seq, W_A), s_new


def _rotate(xs, cos, sin_a, sin_b):
    return xs * cos + pltpu.roll(xs, LANES - ROT_DIM // 2, 1) * sin_a + pltpu.roll(xs, ROT_DIM // 2, 1) * sin_b


def _swa_kernel(*refs, tl, ni, n_blocks, has_cache, from_x, seq_per_chunk):
    it = iter(refs)
    if from_x:
        x_ref, nw_ref = next(it), next(it)
    else:
        h_ref = next(it)
    w_ref, cos_ref, sa_ref, sb_ref, sink_ref = (next(it) for _ in range(5))
    if has_cache:
        kc_ref, vc_ref = next(it), next(it)
    y_ref, knew_ref, vnew_ref = (next(it) for _ in range(3))
    if from_x:
        h_ref = next(it)
    q_s, g_s, kbuf, vbuf = (next(it) for _ in range(4))
    n = pl.program_id(0)
    n_chunks = tl // CHUNK
    band = WINDOW + CHUNK
    kvw = 2 * KV_W_B
    i_proj = jnp.minimum(n, n_blocks - 1) % ni
    i_lag = jnp.maximum(n - 1, 0) % ni
    nxt = n % 2
    cur = 1 - nxt
    n_seg = n_chunks if seq_per_chunk else 1
    seg_rows = tl // n_seg
    seg_prev = [slice((WINDOW + seg_rows) * s, (WINDOW + seg_rows) * s + WINDOW) for s in range(n_seg)]
    seg_new = [slice(sp.stop, sp.stop + seg_rows) for sp in seg_prev]
    seg_last = [slice(sn.stop - WINDOW, sn.stop) for sn in seg_new]

    @pl.when(n == 0)
    def _():
        for buf in (q_s, g_s, kbuf, vbuf):
            buf[...] = jnp.zeros_like(buf)

    if seq_per_chunk:
        for s in range(n_seg):
            kbuf[nxt, seg_prev[s], :] = kc_ref[s]
            vbuf[nxt, seg_prev[s], :] = vc_ref[s]
    else:
        @pl.when(i_proj == 0)
        def _():
            if has_cache:
                kbuf[nxt, seg_prev[0], :] = kc_ref[0]
                vbuf[nxt, seg_prev[0], :] = vc_ref[0]
            else:
                kbuf[nxt, seg_prev[0], :] = jnp.zeros((WINDOW, kvw), F32)
                vbuf[nxt, seg_prev[0], :] = jnp.zeros((WINDOW, kvw), F32)

        @pl.when(i_proj != 0)
        def _():
            kbuf[nxt, seg_prev[0], :] = kbuf[cur, seg_last[0], :]
            vbuf[nxt, seg_prev[0], :] = vbuf[cur, seg_last[0], :]

    if from_x:
        h_ref[0] = _prenorm_bf16(x_ref[0], nw_ref[...])

    def store_new(buf, cols, val):
        for s in range(n_seg):
            buf[nxt, seg_new[s], cols] = val[seg_rows * s:seg_rows * (s + 1)]

    def project(c):
        per = (W_B // MXU_COLS) // n_chunks
        for j in range(c * per, (c + 1) * per):
            sl = slice(MXU_COLS * j, MXU_COLS * (j + 1))
            res = _dot(h_ref[0], w_ref[:, sl])
            yield
            for u in range(MXU_COLS // LANES):
                q_s[nxt, :, MXU_COLS * j + LANES * u:MXU_COLS * j + LANES * (u + 1)] = _rotate(
                    res[:, LANES * u:LANES * (u + 1)], cos_ref[...], sa_ref[...], sb_ref[...])
            g_s[nxt, :, sl] = _dot(h_ref[0], w_ref[:, W_B + 2 * KV_W_B + MXU_COLS * j:W_B + 2 * KV_W_B + MXU_COLS * (j + 1)])
            yield
            if j < 2:
                res = _dot(h_ref[0], w_ref[:, W_B + KV_W_B * j:W_B + KV_W_B * (j + 1)])
                yield
                for u in range(KV_W_B // LANES):
                    a = res[:, LANES * u:LANES * (u + 1)]
                    if j == 0:
                        a = _rotate(a, cos_ref[...], sa_ref[...], sb_ref[...])
                    swapped = pltpu.roll(a, HEAD_DIM_B, 1)
                    buf = kbuf if j == 0 else vbuf
                    store_new(buf, slice(LANES * 2 * u, LANES * (2 * u + 1)), jnp.where(low1, a, swapped))
                    store_new(buf, slice(LANES * (2 * u + 1), LANES * (2 * u + 2)), jnp.where(low1, swapped, a))

    lane = lax.broadcasted_iota(jnp.int32, (CHUNK, LANES), 1)
    low = lane < HEAD_DIM_B
    low1 = lax.broadcasted_iota(jnp.int32, (1, LANES), 1) < HEAD_DIM_B
    key_row = lax.broadcasted_iota(jnp.int32, (band, LANES), 0)

    def pair(c, j):
        r0 = CHUNK * c
        g = j // 2
        qp = q_s[cur, r0:r0 + CHUNK, LANES * j:LANES * (j + 1)]
        q2 = jnp.concatenate([jnp.where(low, qp, 0.0), jnp.where(low, 0.0, qp)], axis=0).astype(BF16)
        b0 = band * c if seq_per_chunk else r0
        kb = kbuf[cur, b0:b0 + band, LANES * g:LANES * (g + 1)].astype(BF16)
        vb = vbuf[cur, b0:b0 + band, LANES * g:LANES * (g + 1)].astype(BF16)
        s = _dot_nt(kb, q2) * ATTN_SCALE
        yield
        if not has_cache:
            s = jnp.where(i_lag * tl + r0 - WINDOW + key_row >= 0, s, -jnp.inf)
        sk = jnp.where(low1, sink_ref[2 * j:2 * j + 1, :], sink_ref[2 * j + 1:2 * j + 2, :])
        m = jnp.maximum(jnp.max(s, axis=0, keepdims=True), sk)
        p = jnp.exp(s - m)
        den = jnp.sum(p, axis=0, keepdims=True) + jnp.exp(sk - m)
        o2 = _dot_tn((p * (1.0 / den)).astype(BF16), vb)
        yield
        o = jnp.where(low, o2[:CHUNK], o2[CHUNK:])
        y = o * _silu(g_s[cur, r0:r0 + CHUNK, LANES * j:LANES * (j + 1)])
        y_ref[0, r0:r0 + CHUNK, LANES * j:LANES * (j + 1)] = y.astype(BF16)

    for c in range(n_chunks):
        _round_robin([pair(c, j) for j in range(N_Q_B // 2)], project(c), heavy_round=1)

    def write_newest(s):
        ktail = kbuf[nxt, seg_last[s], :]
        vtail = vbuf[nxt, seg_last[s], :]
        low_w = lax.broadcasted_iota(jnp.int32, (WINDOW, LANES), 1) < HEAD_DIM_B
        for u in range(KV_W_B // LANES):
            knew_ref[s, :, LANES * u:LANES * (u + 1)] = jnp.where(
                low_w, ktail[:, 2 * LANES * u:2 * LANES * u + LANES],
                ktail[:, 2 * LANES * u + LANES:2 * LANES * (u + 1)])
            vnew_ref[s, :, LANES * u:LANES * (u + 1)] = jnp.where(
                low_w, vtail[:, 2 * LANES * u:2 * LANES * u + LANES],
                vtail[:, 2 * LANES * u + LANES:2 * LANES * (u + 1)])

    if seq_per_chunk:
        for s in range(n_seg):
            write_newest(s)
    else:
        pl.when(i_proj == ni - 1)(lambda: write_newest(0))


def _swa_call(h, x_norm, w_b, cos, sin_a, sin_b, sinks, kc, vc, tl):
    from_x = h is None
    act = x_norm[0] if from_x else h
    n_seq, seq, _ = act.shape
    has_cache = kc is not None
    group, seq_per_chunk = 1, False
    if has_cache:
        act, tl, group, seq_per_chunk = _batch_one_chunk_sequences(act, tl)
    if seq_per_chunk:
        cos, sin_a, sin_b = (jnp.tile(t, (group, 1)) for t in (cos, sin_a, sin_b))
    bsz = act.shape[0]
    ni = act.shape[1] // tl
    n_blocks = bsz * ni
    x_map, y_map, pos_map, _ = _lagged_maps(n_blocks, ni)
    proj_batch_map = lambda n: (jnp.minimum(n, n_blocks - 1) // ni, 0, 0)
    const2 = lambda n: (0, 0)
    wcols = w_b.shape[1]
    act_spec = pl.BlockSpec((1, tl, D_MODEL), x_map)
    in_specs = [act_spec, pl.BlockSpec((1, D_MODEL), const2)] if from_x else [act_spec]
    args = [act, x_norm[1]] if from_x else [act]
    in_specs += [
        pl.BlockSpec((D_MODEL, wcols), const2, pipeline_mode=pl.Buffered(1)),
        pl.BlockSpec((tl, LANES), pos_map),
        pl.BlockSpec((tl, LANES), pos_map),
        pl.BlockSpec((tl, LANES), pos_map),
        pl.BlockSpec((N_Q_B, LANES), const2),
    ]
    args += [w_b, cos, sin_a, sin_b, sinks]
    if has_cache:
        in_specs += [pl.BlockSpec((group, WINDOW, 2 * KV_W_B), proj_batch_map)] * 2
        args += [kc, vc]
    out_specs = [
        pl.BlockSpec((1, tl, W_B), y_map),
        pl.BlockSpec((group, WINDOW, KV_W_B), proj_batch_map),
        pl.BlockSpec((group, WINDOW, KV_W_B), proj_batch_map),
    ]
    out_shape = [
        jax.ShapeDtypeStruct((bsz, ni * tl, W_B), BF16),
        jax.ShapeDtypeStruct((n_seq, WINDOW, KV_W_B), F32),
        jax.ShapeDtypeStruct((n_seq, WINDOW, KV_W_B), F32),
    ]
    if from_x:
        out_specs.append(act_spec)
        out_shape.append(jax.ShapeDtypeStruct((bsz, ni * tl, D_MODEL), BF16))
    kv_rows = group * WINDOW + tl
    outs = pl.pallas_call(
        functools.partial(_swa_kernel, tl=tl, ni=ni, n_blocks=n_blocks, has_cache=has_cache, from_x=from_x,
                          seq_per_chunk=seq_per_chunk),
        grid=(n_blocks + 1,),
        in_specs=in_specs,
        out_specs=out_specs,
        out_shape=out_shape,
        scratch_shapes=[
            pltpu.VMEM((2, tl, W_B), F32),
            pltpu.VMEM((2, tl, W_B), F32),
            pltpu.VMEM((2, kv_rows, 2 * KV_W_B), F32),
            pltpu.VMEM((2, kv_rows, 2 * KV_W_B), F32),
        ],
        compiler_params=pltpu.CompilerParams(
            dimension_semantics=("arbitrary",), vmem_limit_bytes=VMEM_LIMIT),
        name="swa_mixer",
    )(*args)
    outs = list(outs)
    outs[0] = outs[0].reshape(n_seq, seq, W_B)
    if from_x:
        outs[3] = outs[3].reshape(n_seq, seq, D_MODEL)
    return outs


CONV_PAD = 8


def _ssd_kernel(h_ref, w_ref, cw_ref, cb_ref, dtb_ref, alog_ref, dsk_ref, sn_ref, conv0_ref, s0_ref,
                y_ref, snew_ref, convnew_ref,
                ubuf, dt_s, z_s, xc_s, xdt_s, bc_s, la_s, st_s, *, tl, ni, n_blocks, seq_per_chunk):
    n = pl.program_id(0)
    n_chunks = tl // CHUNK
    gw = N_GROUPS_C * D_STATE
    hp = W_C // N_GROUPS_C
    i_proj = jnp.minimum(n, n_blocks - 1) % ni
    i_lag = jnp.maximum(n - 1, 0) % ni
    nxt = n % 2
    cur = 1 - nxt
    n_seg = n_chunks if seq_per_chunk else 1
    seg_rows = tl // n_seg
    seg_new = [slice((CONV_PAD + seg_rows) * s + CONV_PAD, (CONV_PAD + seg_rows) * (s + 1)) for s in range(n_seg)]
    seg_tail = [slice(sn.start - (CONV_W - 1), sn.start) for sn in seg_new]
    seg_last = [slice(sn.stop - (CONV_W - 1), sn.stop) for sn in seg_new]

    @pl.when(n == 0)
    def _():
        for buf in (ubuf, z_s, xc_s, xdt_s, bc_s, la_s):
            buf[...] = jnp.zeros_like(buf)

    if seq_per_chunk:
        for s in range(n_seg):
            ubuf[seg_tail[s], :] = conv0_ref[s]
    else:
        @pl.when(i_proj == 0)
        def _():
            ubuf[seg_tail[0], :] = conv0_ref[0]

        @pl.when(i_lag == 0)
        def _():
            for g in range(N_GROUPS_C):
                st_s[g] = s0_ref[0, hp * g:hp * (g + 1), :].T

    def store_raw(cols, res):
        for s in range(n_seg):
            ubuf[seg_new[s], cols] = res[seg_rows * s:seg_rows * (s + 1)]

    def conv_act(sl):
        win = ubuf[:, sl]
        acc = cb_ref[:, sl]
        for t in range(CONV_W):
            back = CONV_W - 1 - t
            src = pltpu.roll(win, back, 0) if back else win
            acc = acc + src * cw_ref[t:t + 1, sl]
        if not seq_per_chunk:
            ubuf[seg_tail[0], sl] = ubuf[seg_last[0], sl]
        return _silu(jnp.concatenate([acc[sn] for sn in seg_new], axis=0) if n_seg > 1 else acc[seg_new[0]])

    def project(c):
        per = (W_C // MXU_COLS) // n_chunks
        for j in range(c * per, (c + 1) * per):
            sl = slice(MXU_COLS * j, MXU_COLS * (j + 1))
            z_s[nxt, :, sl] = _dot(h_ref[0], w_ref[:, sl])
            yield
            if j == 0:
                dt_raw = _dot(h_ref[0], w_ref[:, W_C + CONV_DIM:W_C + CONV_DIM + LANES])
                yield
                dt_s[...] = _softplus0(dt_raw + dtb_ref[...])
            head_of_col = (MXU_COLS * j + lax.broadcasted_iota(jnp.int32, (LANES, MXU_COLS), 1)) // HEAD_DIM_C
            spread = (lax.broadcasted_iota(jnp.int32, (LANES, MXU_COLS), 0) == head_of_col).astype(BF16)
            dt = _exact_dot(dt_s[...], spread)
            yield
            la_s[nxt, :, sl] = -dt * jnp.exp(alog_ref[:, sl])
            store_raw(sl, _dot(h_ref[0], w_ref[:, W_C + MXU_COLS * j:W_C + MXU_COLS * (j + 1)]))
            yield
            for u in range(MXU_COLS // LANES):
                usl = slice(MXU_COLS * j + LANES * u, MXU_COLS * j + LANES * (u + 1))
                act = conv_act(usl)
                xc_s[nxt, :, usl] = act
                xdt_s[nxt, :, usl] = act * dt[:, LANES * u:LANES * (u + 1)]
            bsl = slice(W_C + MXU_COLS * j, W_C + MXU_COLS * (j + 1))
            store_raw(bsl, _dot(h_ref[0], w_ref[:, W_C + W_C + MXU_COLS * j:W_C + W_C + MXU_COLS * (j + 1)]))
            yield
            for u in range(MXU_COLS // LANES):
                usl = slice(W_C + MXU_COLS * j + LANES * u, W_C + MXU_COLS * j + LANES * (u + 1))
                bc_s[nxt, :, MXU_COLS * j + LANES * u:MXU_COLS * j + LANES * (u + 1)] = conv_act(usl)

    tri = _tri_bf16(CHUNK)
    trow = lax.broadcasted_iota(jnp.int32, (CHUNK, LANES), 0)
    lane = lax.broadcasted_iota(jnp.int32, (CHUNK, LANES), 1)
    s_of_lane = lane % CHUNK
    causal = trow >= s_of_lane
    row2 = lax.broadcasted_iota(jnp.int32, (2 * CHUNK, LANES), 0)
    lane2 = lax.broadcasted_iota(jnp.int32, (2 * CHUNK, LANES), 1)
    blockdiag = (row2 < CHUNK) == (lane2 < HEAD_DIM_C)

    for c in range(n_chunks):
        rows = slice(CHUNK * c, CHUNK * (c + 1))

        def group(g, rows=rows):
            gsl = slice(hp * g, hp * (g + 1))
            bg = bc_s[cur, rows, D_STATE * g:D_STATE * (g + 1)].astype(BF16)
            cg = bc_s[cur, rows, gw + D_STATE * g:gw + D_STATE * (g + 1)].astype(BF16)
            cb2 = _dot_nt(cg, jnp.concatenate([bg, bg], axis=0))
            yield
            st = st_s[g]
            y_state = _dot(cg, st.astype(BF16))
            yield
            la = la_s[cur, rows, gsl]
            b = _cumsum_rows(tri, la)
            yield
            xdt = xdt_s[cur, rows, gsl]
            ys = []
            for u in range(2):
                usl = slice(LANES * u, LANES * (u + 1))
                b_row = jnp.sum(jnp.where(trow <= s_of_lane, la[:, usl], 0.0), axis=0, keepdims=True)
                decay = jnp.where(causal, jnp.exp(b[:, usl] - b_row), 0.0)
                wmat = (cb2 * decay).astype(BF16)
                xbd = jnp.where(blockdiag, jnp.concatenate([xdt[:, usl], xdt[:, usl]], axis=0), 0.0).astype(BF16)
                ys.append(_dot(wmat, xbd))
                yield
            y = jnp.concatenate(ys, axis=1) + y_state * jnp.exp(b)
            o = (y + dsk_ref[:, gsl] * xc_s[cur, rows, gsl]) * _silu(z_s[cur, rows, gsl])
            r = lax.rsqrt(jnp.mean(o * o, axis=-1, keepdims=True) + EPS)
            out = ((o * r) * sn_ref[:, gsl]).astype(BF16)
            b_last = b[CHUNK - 1:CHUNK, :]
            st_new = st * jnp.exp(b_last) + _dot_tn(bg, (xdt * jnp.exp(b_last - b)).astype(BF16))
            return out, st_new

        if seq_per_chunk:
            for g in range(N_GROUPS_C):
                st_s[g] = s0_ref[c, hp * g:hp * (g + 1), :].T
        results = _round_robin([group(g) for g in range(N_GROUPS_C)] + [project(c)])[:N_GROUPS_C]
        for g, (out, st_new) in enumerate(results):
            y_ref[0, rows, hp * g:hp * (g + 1)] = out
            if seq_per_chunk:
                snew_ref[c, hp * g:hp * (g + 1), :] = st_new.T
            else:
                st_s[g] = st_new

    if seq_per_chunk:
        for s in range(n_seg):
            convnew_ref[s] = ubuf[seg_last[s], :]
    else:
        @pl.when(i_proj == ni - 1)
        def _():
            convnew_ref[0] = ubuf[seg_tail[0], :]

        @pl.when(i_lag == ni - 1)
        def _():
            for g in range(N_GROUPS_C):
                snew_ref[0, hp * g:hp * (g + 1), :] = st_s[g].T


def _batch_one_chunk_sequences(h, tl):
    n_seq, seq, _ = h.shape
    group = PROMPT_ROWS // CHUNK
    if seq == CHUNK and n_seq % group == 0:
        return h.reshape(n_seq // group, PROMPT_ROWS, D_MODEL), PROMPT_ROWS, group, True
    return h, tl, 1, False


def _ssd_call(h, w_c, cw, cb, dtb, alog, dsk, sn, conv0, s0, tl):
    n_seq, seq, _ = h.shape
    h, tl, group, seq_per_chunk = _batch_one_chunk_sequences(h, tl)
    bsz = h.shape[0]
    ni = h.shape[1] // tl
    n_blocks = bsz * ni
    x_map, y_map, _, batch_map = _lagged_maps(n_blocks, ni)
    proj_batch_map = lambda n: (jnp.minimum(n, n_blocks - 1) // ni, 0, 0)
    const2 = lambda n: (0, 0)
    wcols = w_c.shape[1]
    y, s_new, conv_new = pl.pallas_call(
        functools.partial(_ssd_kernel, tl=tl, ni=ni, n_blocks=n_blocks, seq_per_chunk=seq_per_chunk),
        grid=(n_blocks + 1,),
        in_specs=[
            pl.BlockSpec((1, tl, D_MODEL), x_map),
            pl.BlockSpec((D_MODEL, wcols), const2, pipeline_mode=pl.Buffered(1)),
            pl.BlockSpec((CONV_W, CONV_DIM), const2),
            pl.BlockSpec((1, CONV_DIM), const2),
            pl.BlockSpec((1, LANES), const2),
            pl.BlockSpec((1, W_C), const2),
            pl.BlockSpec((1, W_C), const2),
            pl.BlockSpec((1, W_C), const2),
            pl.BlockSpec((group, CONV_W - 1, CONV_DIM), proj_batch_map),
            pl.BlockSpec((group, W_C, D_STATE), batch_map(3)),
        ],
        out_specs=[
            pl.BlockSpec((1, tl, W_C), y_map),
            pl.BlockSpec((group, W_C, D_STATE), batch_map(3)),
            pl.BlockSpec((group, CONV_W - 1, CONV_DIM), proj_batch_map),
        ],
        out_shape=[
            jax.ShapeDtypeStruct((bsz, ni * tl, W_C), BF16),
            jax.ShapeDtypeStruct((n_seq, W_C, D_STATE), F32),
            jax.ShapeDtypeStruct((n_seq, CONV_W - 1, CONV_DIM), F32),
        ],
        scratch_shapes=[
            pltpu.VMEM((group * CONV_PAD + tl, CONV_DIM), F32),
            pltpu.VMEM((tl, LANES), F32),
            pltpu.VMEM((2, tl, W_C), F32),
            pltpu.VMEM((2, tl, W_C), F32),
            pltpu.VMEM((2, tl, W_C), F32),
            pltpu.VMEM((2, tl, 2 * N_GROUPS_C * D_STATE), F32),
            pltpu.VMEM((2, tl, W_C), F32),
            pltpu.VMEM((N_GROUPS_C, D_STATE, W_C // N_GROUPS_C), F32),
        ],
        compiler_params=pltpu.CompilerParams(
            dimension_semantics=("arbitrary",), vmem_limit_bytes=VMEM_LIMIT),
        name="ssd_mixer",
    )(h, w_c, cw, cb, dtb, alog, dsk, sn, conv0, s0)
    return y.reshape(n_seq, seq, W_C), s_new, conv_new


def _merge_kernel(*refs, emit_next):
    if emit_next:
        (x_ref, h_ref, ya_ref, yb_ref, yc_ref, npost_ref, nnext_ref, wg_ref, wbr_ref, wout_ref,
         o_ref, hnext_ref, m_s) = refs
    else:
        (x_ref, h_ref, ya_ref, yb_ref, yc_ref, npost_ref, wg_ref, wbr_ref, wout_ref, o_ref, m_s) = refs
    j = pl.program_id(1)
    n_j = m_s.shape[0]

    h = h_ref[...]
    merged = None
    for bi, y_ref in enumerate((ya_ref, yb_ref, yc_ref)):
        term = _sigmoid(_dot(h, wg_ref[bi])) * _dot(y_ref[...], wbr_ref[bi])
        merged = term if merged is None else merged + term
    m_s[j] = merged.astype(BF16)

    @pl.when(j == n_j - 1)
    def _():
        out = _dot(jnp.concatenate([m_s[n] for n in range(n_j)], axis=1), wout_ref[...])
        r = lax.rsqrt(jnp.mean(out * out, axis=-1, keepdims=True) + EPS)
        x_new = x_ref[...] + (out * r) * npost_ref[...]
        o_ref[...] = x_new
        if emit_next:
            hnext_ref[...] = _prenorm_bf16(x_new, nnext_ref[...])


def _merge_call(x2, h2, ya, yb, yc, npost, nnext, wg, wbr, wout, tm, tn):
    rows = x2.shape[0]
    emit_next = nnext is not None
    grid = (rows // tm, D_MODEL // tn)
    const2 = lambda r, j: (0, 0)
    row_block = lambda width: pl.BlockSpec((tm, width), lambda r, j: (r, 0))
    vec = pl.BlockSpec((1, D_MODEL), const2)
    in_specs = [row_block(D_MODEL), row_block(D_MODEL), row_block(W_A), row_block(W_B), row_block(W_C), vec]
    args = [x2, h2, ya, yb, yc, npost]
    if emit_next:
        in_specs.append(vec)
        args.append(nnext)
    in_specs += [
        pl.BlockSpec((3, D_MODEL, tn), lambda r, j: (0, 0, j)),
        pl.BlockSpec((3, W_A, tn), lambda r, j: (0, 0, j)),
        pl.BlockSpec((D_MODEL, D_MODEL), const2, pipeline_mode=pl.Buffered(1)),
    ]
    args += [wg, wbr, wout]
    out_specs = [row_block(D_MODEL)]
    out_shape = [jax.ShapeDtypeStruct((rows, D_MODEL), F32)]
    if emit_next:
        out_specs.append(row_block(D_MODEL))
        out_shape.append(jax.ShapeDtypeStruct((rows, D_MODEL), BF16))
    outs = pl.pallas_call(
        functools.partial(_merge_kernel, emit_next=emit_next),
        grid=grid,
        in_specs=in_specs,
        out_specs=out_specs,
        out_shape=out_shape,
        scratch_shapes=[pltpu.VMEM((D_MODEL // tn, tm, tn), BF16)],
        compiler_params=pltpu.CompilerParams(
            dimension_semantics=("arbitrary", "arbitrary"), vmem_limit_bytes=VMEM_LIMIT),
        name="merge_out",
    )(*args)
    return (outs[0], outs[1]) if emit_next else (outs[0], None)


def _rope_tables(pos):
    half = ROT_DIM // 2
    inv = jnp.power(ROPE_THETA, -jnp.arange(half, dtype=F32) / half)
    ang = pos.astype(F32)[:, None] * inv[None, :]
    cos, sin = jnp.cos(ang), jnp.sin(ang)
    n = pos.shape[0]
    ones = jnp.ones((n, HEAD_DIM_B - ROT_DIM), F32)
    zeros = jnp.zeros((n, HEAD_DIM_B - ROT_DIM), F32)
    zh = jnp.zeros((n, half), F32)
    cos_t = jnp.concatenate([cos, cos, ones], axis=1)
    sin_a = jnp.concatenate([-sin, zh, zeros], axis=1)
    sin_b = jnp.concatenate([zh, sin, zeros], axis=1)
    tile = lambda t: jnp.concatenate([t, t], axis=1)
    return tile(cos_t), tile(sin_a), tile(sin_b)


def _dup_heads(t):
    lead = t.shape[:-1]
    t4 = t.reshape(lead + (N_KV_B, 1, HEAD_DIM_B))
    return jnp.broadcast_to(t4, lead + (N_KV_B, 2, HEAD_DIM_B)).reshape(lead + (2 * KV_W_B,))


def _col_blocks(w):
    return w.reshape(w.shape[0], -1, MXU_COLS).transpose(1, 0, 2)


def _pad_lanes(t):
    return jnp.pad(t, [(0, 0)] * (t.ndim - 1) + [(0, LANES - t.shape[-1])])


def _expand_heads(t):
    return jnp.repeat(t, HEAD_DIM_C, axis=-1)


def _layer(x, h, tl, tables, caches, lw, norm_next):
    bsz, seq, _ = x.shape
    cos, sin_a, sin_b = tables
    kc, vc, s_hgrn, s_ssm, s_conv = caches
    if s_hgrn is None:
        s_hgrn = jnp.zeros((bsz, N_HEADS_A, HEAD_K_A, HEAD_K_A), F32)
        s_ssm = jnp.zeros((bsz, W_C, D_STATE), F32)
        s_conv = jnp.zeros((bsz, CONV_W - 1, CONV_DIM), F32)
    else:
        s_ssm = s_ssm.reshape(bsz, W_C, D_STATE)
        kc = _dup_heads(kc.reshape(bsz, WINDOW, KV_W_B))
        vc = _dup_heads(vc.reshape(bsz, WINDOW, KV_W_B))

    if h is None:
        yb, k_new, v_new, h = _swa_call(None, (x, lw["norm_pre"]), lw["w_b"], cos, sin_a, sin_b, lw["sinks"],
                                        kc, vc, tl)
    else:
        yb, k_new, v_new = _swa_call(h, None, lw["w_b"], cos, sin_a, sin_b, lw["sinks"], kc, vc, tl)
    ya, hgrn_new = _hgrn_call(h, lw["w_a"], lw["lb_pack"], lw["hgrn_norm"], s_hgrn, tl)
    yc, ssm_new, conv_new = _ssd_call(h, lw["w_c"], lw["conv_w"], lw["conv_b"], lw["dt_bias"],
                                      lw["a_log"], lw["d_skip"], lw["ssm_norm"], s_conv, s_ssm, tl)
    rows = bsz * seq
    x_new, h_next = _merge_call(x.reshape(rows, D_MODEL), h.reshape(rows, D_MODEL), ya.reshape(rows, W_A),
                                yb.reshape(rows, W_B), yc.reshape(rows, W_C), lw["norm_post"], norm_next,
                                lw["w_g"], lw["w_br"], lw["w_out"], min(MERGE_ROWS, rows), MXU_COLS)
    states = (k_new.reshape(bsz, WINDOW, N_KV_B, HEAD_DIM_B), v_new.reshape(bsz, WINDOW, N_KV_B, HEAD_DIM_B),
              hgrn_new, ssm_new.reshape(bsz, N_HEADS_C, HEAD_DIM_C, D_STATE), conv_new)
    if h_next is not None:
        h_next = h_next.reshape(bsz, seq, D_MODEL)
    return x_new.reshape(bsz, seq, D_MODEL), h_next, states


def kernel(x_prompt, x_sample, cache_swa_k, cache_swa_v, state_hgrn, state_ssm, state_conv, norm_pre, norm_post, w_in, hgrn_lb_logits, hgrn_norm, swa_sinks, conv_w, conv_b, dt_bias, a_log, d_skip, ssm_norm, w_branch_a, w_branch_b, w_branch_c, w_out):
    depth = w_in.shape[0]
    lbp = jax.nn.softmax(hgrn_lb_logits.astype(F32), axis=0)
    lbc = jnp.cumsum(lbp, axis=0)
    lb_all = lbc - lbc[0:1]

    offs = np.cumsum([0, W_A, W_A, W_A, W_A, W_B, KV_W_B, KV_W_B, W_B, W_C, CONV_DIM, N_HEADS_C,
                      D_MODEL, D_MODEL, D_MODEL])
    col = lambda l, a, b: w_in[l, :, int(offs[a]):int(offs[b])]

    tables_p = _rope_tables(jnp.arange(x_prompt.shape[1], dtype=jnp.int32))
    tables_s = _rope_tables(PAST_LEN + jnp.arange(x_sample.shape[1], dtype=jnp.int32))

    xp, xs = x_prompt, x_sample
    pre = lambda l: norm_pre[l].reshape(1, D_MODEL)
    hp = hs = None
    pst, sst = [], []
    for l in range(depth):
        lb = lb_all[l].reshape(N_HEADS_A, 1, HEAD_K_A)
        norm_next = pre(l + 1) if l + 1 < depth else None
        lw = {
            "norm_pre": pre(l),
            "norm_post": norm_post[l].reshape(1, D_MODEL),
            "w_a": _col_blocks(col(l, 0, 4).astype(BF16)),
            "w_b": col(l, 4, 8).astype(BF16),
            "w_c": jnp.concatenate([col(l, 8, 9), col(l, 9, 10), _pad_lanes(col(l, 10, 11))],
                                   axis=1).astype(BF16),
            "w_g": jnp.stack([col(l, 11, 12), col(l, 12, 13), col(l, 13, 14)]).astype(BF16),
            "w_br": jnp.stack([w_branch_a[l], w_branch_b[l], w_branch_c[l]]).astype(BF16),
            "w_out": w_out[l].astype(BF16),
            "lb_pack": jnp.stack([jnp.log(lb), jnp.log1p(-lb)]) * LOG2E,
            "hgrn_norm": hgrn_norm[l].reshape(1, HEAD_K_A),
            "sinks": jnp.broadcast_to(swa_sinks[l].astype(F32)[:, None], (N_Q_B, LANES)),
            "conv_w": conv_w[l],
            "conv_b": conv_b[l].reshape(1, CONV_DIM),
            "dt_bias": _pad_lanes(dt_bias[l].astype(F32).reshape(1, N_HEADS_C)),
            "a_log": _expand_heads(a_log[l].astype(F32)).reshape(1, W_C),
            "d_skip": _expand_heads(d_skip[l].astype(F32)).reshape(1, W_C),
            "ssm_norm": ssm_norm[l].reshape(1, W_C),
        }
        xp, hp, sp = _layer(xp, hp, PROMPT_ROWS, tables_p, (None, None, None, None, None), lw, norm_next)
        xs, hs, ss = _layer(xs, hs, CHUNK, tables_s,
                            (cache_swa_k[l], cache_swa_v[l], state_hgrn[l], state_ssm[l], state_conv[l]),
                            lw, norm_next)
        pst.append(sp)
        sst.append(ss)

    stack = lambda sts, k: jnp.stack([s[k] for s in sts])
    return (xp, xs,
            stack(pst, 0), stack(pst, 1), stack(pst, 2), stack(pst, 3), stack(pst, 4),
            stack(sst, 0), stack(sst, 1), stack(sst, 2), stack(sst, 3), stack(sst, 4))
```

```python
import functools
import math

import jax
import jax.numpy as jnp
import numpy as np
from jax import lax
from jax.experimental import pallas as pl
from jax.experimental.pallas import tpu as pltpu

F32 = jnp.float32
BF16 = jnp.bfloat16

D_MODEL = 2048
CHUNK = 64
EPS = 1e-6
PAST_LEN = 4096

W_A = 1024
HEAD_K_A = 128
N_HEADS_A = 8

N_Q_B = 16
N_KV_B = 4
HEAD_DIM_B = 64
W_B = 1024
KV_W_B = 256
WINDOW = 128
ROT_DIM = 16
ROPE_THETA = 500000.0
ATTN_SCALE = HEAD_DIM_B ** -0.5

W_C = 1024
HEAD_DIM_C = 64
N_HEADS_C = 16
N_GROUPS_C = 4
D_STATE = 128
CONV_W = 4
CONV_DIM = 2048

LANES = 128
SUBLANES = 8
SUB_BLOCK = 16
MXU_COLS = 256
PROMPT_ROWS = 256
MERGE_ROWS = 512
LOG2E = math.log2(math.e)
VMEM_LIMIT = 56 * 1024 * 1024

NT_DIMS = (((1,), (1,)), ((), ()))
TN_DIMS = (((0,), (0,)), ((), ()))


def _dot(a, b):
    return jnp.dot(a, b, preferred_element_type=F32)


def _dot_nt(a, b):
    return lax.dot_general(a, b, NT_DIMS, preferred_element_type=F32)


def _dot_tn(a, b):
    return lax.dot_general(a, b, TN_DIMS, preferred_element_type=F32)


def _prenorm_bf16(x, w):
    r = lax.rsqrt(jnp.mean(x * x, axis=-1, keepdims=True) + EPS)
    return ((x * r) * w).astype(BF16)


def _sigmoid(x):
    return 0.5 * jnp.tanh(0.5 * x) + 0.5


def _silu(x):
    return x * _sigmoid(x)


def _softplus0(x):
    return jnp.maximum(x, 0.0) + jnp.log(1.0 + jnp.exp(-jnp.abs(x)))


def _tri_bf16(n):
    r = lax.broadcasted_iota(jnp.int32, (n, n), 0)
    c = lax.broadcasted_iota(jnp.int32, (n, n), 1)
    return (r >= c).astype(BF16)


def _split3(x):
    hi = x.astype(BF16)
    r1 = x - hi.astype(F32)
    mid = r1.astype(BF16)
    lo = (r1 - mid.astype(F32)).astype(BF16)
    return hi, mid, lo


def _cumsum_rows(tri, x):
    hi, mid, lo = _split3(x)
    return _dot(tri, hi) + _dot(tri, mid) + _dot(tri, lo)


def _exact_dot(x, sel):
    hi, mid, lo = _split3(x)
    return _dot(hi, sel) + _dot(mid, sel) + _dot(lo, sel)


def _round_robin(gens, background=None, heavy_round=0):
    results = [None] * len(gens)
    live = list(range(len(gens)))
    rnd = 0
    while live:
        if background is not None and rnd == heavy_round:
            for _ in background:
                pass
        for n in list(live):
            try:
                next(gens[n])
            except StopIteration as stop:
                results[n] = stop.value
                live.remove(n)
        rnd += 1
    return results


def _hgrn_kernel(h_ref, w_ref, lb_ref, hn_ref, s0_ref, y_ref, snew_ref,
                 proj_s, st_s, *, tl, ni, seq_per_chunk):
    n = pl.program_id(0)
    n_chunks = tl // CHUNK
    i_lag = jnp.maximum(n - 1, 0) % ni
    nxt = n % 2
    cur = 1 - nxt

    @pl.when(n == 0)
    def _():
        proj_s[...] = jnp.zeros_like(proj_s)

    if not seq_per_chunk:
        @pl.when(i_lag == 0)
        def _():
            for hd in range(N_HEADS_A):
                st_s[hd] = s0_ref[0, hd].T

    n_col_blocks = w_ref.shape[0]
    cols_per_chunk = n_col_blocks // n_chunks

    def project(c):
        for j in range(cols_per_chunk):
            jb = c * cols_per_chunk + j
            res = _dot(h_ref[0], w_ref[jb])
            proj_s[nxt, 2 * jb] = res[:, :LANES]
            proj_s[nxt, 2 * jb + 1] = res[:, LANES:]
            yield

    tri = _tri_bf16(CHUNK)
    n_sub = CHUNK // SUB_BLOCK
    lane8 = lax.broadcasted_iota(jnp.int32, (SUBLANES, LANES), 1)
    sub8 = lax.broadcasted_iota(jnp.int32, (SUBLANES, LANES), 0)
    row64 = lax.broadcasted_iota(jnp.int32, (CHUNK, CHUNK), 0)
    col64 = lax.broadcasted_iota(jnp.int32, (CHUNK, CHUNK), 1)
    below_diag_block = row64 // SUB_BLOCK > col64 // SUB_BLOCK
    zero8 = jnp.zeros((SUBLANES, LANES), F32)

    def zeros(n):
        return jnp.zeros((n, LANES), F32)

    def piece(a, m, u):
        lo = SUB_BLOCK * m + SUBLANES * u
        return a[lo:lo + SUBLANES]

    def unit(hd, r0):
        rows = pl.ds(r0, CHUNK)
        aq = proj_s[cur, hd, rows, :]
        z = proj_s[cur, N_HEADS_A + hd, rows, :]
        v = proj_s[cur, 2 * N_HEADS_A + hd, rows, :].astype(BF16)
        ag = proj_s[cur, 3 * N_HEADS_A + hd, rows, :]
        log_lb = lb_ref[0, hd]
        log1m_lb = lb_ref[1, hd]

        z2 = z * LOG2E
        log_sig = jnp.minimum(z2, 0.0) - jnp.log2(1.0 + jnp.exp2(-jnp.abs(z2)))
        cterm = log1m_lb + log_sig
        log_f = jnp.maximum(log_lb, cterm) + jnp.log2(1.0 + jnp.exp2(-jnp.abs(log_lb - cterm)))
        log_k = cterm - z2
        q = _silu(aq)

        b2 = _cumsum_rows(tri, log_f)
        yield
        c2 = b2 - log_k
        b2_last = b2[CHUNK - 1:CHUNK, :]
        st = st_s[hd]

        o = _dot_nt((q * jnp.exp2(b2)).astype(BF16), st.astype(BF16))
        yield

        q_slabs, k_slabs = [], []
        for m in range(1, n_sub):
            lo = SUB_BLOCK * m
            ref = b2[lo - 1:lo, :]
            qm = q[lo:lo + SUB_BLOCK] * jnp.exp2(b2[lo:lo + SUB_BLOCK] - ref)
            q_parts = [zeros(lo), qm] + ([zeros(CHUNK - lo - SUB_BLOCK)] if lo + SUB_BLOCK < CHUNK else [])
            q_slabs.append(jnp.concatenate(q_parts, axis=0))
            k_slabs.append(jnp.concatenate([jnp.exp2(ref - c2[:lo]), zeros(CHUNK - lo)], axis=0))
        a_off = _dot_nt(jnp.concatenate(q_slabs, axis=1).astype(BF16),
                        jnp.concatenate(k_slabs, axis=1).astype(BF16))
        yield

        order = [(s, m, u) for s in range(SUB_BLOCK) for m in range(n_sub) for u in range(2)
                 if not (u == 0 and s >= SUBLANES)]
        pieces = []
        for s, m, u in order:
            cs = c2[SUB_BLOCK * m + s:SUB_BLOCK * m + s + 1]
            pieces.append(piece(q, m, u) * jnp.exp2(piece(b2, m, u) - cs))
        sums = [jnp.sum(p, axis=-1, keepdims=True) for p in pieces]
        yield
        d = {(m, u): zero8 for m in range(n_sub) for u in range(2)}
        for n, (s, m, u) in enumerate(order):
            d[(m, u)] = jnp.where(lane8 == s, sums[n], d[(m, u)])
        diag_rows = []
        for m in range(n_sub):
            for u in range(2):
                dm = jnp.where(sub8 + SUBLANES * u >= lane8, d[(m, u)], 0.0)
                diag_rows.append(pltpu.roll(dm, SUB_BLOCK * m, 1) if m else dm)
        a_diag = jnp.concatenate(diag_rows, axis=0)[:, :CHUNK]
        a_full = jnp.where(below_diag_block, a_off, a_diag)
        o = o + _dot(a_full.astype(BF16), v)
        yield

        r = lax.rsqrt(jnp.mean(o * o, axis=-1, keepdims=True) + EPS)
        y = ((o * r) * hn_ref[...]) * _silu(ag)

        st_new = st * jnp.exp2(b2_last) + _dot_tn(v, jnp.exp2(b2_last - c2).astype(BF16))
        return y.astype(BF16), st_new

    def body(c, carry):
        r0 = pl.multiple_of(c * CHUNK, CHUNK)
        if seq_per_chunk:
            for hd in range(N_HEADS_A):
                st_s[hd] = s0_ref[c, hd].T
        gens = [unit(hd, r0) for hd in range(N_HEADS_A)] + [project(c)]
        results = _round_robin(gens)[:N_HEADS_A]
        for hd, (y, st_new) in enumerate(results):
            y_ref[0, pl.ds(r0, CHUNK), LANES * hd:LANES * (hd + 1)] = y
            if seq_per_chunk:
                snew_ref[c, hd] = st_new.T
            else:
                st_s[hd] = st_new
        return carry

    lax.fori_loop(0, n_chunks, body, 0)

    if not seq_per_chunk:
        @pl.when(i_lag == ni - 1)
        def _():
            for hd in range(N_HEADS_A):
                snew_ref[0, hd] = st_s[hd].T


def _lagged_maps(n_blocks, ni):
    def split(m):
        return m // ni, m % ni
    x_map = lambda n: split(jnp.minimum(n, n_blocks - 1)) + (0,)
    y_map = lambda n: split(jnp.maximum(n - 1, 0)) + (0,)
    pos_map = lambda n: (jnp.minimum(n, n_blocks - 1) % ni, 0)
    batch_map = lambda nd: (lambda n: (jnp.maximum(n - 1, 0) // ni,) + (0,) * (nd - 1))
    return x_map, y_map, pos_map, batch_map


def _hgrn_call(h, w_a, lb_pack, hn, s0, layer, tl):
    n_seq, seq, _ = h.shape
    h, tl, group, seq_per_chunk = _batch_one_chunk_sequences(h, tl)
    bsz = h.shape[0]
    ni = h.shape[1] // tl
    n_blocks = bsz * ni
    x_map, y_map, _, batch_map = _lagged_maps(n_blocks, ni)
    const2 = lambda n: (0, 0)
    state_spec = pl.BlockSpec((group, N_HEADS_A, HEAD_K_A, HEAD_K_A), batch_map(4))
    y, s_new = pl.pallas_call(
        functools.partial(_hgrn_kernel, tl=tl, ni=ni, seq_per_chunk=seq_per_chunk),
        grid=(n_blocks + 1,),
        in_specs=[
            pl.BlockSpec((1, tl, D_MODEL), x_map),
            pl.BlockSpec(w_a.shape, lambda n: (0, 0, 0), pipeline_mode=pl.Buffered(1)),
            pl.BlockSpec((2, N_HEADS_A, 1, HEAD_K_A), lambda n: (0, 0, 0, 0)),
            pl.BlockSpec((1, HEAD_K_A), const2),
            pl.BlockSpec((None, group, N_HEADS_A, HEAD_K_A, HEAD_K_A), lambda n: (layer,) + batch_map(4)(n)),
        ],
        out_specs=[pl.BlockSpec((1, tl, W_A), y_map), state_spec],
        out_shape=[
            jax.ShapeDtypeStruct((bsz, ni * tl, W_A), BF16),
            jax.ShapeDtypeStruct((n_seq, N_HEADS_A, HEAD_K_A, HEAD_K_A), F32),
        ],
        scratch_shapes=[
            pltpu.VMEM((2, 4 * N_HEADS_A, tl, LANES), F32),
            pltpu.VMEM((N_HEADS_A, HEAD_K_A, HEAD_K_A), F32),
        ],
        compiler_params=pltpu.CompilerParams(
            dimension_semantics=("arbitrary",), vmem_limit_bytes=VMEM_LIMIT),
        name="hgrn_mixer",
    )(h, w_a, lb_pack, hn, s0)
    return y.reshape(n_seq, seq, W_A), s_new


def _rotate(xs, cos, sin_a, sin_b):
    return xs * cos + pltpu.roll(xs, LANES - ROT_DIM // 2, 1) * sin_a + pltpu.roll(xs, ROT_DIM // 2, 1) * sin_b


def _swa_kernel(*refs, tl, ni, n_blocks, has_cache, from_x, seq_per_chunk):
    it = iter(refs)
    if from_x:
        x_ref, nw_ref = next(it), next(it)
    else:
        h_ref = next(it)
    w_ref, cos_ref, sa_ref, sb_ref, sink_ref = (next(it) for _ in range(5))
    if has_cache:
        kc_ref, vc_ref = next(it), next(it)
    y_ref, knew_ref, vnew_ref = (next(it) for _ in range(3))
    if from_x:
        h_ref = next(it)
    q_s, g_s, kbuf, vbuf = (next(it) for _ in range(4))
    n = pl.program_id(0)
    n_chunks = tl // CHUNK
    band = WINDOW + CHUNK
    kvw = 2 * KV_W_B
    i_proj = jnp.minimum(n, n_blocks - 1) % ni
    i_lag = jnp.maximum(n - 1, 0) % ni
    nxt = n % 2
    cur = 1 - nxt
    n_seg = n_chunks if seq_per_chunk else 1
    seg_rows = tl // n_seg
    seg_prev = [slice((WINDOW + seg_rows) * s, (WINDOW + seg_rows) * s + WINDOW) for s in range(n_seg)]
    seg_new = [slice(sp.stop, sp.stop + seg_rows) for sp in seg_prev]
    seg_last = [slice(sn.stop - WINDOW, sn.stop) for sn in seg_new]

    @pl.when(n == 0)
    def _():
        for buf in (q_s, g_s, kbuf, vbuf):
            buf[...] = jnp.zeros_like(buf)

    if seq_per_chunk:
        for s in range(n_seg):
            kbuf[nxt, seg_prev[s], :] = kc_ref[s]
            vbuf[nxt, seg_prev[s], :] = vc_ref[s]
    else:
        @pl.when(i_proj == 0)
        def _():
            if has_cache:
                kbuf[nxt, seg_prev[0], :] = kc_ref[0]
                vbuf[nxt, seg_prev[0], :] = vc_ref[0]
            else:
                kbuf[nxt, seg_prev[0], :] = jnp.zeros((WINDOW, kvw), F32)
                vbuf[nxt, seg_prev[0], :] = jnp.zeros((WINDOW, kvw), F32)

        @pl.when(i_proj != 0)
        def _():
            kbuf[nxt, seg_prev[0], :] = kbuf[cur, seg_last[0], :]
            vbuf[nxt, seg_prev[0], :] = vbuf[cur, seg_last[0], :]

    if from_x:
        h_ref[0] = _prenorm_bf16(x_ref[0], nw_ref[...])

    def store_new(buf, cols, val):
        for s in range(n_seg):
            buf[nxt, seg_new[s], cols] = val[seg_rows * s:seg_rows * (s + 1)]

    def project(c):
        per = (W_B // MXU_COLS) // n_chunks
        for j in range(c * per, (c + 1) * per):
            sl = slice(MXU_COLS * j, MXU_COLS * (j + 1))
            res = _dot(h_ref[0], w_ref[:, sl])
            yield
            for u in range(MXU_COLS // LANES):
                q_s[nxt, :, MXU_COLS * j + LANES * u:MXU_COLS * j + LANES * (u + 1)] = _rotate(
                    res[:, LANES * u:LANES * (u + 1)], cos_ref[...], sa_ref[...], sb_ref[...])
            g_s[nxt, :, sl] = _dot(h_ref[0], w_ref[:, W_B + 2 * KV_W_B + MXU_COLS * j:W_B + 2 * KV_W_B + MXU_COLS * (j + 1)])
            yield
            if j < 2:
                res = _dot(h_ref[0], w_ref[:, W_B + KV_W_B * j:W_B + KV_W_B * (j + 1)])
                yield
                for u in range(KV_W_B // LANES):
                    a = res[:, LANES * u:LANES * (u + 1)]
                    if j == 0:
                        a = _rotate(a, cos_ref[...], sa_ref[...], sb_ref[...])
                    swapped = pltpu.roll(a, HEAD_DIM_B, 1)
                    buf = kbuf if j == 0 else vbuf
                    store_new(buf, slice(LANES * 2 * u, LANES * (2 * u + 1)), jnp.where(low1, a, swapped))
                    store_new(buf, slice(LANES * (2 * u + 1), LANES * (2 * u + 2)), jnp.where(low1, swapped, a))

    lane = lax.broadcasted_iota(jnp.int32, (CHUNK, LANES), 1)
    low = lane < HEAD_DIM_B
    low1 = lax.broadcasted_iota(jnp.int32, (1, LANES), 1) < HEAD_DIM_B
    key_row = lax.broadcasted_iota(jnp.int32, (band, LANES), 0)

    def pair(c, j):
        r0 = CHUNK * c
        g = j // 2
        qp = q_s[cur, r0:r0 + CHUNK, LANES * j:LANES * (j + 1)]
        q2 = jnp.concatenate([jnp.where(low, qp, 0.0), jnp.where(low, 0.0, qp)], axis=0).astype(BF16)
        b0 = band * c if seq_per_chunk else r0
        kb = kbuf[cur, b0:b0 + band, LANES * g:LANES * (g + 1)].astype(BF16)
        vb = vbuf[cur, b0:b0 + band, LANES * g:LANES * (g + 1)].astype(BF16)
        s = _dot_nt(kb, q2) * ATTN_SCALE
        yield
        if not has_cache:
            s = jnp.where(i_lag * tl + r0 - WINDOW + key_row >= 0, s, -jnp.inf)
        sk = jnp.where(low1, sink_ref[2 * j:2 * j + 1, :], sink_ref[2 * j + 1:2 * j + 2, :])
        m = jnp.maximum(jnp.max(s, axis=0, keepdims=True), sk)
        p = jnp.exp(s - m)
        den = jnp.sum(p, axis=0, keepdims=True) + jnp.exp(sk - m)
        o2 = _dot_tn((p * (1.0 / den)).astype(BF16), vb)
        yield
        o = jnp.where(low, o2[:CHUNK], o2[CHUNK:])
        y = o * _silu(g_s[cur, r0:r0 + CHUNK, LANES * j:LANES * (j + 1)])
        y_ref[0, r0:r0 + CHUNK, LANES * j:LANES * (j + 1)] = y.astype(BF16)

    for c in range(n_chunks):
        _round_robin([pair(c, j) for j in range(N_Q_B // 2)], project(c), heavy_round=1)

    def write_newest(s):
        ktail = kbuf[nxt, seg_last[s], :]
        vtail = vbuf[nxt, seg_last[s], :]
        low_w = lax.broadcasted_iota(jnp.int32, (WINDOW, LANES), 1) < HEAD_DIM_B
        for u in range(KV_W_B // LANES):
            knew_ref[s, :, LANES * u:LANES * (u + 1)] = jnp.where(
                low_w, ktail[:, 2 * LANES * u:2 * LANES * u + LANES],
                ktail[:, 2 * LANES * u + LANES:2 * LANES * (u + 1)])
            vnew_ref[s, :, LANES * u:LANES * (u + 1)] = jnp.where(
                low_w, vtail[:, 2 * LANES * u:2 * LANES * u + LANES],
                vtail[:, 2 * LANES * u + LANES:2 * LANES * (u + 1)])

    if seq_per_chunk:
        for s in range(n_seg):
            write_newest(s)
    else:
        pl.when(i_proj == ni - 1)(lambda: write_newest(0))


def _swa_call(h, x_norm, w_b, cos, sin_a, sin_b, sinks, kc, vc, tl):
    from_x = h is None
    act = x_norm[0] if from_x else h
    n_seq, seq, _ = act.shape
    has_cache = kc is not None
    group, seq_per_chunk = 1, False
    if has_cache:
        act, tl, group, seq_per_chunk = _batch_one_chunk_sequences(act, tl)
    if seq_per_chunk:
        cos, sin_a, sin_b = (jnp.tile(t, (group, 1)) for t in (cos, sin_a, sin_b))
    bsz = act.shape[0]
    ni = act.shape[1] // tl
    n_blocks = bsz * ni
    x_map, y_map, pos_map, _ = _lagged_maps(n_blocks, ni)
    proj_batch_map = lambda n: (jnp.minimum(n, n_blocks - 1) // ni, 0, 0)
    const2 = lambda n: (0, 0)
    wcols = w_b.shape[1]
    act_spec = pl.BlockSpec((1, tl, D_MODEL), x_map)
    in_specs = [act_spec, pl.BlockSpec((1, D_MODEL), const2)] if from_x else [act_spec]
    args = [act, x_norm[1]] if from_x else [act]
    in_specs += [
        pl.BlockSpec((D_MODEL, wcols), const2, pipeline_mode=pl.Buffered(1)),
        pl.BlockSpec((tl, LANES), pos_map),
        pl.BlockSpec((tl, LANES), pos_map),
        pl.BlockSpec((tl, LANES), pos_map),
        pl.BlockSpec((N_Q_B, LANES), const2),
    ]
    args += [w_b, cos, sin_a, sin_b, sinks]
    if has_cache:
        in_specs += [pl.BlockSpec((group, WINDOW, 2 * KV_W_B), proj_batch_map)] * 2
        args += [kc, vc]
    out_specs = [
        pl.BlockSpec((1, tl, W_B), y_map),
        pl.BlockSpec((group, WINDOW, KV_W_B), proj_batch_map),
        pl.BlockSpec((group, WINDOW, KV_W_B), proj_batch_map),
    ]
    out_shape = [
        jax.ShapeDtypeStruct((bsz, ni * tl, W_B), BF16),
        jax.ShapeDtypeStruct((n_seq, WINDOW, KV_W_B), F32),
        jax.ShapeDtypeStruct((n_seq, WINDOW, KV_W_B), F32),
    ]
    if from_x:
        out_specs.append(act_spec)
        out_shape.append(jax.ShapeDtypeStruct((bsz, ni * tl, D_MODEL), BF16))
    kv_rows = group * WINDOW + tl
    outs = pl.pallas_call(
        functools.partial(_swa_kernel, tl=tl, ni=ni, n_blocks=n_blocks, has_cache=has_cache, from_x=from_x,
                          seq_per_chunk=seq_per_chunk),
        grid=(n_blocks + 1,),
        in_specs=in_specs,
        out_specs=out_specs,
        out_shape=out_shape,
        scratch_shapes=[
            pltpu.VMEM((2, tl, W_B), F32),
            pltpu.VMEM((2, tl, W_B), F32),
            pltpu.VMEM((2, kv_rows, 2 * KV_W_B), F32),
            pltpu.VMEM((2, kv_rows, 2 * KV_W_B), F32),
        ],
        compiler_params=pltpu.CompilerParams(
            dimension_semantics=("arbitrary",), vmem_limit_bytes=VMEM_LIMIT),
        name="swa_mixer",
    )(*args)
    outs = list(outs)
    outs[0] = outs[0].reshape(n_seq, seq, W_B)
    if from_x:
        outs[3] = outs[3].reshape(n_seq, seq, D_MODEL)
    return outs


CONV_PAD = 8


def _ssd_kernel(h_ref, w_ref, cw_ref, cb_ref, dtb_ref, alog_ref, dsk_ref, sn_ref, conv0_ref, s0_ref,
                y_ref, snew_ref, convnew_ref,
                ubuf, dt_s, z_s, xc_s, xdt_s, bc_s, la_s, st_s, *, tl, ni, n_blocks, seq_per_chunk):
    n = pl.program_id(0)
    n_chunks = tl // CHUNK
    gw = N_GROUPS_C * D_STATE
    hp = W_C // N_GROUPS_C
    i_proj = jnp.minimum(n, n_blocks - 1) % ni
    i_lag = jnp.maximum(n - 1, 0) % ni
    nxt = n % 2
    cur = 1 - nxt
    n_seg = n_chunks if seq_per_chunk else 1
    seg_rows = tl // n_seg
    seg_new = [slice((CONV_PAD + seg_rows) * s + CONV_PAD, (CONV_PAD + seg_rows) * (s + 1)) for s in range(n_seg)]
    seg_tail = [slice(sn.start - (CONV_W - 1), sn.start) for sn in seg_new]
    seg_last = [slice(sn.stop - (CONV_W - 1), sn.stop) for sn in seg_new]

    @pl.when(n == 0)
    def _():
        for buf in (ubuf, z_s, xc_s, xdt_s, bc_s, la_s):
            buf[...] = jnp.zeros_like(buf)

    if seq_per_chunk:
        for s in range(n_seg):
            ubuf[seg_tail[s], :] = conv0_ref[s]
    else:
        @pl.when(i_proj == 0)
        def _():
            ubuf[seg_tail[0], :] = conv0_ref[0]

        @pl.when(i_lag == 0)
        def _():
            for g in range(N_GROUPS_C):
                st_s[g] = s0_ref[0, hp * g:hp * (g + 1), :].T

    def store_raw(cols, res):
        for s in range(n_seg):
            ubuf[seg_new[s], cols] = res[seg_rows * s:seg_rows * (s + 1)]

    def conv_act(sl):
        win = ubuf[:, sl]
        acc = cb_ref[:, sl]
        for t in range(CONV_W):
            back = CONV_W - 1 - t
            src = pltpu.roll(win, back, 0) if back else win
            acc = acc + src * cw_ref[t:t + 1, sl]
        if not seq_per_chunk:
            ubuf[seg_tail[0], sl] = ubuf[seg_last[0], sl]
        return _silu(jnp.concatenate([acc[sn] for sn in seg_new], axis=0) if n_seg > 1 else acc[seg_new[0]])

    def project(c):
        per = (W_C // MXU_COLS) // n_chunks
        for j in range(c * per, (c + 1) * per):
            sl = slice(MXU_COLS * j, MXU_COLS * (j + 1))
            z_s[nxt, :, sl] = _dot(h_ref[0], w_ref[:, sl])
            yield
            if j == 0:
                dt_raw = _dot(h_ref[0], w_ref[:, W_C + CONV_DIM:W_C + CONV_DIM + LANES])
                yield
                dt_s[...] = _softplus0(dt_raw + dtb_ref[...])
            head_of_col = (MXU_COLS * j + lax.broadcasted_iota(jnp.int32, (LANES, MXU_COLS), 1)) // HEAD_DIM_C
            spread = (lax.broadcasted_iota(jnp.int32, (LANES, MXU_COLS), 0) == head_of_col).astype(BF16)
            dt = _exact_dot(dt_s[...], spread)
            yield
            la_s[nxt, :, sl] = -dt * jnp.exp(alog_ref[:, sl])
            store_raw(sl, _dot(h_ref[0], w_ref[:, W_C + MXU_COLS * j:W_C + MXU_COLS * (j + 1)]))
            yield
            for u in range(MXU_COLS // LANES):
                usl = slice(MXU_COLS * j + LANES * u, MXU_COLS * j + LANES * (u + 1))
                act = conv_act(usl)
                xc_s[nxt, :, usl] = act
                xdt_s[nxt, :, usl] = act * dt[:, LANES * u:LANES * (u + 1)]
            bsl = slice(W_C + MXU_COLS * j, W_C + MXU_COLS * (j + 1))
            store_raw(bsl, _dot(h_ref[0], w_ref[:, W_C + W_C + MXU_COLS * j:W_C + W_C + MXU_COLS * (j + 1)]))
            yield
            for u in range(MXU_COLS // LANES):
                usl = slice(W_C + MXU_COLS * j + LANES * u, W_C + MXU_COLS * j + LANES * (u + 1))
                bc_s[nxt, :, MXU_COLS * j + LANES * u:MXU_COLS * j + LANES * (u + 1)] = conv_act(usl)

    tri = _tri_bf16(CHUNK)
    trow = lax.broadcasted_iota(jnp.int32, (CHUNK, LANES), 0)
    lane = lax.broadcasted_iota(jnp.int32, (CHUNK, LANES), 1)
    s_of_lane = lane % CHUNK
    causal = trow >= s_of_lane
    row2 = lax.broadcasted_iota(jnp.int32, (2 * CHUNK, LANES), 0)
    lane2 = lax.broadcasted_iota(jnp.int32, (2 * CHUNK, LANES), 1)
    blockdiag = (row2 < CHUNK) == (lane2 < HEAD_DIM_C)

    for c in range(n_chunks):
        rows = slice(CHUNK * c, CHUNK * (c + 1))

        def group(g, rows=rows):
            gsl = slice(hp * g, hp * (g + 1))
            bg = bc_s[cur, rows, D_STATE * g:D_STATE * (g + 1)].astype(BF16)
            cg = bc_s[cur, rows, gw + D_STATE * g:gw + D_STATE * (g + 1)].astype(BF16)
            cb2 = _dot_nt(cg, jnp.concatenate([bg, bg], axis=0))
            yield
            st = st_s[g]
            y_state = _dot(cg, st.astype(BF16))
            yield
            la = la_s[cur, rows, gsl]
            b = _cumsum_rows(tri, la)
            yield
            xdt = xdt_s[cur, rows, gsl]
            ys = []
            for u in range(2):
                usl = slice(LANES * u, LANES * (u + 1))
                b_row = jnp.sum(jnp.where(trow <= s_of_lane, la[:, usl], 0.0), axis=0, keepdims=True)
                decay = jnp.where(causal, jnp.exp(b[:, usl] - b_row), 0.0)
                wmat = (cb2 * decay).astype(BF16)
                xbd = jnp.where(blockdiag, jnp.concatenate([xdt[:, usl], xdt[:, usl]], axis=0), 0.0).astype(BF16)
                ys.append(_dot(wmat, xbd))
                yield
            y = jnp.concatenate(ys, axis=1) + y_state * jnp.exp(b)
            o = (y + dsk_ref[:, gsl] * xc_s[cur, rows, gsl]) * _silu(z_s[cur, rows, gsl])
            r = lax.rsqrt(jnp.mean(o * o, axis=-1, keepdims=True) + EPS)
            out = ((o * r) * sn_ref[:, gsl]).astype(BF16)
            b_last = b[CHUNK - 1:CHUNK, :]
            st_new = st * jnp.exp(b_last) + _dot_tn(bg, (xdt * jnp.exp(b_last - b)).astype(BF16))
            return out, st_new

        if seq_per_chunk:
            for g in range(N_GROUPS_C):
                st_s[g] = s0_ref[c, hp * g:hp * (g + 1), :].T
        results = _round_robin([group(g) for g in range(N_GROUPS_C)] + [project(c)])[:N_GROUPS_C]
        for g, (out, st_new) in enumerate(results):
            y_ref[0, rows, hp * g:hp * (g + 1)] = out
            if seq_per_chunk:
                snew_ref[c, hp * g:hp * (g + 1), :] = st_new.T
            else:
                st_s[g] = st_new

    if seq_per_chunk:
        for s in range(n_seg):
            convnew_ref[s] = ubuf[seg_last[s], :]
    else:
        @pl.when(i_proj == ni - 1)
        def _():
            convnew_ref[0] = ubuf[seg_tail[0], :]

        @pl.when(i_lag == ni - 1)
        def _():
            for g in range(N_GROUPS_C):
                snew_ref[0, hp * g:hp * (g + 1), :] = st_s[g].T


def _batch_one_chunk_sequences(h, tl):
    n_seq, seq, _ = h.shape
    group = PROMPT_ROWS // CHUNK
    if seq == CHUNK and n_seq % group == 0:
        return h.reshape(n_seq // group, PROMPT_ROWS, D_MODEL), PROMPT_ROWS, group, True
    return h, tl, 1, False


def _ssd_call(h, w_c, cw, cb, dtb, alog, dsk, sn, conv0, s0, layer, tl):
    n_seq, seq, _ = h.shape
    h, tl, group, seq_per_chunk = _batch_one_chunk_sequences(h, tl)
    bsz = h.shape[0]
    ni = h.shape[1] // tl
    n_blocks = bsz * ni
    x_map, y_map, _, batch_map = _lagged_maps(n_blocks, ni)
    proj_batch_map = lambda n: (jnp.minimum(n, n_blocks - 1) // ni, 0, 0)
    const2 = lambda n: (0, 0)
    wcols = w_c.shape[1]
    y, s_new, conv_new = pl.pallas_call(
        functools.partial(_ssd_kernel, tl=tl, ni=ni, n_blocks=n_blocks, seq_per_chunk=seq_per_chunk),
        grid=(n_blocks + 1,),
        in_specs=[
            pl.BlockSpec((1, tl, D_MODEL), x_map),
            pl.BlockSpec((D_MODEL, wcols), const2, pipeline_mode=pl.Buffered(1)),
            pl.BlockSpec((CONV_W, CONV_DIM), const2),
            pl.BlockSpec((1, CONV_DIM), const2),
            pl.BlockSpec((1, LANES), const2),
            pl.BlockSpec((1, W_C), const2),
            pl.BlockSpec((1, W_C), const2),
            pl.BlockSpec((1, W_C), const2),
            pl.BlockSpec((None, group, CONV_W - 1, CONV_DIM), lambda n: (layer,) + proj_batch_map(n)),
            pl.BlockSpec((None, group, W_C, D_STATE), lambda n: (layer,) + batch_map(3)(n)),
        ],
        out_specs=[
            pl.BlockSpec((1, tl, W_C), y_map),
            pl.BlockSpec((group, W_C, D_STATE), batch_map(3)),
            pl.BlockSpec((group, CONV_W - 1, CONV_DIM), proj_batch_map),
        ],
        out_shape=[
            jax.ShapeDtypeStruct((bsz, ni * tl, W_C), BF16),
            jax.ShapeDtypeStruct((n_seq, W_C, D_STATE), F32),
            jax.ShapeDtypeStruct((n_seq, CONV_W - 1, CONV_DIM), F32),
        ],
        scratch_shapes=[
            pltpu.VMEM((group * CONV_PAD + tl, CONV_DIM), F32),
            pltpu.VMEM((tl, LANES), F32),
            pltpu.VMEM((2, tl, W_C), F32),
            pltpu.VMEM((2, tl, W_C), F32),
            pltpu.VMEM((2, tl, W_C), F32),
            pltpu.VMEM((2, tl, 2 * N_GROUPS_C * D_STATE), F32),
            pltpu.VMEM((2, tl, W_C), F32),
            pltpu.VMEM((N_GROUPS_C, D_STATE, W_C // N_GROUPS_C), F32),
        ],
        compiler_params=pltpu.CompilerParams(
            dimension_semantics=("arbitrary",), vmem_limit_bytes=VMEM_LIMIT),
        name="ssd_mixer",
    )(h, w_c, cw, cb, dtb, alog, dsk, sn, conv0, s0)
    return y.reshape(n_seq, seq, W_C), s_new, conv_new


def _merge_kernel(*refs, emit_next):
    it = iter(refs)
    x_ref, h_ref = next(it), next(it)
    y_refs = [next(it) for _ in range(3)]
    npost_ref = next(it)
    nnext_ref = next(it) if emit_next else None
    wg_refs = [next(it) for _ in range(3)]
    wbr_refs = [next(it) for _ in range(3)]
    wout_ref, o_ref = next(it), next(it)
    hnext_ref = next(it) if emit_next else None
    m_s = next(it)
    j = pl.program_id(1)
    n_j = m_s.shape[0]

    h = h_ref[...]
    merged = None
    for y_ref, wg_ref, wbr_ref in zip(y_refs, wg_refs, wbr_refs):
        term = _sigmoid(_dot(h, wg_ref[...])) * _dot(y_ref[...], wbr_ref[...])
        merged = term if merged is None else merged + term
    m_s[j] = merged.astype(BF16)

    @pl.when(j == n_j - 1)
    def _():
        out = _dot(jnp.concatenate([m_s[n] for n in range(n_j)], axis=1), wout_ref[...])
        r = lax.rsqrt(jnp.mean(out * out, axis=-1, keepdims=True) + EPS)
        x_new = x_ref[...] + (out * r) * npost_ref[...]
        o_ref[...] = x_new
        if emit_next:
            hnext_ref[...] = _prenorm_bf16(x_new, nnext_ref[...])


def _merge_call(x2, h2, ya, yb, yc, npost, nnext, wg, wbr, wout, tm, tn):
    rows = x2.shape[0]
    emit_next = nnext is not None
    grid = (rows // tm, D_MODEL // tn)
    const2 = lambda r, j: (0, 0)
    row_block = lambda width: pl.BlockSpec((tm, width), lambda r, j: (r, 0))
    vec = pl.BlockSpec((1, D_MODEL), const2)
    in_specs = [row_block(D_MODEL), row_block(D_MODEL), row_block(W_A), row_block(W_B), row_block(W_C), vec]
    args = [x2, h2, ya, yb, yc, npost]
    if emit_next:
        in_specs.append(vec)
        args.append(nnext)
    col_block = lambda k: pl.BlockSpec((k, tn), lambda r, j: (0, j))
    in_specs += [col_block(D_MODEL)] * 3 + [col_block(W_A)] * 3
    in_specs.append(pl.BlockSpec((D_MODEL, D_MODEL), const2, pipeline_mode=pl.Buffered(1)))
    args += list(wg) + list(wbr) + [wout]
    out_specs = [row_block(D_MODEL)]
    out_shape = [jax.ShapeDtypeStruct((rows, D_MODEL), F32)]
    if emit_next:
        out_specs.append(row_block(D_MODEL))
        out_shape.append(jax.ShapeDtypeStruct((rows, D_MODEL), BF16))
    outs = pl.pallas_call(
        functools.partial(_merge_kernel, emit_next=emit_next),
        grid=grid,
        in_specs=in_specs,
        out_specs=out_specs,
        out_shape=out_shape,
        scratch_shapes=[pltpu.VMEM((D_MODEL // tn, tm, tn), BF16)],
        compiler_params=pltpu.CompilerParams(
            dimension_semantics=("arbitrary", "arbitrary"), vmem_limit_bytes=VMEM_LIMIT),
        name="merge_out",
    )(*args)
    return (outs[0], outs[1]) if emit_next else (outs[0], None)


def _rope_tables(pos):
    half = ROT_DIM // 2
    inv = jnp.power(ROPE_THETA, -jnp.arange(half, dtype=F32) / half)
    ang = pos.astype(F32)[:, None] * inv[None, :]
    cos, sin = jnp.cos(ang), jnp.sin(ang)
    n = pos.shape[0]
    ones = jnp.ones((n, HEAD_DIM_B - ROT_DIM), F32)
    zeros = jnp.zeros((n, HEAD_DIM_B - ROT_DIM), F32)
    zh = jnp.zeros((n, half), F32)
    cos_t = jnp.concatenate([cos, cos, ones], axis=1)
    sin_a = jnp.concatenate([-sin, zh, zeros], axis=1)
    sin_b = jnp.concatenate([zh, sin, zeros], axis=1)
    tile = lambda t: jnp.concatenate([t, t], axis=1)
    return tile(cos_t), tile(sin_a), tile(sin_b)


def _dup_heads(t):
    lead = t.shape[:-1]
    t4 = t.reshape(lead + (N_KV_B, 1, HEAD_DIM_B))
    return jnp.broadcast_to(t4, lead + (N_KV_B, 2, HEAD_DIM_B)).reshape(lead + (2 * KV_W_B,))


def _col_blocks(w):
    return w.reshape(w.shape[0], -1, MXU_COLS).transpose(1, 0, 2)


def _pad_lanes(t):
    return jnp.pad(t, [(0, 0)] * (t.ndim - 1) + [(0, LANES - t.shape[-1])])


def _expand_heads(t):
    return jnp.repeat(t, HEAD_DIM_C, axis=-1)


def _layer(x, h, tl, tables, caches, lw, norm_next):
    bsz, seq, _ = x.shape
    cos, sin_a, sin_b = tables
    kc, vc, s_hgrn, s_ssm, s_conv, layer = caches
    if s_hgrn is None:
        layer = 0
        s_hgrn = jnp.zeros((1, bsz, N_HEADS_A, HEAD_K_A, HEAD_K_A), F32)
        s_ssm = jnp.zeros((1, bsz, W_C, D_STATE), F32)
        s_conv = jnp.zeros((1, bsz, CONV_W - 1, CONV_DIM), F32)
    else:
        s_ssm = s_ssm.reshape(-1, bsz, W_C, D_STATE)
        kc = _dup_heads(kc.reshape(bsz, WINDOW, KV_W_B))
        vc = _dup_heads(vc.reshape(bsz, WINDOW, KV_W_B))

    if h is None:
        yb, k_new, v_new, h = _swa_call(None, (x, lw["norm_pre"]), lw["w_b"], cos, sin_a, sin_b, lw["sinks"],
                                        kc, vc, tl)
    else:
        yb, k_new, v_new = _swa_call(h, None, lw["w_b"], cos, sin_a, sin_b, lw["sinks"], kc, vc, tl)
    ya, hgrn_new = _hgrn_call(h, lw["w_a"], lw["lb_pack"], lw["hgrn_norm"], s_hgrn, layer, tl)
    yc, ssm_new, conv_new = _ssd_call(h, lw["w_c"], lw["conv_w"], lw["conv_b"], lw["dt_bias"],
                                      lw["a_log"], lw["d_skip"], lw["ssm_norm"], s_conv, s_ssm, layer, tl)
    rows = bsz * seq
    x_new, h_next = _merge_call(x.reshape(rows, D_MODEL), h.reshape(rows, D_MODEL), ya.reshape(rows, W_A),
                                yb.reshape(rows, W_B), yc.reshape(rows, W_C), lw["norm_post"], norm_next,
                                lw["w_g"], lw["w_br"], lw["w_out"], min(MERGE_ROWS, rows), MXU_COLS)
    states = (k_new.reshape(bsz, WINDOW, N_KV_B, HEAD_DIM_B), v_new.reshape(bsz, WINDOW, N_KV_B, HEAD_DIM_B),
              hgrn_new, ssm_new.reshape(bsz, N_HEADS_C, HEAD_DIM_C, D_STATE), conv_new)
    if h_next is not None:
        h_next = h_next.reshape(bsz, seq, D_MODEL)
    return x_new.reshape(bsz, seq, D_MODEL), h_next, states


def kernel(x_prompt, x_sample, cache_swa_k, cache_swa_v, state_hgrn, state_ssm, state_conv, norm_pre, norm_post, w_in, hgrn_lb_logits, hgrn_norm, swa_sinks, conv_w, conv_b, dt_bias, a_log, d_skip, ssm_norm, w_branch_a, w_branch_b, w_branch_c, w_out):
    depth = w_in.shape[0]
    lbp = jax.nn.softmax(hgrn_lb_logits.astype(F32), axis=0)
    lbc = jnp.cumsum(lbp, axis=0)
    lb_all = lbc - lbc[0:1]

    offs = np.cumsum([0, W_A, W_A, W_A, W_A, W_B, KV_W_B, KV_W_B, W_B, W_C, CONV_DIM, N_HEADS_C,
                      D_MODEL, D_MODEL, D_MODEL])
    col = lambda l, a, b: w_in[l, :, int(offs[a]):int(offs[b])]

    tables_p = _rope_tables(jnp.arange(x_prompt.shape[1], dtype=jnp.int32))
    tables_s = _rope_tables(PAST_LEN + jnp.arange(x_sample.shape[1], dtype=jnp.int32))

    xp, xs = x_prompt, x_sample
    pre = lambda l: norm_pre[l].reshape(1, D_MODEL)
    hp = hs = None
    pst, sst = [], []
    for l in range(depth):
        lb = lb_all[l].reshape(N_HEADS_A, 1, HEAD_K_A)
        norm_next = pre(l + 1) if l + 1 < depth else None
        lw = {
            "norm_pre": pre(l),
            "norm_post": norm_post[l].reshape(1, D_MODEL),
            "w_a": _col_blocks(col(l, 0, 4).astype(BF16)),
            "w_b": col(l, 4, 8).astype(BF16),
            "w_c": jnp.concatenate([col(l, 8, 9), col(l, 9, 10), _pad_lanes(col(l, 10, 11))],
                                   axis=1).astype(BF16),
            "w_g": [col(l, k, k + 1).astype(BF16) for k in (11, 12, 13)],
            "w_br": [w[l].astype(BF16) for w in (w_branch_a, w_branch_b, w_branch_c)],
            "w_out": w_out[l].astype(BF16),
            "lb_pack": jnp.stack([jnp.log(lb), jnp.log1p(-lb)]) * LOG2E,
            "hgrn_norm": hgrn_norm[l].reshape(1, HEAD_K_A),
            "sinks": jnp.broadcast_to(swa_sinks[l].astype(F32)[:, None], (N_Q_B, LANES)),
            "conv_w": conv_w[l],
            "conv_b": conv_b[l].reshape(1, CONV_DIM),
            "dt_bias": _pad_lanes(dt_bias[l].astype(F32).reshape(1, N_HEADS_C)),
            "a_log": _expand_heads(a_log[l].astype(F32)).reshape(1, W_C),
            "d_skip": _expand_heads(d_skip[l].astype(F32)).reshape(1, W_C),
            "ssm_norm": ssm_norm[l].reshape(1, W_C),
        }
        xp, hp, sp = _layer(xp, hp, PROMPT_ROWS, tables_p, (None,) * 6, lw, norm_next)
        xs, hs, ss = _layer(xs, hs, CHUNK, tables_s,
                            (cache_swa_k[l], cache_swa_v[l], state_hgrn, state_ssm, state_conv, l),
                            lw, norm_next)
        pst.append(sp)
        sst.append(ss)

    stack = lambda sts, k: jnp.stack([s[k] for s in sts])
    return (xp, xs,
            stack(pst, 0), stack(pst, 1), stack(pst, 2), stack(pst, 3), stack(pst, 4),
            stack(sst, 0), stack(sst, 1), stack(sst, 2), stack(sst, 3), stack(sst, 4))
```

```python
import functools
import math

import jax
import jax.numpy as jnp
import numpy as np
from jax import lax
from jax.experimental import pallas as pl
from jax.experimental.pallas import tpu as pltpu

F32 = jnp.float32
BF16 = jnp.bfloat16

D_MODEL = 2048
CHUNK = 64
EPS = 1e-6
PAST_LEN = 4096

W_A = 1024
HEAD_K_A = 128
N_HEADS_A = 8

N_Q_B = 16
N_KV_B = 4
HEAD_DIM_B = 64
W_B = 1024
KV_W_B = 256
WINDOW = 128
ROT_DIM = 16
ROPE_THETA = 500000.0
ATTN_SCALE = HEAD_DIM_B ** -0.5

W_C = 1024
HEAD_DIM_C = 64
N_HEADS_C = 16
N_GROUPS_C = 4
D_STATE = 128
CONV_W = 4
CONV_DIM = 2048

LANES = 128
SUBLANES = 8
SUB_BLOCK = 16
MXU_COLS = 256
PROMPT_ROWS = 256
MERGE_ROWS = 512
LOG2E = math.log2(math.e)
VMEM_LIMIT = 56 * 1024 * 1024

NT_DIMS = (((1,), (1,)), ((), ()))
TN_DIMS = (((0,), (0,)), ((), ()))


def _dot(a, b):
    return jnp.dot(a, b, preferred_element_type=F32)


def _dot_nt(a, b):
    return lax.dot_general(a, b, NT_DIMS, preferred_element_type=F32)


def _dot_tn(a, b):
    return lax.dot_general(a, b, TN_DIMS, preferred_element_type=F32)


def _prenorm_bf16(x, w):
    r = lax.rsqrt(jnp.mean(x * x, axis=-1, keepdims=True) + EPS)
    return ((x * r) * w).astype(BF16)


def _sigmoid(x):
    return 0.5 * jnp.tanh(0.5 * x) + 0.5


def _silu(x):
    return x * _sigmoid(x)


def _softplus0(x):
    return jnp.maximum(x, 0.0) + jnp.log(1.0 + jnp.exp(-jnp.abs(x)))


def _tri_bf16(n):
    r = lax.broadcasted_iota(jnp.int32, (n, n), 0)
    c = lax.broadcasted_iota(jnp.int32, (n, n), 1)
    return (r >= c).astype(BF16)


def _split3(x):
    hi = x.astype(BF16)
    r1 = x - hi.astype(F32)
    mid = r1.astype(BF16)
    lo = (r1 - mid.astype(F32)).astype(BF16)
    return hi, mid, lo


def _cumsum_rows(tri, x):
    hi, mid, lo = _split3(x)
    return _dot(tri, hi) + _dot(tri, mid) + _dot(tri, lo)


def _exact_dot(x, sel):
    hi, mid, lo = _split3(x)
    return _dot(hi, sel) + _dot(mid, sel) + _dot(lo, sel)


def _round_robin(gens, background=None, heavy_round=0):
    results = [None] * len(gens)
    live = list(range(len(gens)))
    rnd = 0
    while live:
        if background is not None and rnd == heavy_round:
            for _ in background:
                pass
        for n in list(live):
            try:
                next(gens[n])
            except StopIteration as stop:
                results[n] = stop.value
                live.remove(n)
        rnd += 1
    return results


def _hgrn_kernel(h_ref, w_ref, lb_ref, hn_ref, s0_ref, y_ref, snew_ref,
                 proj_s, st_s, *, tl, ni, seq_per_chunk):
    n = pl.program_id(0)
    n_chunks = tl // CHUNK
    i_lag = jnp.maximum(n - 1, 0) % ni
    nxt = n % 2
    cur = 1 - nxt

    @pl.when(n == 0)
    def _():
        proj_s[...] = jnp.zeros_like(proj_s)

    if not seq_per_chunk:
        @pl.when(i_lag == 0)
        def _():
            for hd in range(N_HEADS_A):
                st_s[hd] = s0_ref[0, hd].T

    n_col_blocks = w_ref.shape[0]
    cols_per_chunk = n_col_blocks // n_chunks

    def project(c):
        for j in range(cols_per_chunk):
            jb = c * cols_per_chunk + j
            res = _dot(h_ref[0], w_ref[jb])
            proj_s[nxt, 2 * jb] = res[:, :LANES]
            proj_s[nxt, 2 * jb + 1] = res[:, LANES:]
            yield

    tri = _tri_bf16(CHUNK)
    n_sub = CHUNK // SUB_BLOCK
    lane8 = lax.broadcasted_iota(jnp.int32, (SUBLANES, LANES), 1)
    sub8 = lax.broadcasted_iota(jnp.int32, (SUBLANES, LANES), 0)
    row64 = lax.broadcasted_iota(jnp.int32, (CHUNK, CHUNK), 0)
    col64 = lax.broadcasted_iota(jnp.int32, (CHUNK, CHUNK), 1)
    below_diag_block = row64 // SUB_BLOCK > col64 // SUB_BLOCK
    zero8 = jnp.zeros((SUBLANES, LANES), F32)

    def zeros(n):
        return jnp.zeros((n, LANES), F32)

    def piece(a, m, u):
        lo = SUB_BLOCK * m + SUBLANES * u
        return a[lo:lo + SUBLANES]

    def unit(hd, r0):
        rows = pl.ds(r0, CHUNK)
        aq = proj_s[cur, hd, rows, :]
        z = proj_s[cur, N_HEADS_A + hd, rows, :]
        v = proj_s[cur, 2 * N_HEADS_A + hd, rows, :].astype(BF16)
        ag = proj_s[cur, 3 * N_HEADS_A + hd, rows, :]
        log_lb = lb_ref[0, hd]
        log1m_lb = lb_ref[1, hd]

        z2 = z * LOG2E
        log_sig = jnp.minimum(z2, 0.0) - jnp.log2(1.0 + jnp.exp2(-jnp.abs(z2)))
        cterm = log1m_lb + log_sig
        log_f = jnp.maximum(log_lb, cterm) + jnp.log2(1.0 + jnp.exp2(-jnp.abs(log_lb - cterm)))
        log_k = cterm - z2
        q = _silu(aq)

        b2 = _cumsum_rows(tri, log_f)
        yield
        c2 = b2 - log_k
        b2_last = b2[CHUNK - 1:CHUNK, :]
        st = st_s[hd]

        o = _dot_nt((q * jnp.exp2(b2)).astype(BF16), st.astype(BF16))
        yield

        q_slabs, k_slabs = [], []
        for m in range(1, n_sub):
            lo = SUB_BLOCK * m
            ref = b2[lo - 1:lo, :]
            qm = q[lo:lo + SUB_BLOCK] * jnp.exp2(b2[lo:lo + SUB_BLOCK] - ref)
            q_parts = [zeros(lo), qm] + ([zeros(CHUNK - lo - SUB_BLOCK)] if lo + SUB_BLOCK < CHUNK else [])
            q_slabs.append(jnp.concatenate(q_parts, axis=0))
            k_slabs.append(jnp.concatenate([jnp.exp2(ref - c2[:lo]), zeros(CHUNK - lo)], axis=0))
        a_off = _dot_nt(jnp.concatenate(q_slabs, axis=1).astype(BF16),
                        jnp.concatenate(k_slabs, axis=1).astype(BF16))
        yield

        order = [(s, m, u) for s in range(SUB_BLOCK) for m in range(n_sub) for u in range(2)
                 if not (u == 0 and s >= SUBLANES)]
        pieces = []
        for s, m, u in order:
            cs = c2[SUB_BLOCK * m + s:SUB_BLOCK * m + s + 1]
            pieces.append(piece(q, m, u) * jnp.exp2(piece(b2, m, u) - cs))
        sums = [jnp.sum(p, axis=-1, keepdims=True) for p in pieces]
        yield
        d = {(m, u): zero8 for m in range(n_sub) for u in range(2)}
        for n, (s, m, u) in enumerate(order):
            d[(m, u)] = jnp.where(lane8 == s, sums[n], d[(m, u)])
        diag_rows = []
        for m in range(n_sub):
            for u in range(2):
                dm = jnp.where(sub8 + SUBLANES * u >= lane8, d[(m, u)], 0.0)
                diag_rows.append(pltpu.roll(dm, SUB_BLOCK * m, 1) if m else dm)
        a_diag = jnp.concatenate(diag_rows, axis=0)[:, :CHUNK]
        a_full = jnp.where(below_diag_block, a_off, a_diag)
        o = o + _dot(a_full.astype(BF16), v)
        yield

        r = lax.rsqrt(jnp.mean(o * o, axis=-1, keepdims=True) + EPS)
        y = ((o * r) * hn_ref[...]) * _silu(ag)

        st_new = st * jnp.exp2(b2_last) + _dot_tn(v, jnp.exp2(b2_last - c2).astype(BF16))
        return y.astype(BF16), st_new

    for c in range(n_chunks):
        r0 = c * CHUNK
        if seq_per_chunk:
            for hd in range(N_HEADS_A):
                st_s[hd] = s0_ref[c, hd].T
        gens = [unit(hd, r0) for hd in range(N_HEADS_A)] + [project(c)]
        results = _round_robin(gens)[:N_HEADS_A]
        for hd, (y, st_new) in enumerate(results):
            y_ref[0, pl.ds(r0, CHUNK), LANES * hd:LANES * (hd + 1)] = y
            if seq_per_chunk:
                snew_ref[c, hd] = st_new.T
            else:
                st_s[hd] = st_new

    if not seq_per_chunk:
        @pl.when(i_lag == ni - 1)
        def _():
            for hd in range(N_HEADS_A):
                snew_ref[0, hd] = st_s[hd].T


def _lagged_maps(n_blocks, ni):
    def split(m):
        return m // ni, m % ni
    x_map = lambda n: split(jnp.minimum(n, n_blocks - 1)) + (0,)
    y_map = lambda n: split(jnp.maximum(n - 1, 0)) + (0,)
    pos_map = lambda n: (jnp.minimum(n, n_blocks - 1) % ni, 0)
    batch_map = lambda nd: (lambda n: (jnp.maximum(n - 1, 0) // ni,) + (0,) * (nd - 1))
    return x_map, y_map, pos_map, batch_map


def _hgrn_call(h, w_a, lb_pack, hn, s0, layer, tl):
    n_seq, seq, _ = h.shape
    h, tl, group, seq_per_chunk = _batch_one_chunk_sequences(h, tl)
    bsz = h.shape[0]
    ni = h.shape[1] // tl
    n_blocks = bsz * ni
    x_map, y_map, _, batch_map = _lagged_maps(n_blocks, ni)
    const2 = lambda n: (0, 0)
    state_spec = pl.BlockSpec((group, N_HEADS_A, HEAD_K_A, HEAD_K_A), batch_map(4))
    y, s_new = pl.pallas_call(
        functools.partial(_hgrn_kernel, tl=tl, ni=ni, seq_per_chunk=seq_per_chunk),
        grid=(n_blocks + 1,),
        in_specs=[
            pl.BlockSpec((1, tl, D_MODEL), x_map),
            pl.BlockSpec(w_a.shape, lambda n: (0, 0, 0), pipeline_mode=pl.Buffered(1)),
            pl.BlockSpec((2, N_HEADS_A, 1, HEAD_K_A), lambda n: (0, 0, 0, 0)),
            pl.BlockSpec((1, HEAD_K_A), const2),
            pl.BlockSpec((None, group, N_HEADS_A, HEAD_K_A, HEAD_K_A), lambda n: (layer,) + batch_map(4)(n)),
        ],
        out_specs=[pl.BlockSpec((1, tl, W_A), y_map), state_spec],
        out_shape=[
            jax.ShapeDtypeStruct((bsz, ni * tl, W_A), BF16),
            jax.ShapeDtypeStruct((n_seq, N_HEADS_A, HEAD_K_A, HEAD_K_A), F32),
        ],
        scratch_shapes=[
            pltpu.VMEM((2, 4 * N_HEADS_A, tl, LANES), F32),
            pltpu.VMEM((N_HEADS_A, HEAD_K_A, HEAD_K_A), F32),
        ],
        compiler_params=pltpu.CompilerParams(
            dimension_semantics=("arbitrary",), vmem_limit_bytes=VMEM_LIMIT),
        name="hgrn_mixer",
    )(h, w_a, lb_pack, hn, s0)
    return y.reshape(n_seq, seq, W_A), s_new


def _rotate(xs, cos, sin_a, sin_b):
    return xs * cos + pltpu.roll(xs, LANES - ROT_DIM // 2, 1) * sin_a + pltpu.roll(xs, ROT_DIM // 2, 1) * sin_b


def _swa_kernel(*refs, tl, ni, n_blocks, has_cache, from_x, seq_per_chunk):
    it = iter(refs)
    if from_x:
        x_ref, nw_ref = next(it), next(it)
    else:
        h_ref = next(it)
    w_ref, cos_ref, sa_ref, sb_ref, sink_ref = (next(it) for _ in range(5))
    if has_cache:
        kc_ref, vc_ref = next(it), next(it)
    y_ref, knew_ref, vnew_ref = (next(it) for _ in range(3))
    if from_x:
        h_ref = next(it)
    q_s, g_s, kbuf, vbuf = (next(it) for _ in range(4))
    n = pl.program_id(0)
    n_chunks = tl // CHUNK
    band = WINDOW + CHUNK
    kvw = 2 * KV_W_B
    i_proj = jnp.minimum(n, n_blocks - 1) % ni
    i_lag = jnp.maximum(n - 1, 0) % ni
    nxt = n % 2
    cur = 1 - nxt
    n_seg = n_chunks if seq_per_chunk else 1
    seg_rows = tl // n_seg
    seg_prev = [slice((WINDOW + seg_rows) * s, (WINDOW + seg_rows) * s + WINDOW) for s in range(n_seg)]
    seg_new = [slice(sp.stop, sp.stop + seg_rows) for sp in seg_prev]
    seg_last = [slice(sn.stop - WINDOW, sn.stop) for sn in seg_new]

    @pl.when(n == 0)
    def _():
        for buf in (q_s, g_s, kbuf, vbuf):
            buf[...] = jnp.zeros_like(buf)

    if seq_per_chunk:
        for s in range(n_seg):
            kbuf[nxt, seg_prev[s], :] = kc_ref[s]
            vbuf[nxt, seg_prev[s], :] = vc_ref[s]
    else:
        @pl.when(i_proj == 0)
        def _():
            if has_cache:
                kbuf[nxt, seg_prev[0], :] = kc_ref[0]
                vbuf[nxt, seg_prev[0], :] = vc_ref[0]
            else:
                kbuf[nxt, seg_prev[0], :] = jnp.zeros((WINDOW, kvw), F32)
                vbuf[nxt, seg_prev[0], :] = jnp.zeros((WINDOW, kvw), F32)

        @pl.when(i_proj != 0)
        def _():
            kbuf[nxt, seg_prev[0], :] = kbuf[cur, seg_last[0], :]
            vbuf[nxt, seg_prev[0], :] = vbuf[cur, seg_last[0], :]

    if from_x:
        h_ref[0] = _prenorm_bf16(x_ref[0], nw_ref[...])

    def store_new(buf, cols, val):
        for s in range(n_seg):
            buf[nxt, seg_new[s], cols] = val[seg_rows * s:seg_rows * (s + 1)]

    def project(c):
        per = (W_B // MXU_COLS) // n_chunks
        for j in range(c * per, (c + 1) * per):
            sl = slice(MXU_COLS * j, MXU_COLS * (j + 1))
            res = _dot(h_ref[0], w_ref[:, sl])
            yield
            for u in range(MXU_COLS // LANES):
                q_s[nxt, :, MXU_COLS * j + LANES * u:MXU_COLS * j + LANES * (u + 1)] = _rotate(
                    res[:, LANES * u:LANES * (u + 1)], cos_ref[...], sa_ref[...], sb_ref[...])
            g_s[nxt, :, sl] = _dot(h_ref[0], w_ref[:, W_B + 2 * KV_W_B + MXU_COLS * j:W_B + 2 * KV_W_B + MXU_COLS * (j + 1)])
            yield
            if j < 2:
                res = _dot(h_ref[0], w_ref[:, W_B + KV_W_B * j:W_B + KV_W_B * (j + 1)])
                yield
                for u in range(KV_W_B // LANES):
                    a = res[:, LANES * u:LANES * (u + 1)]
                    if j == 0:
                        a = _rotate(a, cos_ref[...], sa_ref[...], sb_ref[...])
                    swapped = pltpu.roll(a, HEAD_DIM_B, 1)
                    buf = kbuf if j == 0 else vbuf
                    store_new(buf, slice(LANES * 2 * u, LANES * (2 * u + 1)), jnp.where(low1, a, swapped))
                    store_new(buf, slice(LANES * (2 * u + 1), LANES * (2 * u + 2)), jnp.where(low1, swapped, a))

    lane = lax.broadcasted_iota(jnp.int32, (CHUNK, LANES), 1)
    low = lane < HEAD_DIM_B
    low1 = lax.broadcasted_iota(jnp.int32, (1, LANES), 1) < HEAD_DIM_B
    key_row = lax.broadcasted_iota(jnp.int32, (band, LANES), 0)

    def pair(c, j):
        r0 = CHUNK * c
        g = j // 2
        qp = q_s[cur, r0:r0 + CHUNK, LANES * j:LANES * (j + 1)]
        q2 = jnp.concatenate([jnp.where(low, qp, 0.0), jnp.where(low, 0.0, qp)], axis=0).astype(BF16)
        b0 = band * c if seq_per_chunk else r0
        kb = kbuf[cur, b0:b0 + band, LANES * g:LANES * (g + 1)].astype(BF16)
        vb = vbuf[cur, b0:b0 + band, LANES * g:LANES * (g + 1)].astype(BF16)
        s = _dot_nt(kb, q2) * ATTN_SCALE
        yield
        if not has_cache:
            s = jnp.where(i_lag * tl + r0 - WINDOW + key_row >= 0, s, -jnp.inf)
        sk = jnp.where(low1, sink_ref[2 * j:2 * j + 1, :], sink_ref[2 * j + 1:2 * j + 2, :])
        m = jnp.maximum(jnp.max(s, axis=0, keepdims=True), sk)
        p = jnp.exp(s - m)
        den = jnp.sum(p, axis=0, keepdims=True) + jnp.exp(sk - m)
        o2 = _dot_tn((p * (1.0 / den)).astype(BF16), vb)
        yield
        o = jnp.where(low, o2[:CHUNK], o2[CHUNK:])
        y = o * _silu(g_s[cur, r0:r0 + CHUNK, LANES * j:LANES * (j + 1)])
        y_ref[0, r0:r0 + CHUNK, LANES * j:LANES * (j + 1)] = y.astype(BF16)

    for c in range(n_chunks):
        _round_robin([pair(c, j) for j in range(N_Q_B // 2)], project(c), heavy_round=1)

    def write_newest(s):
        ktail = kbuf[nxt, seg_last[s], :]
        vtail = vbuf[nxt, seg_last[s], :]
        low_w = lax.broadcasted_iota(jnp.int32, (WINDOW, LANES), 1) < HEAD_DIM_B
        for u in range(KV_W_B // LANES):
            knew_ref[s, :, LANES * u:LANES * (u + 1)] = jnp.where(
                low_w, ktail[:, 2 * LANES * u:2 * LANES * u + LANES],
                ktail[:, 2 * LANES * u + LANES:2 * LANES * (u + 1)])
            vnew_ref[s, :, LANES * u:LANES * (u + 1)] = jnp.where(
                low_w, vtail[:, 2 * LANES * u:2 * LANES * u + LANES],
                vtail[:, 2 * LANES * u + LANES:2 * LANES * (u + 1)])

    if seq_per_chunk:
        for s in range(n_seg):
            write_newest(s)
    else:
        pl.when(i_proj == ni - 1)(lambda: write_newest(0))


def _swa_call(h, x_norm, w_b, cos, sin_a, sin_b, sinks, kc, vc, tl):
    from_x = h is None
    act = x_norm[0] if from_x else h
    n_seq, seq, _ = act.shape
    has_cache = kc is not None
    group, seq_per_chunk = 1, False
    if has_cache:
        act, tl, group, seq_per_chunk = _batch_one_chunk_sequences(act, tl)
    if seq_per_chunk:
        cos, sin_a, sin_b = (jnp.tile(t, (group, 1)) for t in (cos, sin_a, sin_b))
    bsz = act.shape[0]
    ni = act.shape[1] // tl
    n_blocks = bsz * ni
    x_map, y_map, pos_map, _ = _lagged_maps(n_blocks, ni)
    proj_batch_map = lambda n: (jnp.minimum(n, n_blocks - 1) // ni, 0, 0)
    const2 = lambda n: (0, 0)
    wcols = w_b.shape[1]
    act_spec = pl.BlockSpec((1, tl, D_MODEL), x_map)
    in_specs = [act_spec, pl.BlockSpec((1, D_MODEL), const2)] if from_x else [act_spec]
    args = [act, x_norm[1]] if from_x else [act]
    in_specs += [
        pl.BlockSpec((D_MODEL, wcols), const2, pipeline_mode=pl.Buffered(1)),
        pl.BlockSpec((tl, LANES), pos_map),
        pl.BlockSpec((tl, LANES), pos_map),
        pl.BlockSpec((tl, LANES), pos_map),
        pl.BlockSpec((N_Q_B, LANES), const2),
    ]
    args += [w_b, cos, sin_a, sin_b, sinks]
    if has_cache:
        in_specs += [pl.BlockSpec((group, WINDOW, 2 * KV_W_B), proj_batch_map)] * 2
        args += [kc, vc]
    out_specs = [
        pl.BlockSpec((1, tl, W_B), y_map),
        pl.BlockSpec((group, WINDOW, KV_W_B), proj_batch_map),
        pl.BlockSpec((group, WINDOW, KV_W_B), proj_batch_map),
    ]
    out_shape = [
        jax.ShapeDtypeStruct((bsz, ni * tl, W_B), BF16),
        jax.ShapeDtypeStruct((n_seq, WINDOW, KV_W_B), F32),
        jax.ShapeDtypeStruct((n_seq, WINDOW, KV_W_B), F32),
    ]
    if from_x:
        out_specs.append(act_spec)
        out_shape.append(jax.ShapeDtypeStruct((bsz, ni * tl, D_MODEL), BF16))
    kv_rows = group * WINDOW + tl
    outs = pl.pallas_call(
        functools.partial(_swa_kernel, tl=tl, ni=ni, n_blocks=n_blocks, has_cache=has_cache, from_x=from_x,
                          seq_per_chunk=seq_per_chunk),
        grid=(n_blocks + 1,),
        in_specs=in_specs,
        out_specs=out_specs,
        out_shape=out_shape,
        scratch_shapes=[
            pltpu.VMEM((2, tl, W_B), F32),
            pltpu.VMEM((2, tl, W_B), F32),
            pltpu.VMEM((2, kv_rows, 2 * KV_W_B), F32),
            pltpu.VMEM((2, kv_rows, 2 * KV_W_B), F32),
        ],
        compiler_params=pltpu.CompilerParams(
            dimension_semantics=("arbitrary",), vmem_limit_bytes=VMEM_LIMIT),
        name="swa_mixer",
    )(*args)
    outs = list(outs)
    outs[0] = outs[0].reshape(n_seq, seq, W_B)
    if from_x:
        outs[3] = outs[3].reshape(n_seq, seq, D_MODEL)
    return outs


CONV_PAD = 8


def _ssd_kernel(h_ref, w_ref, cw_ref, cb_ref, dtb_ref, alog_ref, dsk_ref, sn_ref, conv0_ref, s0_ref,
                y_ref, snew_ref, convnew_ref,
                ubuf, dt_s, z_s, xc_s, xdt_s, bc_s, la_s, st_s, *, tl, ni, n_blocks, seq_per_chunk):
    n = pl.program_id(0)
    n_chunks = tl // CHUNK
    gw = N_GROUPS_C * D_STATE
    hp = W_C // N_GROUPS_C
    i_proj = jnp.minimum(n, n_blocks - 1) % ni
    i_lag = jnp.maximum(n - 1, 0) % ni
    nxt = n % 2
    cur = 1 - nxt
    n_seg = n_chunks if seq_per_chunk else 1
    seg_rows = tl // n_seg
    seg_new = [slice((CONV_PAD + seg_rows) * s + CONV_PAD, (CONV_PAD + seg_rows) * (s + 1)) for s in range(n_seg)]
    seg_tail = [slice(sn.start - (CONV_W - 1), sn.start) for sn in seg_new]
    seg_last = [slice(sn.stop - (CONV_W - 1), sn.stop) for sn in seg_new]

    @pl.when(n == 0)
    def _():
        for buf in (ubuf, z_s, xc_s, xdt_s, bc_s, la_s):
            buf[...] = jnp.zeros_like(buf)

    if seq_per_chunk:
        for s in range(n_seg):
            ubuf[seg_tail[s], :] = conv0_ref[s]
    else:
        @pl.when(i_proj == 0)
        def _():
            ubuf[seg_tail[0], :] = conv0_ref[0]

        @pl.when(i_lag == 0)
        def _():
            for g in range(N_GROUPS_C):
                st_s[g] = s0_ref[0, hp * g:hp * (g + 1), :].T

    def store_raw(cols, res):
        for s in range(n_seg):
            ubuf[seg_new[s], cols] = res[seg_rows * s:seg_rows * (s + 1)]

    def conv_act(sl):
        win = ubuf[:, sl]
        acc = cb_ref[:, sl]
        for t in range(CONV_W):
            back = CONV_W - 1 - t
            src = pltpu.roll(win, back, 0) if back else win
            acc = acc + src * cw_ref[t:t + 1, sl]
        if not seq_per_chunk:
            ubuf[seg_tail[0], sl] = ubuf[seg_last[0], sl]
        return _silu(jnp.concatenate([acc[sn] for sn in seg_new], axis=0) if n_seg > 1 else acc[seg_new[0]])

    def project(c):
        per = (W_C // MXU_COLS) // n_chunks
        for j in range(c * per, (c + 1) * per):
            sl = slice(MXU_COLS * j, MXU_COLS * (j + 1))
            z_s[nxt, :, sl] = _dot(h_ref[0], w_ref[:, sl])
            yield
            if j == 0:
                dt_raw = _dot(h_ref[0], w_ref[:, W_C + CONV_DIM:W_C + CONV_DIM + LANES])
                yield
                dt_s[...] = _softplus0(dt_raw + dtb_ref[...])
            head_of_col = (MXU_COLS * j + lax.broadcasted_iota(jnp.int32, (LANES, MXU_COLS), 1)) // HEAD_DIM_C
            spread = (lax.broadcasted_iota(jnp.int32, (LANES, MXU_COLS), 0) == head_of_col).astype(BF16)
            dt = _exact_dot(dt_s[...], spread)
            yield
            la_s[nxt, :, sl] = -dt * jnp.exp(alog_ref[:, sl])
            store_raw(sl, _dot(h_ref[0], w_ref[:, W_C + MXU_COLS * j:W_C + MXU_COLS * (j + 1)]))
            yield
            for u in range(MXU_COLS // LANES):
                usl = slice(MXU_COLS * j + LANES * u, MXU_COLS * j + LANES * (u + 1))
                act = conv_act(usl)
                xc_s[nxt, :, usl] = act
                xdt_s[nxt, :, usl] = act * dt[:, LANES * u:LANES * (u + 1)]
            bsl = slice(W_C + MXU_COLS * j, W_C + MXU_COLS * (j + 1))
            store_raw(bsl, _dot(h_ref[0], w_ref[:, W_C + W_C + MXU_COLS * j:W_C + W_C + MXU_COLS * (j + 1)]))
            yield
            for u in range(MXU_COLS // LANES):
                usl = slice(W_C + MXU_COLS * j + LANES * u, W_C + MXU_COLS * j + LANES * (u + 1))
                bc_s[nxt, :, MXU_COLS * j + LANES * u:MXU_COLS * j + LANES * (u + 1)] = conv_act(usl)

    tri = _tri_bf16(CHUNK)
    trow = lax.broadcasted_iota(jnp.int32, (CHUNK, LANES), 0)
    lane = lax.broadcasted_iota(jnp.int32, (CHUNK, LANES), 1)
    s_of_lane = lane % CHUNK
    causal = trow >= s_of_lane
    row2 = lax.broadcasted_iota(jnp.int32, (2 * CHUNK, LANES), 0)
    lane2 = lax.broadcasted_iota(jnp.int32, (2 * CHUNK, LANES), 1)
    blockdiag = (row2 < CHUNK) == (lane2 < HEAD_DIM_C)

    for c in range(n_chunks):
        rows = slice(CHUNK * c, CHUNK * (c + 1))

        def group(g, rows=rows):
            gsl = slice(hp * g, hp * (g + 1))
            bg = bc_s[cur, rows, D_STATE * g:D_STATE * (g + 1)].astype(BF16)
            cg = bc_s[cur, rows, gw + D_STATE * g:gw + D_STATE * (g + 1)].astype(BF16)
            cb2 = _dot_nt(cg, jnp.concatenate([bg, bg], axis=0))
            yield
            st = st_s[g]
            y_state = _dot(cg, st.astype(BF16))
            yield
            la = la_s[cur, rows, gsl]
            b = _cumsum_rows(tri, la)
            yield
            xdt = xdt_s[cur, rows, gsl]
            ys = []
            for u in range(2):
                usl = slice(LANES * u, LANES * (u + 1))
                b_row = jnp.sum(jnp.where(trow <= s_of_lane, la[:, usl], 0.0), axis=0, keepdims=True)
                decay = jnp.where(causal, jnp.exp(b[:, usl] - b_row), 0.0)
                wmat = (cb2 * decay).astype(BF16)
                xbd = jnp.where(blockdiag, jnp.concatenate([xdt[:, usl], xdt[:, usl]], axis=0), 0.0).astype(BF16)
                ys.append(_dot(wmat, xbd))
                yield
            y = jnp.concatenate(ys, axis=1) + y_state * jnp.exp(b)
            o = (y + dsk_ref[:, gsl] * xc_s[cur, rows, gsl]) * _silu(z_s[cur, rows, gsl])
            r = lax.rsqrt(jnp.mean(o * o, axis=-1, keepdims=True) + EPS)
            out = ((o * r) * sn_ref[:, gsl]).astype(BF16)
            b_last = b[CHUNK - 1:CHUNK, :]
            st_new = st * jnp.exp(b_last) + _dot_tn(bg, (xdt * jnp.exp(b_last - b)).astype(BF16))
            return out, st_new

        if seq_per_chunk:
            for g in range(N_GROUPS_C):
                st_s[g] = s0_ref[c, hp * g:hp * (g + 1), :].T
        results = _round_robin([group(g) for g in range(N_GROUPS_C)] + [project(c)])[:N_GROUPS_C]
        for g, (out, st_new) in enumerate(results):
            y_ref[0, rows, hp * g:hp * (g + 1)] = out
            if seq_per_chunk:
                snew_ref[c, hp * g:hp * (g + 1), :] = st_new.T
            else:
                st_s[g] = st_new

    if seq_per_chunk:
        for s in range(n_seg):
            convnew_ref[s] = ubuf[seg_last[s], :]
    else:
        @pl.when(i_proj == ni - 1)
        def _():
            convnew_ref[0] = ubuf[seg_tail[0], :]

        @pl.when(i_lag == ni - 1)
        def _():
            for g in range(N_GROUPS_C):
                snew_ref[0, hp * g:hp * (g + 1), :] = st_s[g].T


def _batch_one_chunk_sequences(h, tl):
    n_seq, seq, _ = h.shape
    group = PROMPT_ROWS // CHUNK
    if seq == CHUNK and n_seq % group == 0:
        return h.reshape(n_seq // group, PROMPT_ROWS, D_MODEL), PROMPT_ROWS, group, True
    return h, tl, 1, False


def _ssd_call(h, w_c, cw, cb, dtb, alog, dsk, sn, conv0, s0, layer, tl):
    n_seq, seq, _ = h.shape
    h, tl, group, seq_per_chunk = _batch_one_chunk_sequences(h, tl)
    bsz = h.shape[0]
    ni = h.shape[1] // tl
    n_blocks = bsz * ni
    x_map, y_map, _, batch_map = _lagged_maps(n_blocks, ni)
    proj_batch_map = lambda n: (jnp.minimum(n, n_blocks - 1) // ni, 0, 0)
    const2 = lambda n: (0, 0)
    wcols = w_c.shape[1]
    y, s_new, conv_new = pl.pallas_call(
        functools.partial(_ssd_kernel, tl=tl, ni=ni, n_blocks=n_blocks, seq_per_chunk=seq_per_chunk),
        grid=(n_blocks + 1,),
        in_specs=[
            pl.BlockSpec((1, tl, D_MODEL), x_map),
            pl.BlockSpec((D_MODEL, wcols), const2, pipeline_mode=pl.Buffered(1)),
            pl.BlockSpec((CONV_W, CONV_DIM), const2),
            pl.BlockSpec((1, CONV_DIM), const2),
            pl.BlockSpec((1, LANES), const2),
            pl.BlockSpec((1, W_C), const2),
            pl.BlockSpec((1, W_C), const2),
            pl.BlockSpec((1, W_C), const2),
            pl.BlockSpec((None, group, CONV_W - 1, CONV_DIM), lambda n: (layer,) + proj_batch_map(n)),
            pl.BlockSpec((None, group, W_C, D_STATE), lambda n: (layer,) + batch_map(3)(n)),
        ],
        out_specs=[
            pl.BlockSpec((1, tl, W_C), y_map),
            pl.BlockSpec((group, W_C, D_STATE), batch_map(3)),
            pl.BlockSpec((group, CONV_W - 1, CONV_DIM), proj_batch_map),
        ],
        out_shape=[
            jax.ShapeDtypeStruct((bsz, ni * tl, W_C), BF16),
            jax.ShapeDtypeStruct((n_seq, W_C, D_STATE), F32),
            jax.ShapeDtypeStruct((n_seq, CONV_W - 1, CONV_DIM), F32),
        ],
        scratch_shapes=[
            pltpu.VMEM((group * CONV_PAD + tl, CONV_DIM), F32),
            pltpu.VMEM((tl, LANES), F32),
            pltpu.VMEM((2, tl, W_C), F32),
            pltpu.VMEM((2, tl, W_C), F32),
            pltpu.VMEM((2, tl, W_C), F32),
            pltpu.VMEM((2, tl, 2 * N_GROUPS_C * D_STATE), F32),
            pltpu.VMEM((2, tl, W_C), F32),
            pltpu.VMEM((N_GROUPS_C, D_STATE, W_C // N_GROUPS_C), F32),
        ],
        compiler_params=pltpu.CompilerParams(
            dimension_semantics=("arbitrary",), vmem_limit_bytes=VMEM_LIMIT),
        name="ssd_mixer",
    )(h, w_c, cw, cb, dtb, alog, dsk, sn, conv0, s0)
    return y.reshape(n_seq, seq, W_C), s_new, conv_new


def _merge_kernel(*refs, emit_next):
    it = iter(refs)
    x_ref, h_ref = next(it), next(it)
    y_refs = [next(it) for _ in range(3)]
    npost_ref = next(it)
    nnext_ref = next(it) if emit_next else None
    wg_refs = [next(it) for _ in range(3)]
    wbr_refs = [next(it) for _ in range(3)]
    wout_ref, o_ref = next(it), next(it)
    hnext_ref = next(it) if emit_next else None
    m_s = next(it)
    j = pl.program_id(1)
    n_j = m_s.shape[0]

    h = h_ref[...]
    merged = None
    for y_ref, wg_ref, wbr_ref in zip(y_refs, wg_refs, wbr_refs):
        term = _sigmoid(_dot(h, wg_ref[...])) * _dot(y_ref[...], wbr_ref[...])
        merged = term if merged is None else merged + term
    m_s[j] = merged.astype(BF16)

    @pl.when(j == n_j - 1)
    def _():
        out = _dot(jnp.concatenate([m_s[n] for n in range(n_j)], axis=1), wout_ref[...])
        r = lax.rsqrt(jnp.mean(out * out, axis=-1, keepdims=True) + EPS)
        x_new = x_ref[...] + (out * r) * npost_ref[...]
        o_ref[...] = x_new
        if emit_next:
            hnext_ref[...] = _prenorm_bf16(x_new, nnext_ref[...])


def _merge_call(x2, h2, ya, yb, yc, npost, nnext, wg, wbr, wout, tm, tn):
    rows = x2.shape[0]
    emit_next = nnext is not None
    grid = (rows // tm, D_MODEL // tn)
    const2 = lambda r, j: (0, 0)
    row_block = lambda width: pl.BlockSpec((tm, width), lambda r, j: (r, 0))
    vec = pl.BlockSpec((1, D_MODEL), const2)
    in_specs = [row_block(D_MODEL), row_block(D_MODEL), row_block(W_A), row_block(W_B), row_block(W_C), vec]
    args = [x2, h2, ya, yb, yc, npost]
    if emit_next:
        in_specs.append(vec)
        args.append(nnext)
    col_block = lambda k: pl.BlockSpec((k, tn), lambda r, j: (0, j))
    in_specs += [col_block(D_MODEL)] * 3 + [col_block(W_A)] * 3
    in_specs.append(pl.BlockSpec((D_MODEL, D_MODEL), const2, pipeline_mode=pl.Buffered(1)))
    args += list(wg) + list(wbr) + [wout]
    out_specs = [row_block(D_MODEL)]
    out_shape = [jax.ShapeDtypeStruct((rows, D_MODEL), F32)]
    if emit_next:
        out_specs.append(row_block(D_MODEL))
        out_shape.append(jax.ShapeDtypeStruct((rows, D_MODEL), BF16))
    outs = pl.pallas_call(
        functools.partial(_merge_kernel, emit_next=emit_next),
        grid=grid,
        in_specs=in_specs,
        out_specs=out_specs,
        out_shape=out_shape,
        scratch_shapes=[pltpu.VMEM((D_MODEL // tn, tm, tn), BF16)],
        compiler_params=pltpu.CompilerParams(
            dimension_semantics=("arbitrary", "arbitrary"), vmem_limit_bytes=VMEM_LIMIT),
        name="merge_out",
    )(*args)
    return (outs[0], outs[1]) if emit_next else (outs[0], None)


def _rope_tables(pos):
    half = ROT_DIM // 2
    inv = jnp.power(ROPE_THETA, -jnp.arange(half, dtype=F32) / half)
    ang = pos.astype(F32)[:, None] * inv[None, :]
    cos, sin = jnp.cos(ang), jnp.sin(ang)
    n = pos.shape[0]
    ones = jnp.ones((n, HEAD_DIM_B - ROT_DIM), F32)
    zeros = jnp.zeros((n, HEAD_DIM_B - ROT_DIM), F32)
    zh = jnp.zeros((n, half), F32)
    cos_t = jnp.concatenate([cos, cos, ones], axis=1)
    sin_a = jnp.concatenate([-sin, zh, zeros], axis=1)
    sin_b = jnp.concatenate([zh, sin, zeros], axis=1)
    tile = lambda t: jnp.concatenate([t, t], axis=1)
    return tile(cos_t), tile(sin_a), tile(sin_b)


def _dup_heads(t):
    lead = t.shape[:-1]
    t4 = t.reshape(lead + (N_KV_B, 1, HEAD_DIM_B))
    return jnp.broadcast_to(t4, lead + (N_KV_B, 2, HEAD_DIM_B)).reshape(lead + (2 * KV_W_B,))


def _col_blocks(w):
    return w.reshape(w.shape[0], -1, MXU_COLS).transpose(1, 0, 2)


def _pad_lanes(t):
    return jnp.pad(t, [(0, 0)] * (t.ndim - 1) + [(0, LANES - t.shape[-1])])


def _expand_heads(t):
    return jnp.repeat(t, HEAD_DIM_C, axis=-1)


def _layer(x, h, tl, tables, caches, lw, norm_next):
    bsz, seq, _ = x.shape
    cos, sin_a, sin_b = tables
    kc, vc, s_hgrn, s_ssm, s_conv, layer = caches
    if s_hgrn is None:
        layer = 0
        s_hgrn = jnp.zeros((1, bsz, N_HEADS_A, HEAD_K_A, HEAD_K_A), F32)
        s_ssm = jnp.zeros((1, bsz, W_C, D_STATE), F32)
        s_conv = jnp.zeros((1, bsz, CONV_W - 1, CONV_DIM), F32)
    else:
        s_ssm = s_ssm.reshape(-1, bsz, W_C, D_STATE)
        kc = _dup_heads(kc.reshape(bsz, WINDOW, KV_W_B))
        vc = _dup_heads(vc.reshape(bsz, WINDOW, KV_W_B))

    if h is None:
        yb, k_new, v_new, h = _swa_call(None, (x, lw["norm_pre"]), lw["w_b"], cos, sin_a, sin_b, lw["sinks"],
                                        kc, vc, tl)
    else:
        yb, k_new, v_new = _swa_call(h, None, lw["w_b"], cos, sin_a, sin_b, lw["sinks"], kc, vc, tl)
    ya, hgrn_new = _hgrn_call(h, lw["w_a"], lw["lb_pack"], lw["hgrn_norm"], s_hgrn, layer, tl)
    yc, ssm_new, conv_new = _ssd_call(h, lw["w_c"], lw["conv_w"], lw["conv_b"], lw["dt_bias"],
                                      lw["a_log"], lw["d_skip"], lw["ssm_norm"], s_conv, s_ssm, layer, tl)
    rows = bsz * seq
    x_new, h_next = _merge_call(x.reshape(rows, D_MODEL), h.reshape(rows, D_MODEL), ya.reshape(rows, W_A),
                                yb.reshape(rows, W_B), yc.reshape(rows, W_C), lw["norm_post"], norm_next,
                                lw["w_g"], lw["w_br"], lw["w_out"], min(MERGE_ROWS, rows), MXU_COLS)
    states = (k_new.reshape(bsz, WINDOW, N_KV_B, HEAD_DIM_B), v_new.reshape(bsz, WINDOW, N_KV_B, HEAD_DIM_B),
              hgrn_new, ssm_new.reshape(bsz, N_HEADS_C, HEAD_DIM_C, D_STATE), conv_new)
    if h_next is not None:
        h_next = h_next.reshape(bsz, seq, D_MODEL)
    return x_new.reshape(bsz, seq, D_MODEL), h_next, states


def kernel(x_prompt, x_sample, cache_swa_k, cache_swa_v, state_hgrn, state_ssm, state_conv, norm_pre, norm_post, w_in, hgrn_lb_logits, hgrn_norm, swa_sinks, conv_w, conv_b, dt_bias, a_log, d_skip, ssm_norm, w_branch_a, w_branch_b, w_branch_c, w_out):
    depth = w_in.shape[0]
    lbp = jax.nn.softmax(hgrn_lb_logits.astype(F32), axis=0)
    lbc = jnp.cumsum(lbp, axis=0)
    lb_all = lbc - lbc[0:1]

    offs = np.cumsum([0, W_A, W_A, W_A, W_A, W_B, KV_W_B, KV_W_B, W_B, W_C, CONV_DIM, N_HEADS_C,
                      D_MODEL, D_MODEL, D_MODEL])
    col = lambda l, a, b: w_in[l, :, int(offs[a]):int(offs[b])]

    tables_p = _rope_tables(jnp.arange(x_prompt.shape[1], dtype=jnp.int32))
    tables_s = _rope_tables(PAST_LEN + jnp.arange(x_sample.shape[1], dtype=jnp.int32))

    xp, xs = x_prompt, x_sample
    pre = lambda l: norm_pre[l].reshape(1, D_MODEL)
    hp = hs = None
    pst, sst = [], []
    for l in range(depth):
        lb = lb_all[l].reshape(N_HEADS_A, 1, HEAD_K_A)
        norm_next = pre(l + 1) if l + 1 < depth else None
        lw = {
            "norm_pre": pre(l),
            "norm_post": norm_post[l].reshape(1, D_MODEL),
            "w_a": _col_blocks(col(l, 0, 4).astype(BF16)),
            "w_b": col(l, 4, 8).astype(BF16),
            "w_c": jnp.concatenate([col(l, 8, 9), col(l, 9, 10), _pad_lanes(col(l, 10, 11))],
                                   axis=1).astype(BF16),
            "w_g": [col(l, k, k + 1).astype(BF16) for k in (11, 12, 13)],
            "w_br": [w[l].astype(BF16) for w in (w_branch_a, w_branch_b, w_branch_c)],
            "w_out": w_out[l].astype(BF16),
            "lb_pack": jnp.stack([jnp.log(lb), jnp.log1p(-lb)]) * LOG2E,
            "hgrn_norm": hgrn_norm[l].reshape(1, HEAD_K_A),
            "sinks": jnp.broadcast_to(swa_sinks[l].astype(F32)[:, None], (N_Q_B, LANES)),
            "conv_w": conv_w[l],
            "conv_b": conv_b[l].reshape(1, CONV_DIM),
            "dt_bias": _pad_lanes(dt_bias[l].astype(F32).reshape(1, N_HEADS_C)),
            "a_log": _expand_heads(a_log[l].astype(F32)).reshape(1, W_C),
            "d_skip": _expand_heads(d_skip[l].astype(F32)).reshape(1, W_C),
            "ssm_norm": ssm_norm[l].reshape(1, W_C),
        }
        xp, hp, sp = _layer(xp, hp, PROMPT_ROWS, tables_p, (None,) * 6, lw, norm_next)
        xs, hs, ss = _layer(xs, hs, CHUNK, tables_s,
                            (cache_swa_k[l], cache_swa_v[l], state_hgrn, state_ssm, state_conv, l),
                            lw, norm_next)
        pst.append(sp)
        sst.append(ss)

    stack = lambda sts, k: jnp.stack([s[k] for s in sts])
    return (xp, xs,
            stack(pst, 0), stack(pst, 1), stack(pst, 2), stack(pst, 3), stack(pst, 4),
            stack(sst, 0), stack(sst, 1), stack(sst, 2), stack(sst, 3), stack(sst, 4))
```

```python
import functools
import math

import jax
import jax.numpy as jnp
import numpy as np
from jax import lax
from jax.experimental import pallas as pl
from jax.experimental.pallas import tpu as pltpu

F32 = jnp.float32
BF16 = jnp.bfloat16

D_MODEL = 2048
CHUNK = 64
EPS = 1e-6
PAST_LEN = 4096

W_A = 1024
HEAD_K_A = 128
N_HEADS_A = 8

N_Q_B = 16
N_KV_B = 4
HEAD_DIM_B = 64
W_B = 1024
KV_W_B = 256
WINDOW = 128
ROT_DIM = 16
ROPE_THETA = 500000.0
ATTN_SCALE = HEAD_DIM_B ** -0.5

W_C = 1024
HEAD_DIM_C = 64
N_HEADS_C = 16
N_GROUPS_C = 4
D_STATE = 128
CONV_W = 4
CONV_DIM = 2048

LANES = 128
SUBLANES = 8
SUB_BLOCK = 16
MXU_COLS = 256
PROMPT_ROWS = 256
MERGE_ROWS = 512
LOG2E = math.log2(math.e)
VMEM_LIMIT = 56 * 1024 * 1024

NT_DIMS = (((1,), (1,)), ((), ()))
TN_DIMS = (((0,), (0,)), ((), ()))


def _dot(a, b):
    return jnp.dot(a, b, preferred_element_type=F32)


def _dot_nt(a, b):
    return lax.dot_general(a, b, NT_DIMS, preferred_element_type=F32)


def _dot_tn(a, b):
    return lax.dot_general(a, b, TN_DIMS, preferred_element_type=F32)


def _prenorm_bf16(x, w):
    r = lax.rsqrt(jnp.mean(x * x, axis=-1, keepdims=True) + EPS)
    return ((x * r) * w).astype(BF16)


def _sigmoid(x):
    return 0.5 * jnp.tanh(0.5 * x) + 0.5


def _silu(x):
    return x * _sigmoid(x)


def _softplus0(x):
    return jnp.maximum(x, 0.0) + jnp.log(1.0 + jnp.exp(-jnp.abs(x)))


def _tri_bf16(n):
    r = lax.broadcasted_iota(jnp.int32, (n, n), 0)
    c = lax.broadcasted_iota(jnp.int32, (n, n), 1)
    return (r >= c).astype(BF16)


def _split3(x):
    hi = x.astype(BF16)
    r1 = x - hi.astype(F32)
    mid = r1.astype(BF16)
    lo = (r1 - mid.astype(F32)).astype(BF16)
    return hi, mid, lo


def _cumsum_rows(tri, x):
    hi, mid, lo = _split3(x)
    return _dot(tri, hi) + _dot(tri, mid) + _dot(tri, lo)


def _exact_dot(x, sel):
    hi, mid, lo = _split3(x)
    return _dot(hi, sel) + _dot(mid, sel) + _dot(lo, sel)


def _round_robin(gens, background=None, heavy_round=0):
    results = [None] * len(gens)
    live = list(range(len(gens)))
    rnd = 0
    while live:
        if background is not None and rnd == heavy_round:
            for _ in background:
                pass
        for n in list(live):
            try:
                next(gens[n])
            except StopIteration as stop:
                results[n] = stop.value
                live.remove(n)
        rnd += 1
    return results


def _hgrn_kernel(h_ref, w_ref, lb_ref, hn_ref, s0_ref, y_ref, snew_ref,
                 proj_s, st_s, *, tl, ni, seq_per_chunk):
    n = pl.program_id(0)
    n_chunks = tl // CHUNK
    i_lag = jnp.maximum(n - 1, 0) % ni
    nxt = n % 2
    cur = 1 - nxt

    @pl.when(n == 0)
    def _():
        proj_s[...] = jnp.zeros_like(proj_s)

    if not seq_per_chunk:
        @pl.when(i_lag == 0)
        def _():
            for hd in range(N_HEADS_A):
                st_s[hd] = s0_ref[0, hd].T

    cols_per_chunk = (w_ref.shape[1] // MXU_COLS) // n_chunks

    def project(c):
        for j in range(cols_per_chunk):
            jb = c * cols_per_chunk + j
            res = _dot(h_ref[0], w_ref[:, MXU_COLS * jb:MXU_COLS * (jb + 1)])
            proj_s[nxt, 2 * jb] = res[:, :LANES]
            proj_s[nxt, 2 * jb + 1] = res[:, LANES:]
            yield

    tri = _tri_bf16(CHUNK)
    n_sub = CHUNK // SUB_BLOCK
    lane8 = lax.broadcasted_iota(jnp.int32, (SUBLANES, LANES), 1)
    sub8 = lax.broadcasted_iota(jnp.int32, (SUBLANES, LANES), 0)
    row64 = lax.broadcasted_iota(jnp.int32, (CHUNK, CHUNK), 0)
    col64 = lax.broadcasted_iota(jnp.int32, (CHUNK, CHUNK), 1)
    below_diag_block = row64 // SUB_BLOCK > col64 // SUB_BLOCK
    zero8 = jnp.zeros((SUBLANES, LANES), F32)

    def zeros(n):
        return jnp.zeros((n, LANES), F32)

    def piece(a, m, u):
        lo = SUB_BLOCK * m + SUBLANES * u
        return a[lo:lo + SUBLANES]

    def unit(hd, r0):
        rows = pl.ds(r0, CHUNK)
        aq = proj_s[cur, hd, rows, :]
        z = proj_s[cur, N_HEADS_A + hd, rows, :]
        v = proj_s[cur, 2 * N_HEADS_A + hd, rows, :].astype(BF16)
        ag = proj_s[cur, 3 * N_HEADS_A + hd, rows, :]
        log_lb = lb_ref[0, hd]
        log1m_lb = lb_ref[1, hd]

        z2 = z * LOG2E
        log_sig = jnp.minimum(z2, 0.0) - jnp.log2(1.0 + jnp.exp2(-jnp.abs(z2)))
        cterm = log1m_lb + log_sig
        log_f = jnp.maximum(log_lb, cterm) + jnp.log2(1.0 + jnp.exp2(-jnp.abs(log_lb - cterm)))
        log_k = cterm - z2
        q = _silu(aq)

        b2 = _cumsum_rows(tri, log_f)
        yield
        c2 = b2 - log_k
        b2_last = b2[CHUNK - 1:CHUNK, :]
        st = st_s[hd]

        o = _dot_nt((q * jnp.exp2(b2)).astype(BF16), st.astype(BF16))
        yield

        q_slabs, k_slabs = [], []
        for m in range(1, n_sub):
            lo = SUB_BLOCK * m
            ref = b2[lo - 1:lo, :]
            qm = q[lo:lo + SUB_BLOCK] * jnp.exp2(b2[lo:lo + SUB_BLOCK] - ref)
            q_parts = [zeros(lo), qm] + ([zeros(CHUNK - lo - SUB_BLOCK)] if lo + SUB_BLOCK < CHUNK else [])
            q_slabs.append(jnp.concatenate(q_parts, axis=0))
            k_slabs.append(jnp.concatenate([jnp.exp2(ref - c2[:lo]), zeros(CHUNK - lo)], axis=0))
        a_off = _dot_nt(jnp.concatenate(q_slabs, axis=1).astype(BF16),
                        jnp.concatenate(k_slabs, axis=1).astype(BF16))
        yield

        order = [(s, m, u) for s in range(SUB_BLOCK) for m in range(n_sub) for u in range(2)
                 if not (u == 0 and s >= SUBLANES)]
        pieces = []
        for s, m, u in order:
            cs = c2[SUB_BLOCK * m + s:SUB_BLOCK * m + s + 1]
            pieces.append(piece(q, m, u) * jnp.exp2(piece(b2, m, u) - cs))
        sums = [jnp.sum(p, axis=-1, keepdims=True) for p in pieces]
        yield
        d = {(m, u): zero8 for m in range(n_sub) for u in range(2)}
        for n, (s, m, u) in enumerate(order):
            d[(m, u)] = jnp.where(lane8 == s, sums[n], d[(m, u)])
        diag_rows = []
        for m in range(n_sub):
            for u in range(2):
                dm = jnp.where(sub8 + SUBLANES * u >= lane8, d[(m, u)], 0.0)
                diag_rows.append(pltpu.roll(dm, SUB_BLOCK * m, 1) if m else dm)
        a_diag = jnp.concatenate(diag_rows, axis=0)[:, :CHUNK]
        a_full = jnp.where(below_diag_block, a_off, a_diag)
        o = o + _dot(a_full.astype(BF16), v)
        yield

        r = lax.rsqrt(jnp.mean(o * o, axis=-1, keepdims=True) + EPS)
        y = ((o * r) * hn_ref[...]) * _silu(ag)

        st_new = st * jnp.exp2(b2_last) + _dot_tn(v, jnp.exp2(b2_last - c2).astype(BF16))
        return y.astype(BF16), st_new

    for c in range(n_chunks):
        r0 = c * CHUNK
        if seq_per_chunk:
            for hd in range(N_HEADS_A):
                st_s[hd] = s0_ref[c, hd].T
        gens = [unit(hd, r0) for hd in range(N_HEADS_A)] + [project(c)]
        results = _round_robin(gens)[:N_HEADS_A]
        for hd, (y, st_new) in enumerate(results):
            y_ref[0, pl.ds(r0, CHUNK), LANES * hd:LANES * (hd + 1)] = y
            if seq_per_chunk:
                snew_ref[c, hd] = st_new.T
            else:
                st_s[hd] = st_new

    if not seq_per_chunk:
        @pl.when(i_lag == ni - 1)
        def _():
            for hd in range(N_HEADS_A):
                snew_ref[0, hd] = st_s[hd].T


def _lagged_maps(n_blocks, ni):
    def split(m):
        return m // ni, m % ni
    x_map = lambda n: split(jnp.minimum(n, n_blocks - 1)) + (0,)
    y_map = lambda n: split(jnp.maximum(n - 1, 0)) + (0,)
    pos_map = lambda n: (jnp.minimum(n, n_blocks - 1) % ni, 0)
    batch_map = lambda nd: (lambda n: (jnp.maximum(n - 1, 0) // ni,) + (0,) * (nd - 1))
    return x_map, y_map, pos_map, batch_map


def _hgrn_call(h, w_all, w_layer, lb_pack, hn, s0, layer, tl):
    n_seq, seq, _ = h.shape
    h, tl, group, seq_per_chunk = _batch_one_chunk_sequences(h, tl)
    bsz = h.shape[0]
    ni = h.shape[1] // tl
    n_blocks = bsz * ni
    x_map, y_map, _, batch_map = _lagged_maps(n_blocks, ni)
    const2 = lambda n: (0, 0)
    state_spec = pl.BlockSpec((group, N_HEADS_A, HEAD_K_A, HEAD_K_A), batch_map(4))
    y, s_new = pl.pallas_call(
        functools.partial(_hgrn_kernel, tl=tl, ni=ni, seq_per_chunk=seq_per_chunk),
        grid=(n_blocks + 1,),
        in_specs=[
            pl.BlockSpec((1, tl, D_MODEL), x_map),
            pl.BlockSpec((None, D_MODEL, 4 * W_A), lambda n: (w_layer, 0, 0), pipeline_mode=pl.Buffered(1)),
            pl.BlockSpec((2, N_HEADS_A, 1, HEAD_K_A), lambda n: (0, 0, 0, 0)),
            pl.BlockSpec((1, HEAD_K_A), const2),
            pl.BlockSpec((None, group, N_HEADS_A, HEAD_K_A, HEAD_K_A), lambda n: (layer,) + batch_map(4)(n)),
        ],
        out_specs=[pl.BlockSpec((1, tl, W_A), y_map), state_spec],
        out_shape=[
            jax.ShapeDtypeStruct((bsz, ni * tl, W_A), BF16),
            jax.ShapeDtypeStruct((n_seq, N_HEADS_A, HEAD_K_A, HEAD_K_A), F32),
        ],
        scratch_shapes=[
            pltpu.VMEM((2, 4 * N_HEADS_A, tl, LANES), F32),
            pltpu.VMEM((N_HEADS_A, HEAD_K_A, HEAD_K_A), F32),
        ],
        compiler_params=pltpu.CompilerParams(
            dimension_semantics=("arbitrary",), vmem_limit_bytes=VMEM_LIMIT),
        name="hgrn_mixer",
    )(h, w_all, lb_pack, hn, s0)
    return y.reshape(n_seq, seq, W_A), s_new


def _rotate(xs, cos, sin_a, sin_b):
    return xs * cos + pltpu.roll(xs, LANES - ROT_DIM // 2, 1) * sin_a + pltpu.roll(xs, ROT_DIM // 2, 1) * sin_b


def _swa_kernel(*refs, tl, ni, n_blocks, has_cache, from_x, seq_per_chunk):
    it = iter(refs)
    if from_x:
        x_ref, nw_ref = next(it), next(it)
    else:
        h_ref = next(it)
    w_ref, cos_ref, sa_ref, sb_ref, sink_ref = (next(it) for _ in range(5))
    if has_cache:
        kc_ref, vc_ref = next(it), next(it)
    y_ref, knew_ref, vnew_ref = (next(it) for _ in range(3))
    if from_x:
        h_ref = next(it)
    q_s, g_s, kbuf, vbuf = (next(it) for _ in range(4))
    n = pl.program_id(0)
    n_chunks = tl // CHUNK
    band = WINDOW + CHUNK
    kvw = 2 * KV_W_B
    i_proj = jnp.minimum(n, n_blocks - 1) % ni
    i_lag = jnp.maximum(n - 1, 0) % ni
    nxt = n % 2
    cur = 1 - nxt
    n_seg = n_chunks if seq_per_chunk else 1
    seg_rows = tl // n_seg
    seg_prev = [slice((WINDOW + seg_rows) * s, (WINDOW + seg_rows) * s + WINDOW) for s in range(n_seg)]
    seg_new = [slice(sp.stop, sp.stop + seg_rows) for sp in seg_prev]
    seg_last = [slice(sn.stop - WINDOW, sn.stop) for sn in seg_new]

    @pl.when(n == 0)
    def _():
        for buf in (q_s, g_s, kbuf, vbuf):
            buf[...] = jnp.zeros_like(buf)

    if seq_per_chunk:
        for s in range(n_seg):
            kbuf[nxt, seg_prev[s], :] = kc_ref[s]
            vbuf[nxt, seg_prev[s], :] = vc_ref[s]
    else:
        @pl.when(i_proj == 0)
        def _():
            if has_cache:
                kbuf[nxt, seg_prev[0], :] = kc_ref[0]
                vbuf[nxt, seg_prev[0], :] = vc_ref[0]
            else:
                kbuf[nxt, seg_prev[0], :] = jnp.zeros((WINDOW, kvw), F32)
                vbuf[nxt, seg_prev[0], :] = jnp.zeros((WINDOW, kvw), F32)

        @pl.when(i_proj != 0)
        def _():
            kbuf[nxt, seg_prev[0], :] = kbuf[cur, seg_last[0], :]
            vbuf[nxt, seg_prev[0], :] = vbuf[cur, seg_last[0], :]

    if from_x:
        h_ref[0] = _prenorm_bf16(x_ref[0], nw_ref[...])

    def store_new(buf, cols, val):
        for s in range(n_seg):
            buf[nxt, seg_new[s], cols] = val[seg_rows * s:seg_rows * (s + 1)]

    def project(c):
        per = (W_B // MXU_COLS) // n_chunks
        for j in range(c * per, (c + 1) * per):
            sl = slice(MXU_COLS * j, MXU_COLS * (j + 1))
            res = _dot(h_ref[0], w_ref[:, sl])
            yield
            for u in range(MXU_COLS // LANES):
                q_s[nxt, :, MXU_COLS * j + LANES * u:MXU_COLS * j + LANES * (u + 1)] = _rotate(
                    res[:, LANES * u:LANES * (u + 1)], cos_ref[...], sa_ref[...], sb_ref[...])
            g_s[nxt, :, sl] = _dot(h_ref[0], w_ref[:, W_B + 2 * KV_W_B + MXU_COLS * j:W_B + 2 * KV_W_B + MXU_COLS * (j + 1)])
            yield
            if j < 2:
                res = _dot(h_ref[0], w_ref[:, W_B + KV_W_B * j:W_B + KV_W_B * (j + 1)])
                yield
                for u in range(KV_W_B // LANES):
                    a = res[:, LANES * u:LANES * (u + 1)]
                    if j == 0:
                        a = _rotate(a, cos_ref[...], sa_ref[...], sb_ref[...])
                    swapped = pltpu.roll(a, HEAD_DIM_B, 1)
                    buf = kbuf if j == 0 else vbuf
                    store_new(buf, slice(LANES * 2 * u, LANES * (2 * u + 1)), jnp.where(low1, a, swapped))
                    store_new(buf, slice(LANES * (2 * u + 1), LANES * (2 * u + 2)), jnp.where(low1, swapped, a))

    lane = lax.broadcasted_iota(jnp.int32, (CHUNK, LANES), 1)
    low = lane < HEAD_DIM_B
    low1 = lax.broadcasted_iota(jnp.int32, (1, LANES), 1) < HEAD_DIM_B
    key_row = lax.broadcasted_iota(jnp.int32, (band, LANES), 0)

    def pair(c, j):
        r0 = CHUNK * c
        g = j // 2
        qp = q_s[cur, r0:r0 + CHUNK, LANES * j:LANES * (j + 1)]
        q2 = jnp.concatenate([jnp.where(low, qp, 0.0), jnp.where(low, 0.0, qp)], axis=0).astype(BF16)
        b0 = band * c if seq_per_chunk else r0
        kb = kbuf[cur, b0:b0 + band, LANES * g:LANES * (g + 1)].astype(BF16)
        vb = vbuf[cur, b0:b0 + band, LANES * g:LANES * (g + 1)].astype(BF16)
        s = _dot_nt(kb, q2) * ATTN_SCALE
        yield
        if not has_cache:
            s = jnp.where(i_lag * tl + r0 - WINDOW + key_row >= 0, s, -jnp.inf)
        sk = jnp.where(low1, sink_ref[2 * j:2 * j + 1, :], sink_ref[2 * j + 1:2 * j + 2, :])
        m = jnp.maximum(jnp.max(s, axis=0, keepdims=True), sk)
        p = jnp.exp(s - m)
        den = jnp.sum(p, axis=0, keepdims=True) + jnp.exp(sk - m)
        o2 = _dot_tn((p * (1.0 / den)).astype(BF16), vb)
        yield
        o = jnp.where(low, o2[:CHUNK], o2[CHUNK:])
        y = o * _silu(g_s[cur, r0:r0 + CHUNK, LANES * j:LANES * (j + 1)])
        y_ref[0, r0:r0 + CHUNK, LANES * j:LANES * (j + 1)] = y.astype(BF16)

    for c in range(n_chunks):
        _round_robin([pair(c, j) for j in range(N_Q_B // 2)], project(c), heavy_round=1)

    def write_newest(s):
        ktail = kbuf[nxt, seg_last[s], :]
        vtail = vbuf[nxt, seg_last[s], :]
        low_w = lax.broadcasted_iota(jnp.int32, (WINDOW, LANES), 1) < HEAD_DIM_B
        for u in range(KV_W_B // LANES):
            knew_ref[s, :, LANES * u:LANES * (u + 1)] = jnp.where(
                low_w, ktail[:, 2 * LANES * u:2 * LANES * u + LANES],
                ktail[:, 2 * LANES * u + LANES:2 * LANES * (u + 1)])
            vnew_ref[s, :, LANES * u:LANES * (u + 1)] = jnp.where(
                low_w, vtail[:, 2 * LANES * u:2 * LANES * u + LANES],
                vtail[:, 2 * LANES * u + LANES:2 * LANES * (u + 1)])

    if seq_per_chunk:
        for s in range(n_seg):
            write_newest(s)
    else:
        pl.when(i_proj == ni - 1)(lambda: write_newest(0))


def _swa_call(h, x_norm, w_b, cos, sin_a, sin_b, sinks, kc, vc, tl):
    from_x = h is None
    act = x_norm[0] if from_x else h
    n_seq, seq, _ = act.shape
    has_cache = kc is not None
    group, seq_per_chunk = 1, False
    if has_cache:
        act, tl, group, seq_per_chunk = _batch_one_chunk_sequences(act, tl)
    if seq_per_chunk:
        cos, sin_a, sin_b = (jnp.tile(t, (group, 1)) for t in (cos, sin_a, sin_b))
    bsz = act.shape[0]
    ni = act.shape[1] // tl
    n_blocks = bsz * ni
    x_map, y_map, pos_map, _ = _lagged_maps(n_blocks, ni)
    proj_batch_map = lambda n: (jnp.minimum(n, n_blocks - 1) // ni, 0, 0)
    const2 = lambda n: (0, 0)
    wcols = w_b.shape[1]
    act_spec = pl.BlockSpec((1, tl, D_MODEL), x_map)
    in_specs = [act_spec, pl.BlockSpec((1, D_MODEL), const2)] if from_x else [act_spec]
    args = [act, x_norm[1]] if from_x else [act]
    in_specs += [
        pl.BlockSpec((D_MODEL, wcols), const2, pipeline_mode=pl.Buffered(1)),
        pl.BlockSpec((tl, LANES), pos_map),
        pl.BlockSpec((tl, LANES), pos_map),
        pl.BlockSpec((tl, LANES), pos_map),
        pl.BlockSpec((N_Q_B, LANES), const2),
    ]
    args += [w_b, cos, sin_a, sin_b, sinks]
    if has_cache:
        in_specs += [pl.BlockSpec((group, WINDOW, 2 * KV_W_B), proj_batch_map)] * 2
        args += [kc, vc]
    out_specs = [
        pl.BlockSpec((1, tl, W_B), y_map),
        pl.BlockSpec((group, WINDOW, KV_W_B), proj_batch_map),
        pl.BlockSpec((group, WINDOW, KV_W_B), proj_batch_map),
    ]
    out_shape = [
        jax.ShapeDtypeStruct((bsz, ni * tl, W_B), BF16),
        jax.ShapeDtypeStruct((n_seq, WINDOW, KV_W_B), F32),
        jax.ShapeDtypeStruct((n_seq, WINDOW, KV_W_B), F32),
    ]
    if from_x:
        out_specs.append(act_spec)
        out_shape.append(jax.ShapeDtypeStruct((bsz, ni * tl, D_MODEL), BF16))
    kv_rows = group * WINDOW + tl
    outs = pl.pallas_call(
        functools.partial(_swa_kernel, tl=tl, ni=ni, n_blocks=n_blocks, has_cache=has_cache, from_x=from_x,
                          seq_per_chunk=seq_per_chunk),
        grid=(n_blocks + 1,),
        in_specs=in_specs,
        out_specs=out_specs,
        out_shape=out_shape,
        scratch_shapes=[
            pltpu.VMEM((2, tl, W_B), F32),
            pltpu.VMEM((2, tl, W_B), F32),
            pltpu.VMEM((2, kv_rows, 2 * KV_W_B), F32),
            pltpu.VMEM((2, kv_rows, 2 * KV_W_B), F32),
        ],
        compiler_params=pltpu.CompilerParams(
            dimension_semantics=("arbitrary",), vmem_limit_bytes=VMEM_LIMIT),
        name="swa_mixer",
    )(*args)
    outs = list(outs)
    outs[0] = outs[0].reshape(n_seq, seq, W_B)
    if from_x:
        outs[3] = outs[3].reshape(n_seq, seq, D_MODEL)
    return outs


CONV_PAD = 8


def _ssd_kernel(h_ref, w_ref, cw_ref, cb_ref, dtb_ref, alog_ref, dsk_ref, sn_ref, conv0_ref, s0_ref,
                y_ref, snew_ref, convnew_ref,
                ubuf, dt_s, z_s, xc_s, xdt_s, bc_s, la_s, st_s, *, tl, ni, n_blocks, seq_per_chunk):
    n = pl.program_id(0)
    n_chunks = tl // CHUNK
    gw = N_GROUPS_C * D_STATE
    hp = W_C // N_GROUPS_C
    i_proj = jnp.minimum(n, n_blocks - 1) % ni
    i_lag = jnp.maximum(n - 1, 0) % ni
    nxt = n % 2
    cur = 1 - nxt
    n_seg = n_chunks if seq_per_chunk else 1
    seg_rows = tl // n_seg
    seg_new = [slice((CONV_PAD + seg_rows) * s + CONV_PAD, (CONV_PAD + seg_rows) * (s + 1)) for s in range(n_seg)]
    seg_tail = [slice(sn.start - (CONV_W - 1), sn.start) for sn in seg_new]
    seg_last = [slice(sn.stop - (CONV_W - 1), sn.stop) for sn in seg_new]

    @pl.when(n == 0)
    def _():
        for buf in (ubuf, z_s, xc_s, xdt_s, bc_s, la_s):
            buf[...] = jnp.zeros_like(buf)

    if seq_per_chunk:
        for s in range(n_seg):
            ubuf[seg_tail[s], :] = conv0_ref[s]
    else:
        @pl.when(i_proj == 0)
        def _():
            ubuf[seg_tail[0], :] = conv0_ref[0]

        @pl.when(i_lag == 0)
        def _():
            for g in range(N_GROUPS_C):
                st_s[g] = s0_ref[0, hp * g:hp * (g + 1), :].T

    def store_raw(cols, res):
        for s in range(n_seg):
            ubuf[seg_new[s], cols] = res[seg_rows * s:seg_rows * (s + 1)]

    def conv_act(sl):
        win = ubuf[:, sl]
        acc = cb_ref[:, sl]
        for t in range(CONV_W):
            back = CONV_W - 1 - t
            src = pltpu.roll(win, back, 0) if back else win
            acc = acc + src * cw_ref[t:t + 1, sl]
        if not seq_per_chunk:
            ubuf[seg_tail[0], sl] = ubuf[seg_last[0], sl]
        return _silu(jnp.concatenate([acc[sn] for sn in seg_new], axis=0) if n_seg > 1 else acc[seg_new[0]])

    def project(c):
        per = (W_C // MXU_COLS) // n_chunks
        for j in range(c * per, (c + 1) * per):
            sl = slice(MXU_COLS * j, MXU_COLS * (j + 1))
            z_s[nxt, :, sl] = _dot(h_ref[0], w_ref[:, sl])
            yield
            if j == 0:
                dt_raw = _dot(h_ref[0], w_ref[:, W_C + CONV_DIM:W_C + CONV_DIM + LANES])
                yield
                dt_s[...] = _softplus0(dt_raw + dtb_ref[...])
            head_of_col = (MXU_COLS * j + lax.broadcasted_iota(jnp.int32, (LANES, MXU_COLS), 1)) // HEAD_DIM_C
            spread = (lax.broadcasted_iota(jnp.int32, (LANES, MXU_COLS), 0) == head_of_col).astype(BF16)
            dt = _exact_dot(dt_s[...], spread)
            yield
            la_s[nxt, :, sl] = -dt * jnp.exp(alog_ref[:, sl])
            store_raw(sl, _dot(h_ref[0], w_ref[:, W_C + MXU_COLS * j:W_C + MXU_COLS * (j + 1)]))
            yield
            for u in range(MXU_COLS // LANES):
                usl = slice(MXU_COLS * j + LANES * u, MXU_COLS * j + LANES * (u + 1))
                act = conv_act(usl)
                xc_s[nxt, :, usl] = act
                xdt_s[nxt, :, usl] = act * dt[:, LANES * u:LANES * (u + 1)]
            bsl = slice(W_C + MXU_COLS * j, W_C + MXU_COLS * (j + 1))
            store_raw(bsl, _dot(h_ref[0], w_ref[:, W_C + W_C + MXU_COLS * j:W_C + W_C + MXU_COLS * (j + 1)]))
            yield
            for u in range(MXU_COLS // LANES):
                usl = slice(W_C + MXU_COLS * j + LANES * u, W_C + MXU_COLS * j + LANES * (u + 1))
                bc_s[nxt, :, MXU_COLS * j + LANES * u:MXU_COLS * j + LANES * (u + 1)] = conv_act(usl)

    tri = _tri_bf16(CHUNK)
    trow = lax.broadcasted_iota(jnp.int32, (CHUNK, LANES), 0)
    lane = lax.broadcasted_iota(jnp.int32, (CHUNK, LANES), 1)
    s_of_lane = lane % CHUNK
    causal = trow >= s_of_lane
    row2 = lax.broadcasted_iota(jnp.int32, (2 * CHUNK, LANES), 0)
    lane2 = lax.broadcasted_iota(jnp.int32, (2 * CHUNK, LANES), 1)
    blockdiag = (row2 < CHUNK) == (lane2 < HEAD_DIM_C)

    for c in range(n_chunks):
        rows = slice(CHUNK * c, CHUNK * (c + 1))

        def group(g, rows=rows):
            gsl = slice(hp * g, hp * (g + 1))
            bg = bc_s[cur, rows, D_STATE * g:D_STATE * (g + 1)].astype(BF16)
            cg = bc_s[cur, rows, gw + D_STATE * g:gw + D_STATE * (g + 1)].astype(BF16)
            cb2 = _dot_nt(cg, jnp.concatenate([bg, bg], axis=0))
            yield
            st = st_s[g]
            y_state = _dot(cg, st.astype(BF16))
            yield
            la = la_s[cur, rows, gsl]
            b = _cumsum_rows(tri, la)
            yield
            xdt = xdt_s[cur, rows, gsl]
            ys = []
            for u in range(2):
                usl = slice(LANES * u, LANES * (u + 1))
                b_row = jnp.sum(jnp.where(trow <= s_of_lane, la[:, usl], 0.0), axis=0, keepdims=True)
                decay = jnp.where(causal, jnp.exp(b[:, usl] - b_row), 0.0)
                wmat = (cb2 * decay).astype(BF16)
                xbd = jnp.where(blockdiag, jnp.concatenate([xdt[:, usl], xdt[:, usl]], axis=0), 0.0).astype(BF16)
                ys.append(_dot(wmat, xbd))
                yield
            y = jnp.concatenate(ys, axis=1) + y_state * jnp.exp(b)
            o = (y + dsk_ref[:, gsl] * xc_s[cur, rows, gsl]) * _silu(z_s[cur, rows, gsl])
            r = lax.rsqrt(jnp.mean(o * o, axis=-1, keepdims=True) + EPS)
            out = ((o * r) * sn_ref[:, gsl]).astype(BF16)
            b_last = b[CHUNK - 1:CHUNK, :]
            st_new = st * jnp.exp(b_last) + _dot_tn(bg, (xdt * jnp.exp(b_last - b)).astype(BF16))
            return out, st_new

        if seq_per_chunk:
            for g in range(N_GROUPS_C):
                st_s[g] = s0_ref[c, hp * g:hp * (g + 1), :].T
        results = _round_robin([group(g) for g in range(N_GROUPS_C)] + [project(c)])[:N_GROUPS_C]
        for g, (out, st_new) in enumerate(results):
            y_ref[0, rows, hp * g:hp * (g + 1)] = out
            if seq_per_chunk:
                snew_ref[c, hp * g:hp * (g + 1), :] = st_new.T
            else:
                st_s[g] = st_new

    if seq_per_chunk:
        for s in range(n_seg):
            convnew_ref[s] = ubuf[seg_last[s], :]
    else:
        @pl.when(i_proj == ni - 1)
        def _():
            convnew_ref[0] = ubuf[seg_tail[0], :]

        @pl.when(i_lag == ni - 1)
        def _():
            for g in range(N_GROUPS_C):
                snew_ref[0, hp * g:hp * (g + 1), :] = st_s[g].T


def _batch_one_chunk_sequences(h, tl):
    n_seq, seq, _ = h.shape
    group = PROMPT_ROWS // CHUNK
    if seq == CHUNK and n_seq % group == 0:
        return h.reshape(n_seq // group, PROMPT_ROWS, D_MODEL), PROMPT_ROWS, group, True
    return h, tl, 1, False


def _ssd_call(h, w_c, cw, cb, dtb, alog, dsk, sn, conv0, s0, layer, tl):
    n_seq, seq, _ = h.shape
    h, tl, group, seq_per_chunk = _batch_one_chunk_sequences(h, tl)
    bsz = h.shape[0]
    ni = h.shape[1] // tl
    n_blocks = bsz * ni
    x_map, y_map, _, batch_map = _lagged_maps(n_blocks, ni)
    proj_batch_map = lambda n: (jnp.minimum(n, n_blocks - 1) // ni, 0, 0)
    const2 = lambda n: (0, 0)
    wcols = w_c.shape[1]
    y, s_new, conv_new = pl.pallas_call(
        functools.partial(_ssd_kernel, tl=tl, ni=ni, n_blocks=n_blocks, seq_per_chunk=seq_per_chunk),
        grid=(n_blocks + 1,),
        in_specs=[
            pl.BlockSpec((1, tl, D_MODEL), x_map),
            pl.BlockSpec((D_MODEL, wcols), const2, pipeline_mode=pl.Buffered(1)),
            pl.BlockSpec((CONV_W, CONV_DIM), const2),
            pl.BlockSpec((1, CONV_DIM), const2),
            pl.BlockSpec((1, LANES), const2),
            pl.BlockSpec((1, W_C), const2),
            pl.BlockSpec((1, W_C), const2),
            pl.BlockSpec((1, W_C), const2),
            pl.BlockSpec((None, group, CONV_W - 1, CONV_DIM), lambda n: (layer,) + proj_batch_map(n)),
            pl.BlockSpec((None, group, W_C, D_STATE), lambda n: (layer,) + batch_map(3)(n)),
        ],
        out_specs=[
            pl.BlockSpec((1, tl, W_C), y_map),
            pl.BlockSpec((group, W_C, D_STATE), batch_map(3)),
            pl.BlockSpec((group, CONV_W - 1, CONV_DIM), proj_batch_map),
        ],
        out_shape=[
            jax.ShapeDtypeStruct((bsz, ni * tl, W_C), BF16),
            jax.ShapeDtypeStruct((n_seq, W_C, D_STATE), F32),
            jax.ShapeDtypeStruct((n_seq, CONV_W - 1, CONV_DIM), F32),
        ],
        scratch_shapes=[
            pltpu.VMEM((group * CONV_PAD + tl, CONV_DIM), F32),
            pltpu.VMEM((tl, LANES), F32),
            pltpu.VMEM((2, tl, W_C), F32),
            pltpu.VMEM((2, tl, W_C), F32),
            pltpu.VMEM((2, tl, W_C), F32),
            pltpu.VMEM((2, tl, 2 * N_GROUPS_C * D_STATE), F32),
            pltpu.VMEM((2, tl, W_C), F32),
            pltpu.VMEM((N_GROUPS_C, D_STATE, W_C // N_GROUPS_C), F32),
        ],
        compiler_params=pltpu.CompilerParams(
            dimension_semantics=("arbitrary",), vmem_limit_bytes=VMEM_LIMIT),
        name="ssd_mixer",
    )(h, w_c, cw, cb, dtb, alog, dsk, sn, conv0, s0)
    return y.reshape(n_seq, seq, W_C), s_new, conv_new


def _merge_kernel(*refs, emit_next):
    it = iter(refs)
    x_ref, h_ref = next(it), next(it)
    y_refs = [next(it) for _ in range(3)]
    npost_ref = next(it)
    nnext_ref = next(it) if emit_next else None
    wg_refs = [next(it) for _ in range(3)]
    wbr_refs = [next(it) for _ in range(3)]
    wout_ref, o_ref = next(it), next(it)
    hnext_ref = next(it) if emit_next else None
    m_s = next(it)
    j = pl.program_id(1)
    n_j = m_s.shape[0]

    h = h_ref[...]
    merged = None
    for y_ref, wg_ref, wbr_ref in zip(y_refs, wg_refs, wbr_refs):
        term = _sigmoid(_dot(h, wg_ref[...])) * _dot(y_ref[...], wbr_ref[...])
        merged = term if merged is None else merged + term
    m_s[j] = merged.astype(BF16)

    @pl.when(j == n_j - 1)
    def _():
        out = _dot(jnp.concatenate([m_s[n] for n in range(n_j)], axis=1), wout_ref[...])
        r = lax.rsqrt(jnp.mean(out * out, axis=-1, keepdims=True) + EPS)
        x_new = x_ref[...] + (out * r) * npost_ref[...]
        o_ref[...] = x_new
        if emit_next:
            hnext_ref[...] = _prenorm_bf16(x_new, nnext_ref[...])


def _merge_call(x2, h2, ya, yb, yc, npost, nnext, wg, wbr, wout, tm, tn):
    rows = x2.shape[0]
    emit_next = nnext is not None
    grid = (rows // tm, D_MODEL // tn)
    const2 = lambda r, j: (0, 0)
    row_block = lambda width: pl.BlockSpec((tm, width), lambda r, j: (r, 0))
    vec = pl.BlockSpec((1, D_MODEL), const2)
    in_specs = [row_block(D_MODEL), row_block(D_MODEL), row_block(W_A), row_block(W_B), row_block(W_C), vec]
    args = [x2, h2, ya, yb, yc, npost]
    if emit_next:
        in_specs.append(vec)
        args.append(nnext)
    col_block = lambda k: pl.BlockSpec((k, tn), lambda r, j: (0, j))
    in_specs += [col_block(D_MODEL)] * 3 + [col_block(W_A)] * 3
    in_specs.append(pl.BlockSpec((D_MODEL, D_MODEL), const2, pipeline_mode=pl.Buffered(1)))
    args += list(wg) + list(wbr) + [wout]
    out_specs = [row_block(D_MODEL)]
    out_shape = [jax.ShapeDtypeStruct((rows, D_MODEL), F32)]
    if emit_next:
        out_specs.append(row_block(D_MODEL))
        out_shape.append(jax.ShapeDtypeStruct((rows, D_MODEL), BF16))
    outs = pl.pallas_call(
        functools.partial(_merge_kernel, emit_next=emit_next),
        grid=grid,
        in_specs=in_specs,
        out_specs=out_specs,
        out_shape=out_shape,
        scratch_shapes=[pltpu.VMEM((D_MODEL // tn, tm, tn), BF16)],
        compiler_params=pltpu.CompilerParams(
            dimension_semantics=("arbitrary", "arbitrary"), vmem_limit_bytes=VMEM_LIMIT),
        name="merge_out",
    )(*args)
    return (outs[0], outs[1]) if emit_next else (outs[0], None)


def _rope_tables(pos):
    half = ROT_DIM // 2
    inv = jnp.power(ROPE_THETA, -jnp.arange(half, dtype=F32) / half)
    ang = pos.astype(F32)[:, None] * inv[None, :]
    cos, sin = jnp.cos(ang), jnp.sin(ang)
    n = pos.shape[0]
    ones = jnp.ones((n, HEAD_DIM_B - ROT_DIM), F32)
    zeros = jnp.zeros((n, HEAD_DIM_B - ROT_DIM), F32)
    zh = jnp.zeros((n, half), F32)
    cos_t = jnp.concatenate([cos, cos, ones], axis=1)
    sin_a = jnp.concatenate([-sin, zh, zeros], axis=1)
    sin_b = jnp.concatenate([zh, sin, zeros], axis=1)
    tile = lambda t: jnp.concatenate([t, t], axis=1)
    return tile(cos_t), tile(sin_a), tile(sin_b)


def _dup_heads(t):
    lead = t.shape[:-1]
    t4 = t.reshape(lead + (N_KV_B, 1, HEAD_DIM_B))
    return jnp.broadcast_to(t4, lead + (N_KV_B, 2, HEAD_DIM_B)).reshape(lead + (2 * KV_W_B,))


def _pad_lanes(t):
    return jnp.pad(t, [(0, 0)] * (t.ndim - 1) + [(0, LANES - t.shape[-1])])


def _expand_heads(t):
    return jnp.repeat(t, HEAD_DIM_C, axis=-1)


def _layer(x, h, tl, tables, caches, lw, norm_next):
    bsz, seq, _ = x.shape
    cos, sin_a, sin_b = tables
    kc, vc, s_hgrn, s_ssm, s_conv, layer = caches
    if s_hgrn is None:
        layer = 0
        s_hgrn = jnp.zeros((1, bsz, N_HEADS_A, HEAD_K_A, HEAD_K_A), F32)
        s_ssm = jnp.zeros((1, bsz, W_C, D_STATE), F32)
        s_conv = jnp.zeros((1, bsz, CONV_W - 1, CONV_DIM), F32)
    else:
        s_ssm = s_ssm.reshape(-1, bsz, W_C, D_STATE)
        kc = _dup_heads(kc.reshape(bsz, WINDOW, KV_W_B))
        vc = _dup_heads(vc.reshape(bsz, WINDOW, KV_W_B))

    if h is None:
        yb, k_new, v_new, h = _swa_call(None, (x, lw["norm_pre"]), lw["w_b"], cos, sin_a, sin_b, lw["sinks"],
                                        kc, vc, tl)
    else:
        yb, k_new, v_new = _swa_call(h, None, lw["w_b"], cos, sin_a, sin_b, lw["sinks"], kc, vc, tl)
    ya, hgrn_new = _hgrn_call(h, *lw["w_a"], lw["lb_pack"], lw["hgrn_norm"], s_hgrn, layer, tl)
    yc, ssm_new, conv_new = _ssd_call(h, lw["w_c"], lw["conv_w"], lw["conv_b"], lw["dt_bias"],
                                      lw["a_log"], lw["d_skip"], lw["ssm_norm"], s_conv, s_ssm, layer, tl)
    rows = bsz * seq
    x_new, h_next = _merge_call(x.reshape(rows, D_MODEL), h.reshape(rows, D_MODEL), ya.reshape(rows, W_A),
                                yb.reshape(rows, W_B), yc.reshape(rows, W_C), lw["norm_post"], norm_next,
                                lw["w_g"], lw["w_br"], lw["w_out"], min(MERGE_ROWS, rows), MXU_COLS)
    states = (k_new.reshape(bsz, WINDOW, N_KV_B, HEAD_DIM_B), v_new.reshape(bsz, WINDOW, N_KV_B, HEAD_DIM_B),
              hgrn_new, ssm_new.reshape(bsz, N_HEADS_C, HEAD_DIM_C, D_STATE), conv_new)
    if h_next is not None:
        h_next = h_next.reshape(bsz, seq, D_MODEL)
    return x_new.reshape(bsz, seq, D_MODEL), h_next, states


def kernel(x_prompt, x_sample, cache_swa_k, cache_swa_v, state_hgrn, state_ssm, state_conv, norm_pre, norm_post, w_in, hgrn_lb_logits, hgrn_norm, swa_sinks, conv_w, conv_b, dt_bias, a_log, d_skip, ssm_norm, w_branch_a, w_branch_b, w_branch_c, w_out):
    depth = w_in.shape[0]
    lbp = jax.nn.softmax(hgrn_lb_logits.astype(F32), axis=0)
    lbc = jnp.cumsum(lbp, axis=0)
    lb_all = lbc - lbc[0:1]

    offs = np.cumsum([0, W_A, W_A, W_A, W_A, W_B, KV_W_B, KV_W_B, W_B, W_C, CONV_DIM, N_HEADS_C,
                      D_MODEL, D_MODEL, D_MODEL])
    w_in_bf16 = w_in.astype(BF16)
    col = lambda l, a, b: w_in_bf16[l, :, int(offs[a]):int(offs[b])]

    tables_p = _rope_tables(jnp.arange(x_prompt.shape[1], dtype=jnp.int32))
    tables_s = _rope_tables(PAST_LEN + jnp.arange(x_sample.shape[1], dtype=jnp.int32))

    xp, xs = x_prompt, x_sample
    pre = lambda l: norm_pre[l].reshape(1, D_MODEL)
    hp = hs = None
    pst, sst = [], []
    for l in range(depth):
        lb = lb_all[l].reshape(N_HEADS_A, 1, HEAD_K_A)
        norm_next = pre(l + 1) if l + 1 < depth else None
        lw = {
            "norm_pre": pre(l),
            "norm_post": norm_post[l].reshape(1, D_MODEL),
            "w_a": (w_in_bf16, l),
            "w_b": col(l, 4, 8),
            "w_c": jnp.concatenate([col(l, 8, 9), col(l, 9, 10), _pad_lanes(col(l, 10, 11))], axis=1),
            "w_g": [col(l, k, k + 1) for k in (11, 12, 13)],
            "w_br": [w[l].astype(BF16) for w in (w_branch_a, w_branch_b, w_branch_c)],
            "w_out": w_out[l].astype(BF16),
            "lb_pack": jnp.stack([jnp.log(lb), jnp.log1p(-lb)]) * LOG2E,
            "hgrn_norm": hgrn_norm[l].reshape(1, HEAD_K_A),
            "sinks": jnp.broadcast_to(swa_sinks[l].astype(F32)[:, None], (N_Q_B, LANES)),
            "conv_w": conv_w[l],
            "conv_b": conv_b[l].reshape(1, CONV_DIM),
            "dt_bias": _pad_lanes(dt_bias[l].astype(F32).reshape(1, N_HEADS_C)),
            "a_log": _expand_heads(a_log[l].astype(F32)).reshape(1, W_C),
            "d_skip": _expand_heads(d_skip[l].astype(F32)).reshape(1, W_C),
            "ssm_norm": ssm_norm[l].reshape(1, W_C),
        }
        xp, hp, sp = _layer(xp, hp, PROMPT_ROWS, tables_p, (None,) * 6, lw, norm_next)
        xs, hs, ss = _layer(xs, hs, CHUNK, tables_s,
                            (cache_swa_k[l], cache_swa_v[l], state_hgrn, state_ssm, state_conv, l),
                            lw, norm_next)
        pst.append(sp)
        sst.append(ss)

    stack = lambda sts, k: jnp.stack([s[k] for s in sts])
    return (xp, xs,
            stack(pst, 0), stack(pst, 1), stack(pst, 2), stack(pst, 3), stack(pst, 4),
            stack(sst, 0), stack(sst, 1), stack(sst, 2), stack(sst, 3), stack(sst, 4))
```

```python
import functools
import math

import jax
import jax.numpy as jnp
import numpy as np
from jax import lax
from jax.experimental import pallas as pl
from jax.experimental.pallas import tpu as pltpu

F32 = jnp.float32
BF16 = jnp.bfloat16

D_MODEL = 2048
CHUNK = 64
EPS = 1e-6
PAST_LEN = 4096

W_A = 1024
HEAD_K_A = 128
N_HEADS_A = 8

N_Q_B = 16
N_KV_B = 4
HEAD_DIM_B = 64
W_B = 1024
KV_W_B = 256
WINDOW = 128
ROT_DIM = 16
ROPE_THETA = 500000.0
ATTN_SCALE = HEAD_DIM_B ** -0.5

W_C = 1024
HEAD_DIM_C = 64
N_HEADS_C = 16
N_GROUPS_C = 4
D_STATE = 128
CONV_W = 4
CONV_DIM = 2048

LANES = 128
SUBLANES = 8
SUB_BLOCK = 16
MXU_COLS = 256
PROMPT_ROWS = 256
MERGE_ROWS = 512
LOG2E = math.log2(math.e)
VMEM_LIMIT = 56 * 1024 * 1024

NT_DIMS = (((1,), (1,)), ((), ()))
TN_DIMS = (((0,), (0,)), ((), ()))


def _dot(a, b):
    return jnp.dot(a, b, preferred_element_type=F32)


def _dot_nt(a, b):
    return lax.dot_general(a, b, NT_DIMS, preferred_element_type=F32)


def _dot_tn(a, b):
    return lax.dot_general(a, b, TN_DIMS, preferred_element_type=F32)


def _prenorm_bf16(x, w):
    r = lax.rsqrt(jnp.mean(x * x, axis=-1, keepdims=True) + EPS)
    return ((x * r) * w).astype(BF16)


def _sigmoid(x):
    return 0.5 * jnp.tanh(0.5 * x) + 0.5


def _silu(x):
    return x * _sigmoid(x)


def _softplus0(x):
    return jnp.maximum(x, 0.0) + jnp.log(1.0 + jnp.exp(-jnp.abs(x)))


def _tri_bf16(n):
    r = lax.broadcasted_iota(jnp.int32, (n, n), 0)
    c = lax.broadcasted_iota(jnp.int32, (n, n), 1)
    return (r >= c).astype(BF16)


def _split3(x):
    hi = x.astype(BF16)
    r1 = x - hi.astype(F32)
    mid = r1.astype(BF16)
    lo = (r1 - mid.astype(F32)).astype(BF16)
    return hi, mid, lo


def _cumsum_rows(tri, x):
    hi, mid, lo = _split3(x)
    return _dot(tri, hi) + _dot(tri, mid) + _dot(tri, lo)


def _exact_dot(x, sel):
    hi, mid, lo = _split3(x)
    return _dot(hi, sel) + _dot(mid, sel) + _dot(lo, sel)


def _round_robin(gens, background=None, heavy_round=0):
    results = [None] * len(gens)
    live = list(range(len(gens)))
    rnd = 0
    while live:
        if background is not None and rnd == heavy_round:
            for _ in background:
                pass
        for n in list(live):
            try:
                next(gens[n])
            except StopIteration as stop:
                results[n] = stop.value
                live.remove(n)
        rnd += 1
    return results


def _hgrn_kernel(h_ref, w_ref, lb_ref, hn_ref, s0_ref, y_ref, snew_ref,
                 proj_s, st_s, *, tl, ni, seq_per_chunk):
    n = pl.program_id(0)
    n_chunks = tl // CHUNK
    i_lag = jnp.maximum(n - 1, 0) % ni
    nxt = n % 2
    cur = 1 - nxt

    @pl.when(n == 0)
    def _():
        proj_s[...] = jnp.zeros_like(proj_s)

    if not seq_per_chunk:
        @pl.when(i_lag == 0)
        def _():
            for hd in range(N_HEADS_A):
                st_s[hd] = s0_ref[0, hd].T

    cols_per_chunk = (w_ref.shape[1] // MXU_COLS) // n_chunks

    def project(c):
        for j in range(cols_per_chunk):
            jb = c * cols_per_chunk + j
            res = _dot(h_ref[0], w_ref[:, MXU_COLS * jb:MXU_COLS * (jb + 1)])
            proj_s[nxt, 2 * jb] = res[:, :LANES]
            proj_s[nxt, 2 * jb + 1] = res[:, LANES:]
            yield

    tri = _tri_bf16(CHUNK)
    n_sub = CHUNK // SUB_BLOCK
    lane8 = lax.broadcasted_iota(jnp.int32, (SUBLANES, LANES), 1)
    sub8 = lax.broadcasted_iota(jnp.int32, (SUBLANES, LANES), 0)
    row64 = lax.broadcasted_iota(jnp.int32, (CHUNK, CHUNK), 0)
    col64 = lax.broadcasted_iota(jnp.int32, (CHUNK, CHUNK), 1)
    below_diag_block = row64 // SUB_BLOCK > col64 // SUB_BLOCK
    zero8 = jnp.zeros((SUBLANES, LANES), F32)

    def zeros(n):
        return jnp.zeros((n, LANES), F32)

    def piece(a, m, u):
        lo = SUB_BLOCK * m + SUBLANES * u
        return a[lo:lo + SUBLANES]

    def unit(hd, r0):
        rows = pl.ds(r0, CHUNK)
        aq = proj_s[cur, hd, rows, :]
        z = proj_s[cur, N_HEADS_A + hd, rows, :]
        v = proj_s[cur, 2 * N_HEADS_A + hd, rows, :].astype(BF16)
        ag = proj_s[cur, 3 * N_HEADS_A + hd, rows, :]
        log_lb = lb_ref[0, hd]
        log1m_lb = lb_ref[1, hd]

        z2 = z * LOG2E
        log_sig = jnp.minimum(z2, 0.0) - jnp.log2(1.0 + jnp.exp2(-jnp.abs(z2)))
        cterm = log1m_lb + log_sig
        log_f = jnp.maximum(log_lb, cterm) + jnp.log2(1.0 + jnp.exp2(-jnp.abs(log_lb - cterm)))
        log_k = cterm - z2
        q = _silu(aq)

        b2 = _cumsum_rows(tri, log_f)
        yield
        c2 = b2 - log_k
        b2_last = b2[CHUNK - 1:CHUNK, :]
        st = st_s[hd]

        o = _dot_nt((q * jnp.exp2(b2)).astype(BF16), st.astype(BF16))
        yield

        q_slabs, k_slabs = [], []
        for m in range(1, n_sub):
            lo = SUB_BLOCK * m
            ref = b2[lo - 1:lo, :]
            qm = q[lo:lo + SUB_BLOCK] * jnp.exp2(b2[lo:lo + SUB_BLOCK] - ref)
            q_parts = [zeros(lo), qm] + ([zeros(CHUNK - lo - SUB_BLOCK)] if lo + SUB_BLOCK < CHUNK else [])
            q_slabs.append(jnp.concatenate(q_parts, axis=0))
            k_slabs.append(jnp.concatenate([jnp.exp2(ref - c2[:lo]), zeros(CHUNK - lo)], axis=0))
        a_off = _dot_nt(jnp.concatenate(q_slabs, axis=1).astype(BF16),
                        jnp.concatenate(k_slabs, axis=1).astype(BF16))
        yield

        order = [(s, m, u) for s in range(SUB_BLOCK) for m in range(n_sub) for u in range(2)
                 if not (u == 0 and s >= SUBLANES)]
        pieces = []
        for s, m, u in order:
            cs = c2[SUB_BLOCK * m + s:SUB_BLOCK * m + s + 1]
            pieces.append(piece(q, m, u) * jnp.exp2(piece(b2, m, u) - cs))
        sums = [jnp.sum(p, axis=-1, keepdims=True) for p in pieces]
        yield
        d = {(m, u): zero8 for m in range(n_sub) for u in range(2)}
        for n, (s, m, u) in enumerate(order):
            d[(m, u)] = jnp.where(lane8 == s, sums[n], d[(m, u)])
        diag_rows = []
        for m in range(n_sub):
            for u in range(2):
                dm = jnp.where(sub8 + SUBLANES * u >= lane8, d[(m, u)], 0.0)
                diag_rows.append(pltpu.roll(dm, SUB_BLOCK * m, 1) if m else dm)
        a_diag = jnp.concatenate(diag_rows, axis=0)[:, :CHUNK]
        a_full = jnp.where(below_diag_block, a_off, a_diag)
        o = o + _dot(a_full.astype(BF16), v)
        yield

        r = lax.rsqrt(jnp.mean(o * o, axis=-1, keepdims=True) + EPS)
        y = ((o * r) * hn_ref[...]) * _silu(ag)

        st_new = st * jnp.exp2(b2_last) + _dot_tn(v, jnp.exp2(b2_last - c2).astype(BF16))
        return y.astype(BF16), st_new

    for c in range(n_chunks):
        r0 = c * CHUNK
        if seq_per_chunk:
            for hd in range(N_HEADS_A):
                st_s[hd] = s0_ref[c, hd].T
        gens = [unit(hd, r0) for hd in range(N_HEADS_A)] + [project(c)]
        results = _round_robin(gens)[:N_HEADS_A]
        for hd, (y, st_new) in enumerate(results):
            y_ref[0, pl.ds(r0, CHUNK), LANES * hd:LANES * (hd + 1)] = y
            if seq_per_chunk:
                snew_ref[c, hd] = st_new.T
            else:
                st_s[hd] = st_new

    if not seq_per_chunk:
        @pl.when(i_lag == ni - 1)
        def _():
            for hd in range(N_HEADS_A):
                snew_ref[0, hd] = st_s[hd].T


def _lagged_maps(n_blocks, ni):
    def split(m):
        return m // ni, m % ni
    x_map = lambda n: split(jnp.minimum(n, n_blocks - 1)) + (0,)
    y_map = lambda n: split(jnp.maximum(n - 1, 0)) + (0,)
    pos_map = lambda n: (jnp.minimum(n, n_blocks - 1) % ni, 0)
    batch_map = lambda nd: (lambda n: (jnp.maximum(n - 1, 0) // ni,) + (0,) * (nd - 1))
    return x_map, y_map, pos_map, batch_map


def _hgrn_call(h, w_all, w_layer, lb_pack, hn, s0, layer, tl):
    n_seq, seq, _ = h.shape
    h, tl, group, seq_per_chunk = _batch_one_chunk_sequences(h, tl)
    bsz = h.shape[0]
    ni = h.shape[1] // tl
    n_blocks = bsz * ni
    x_map, y_map, _, batch_map = _lagged_maps(n_blocks, ni)
    const2 = lambda n: (0, 0)
    state_spec = pl.BlockSpec((group, N_HEADS_A, HEAD_K_A, HEAD_K_A), batch_map(4))
    y, s_new = pl.pallas_call(
        functools.partial(_hgrn_kernel, tl=tl, ni=ni, seq_per_chunk=seq_per_chunk),
        grid=(n_blocks + 1,),
        in_specs=[
            pl.BlockSpec((1, tl, D_MODEL), x_map),
            pl.BlockSpec((None, D_MODEL, 4 * W_A), lambda n: (w_layer, 0, 0), pipeline_mode=pl.Buffered(1)),
            pl.BlockSpec((2, N_HEADS_A, 1, HEAD_K_A), lambda n: (0, 0, 0, 0)),
            pl.BlockSpec((1, HEAD_K_A), const2),
            pl.BlockSpec((None, group, N_HEADS_A, HEAD_K_A, HEAD_K_A), lambda n: (layer,) + batch_map(4)(n)),
        ],
        out_specs=[pl.BlockSpec((1, tl, W_A), y_map), state_spec],
        out_shape=[
            jax.ShapeDtypeStruct((bsz, ni * tl, W_A), BF16),
            jax.ShapeDtypeStruct((n_seq, N_HEADS_A, HEAD_K_A, HEAD_K_A), F32),
        ],
        scratch_shapes=[
            pltpu.VMEM((2, 4 * N_HEADS_A, tl, LANES), F32),
            pltpu.VMEM((N_HEADS_A, HEAD_K_A, HEAD_K_A), F32),
        ],
        compiler_params=pltpu.CompilerParams(
            dimension_semantics=("arbitrary",), vmem_limit_bytes=VMEM_LIMIT),
        name="hgrn_mixer",
    )(h, w_all, lb_pack, hn, s0)
    return y.reshape(n_seq, seq, W_A), s_new


def _rotate(xs, cos, sin_a, sin_b):
    return xs * cos + pltpu.roll(xs, LANES - ROT_DIM // 2, 1) * sin_a + pltpu.roll(xs, ROT_DIM // 2, 1) * sin_b


def _swa_kernel(*refs, tl, ni, n_blocks, has_cache, seq_per_chunk):
    it = iter(refs)
    x_ref, nw_ref, w_ref, cos_ref, sa_ref, sb_ref, sink_ref = (next(it) for _ in range(7))
    if has_cache:
        kc_ref, vc_ref = next(it), next(it)
    y_ref, knew_ref, vnew_ref, h_ref = (next(it) for _ in range(4))
    q_s, g_s, kbuf, vbuf = (next(it) for _ in range(4))
    n = pl.program_id(0)
    n_chunks = tl // CHUNK
    band = WINDOW + CHUNK
    kvw = 2 * KV_W_B
    i_proj = jnp.minimum(n, n_blocks - 1) % ni
    i_lag = jnp.maximum(n - 1, 0) % ni
    nxt = n % 2
    cur = 1 - nxt
    n_seg = n_chunks if seq_per_chunk else 1
    seg_rows = tl // n_seg
    seg_prev = [slice((WINDOW + seg_rows) * s, (WINDOW + seg_rows) * s + WINDOW) for s in range(n_seg)]
    seg_new = [slice(sp.stop, sp.stop + seg_rows) for sp in seg_prev]
    seg_last = [slice(sn.stop - WINDOW, sn.stop) for sn in seg_new]

    @pl.when(n == 0)
    def _():
        for buf in (q_s, g_s, kbuf, vbuf):
            buf[...] = jnp.zeros_like(buf)

    if seq_per_chunk:
        for s in range(n_seg):
            kbuf[nxt, seg_prev[s], :] = kc_ref[s]
            vbuf[nxt, seg_prev[s], :] = vc_ref[s]
    else:
        @pl.when(i_proj == 0)
        def _():
            if has_cache:
                kbuf[nxt, seg_prev[0], :] = kc_ref[0]
                vbuf[nxt, seg_prev[0], :] = vc_ref[0]
            else:
                kbuf[nxt, seg_prev[0], :] = jnp.zeros((WINDOW, kvw), F32)
                vbuf[nxt, seg_prev[0], :] = jnp.zeros((WINDOW, kvw), F32)

        @pl.when(i_proj != 0)
        def _():
            kbuf[nxt, seg_prev[0], :] = kbuf[cur, seg_last[0], :]
            vbuf[nxt, seg_prev[0], :] = vbuf[cur, seg_last[0], :]

    h_ref[0] = _prenorm_bf16(x_ref[0], nw_ref[...])

    def store_new(buf, cols, val):
        for s in range(n_seg):
            buf[nxt, seg_new[s], cols] = val[seg_rows * s:seg_rows * (s + 1)]

    def project(c):
        per = (W_B // MXU_COLS) // n_chunks
        for j in range(c * per, (c + 1) * per):
            sl = slice(MXU_COLS * j, MXU_COLS * (j + 1))
            res = _dot(h_ref[0], w_ref[:, sl])
            yield
            for u in range(MXU_COLS // LANES):
                q_s[nxt, :, MXU_COLS * j + LANES * u:MXU_COLS * j + LANES * (u + 1)] = _rotate(
                    res[:, LANES * u:LANES * (u + 1)], cos_ref[...], sa_ref[...], sb_ref[...])
            g_s[nxt, :, sl] = _dot(h_ref[0], w_ref[:, W_B + 2 * KV_W_B + MXU_COLS * j:W_B + 2 * KV_W_B + MXU_COLS * (j + 1)])
            yield
            if j < 2:
                res = _dot(h_ref[0], w_ref[:, W_B + KV_W_B * j:W_B + KV_W_B * (j + 1)])
                yield
                for u in range(KV_W_B // LANES):
                    a = res[:, LANES * u:LANES * (u + 1)]
                    if j == 0:
                        a = _rotate(a, cos_ref[...], sa_ref[...], sb_ref[...])
                    swapped = pltpu.roll(a, HEAD_DIM_B, 1)
                    buf = kbuf if j == 0 else vbuf
                    store_new(buf, slice(LANES * 2 * u, LANES * (2 * u + 1)), jnp.where(low1, a, swapped))
                    store_new(buf, slice(LANES * (2 * u + 1), LANES * (2 * u + 2)), jnp.where(low1, swapped, a))

    lane = lax.broadcasted_iota(jnp.int32, (CHUNK, LANES), 1)
    low = lane < HEAD_DIM_B
    low1 = lax.broadcasted_iota(jnp.int32, (1, LANES), 1) < HEAD_DIM_B
    key_row = lax.broadcasted_iota(jnp.int32, (band, LANES), 0)

    def pair(c, j):
        r0 = CHUNK * c
        g = j // 2
        qp = q_s[cur, r0:r0 + CHUNK, LANES * j:LANES * (j + 1)]
        q2 = jnp.concatenate([jnp.where(low, qp, 0.0), jnp.where(low, 0.0, qp)], axis=0).astype(BF16)
        b0 = band * c if seq_per_chunk else r0
        kb = kbuf[cur, b0:b0 + band, LANES * g:LANES * (g + 1)].astype(BF16)
        vb = vbuf[cur, b0:b0 + band, LANES * g:LANES * (g + 1)].astype(BF16)
        s = _dot_nt(kb, q2) * ATTN_SCALE
        yield
        if not has_cache:
            s = jnp.where(i_lag * tl + r0 - WINDOW + key_row >= 0, s, -jnp.inf)
        sk = jnp.where(low1, sink_ref[2 * j:2 * j + 1, :], sink_ref[2 * j + 1:2 * j + 2, :])
        m = jnp.maximum(jnp.max(s, axis=0, keepdims=True), sk)
        p = jnp.exp(s - m)
        den = jnp.sum(p, axis=0, keepdims=True) + jnp.exp(sk - m)
        o2 = _dot_tn((p * (1.0 / den)).astype(BF16), vb)
        yield
        o = jnp.where(low, o2[:CHUNK], o2[CHUNK:])
        y = o * _silu(g_s[cur, r0:r0 + CHUNK, LANES * j:LANES * (j + 1)])
        y_ref[0, r0:r0 + CHUNK, LANES * j:LANES * (j + 1)] = y.astype(BF16)

    for c in range(n_chunks):
        _round_robin([pair(c, j) for j in range(N_Q_B // 2)], project(c), heavy_round=1)

    def write_newest(s):
        ktail = kbuf[nxt, seg_last[s], :]
        vtail = vbuf[nxt, seg_last[s], :]
        low_w = lax.broadcasted_iota(jnp.int32, (WINDOW, LANES), 1) < HEAD_DIM_B
        for u in range(KV_W_B // LANES):
            knew_ref[s, :, LANES * u:LANES * (u + 1)] = jnp.where(
                low_w, ktail[:, 2 * LANES * u:2 * LANES * u + LANES],
                ktail[:, 2 * LANES * u + LANES:2 * LANES * (u + 1)])
            vnew_ref[s, :, LANES * u:LANES * (u + 1)] = jnp.where(
                low_w, vtail[:, 2 * LANES * u:2 * LANES * u + LANES],
                vtail[:, 2 * LANES * u + LANES:2 * LANES * (u + 1)])

    if seq_per_chunk:
        for s in range(n_seg):
            write_newest(s)
    else:
        pl.when(i_proj == ni - 1)(lambda: write_newest(0))


def _swa_call(act, nw, w_b, cos, sin_a, sin_b, sinks, kc, vc, tl):
    n_seq, seq, _ = act.shape
    has_cache = kc is not None
    group, seq_per_chunk = 1, False
    if has_cache:
        act, tl, group, seq_per_chunk = _batch_one_chunk_sequences(act, tl)
    if seq_per_chunk:
        cos, sin_a, sin_b = (jnp.tile(t, (group, 1)) for t in (cos, sin_a, sin_b))
    bsz = act.shape[0]
    ni = act.shape[1] // tl
    n_blocks = bsz * ni
    x_map, y_map, pos_map, _ = _lagged_maps(n_blocks, ni)
    proj_batch_map = lambda n: (jnp.minimum(n, n_blocks - 1) // ni, 0, 0)
    const2 = lambda n: (0, 0)
    wcols = w_b.shape[1]
    act_spec = pl.BlockSpec((1, tl, D_MODEL), x_map)
    in_specs = [
        act_spec,
        pl.BlockSpec((1, D_MODEL), const2),
        pl.BlockSpec((D_MODEL, wcols), const2, pipeline_mode=pl.Buffered(1)),
        pl.BlockSpec((tl, LANES), pos_map),
        pl.BlockSpec((tl, LANES), pos_map),
        pl.BlockSpec((tl, LANES), pos_map),
        pl.BlockSpec((N_Q_B, LANES), const2),
    ]
    args = [act, nw, w_b, cos, sin_a, sin_b, sinks]
    if has_cache:
        in_specs += [pl.BlockSpec((group, WINDOW, 2 * KV_W_B), proj_batch_map)] * 2
        args += [kc, vc]
    out_specs = [
        pl.BlockSpec((1, tl, W_B), y_map),
        pl.BlockSpec((group, WINDOW, KV_W_B), proj_batch_map),
        pl.BlockSpec((group, WINDOW, KV_W_B), proj_batch_map),
        act_spec,
    ]
    out_shape = [
        jax.ShapeDtypeStruct((bsz, ni * tl, W_B), BF16),
        jax.ShapeDtypeStruct((n_seq, WINDOW, KV_W_B), F32),
        jax.ShapeDtypeStruct((n_seq, WINDOW, KV_W_B), F32),
        jax.ShapeDtypeStruct((bsz, ni * tl, D_MODEL), BF16),
    ]
    kv_rows = group * WINDOW + tl
    y, k_new, v_new, h = pl.pallas_call(
        functools.partial(_swa_kernel, tl=tl, ni=ni, n_blocks=n_blocks, has_cache=has_cache,
                          seq_per_chunk=seq_per_chunk),
        grid=(n_blocks + 1,),
        in_specs=in_specs,
        out_specs=out_specs,
        out_shape=out_shape,
        scratch_shapes=[
            pltpu.VMEM((2, tl, W_B), F32),
            pltpu.VMEM((2, tl, W_B), F32),
            pltpu.VMEM((2, kv_rows, 2 * KV_W_B), F32),
            pltpu.VMEM((2, kv_rows, 2 * KV_W_B), F32),
        ],
        compiler_params=pltpu.CompilerParams(
            dimension_semantics=("arbitrary",), vmem_limit_bytes=VMEM_LIMIT),
        name="swa_mixer",
    )(*args)
    return y.reshape(n_seq, seq, W_B), k_new, v_new, h.reshape(n_seq, seq, D_MODEL)


CONV_PAD = 8


def _ssd_kernel(h_ref, w_ref, cw_ref, cb_ref, dtb_ref, alog_ref, dsk_ref, sn_ref, conv0_ref, s0_ref,
                y_ref, snew_ref, convnew_ref,
                ubuf, dt_s, z_s, xc_s, xdt_s, bc_s, la_s, st_s, *, tl, ni, n_blocks, seq_per_chunk):
    n = pl.program_id(0)
    n_chunks = tl // CHUNK
    gw = N_GROUPS_C * D_STATE
    hp = W_C // N_GROUPS_C
    i_proj = jnp.minimum(n, n_blocks - 1) % ni
    i_lag = jnp.maximum(n - 1, 0) % ni
    nxt = n % 2
    cur = 1 - nxt
    n_seg = n_chunks if seq_per_chunk else 1
    seg_rows = tl // n_seg
    seg_new = [slice((CONV_PAD + seg_rows) * s + CONV_PAD, (CONV_PAD + seg_rows) * (s + 1)) for s in range(n_seg)]
    seg_tail = [slice(sn.start - (CONV_W - 1), sn.start) for sn in seg_new]
    seg_last = [slice(sn.stop - (CONV_W - 1), sn.stop) for sn in seg_new]

    @pl.when(n == 0)
    def _():
        for buf in (ubuf, z_s, xc_s, xdt_s, bc_s, la_s):
            buf[...] = jnp.zeros_like(buf)

    if seq_per_chunk:
        for s in range(n_seg):
            ubuf[seg_tail[s], :] = conv0_ref[s]
    else:
        @pl.when(i_proj == 0)
        def _():
            ubuf[seg_tail[0], :] = conv0_ref[0]

        @pl.when(i_lag == 0)
        def _():
            for g in range(N_GROUPS_C):
                st_s[g] = s0_ref[0, hp * g:hp * (g + 1), :].T

    def store_raw(cols, res):
        for s in range(n_seg):
            ubuf[seg_new[s], cols] = res[seg_rows * s:seg_rows * (s + 1)]

    def conv_act(sl):
        win = ubuf[:, sl]
        acc = cb_ref[:, sl]
        for t in range(CONV_W):
            back = CONV_W - 1 - t
            src = pltpu.roll(win, back, 0) if back else win
            acc = acc + src * cw_ref[t:t + 1, sl]
        if not seq_per_chunk:
            ubuf[seg_tail[0], sl] = ubuf[seg_last[0], sl]
        return _silu(jnp.concatenate([acc[sn] for sn in seg_new], axis=0) if n_seg > 1 else acc[seg_new[0]])

    def project(c):
        per = (W_C // MXU_COLS) // n_chunks
        for j in range(c * per, (c + 1) * per):
            sl = slice(MXU_COLS * j, MXU_COLS * (j + 1))
            z_s[nxt, :, sl] = _dot(h_ref[0], w_ref[:, sl])
            yield
            if j == 0:
                dt_raw = _dot(h_ref[0], w_ref[:, W_C + CONV_DIM:W_C + CONV_DIM + LANES])
                yield
                dt_s[...] = _softplus0(dt_raw + dtb_ref[...])
            head_of_col = (MXU_COLS * j + lax.broadcasted_iota(jnp.int32, (LANES, MXU_COLS), 1)) // HEAD_DIM_C
            spread = (lax.broadcasted_iota(jnp.int32, (LANES, MXU_COLS), 0) == head_of_col).astype(BF16)
            dt = _exact_dot(dt_s[...], spread)
            yield
            la_s[nxt, :, sl] = -dt * jnp.exp(alog_ref[:, sl])
            store_raw(sl, _dot(h_ref[0], w_ref[:, W_C + MXU_COLS * j:W_C + MXU_COLS * (j + 1)]))
            yield
            for u in range(MXU_COLS // LANES):
                usl = slice(MXU_COLS * j + LANES * u, MXU_COLS * j + LANES * (u + 1))
                act = conv_act(usl)
                xc_s[nxt, :, usl] = act
                xdt_s[nxt, :, usl] = act * dt[:, LANES * u:LANES * (u + 1)]
            bsl = slice(W_C + MXU_COLS * j, W_C + MXU_COLS * (j + 1))
            store_raw(bsl, _dot(h_ref[0], w_ref[:, W_C + W_C + MXU_COLS * j:W_C + W_C + MXU_COLS * (j + 1)]))
            yield
            for u in range(MXU_COLS // LANES):
                usl = slice(W_C + MXU_COLS * j + LANES * u, W_C + MXU_COLS * j + LANES * (u + 1))
                bc_s[nxt, :, MXU_COLS * j + LANES * u:MXU_COLS * j + LANES * (u + 1)] = conv_act(usl)

    tri = _tri_bf16(CHUNK)
    trow = lax.broadcasted_iota(jnp.int32, (CHUNK, LANES), 0)
    lane = lax.broadcasted_iota(jnp.int32, (CHUNK, LANES), 1)
    s_of_lane = lane % CHUNK
    causal = trow >= s_of_lane
    row2 = lax.broadcasted_iota(jnp.int32, (2 * CHUNK, LANES), 0)
    lane2 = lax.broadcasted_iota(jnp.int32, (2 * CHUNK, LANES), 1)
    blockdiag = (row2 < CHUNK) == (lane2 < HEAD_DIM_C)

    for c in range(n_chunks):
        rows = slice(CHUNK * c, CHUNK * (c + 1))

        def group(g, rows=rows):
            gsl = slice(hp * g, hp * (g + 1))
            bg = bc_s[cur, rows, D_STATE * g:D_STATE * (g + 1)].astype(BF16)
            cg = bc_s[cur, rows, gw + D_STATE * g:gw + D_STATE * (g + 1)].astype(BF16)
            cb2 = _dot_nt(cg, jnp.concatenate([bg, bg], axis=0))
            yield
            st = st_s[g]
            y_state = _dot(cg, st.astype(BF16))
            yield
            la = la_s[cur, rows, gsl]
            b = _cumsum_rows(tri, la)
            yield
            xdt = xdt_s[cur, rows, gsl]
            ys = []
            for u in range(2):
                usl = slice(LANES * u, LANES * (u + 1))
                b_row = jnp.sum(jnp.where(trow <= s_of_lane, la[:, usl], 0.0), axis=0, keepdims=True)
                decay = jnp.where(causal, jnp.exp(b[:, usl] - b_row), 0.0)
                wmat = (cb2 * decay).astype(BF16)
                xbd = jnp.where(blockdiag, jnp.concatenate([xdt[:, usl], xdt[:, usl]], axis=0), 0.0).astype(BF16)
                ys.append(_dot(wmat, xbd))
                yield
            y = jnp.concatenate(ys, axis=1) + y_state * jnp.exp(b)
            o = (y + dsk_ref[:, gsl] * xc_s[cur, rows, gsl]) * _silu(z_s[cur, rows, gsl])
            r = lax.rsqrt(jnp.mean(o * o, axis=-1, keepdims=True) + EPS)
            out = ((o * r) * sn_ref[:, gsl]).astype(BF16)
            b_last = b[CHUNK - 1:CHUNK, :]
            st_new = st * jnp.exp(b_last) + _dot_tn(bg, (xdt * jnp.exp(b_last - b)).astype(BF16))
            return out, st_new

        if seq_per_chunk:
            for g in range(N_GROUPS_C):
                st_s[g] = s0_ref[c, hp * g:hp * (g + 1), :].T
        results = _round_robin([group(g) for g in range(N_GROUPS_C)] + [project(c)])[:N_GROUPS_C]
        for g, (out, st_new) in enumerate(results):
            y_ref[0, rows, hp * g:hp * (g + 1)] = out
            if seq_per_chunk:
                snew_ref[c, hp * g:hp * (g + 1), :] = st_new.T
            else:
                st_s[g] = st_new

    if seq_per_chunk:
        for s in range(n_seg):
            convnew_ref[s] = ubuf[seg_last[s], :]
    else:
        @pl.when(i_proj == ni - 1)
        def _():
            convnew_ref[0] = ubuf[seg_tail[0], :]

        @pl.when(i_lag == ni - 1)
        def _():
            for g in range(N_GROUPS_C):
                snew_ref[0, hp * g:hp * (g + 1), :] = st_s[g].T


def _batch_one_chunk_sequences(h, tl):
    n_seq, seq, _ = h.shape
    group = PROMPT_ROWS // CHUNK
    if seq == CHUNK and n_seq % group == 0:
        return h.reshape(n_seq // group, PROMPT_ROWS, D_MODEL), PROMPT_ROWS, group, True
    return h, tl, 1, False


def _ssd_call(h, w_c, cw, cb, dtb, alog, dsk, sn, conv0, s0, layer, tl):
    n_seq, seq, _ = h.shape
    h, tl, group, seq_per_chunk = _batch_one_chunk_sequences(h, tl)
    bsz = h.shape[0]
    ni = h.shape[1] // tl
    n_blocks = bsz * ni
    x_map, y_map, _, batch_map = _lagged_maps(n_blocks, ni)
    proj_batch_map = lambda n: (jnp.minimum(n, n_blocks - 1) // ni, 0, 0)
    const2 = lambda n: (0, 0)
    wcols = w_c.shape[1]
    y, s_new, conv_new = pl.pallas_call(
        functools.partial(_ssd_kernel, tl=tl, ni=ni, n_blocks=n_blocks, seq_per_chunk=seq_per_chunk),
        grid=(n_blocks + 1,),
        in_specs=[
            pl.BlockSpec((1, tl, D_MODEL), x_map),
            pl.BlockSpec((D_MODEL, wcols), const2, pipeline_mode=pl.Buffered(1)),
            pl.BlockSpec((CONV_W, CONV_DIM), const2),
            pl.BlockSpec((1, CONV_DIM), const2),
            pl.BlockSpec((1, LANES), const2),
            pl.BlockSpec((1, W_C), const2),
            pl.BlockSpec((1, W_C), const2),
            pl.BlockSpec((1, W_C), const2),
            pl.BlockSpec((None, group, CONV_W - 1, CONV_DIM), lambda n: (layer,) + proj_batch_map(n)),
            pl.BlockSpec((None, group, W_C, D_STATE), lambda n: (layer,) + batch_map(3)(n)),
        ],
        out_specs=[
            pl.BlockSpec((1, tl, W_C), y_map),
            pl.BlockSpec((group, W_C, D_STATE), batch_map(3)),
            pl.BlockSpec((group, CONV_W - 1, CONV_DIM), proj_batch_map),
        ],
        out_shape=[
            jax.ShapeDtypeStruct((bsz, ni * tl, W_C), BF16),
            jax.ShapeDtypeStruct((n_seq, W_C, D_STATE), F32),
            jax.ShapeDtypeStruct((n_seq, CONV_W - 1, CONV_DIM), F32),
        ],
        scratch_shapes=[
            pltpu.VMEM((group * CONV_PAD + tl, CONV_DIM), F32),
            pltpu.VMEM((tl, LANES), F32),
            pltpu.VMEM((2, tl, W_C), F32),
            pltpu.VMEM((2, tl, W_C), F32),
            pltpu.VMEM((2, tl, W_C), F32),
            pltpu.VMEM((2, tl, 2 * N_GROUPS_C * D_STATE), F32),
            pltpu.VMEM((2, tl, W_C), F32),
            pltpu.VMEM((N_GROUPS_C, D_STATE, W_C // N_GROUPS_C), F32),
        ],
        compiler_params=pltpu.CompilerParams(
            dimension_semantics=("arbitrary",), vmem_limit_bytes=VMEM_LIMIT),
        name="ssd_mixer",
    )(h, w_c, cw, cb, dtb, alog, dsk, sn, conv0, s0)
    return y.reshape(n_seq, seq, W_C), s_new, conv_new


def _merge_kernel(*refs):
    it = iter(refs)
    x_ref, h_ref = next(it), next(it)
    y_refs = [next(it) for _ in range(3)]
    npost_ref = next(it)
    wg_refs = [next(it) for _ in range(3)]
    wbr_refs = [next(it) for _ in range(3)]
    wout_ref, o_ref, m_s = next(it), next(it), next(it)
    j = pl.program_id(1)
    n_j = m_s.shape[0]

    h = h_ref[...]
    merged = None
    for y_ref, wg_ref, wbr_ref in zip(y_refs, wg_refs, wbr_refs):
        term = _sigmoid(_dot(h, wg_ref[...])) * _dot(y_ref[...], wbr_ref[...])
        merged = term if merged is None else merged + term
    m_s[j] = merged.astype(BF16)

    @pl.when(j == n_j - 1)
    def _():
        out = _dot(jnp.concatenate([m_s[n] for n in range(n_j)], axis=1), wout_ref[...])
        r = lax.rsqrt(jnp.mean(out * out, axis=-1, keepdims=True) + EPS)
        o_ref[...] = x_ref[...] + (out * r) * npost_ref[...]


def _merge_call(x2, h2, ya, yb, yc, npost, wg, wbr, wout, tm, tn):
    rows = x2.shape[0]
    grid = (rows // tm, D_MODEL // tn)
    const2 = lambda r, j: (0, 0)
    row_block = lambda width: pl.BlockSpec((tm, width), lambda r, j: (r, 0))
    col_block = lambda k: pl.BlockSpec((k, tn), lambda r, j: (0, j))
    in_specs = [row_block(D_MODEL), row_block(D_MODEL), row_block(W_A), row_block(W_B), row_block(W_C),
                pl.BlockSpec((1, D_MODEL), const2)]
    in_specs += [col_block(D_MODEL)] * 3 + [col_block(W_A)] * 3
    in_specs.append(pl.BlockSpec((D_MODEL, D_MODEL), const2, pipeline_mode=pl.Buffered(1)))
    return pl.pallas_call(
        _merge_kernel,
        grid=grid,
        in_specs=in_specs,
        out_specs=row_block(D_MODEL),
        out_shape=jax.ShapeDtypeStruct((rows, D_MODEL), F32),
        scratch_shapes=[pltpu.VMEM((D_MODEL // tn, tm, tn), BF16)],
        compiler_params=pltpu.CompilerParams(
            dimension_semantics=("arbitrary", "arbitrary"), vmem_limit_bytes=VMEM_LIMIT),
        name="merge_out",
    )(x2, h2, ya, yb, yc, npost, *wg, *wbr, wout)


def _rope_tables(pos):
    half = ROT_DIM // 2
    inv = jnp.power(ROPE_THETA, -jnp.arange(half, dtype=F32) / half)
    ang = pos.astype(F32)[:, None] * inv[None, :]
    cos, sin = jnp.cos(ang), jnp.sin(ang)
    n = pos.shape[0]
    ones = jnp.ones((n, HEAD_DIM_B - ROT_DIM), F32)
    zeros = jnp.zeros((n, HEAD_DIM_B - ROT_DIM), F32)
    zh = jnp.zeros((n, half), F32)
    cos_t = jnp.concatenate([cos, cos, ones], axis=1)
    sin_a = jnp.concatenate([-sin, zh, zeros], axis=1)
    sin_b = jnp.concatenate([zh, sin, zeros], axis=1)
    tile = lambda t: jnp.concatenate([t, t], axis=1)
    return tile(cos_t), tile(sin_a), tile(sin_b)


def _dup_heads(t):
    lead = t.shape[:-1]
    t4 = t.reshape(lead + (N_KV_B, 1, HEAD_DIM_B))
    return jnp.broadcast_to(t4, lead + (N_KV_B, 2, HEAD_DIM_B)).reshape(lead + (2 * KV_W_B,))


def _pad_lanes(t):
    return jnp.pad(t, [(0, 0)] * (t.ndim - 1) + [(0, LANES - t.shape[-1])])


def _expand_heads(t):
    return jnp.repeat(t, HEAD_DIM_C, axis=-1)


def _layer(x, tl, tables, caches, lw):
    bsz, seq, _ = x.shape
    cos, sin_a, sin_b = tables
    kc, vc, s_hgrn, s_ssm, s_conv, layer = caches
    if s_hgrn is None:
        layer = 0
        s_hgrn = jnp.zeros((1, bsz, N_HEADS_A, HEAD_K_A, HEAD_K_A), F32)
        s_ssm = jnp.zeros((1, bsz, W_C, D_STATE), F32)
        s_conv = jnp.zeros((1, bsz, CONV_W - 1, CONV_DIM), F32)
    else:
        s_ssm = s_ssm.reshape(-1, bsz, W_C, D_STATE)
        kc = _dup_heads(kc.reshape(bsz, WINDOW, KV_W_B))
        vc = _dup_heads(vc.reshape(bsz, WINDOW, KV_W_B))

    yb, k_new, v_new, h = _swa_call(x, lw["norm_pre"], lw["w_b"], cos, sin_a, sin_b, lw["sinks"], kc, vc, tl)
    ya, hgrn_new = _hgrn_call(h, *lw["w_a"], lw["lb_pack"], lw["hgrn_norm"], s_hgrn, layer, tl)
    yc, ssm_new, conv_new = _ssd_call(h, lw["w_c"], lw["conv_w"], lw["conv_b"], lw["dt_bias"],
                                      lw["a_log"], lw["d_skip"], lw["ssm_norm"], s_conv, s_ssm, layer, tl)
    rows = bsz * seq
    x_new = _merge_call(x.reshape(rows, D_MODEL), h.reshape(rows, D_MODEL), ya.reshape(rows, W_A),
                        yb.reshape(rows, W_B), yc.reshape(rows, W_C), lw["norm_post"],
                        lw["w_g"], lw["w_br"], lw["w_out"], min(MERGE_ROWS, rows), MXU_COLS)
    states = (k_new.reshape(bsz, WINDOW, N_KV_B, HEAD_DIM_B), v_new.reshape(bsz, WINDOW, N_KV_B, HEAD_DIM_B),
              hgrn_new, ssm_new.reshape(bsz, N_HEADS_C, HEAD_DIM_C, D_STATE), conv_new)
    return x_new.reshape(bsz, seq, D_MODEL), states


def kernel(x_prompt, x_sample, cache_swa_k, cache_swa_v, state_hgrn, state_ssm, state_conv, norm_pre, norm_post, w_in, hgrn_lb_logits, hgrn_norm, swa_sinks, conv_w, conv_b, dt_bias, a_log, d_skip, ssm_norm, w_branch_a, w_branch_b, w_branch_c, w_out):
    depth = w_in.shape[0]
    lbp = jax.nn.softmax(hgrn_lb_logits.astype(F32), axis=0)
    lbc = jnp.cumsum(lbp, axis=0)
    lb_all = lbc - lbc[0:1]

    offs = np.cumsum([0, W_A, W_A, W_A, W_A, W_B, KV_W_B, KV_W_B, W_B, W_C, CONV_DIM, N_HEADS_C,
                      D_MODEL, D_MODEL, D_MODEL])
    w_in_bf16 = w_in.astype(BF16)
    col = lambda l, a, b: w_in_bf16[l, :, int(offs[a]):int(offs[b])]

    tables_p = _rope_tables(jnp.arange(x_prompt.shape[1], dtype=jnp.int32))
    tables_s = _rope_tables(PAST_LEN + jnp.arange(x_sample.shape[1], dtype=jnp.int32))

    xp, xs = x_prompt, x_sample
    pst, sst = [], []
    for l in range(depth):
        lb = lb_all[l].reshape(N_HEADS_A, 1, HEAD_K_A)
        lw = {
            "norm_pre": norm_pre[l].reshape(1, D_MODEL),
            "norm_post": norm_post[l].reshape(1, D_MODEL),
            "w_a": (w_in_bf16, l),
            "w_b": col(l, 4, 8),
            "w_c": jnp.concatenate([col(l, 8, 9), col(l, 9, 10), _pad_lanes(col(l, 10, 11))], axis=1),
            "w_g": [col(l, k, k + 1) for k in (11, 12, 13)],
            "w_br": [w[l].astype(BF16) for w in (w_branch_a, w_branch_b, w_branch_c)],
            "w_out": w_out[l].astype(BF16),
            "lb_pack": jnp.stack([jnp.log(lb), jnp.log1p(-lb)]) * LOG2E,
            "hgrn_norm": hgrn_norm[l].reshape(1, HEAD_K_A),
            "sinks": jnp.broadcast_to(swa_sinks[l].astype(F32)[:, None], (N_Q_B, LANES)),
            "conv_w": conv_w[l],
            "conv_b": conv_b[l].reshape(1, CONV_DIM),
            "dt_bias": _pad_lanes(dt_bias[l].astype(F32).reshape(1, N_HEADS_C)),
            "a_log": _expand_heads(a_log[l].astype(F32)).reshape(1, W_C),
            "d_skip": _expand_heads(d_skip[l].astype(F32)).reshape(1, W_C),
            "ssm_norm": ssm_norm[l].reshape(1, W_C),
        }
        xp, sp = _layer(xp, PROMPT_ROWS, tables_p, (None,) * 6, lw)
        xs, ss = _layer(xs, CHUNK, tables_s,
                        (cache_swa_k[l], cache_swa_v[l], state_hgrn, state_ssm, state_conv, l), lw)
        pst.append(sp)
        sst.append(ss)

    stack = lambda sts, k: jnp.stack([s[k] for s in sts])
    return (xp, xs,
            stack(pst, 0), stack(pst, 1), stack(pst, 2), stack(pst, 3), stack(pst, 4),
            stack(sst, 0), stack(sst, 1), stack(sst, 2), stack(sst, 3), stack(sst, 4))
```
